```python
import math
import jax, jax.numpy as jnp
from jax import lax
import numpy as np

D_MODEL = 1024
BATCH = 8
SEQ = 2048
DEPTH = 2

N_META = 16
ROPE_THETA = 500000.0
LN_EPS = 1e-5
GLA_HEADS = 4
GLA_DK = 64
GLA_DV = 128
GLA_RANK = 16
GLA_TAU = 16.0
GLA_CHUNK = 64
GLA_W = GLA_HEADS * GLA_DV
DSA_HEADS = 8
DSA_KV_HEADS = 2
DSA_GROUP = DSA_HEADS // DSA_KV_HEADS
DSA_HD = 64
DSA_W = DSA_HEADS * DSA_HD
IDX_HEADS = 8
IDX_HD = 64
TOPK_MAX = 256
Q_BLOCK = 128
ROPE_DIM = DSA_HD // 4
MIX_W = GLA_W + DSA_W
IN_SPLITS = (
    GLA_HEADS * GLA_DK,
    GLA_HEADS * GLA_DK,
    GLA_W,
    GLA_RANK,
    GLA_W,
    DSA_HEADS * DSA_HD,
    DSA_KV_HEADS * DSA_HD,
    DSA_KV_HEADS * DSA_HD,
    IDX_HEADS * IDX_HD,
    IDX_HD,
    IDX_HEADS,
    DSA_W,
)
IN_W = sum(IN_SPLITS)
DEEPNORM_ALPHA = (2.0 * DEPTH) ** 0.25
DEEPNORM_BETA = (8.0 * DEPTH) ** -0.25

kernel_name = "hymba_gla_dsa_deepnorm_trunk"


def _layer_norm(x, g, b):
    xf = x.astype(jnp.float32)
    mu = xf.mean(-1, keepdims=True)
    var = jnp.square(xf - mu).mean(-1, keepdims=True)
    return ((xf - mu) * lax.rsqrt(var + LN_EPS) * g.astype(jnp.float32) + b.astype(jnp.float32)).astype(x.dtype)


def _split_cols(p):
    out, o = [], 0
    for w in IN_SPLITS:
        out.append(p[..., o:o + w])
        o += w
    return out


def _rope_tables(n_pos):
    inv = ROPE_THETA ** (-jnp.arange(0, ROPE_DIM, 2, dtype=jnp.float32) / ROPE_DIM)
    ang = jnp.arange(n_pos, dtype=jnp.float32)[:, None] * inv[None, :]
    return jnp.cos(ang), jnp.sin(ang)


def _partial_rope(x, cos, sin):
    half = ROPE_DIM // 2
    shape = (1, cos.shape[0]) + (1,) * (x.ndim - 3) + (half,)
    c = cos.reshape(shape).astype(x.dtype)
    s = sin.reshape(shape).astype(x.dtype)
    x1, x2, rest = x[..., :half], x[..., half:ROPE_DIM], x[..., ROPE_DIM:]
    return jnp.concatenate([x1 * c - x2 * s, x2 * c + x1 * s, rest], axis=-1)


def _gla(q, k, v, g):
    B, L, H, _ = q.shape
    pad = (-N_META) % GLA_CHUNK
    padw = ((0, 0), (pad, 0), (0, 0), (0, 0))
    q, k, v, g = [jnp.pad(a.astype(jnp.float32), padw) for a in (q, k, v, g)]
    n = (L + pad) // GLA_CHUNK

    def to_chunks(a):
        return a.reshape(B, n, GLA_CHUNK, H, a.shape[-1]).transpose(1, 0, 3, 2, 4)

    causal = jnp.tril(jnp.ones((GLA_CHUNK, GLA_CHUNK), dtype=bool))[:, :, None]

    def step(S, inp):
        qi, ki, vi, gi = inp
        b = jnp.cumsum(gi, axis=2)
        o_inter = jnp.einsum('bhtd,bhde->bhte', qi * jnp.exp(b), S)
        diff = b[:, :, :, None, :] - b[:, :, None, :, :]
        decay = jnp.exp(jnp.where(causal, diff, -jnp.inf))
        A = jnp.einsum('bhtd,bhsd,bhtsd->bhts', qi, ki, decay)
        o = o_inter + jnp.einsum('bhts,bhse->bhte', A, vi)
        b_last = b[:, :, -1:, :]
        S_new = jnp.exp(b_last[:, :, 0, :])[..., None] * S + jnp.einsum(
            'bhsd,bhse->bhde', ki * jnp.exp(b_last - b), vi)
        return S_new, o

    S0 = jnp.zeros((B, H, GLA_DK, GLA_DV), jnp.float32)
    _, o = lax.scan(step, S0, (to_chunks(q), to_chunks(k), to_chunks(v), to_chunks(g)))
    o = o.transpose(1, 0, 3, 2, 4).reshape(B, n * GLA_CHUNK, H, GLA_DV)
    return o[:, pad:]


def _dsa(q, k, v, iq, ik, iw, n_real):
    B, L = q.shape[:2]
    topk = min(TOPK_MAX, n_real // 4)
    nb = -(-L // Q_BLOCK)
    Lp = nb * Q_BLOCK

    def blocks(a):
        a = jnp.pad(a, ((0, 0), (0, Lp - L)) + ((0, 0),) * (a.ndim - 2))
        return a.reshape((B, nb, Q_BLOCK) + a.shape[2:]).swapaxes(0, 1)

    kv = jnp.concatenate([k, v], axis=-1)
    key_pos = jnp.arange(L)

    def one_block(inp):
        start, qb, iqb, iwb = inp
        t = start + jnp.arange(Q_BLOCK)
        dots = jnp.einsum('bqjd,bsd->bqjs', iqb, ik).astype(jnp.float32) * (IDX_HD ** -0.5)
        score = jnp.einsum('bqj,bqjs->bqs', iwb.astype(jnp.float32) * (IDX_HEADS ** -0.5), jax.nn.relu(dots))
        causal = key_pos[None, :] <= t[:, None]
        score = jnp.where(causal[None], score, -jnp.inf)
        _, sel = lax.top_k(score, topk)
        kv_sel = jax.vmap(lambda a, i: a[i])(kv, sel)
        k_sel, v_sel = kv_sel[..., :DSA_HD], kv_sel[..., DSA_HD:]
        valid = sel <= t[None, :, None]
        qg = qb.reshape(B, Q_BLOCK, DSA_KV_HEADS, DSA_GROUP, DSA_HD)
        s = jnp.einsum('bqgrd,bqkgd->bqgrk', qg, k_sel).astype(jnp.float32) * (DSA_HD ** -0.5)
        s = jnp.where(valid[:, :, None, None, :], s, -jnp.inf)
        p = jax.nn.softmax(s, axis=-1).astype(v_sel.dtype)
        o = jnp.einsum('bqgrk,bqkgd->bqgrd', p, v_sel)
        return o.reshape(B, Q_BLOCK, DSA_W)

    starts = jnp.arange(nb) * Q_BLOCK
    out = lax.map(one_block, (starts, blocks(q), blocks(iq), blocks(iw)))
    return out.swapaxes(0, 1).reshape(B, Lp, DSA_W)[:, :L]


def _layer(h, cos, sin, n_real, w_in, gla_wg2, gla_bg, gla_norm_g, idx_k_g, idx_k_b, w_out, ln_g, ln_b):
    B, L, _ = h.shape
    proj = jnp.einsum('bld,de->ble', h, w_in)
    g_q, g_k, g_v, g_lr, g_z, a_q, a_k, a_v, i_q, i_k, i_w, a_z = _split_cols(proj)

    gq = g_q.reshape(B, L, GLA_HEADS, GLA_DK) * (GLA_DK ** -0.5)
    gk = g_k.reshape(B, L, GLA_HEADS, GLA_DK)
    gv = g_v.reshape(B, L, GLA_HEADS, GLA_DV)
    glog = jax.nn.log_sigmoid(jnp.einsum('blr,rk->blk', g_lr, gla_wg2).astype(jnp.float32)
                              + gla_bg.astype(jnp.float32)) / GLA_TAU
    glog = glog.reshape(B, L, GLA_HEADS, GLA_DK)
    o_gla = _gla(gq, gk, gv, glog)
    o_gla = o_gla * lax.rsqrt(jnp.mean(jnp.square(o_gla), -1, keepdims=True) + LN_EPS) * gla_norm_g.astype(jnp.float32)
    o_gla = o_gla.reshape(B, L, GLA_W).astype(h.dtype) * jax.nn.silu(g_z)

    aq = _partial_rope(a_q.reshape(B, L, DSA_HEADS, DSA_HD), cos, sin)
    ak = _partial_rope(a_k.reshape(B, L, DSA_KV_HEADS, DSA_HD), cos, sin)
    av = a_v.reshape(B, L, DSA_KV_HEADS, DSA_HD)
    iq = _partial_rope(i_q.reshape(B, L, IDX_HEADS, IDX_HD), cos, sin)
    ik = _partial_rope(_layer_norm(i_k, idx_k_g, idx_k_b), cos, sin)
    o_dsa = _dsa(aq, ak, av, iq, ik, i_w, n_real) * jax.nn.silu(a_z)

    y = jnp.einsum('blm,md->bld', jnp.concatenate([o_gla, o_dsa], axis=-1), w_out)
    return _layer_norm(DEEPNORM_ALPHA * h + y, ln_g, ln_b)


def setup_inputs(seed: int = 0) -> dict:
    key = jax.random.key(seed)
    ks = jax.random.split(key, 13)
    f32 = jnp.float32
    nrm = lambda k, s: jax.random.normal(k, s, f32)
    return {
        "x": nrm(ks[0], (BATCH, SEQ, D_MODEL)),
        "meta_tokens": nrm(ks[1], (N_META, D_MODEL)),
        "ln_in_g": 1.0 + 0.02 * nrm(ks[2], (D_MODEL,)),
        "ln_in_b": 0.02 * nrm(ks[3], (D_MODEL,)),
        "w_in": nrm(ks[4], (DEPTH, D_MODEL, IN_W)) * D_MODEL ** -0.5,
        "gla_wg2": nrm(ks[5], (DEPTH, GLA_RANK, GLA_HEADS * GLA_DK)) * GLA_RANK ** -0.5,
        "gla_bg": 0.1 * nrm(ks[6], (DEPTH, GLA_HEADS * GLA_DK)),
        "gla_norm_g": 1.0 + 0.02 * nrm(ks[7], (DEPTH, GLA_DV)),
        "idx_k_g": 1.0 + 0.02 * nrm(ks[8], (DEPTH, IDX_HD)),
        "idx_k_b": 0.02 * nrm(ks[9], (DEPTH, IDX_HD)),
        "w_out": nrm(ks[10], (DEPTH, MIX_W, D_MODEL)) * (MIX_W ** -0.5) * DEEPNORM_BETA,
        "ln_g": 1.0 + 0.02 * nrm(ks[11], (DEPTH, D_MODEL)),
        "ln_b": 0.02 * nrm(ks[12], (DEPTH, D_MODEL)),
    }


def reference(x, meta_tokens, ln_in_g, ln_in_b, w_in, gla_wg2, gla_bg, gla_norm_g, idx_k_g, idx_k_b, w_out, ln_g, ln_b):
    B, S, _ = x.shape
    meta = jnp.broadcast_to(meta_tokens[None].astype(x.dtype), (B, N_META, D_MODEL))
    h = jnp.concatenate([meta, x], axis=1)
    h = _layer_norm(h, ln_in_g, ln_in_b)
    cos, sin = _rope_tables(N_META + S)
    for i in range(DEPTH):
        h = _layer(h, cos, sin, S, w_in[i], gla_wg2[i], gla_bg[i], gla_norm_g[i],
                   idx_k_g[i], idx_k_b[i], w_out[i], ln_g[i], ln_b[i])
    return h[:, N_META:]
```

```python
import functools

import numpy as np
import jax
import jax.numpy as jnp
from jax import lax
from jax.experimental import pallas as pl
from jax.experimental.pallas import tpu as pltpu

F32 = jnp.float32
BF16 = jnp.bfloat16
I32 = jnp.int32

D_MODEL = 1024
N_META = 16
ROPE_THETA = 500000.0
LN_EPS = 1e-5
GLA_HEADS = 4
GLA_DK = 64
GLA_DV = 128
GLA_RANK = 16
GLA_TAU = 16.0
GLA_CHUNK = 64
GLA_W = GLA_HEADS * GLA_DV
DSA_HEADS = 8
DSA_KV_HEADS = 2
DSA_GROUP = DSA_HEADS // DSA_KV_HEADS
DSA_HD = 64
DSA_W = DSA_HEADS * DSA_HD
IDX_HEADS = 8
IDX_HD = 64
TOPK_MAX = 256
ROPE_DIM = DSA_HD // 4
ROPE_HALF = ROPE_DIM // 2
MIX_W = GLA_W + DSA_W

LANES = 128
PAD = LANES - N_META
INT_MIN = -(2 ** 31)
VMEM_LIMIT = 48 * 1024 * 1024

O_GQ, O_GK, O_GV, O_GZ = 0, 256, 512, 1024
O_AQ, O_AK, O_AV, O_IQ, O_AZ, O_MISC = 1536, 2048, 2176, 2304, 2816, 3328
W_PACKED = 3456
MISC_GLR = IDX_HD
MISC_IW = IDX_HD + GLA_RANK
HEAD_PERM = tuple(m + DSA_GROUP * r for m in range(DSA_GROUP) for r in range(2))


def _dot(a, b):
    return jnp.dot(a, b, preferred_element_type=F32)


def _dot_nt(a, b):
    return lax.dot_general(a, b, (((1,), (1,)), ((), ())), preferred_element_type=F32)


def _dot_tn(a, b):
    return lax.dot_general(a, b, (((0,), (0,)), ((), ())), preferred_element_type=F32)


def _layer_norm_rows(u, g, b):
    mu = jnp.mean(u, axis=-1, keepdims=True)
    d = u - mu
    var = jnp.mean(d * d, axis=-1, keepdims=True)
    return d * lax.rsqrt(var + LN_EPS) * g + b


def _embed_kernel(x_ref, meta_ref, g_ref, b_ref, o_ref):
    j = pl.program_id(1)
    src = jnp.where(j == 0, meta_ref[...], x_ref[0, 0])
    o_ref[0] = _layer_norm_rows(src, g_ref[...], b_ref[...])


def _embed(x, meta_pad, g, b):
    B, S, D = x.shape
    nblk = S // LANES + 1
    x4 = x.reshape(B, S // LANES, LANES, D)
    return pl.pallas_call(
        _embed_kernel,
        grid=(B, nblk),
        in_specs=[
            pl.BlockSpec((1, 1, LANES, D), lambda b_, j: (b_, jnp.maximum(j - 1, 0), 0, 0)),
            pl.BlockSpec((LANES, D), lambda b_, j: (0, 0)),
            pl.BlockSpec((1, D), lambda b_, j: (0, 0)),
            pl.BlockSpec((1, D), lambda b_, j: (0, 0)),
        ],
        out_specs=pl.BlockSpec((1, LANES, D), lambda b_, j: (b_ * nblk + j, 0, 0)),
        out_shape=jax.ShapeDtypeStruct((B * nblk, LANES, D), F32),
        name="embed_ln",
    )(x4, meta_pad, g, b)


def _rope_slab(x, c, sa, sb):
    return x * c + pltpu.roll(x, ROPE_HALF, 1) * sa + pltpu.roll(x, LANES - ROPE_HALF, 1) * sb


def _proj_kernel(h_ref, w_ref, wg2h_ref, wg2l_ref, bg_ref, ikg_ref, ikb_ref,
                 cos_ref, sa_ref, sb_ref,
                 gq_ref, gk_ref, gv_ref, glog_ref, gz_ref, aq_ref, ak_ref, avt_ref,
                 iq_ref, ik_ref, iwt_ref, az_ref, *, tm):
    hb = h_ref[...].astype(BF16)

    def seg(o, w):
        return _dot(hb, w_ref[:, o:o + w])

    gq_ref[...] = seg(O_GQ, 256)
    gk_ref[...] = seg(O_GK, 256)
    gv_ref[...] = seg(O_GV, 512)
    gz_ref[...] = seg(O_GZ, 512)
    az_ref[...] = seg(O_AZ, 512)

    c, sa, sb = cos_ref[...], sa_ref[...], sb_ref[...]
    aq = seg(O_AQ, 512)
    iq = seg(O_IQ, 512)
    for m in range(4):
        sl = slice(m * LANES, (m + 1) * LANES)
        aq_ref[:, sl] = _rope_slab(aq[:, sl], c, sa, sb).astype(BF16)
        iq_ref[:, sl] = _rope_slab(iq[:, sl], c, sa, sb).astype(BF16)
    ak_ref[...] = _rope_slab(seg(O_AK, LANES), c, sa, sb).astype(BF16)

    av = seg(O_AV, LANES)
    misc = seg(O_MISC, LANES)
    misc_t_scale = IDX_HEADS ** -0.5
    for r in range(tm // LANES):
        rows = slice(r * LANES, (r + 1) * LANES)
        avt_ref[r] = av[rows, :].T.astype(BF16)
        iwt_ref[r] = misc[rows, :].T[MISC_IW:MISC_IW + IDX_HEADS, :] * misc_t_scale

    m_hi = misc.astype(BF16)
    m_lo = (misc - m_hi.astype(F32)).astype(BF16)
    xg = (_dot(m_hi, wg2h_ref[...]) + _dot(m_lo, wg2h_ref[...]) + _dot(m_hi, wg2l_ref[...])
          + bg_ref[...])
    glog_ref[...] = (jnp.minimum(xg, 0.0) - jnp.log1p(jnp.exp(-jnp.abs(xg)))) * (1.0 / GLA_TAU)

    lane = lax.broadcasted_iota(I32, (1, LANES), 1)
    is_key = lane < IDX_HD
    mu = jnp.sum(jnp.where(is_key, misc, 0.0), axis=-1, keepdims=True) * (1.0 / IDX_HD)
    d = jnp.where(is_key, misc - mu, 0.0)
    var = jnp.sum(d * d, axis=-1, keepdims=True) * (1.0 / IDX_HD)
    ikn = d * lax.rsqrt(var + LN_EPS) * ikg_ref[...] + ikb_ref[...]
    ikr = _rope_slab(ikn, c, sa, sb)
    ik_ref[...] = jnp.where(is_key, ikr, pltpu.roll(ikr, IDX_HD, 1)).astype(BF16)


def _proj(h2, w, wg2h, wg2l, bg, ikg, ikb, cos_t, sa_t, sb_t, tm):
    R = h2.shape[0]
    nt = R // tm
    nsub = tm // LANES
    row = lambda i: (i, 0)
    const = lambda i: (0, 0)
    kern = functools.partial(_proj_kernel, tm=tm)
    out_shape = (
        jax.ShapeDtypeStruct((R, 256), F32),
        jax.ShapeDtypeStruct((R, 256), F32),
        jax.ShapeDtypeStruct((R, 512), F32),
        jax.ShapeDtypeStruct((R, 256), F32),
        jax.ShapeDtypeStruct((R, 512), F32),
        jax.ShapeDtypeStruct((R, 512), BF16),
        jax.ShapeDtypeStruct((R, LANES), BF16),
        jax.ShapeDtypeStruct((R // LANES, LANES, LANES), BF16),
        jax.ShapeDtypeStruct((R, 512), BF16),
        jax.ShapeDtypeStruct((R, LANES), BF16),
        jax.ShapeDtypeStruct((R // LANES, IDX_HEADS, LANES), F32),
        jax.ShapeDtypeStruct((R, 512), F32),
    )
    out_specs = (
        pl.BlockSpec((tm, 256), row),
        pl.BlockSpec((tm, 256), row),
        pl.BlockSpec((tm, 512), row),
        pl.BlockSpec((tm, 256), row),
        pl.BlockSpec((tm, 512), row),
        pl.BlockSpec((tm, 512), row),
        pl.BlockSpec((tm, LANES), row),
        pl.BlockSpec((nsub, LANES, LANES), lambda i: (i, 0, 0)),
        pl.BlockSpec((tm, 512), row),
        pl.BlockSpec((tm, LANES), row),
        pl.BlockSpec((nsub, IDX_HEADS, LANES), lambda i: (i, 0, 0)),
        pl.BlockSpec((tm, 512), row),
    )
    in_specs = [
        pl.BlockSpec((tm, D_MODEL), row),
        pl.BlockSpec((D_MODEL, W_PACKED), const),
        pl.BlockSpec((LANES, 256), const),
        pl.BlockSpec((LANES, 256), const),
        pl.BlockSpec((1, 256), const),
        pl.BlockSpec((1, LANES), const),
        pl.BlockSpec((1, LANES), const),
        pl.BlockSpec((tm, LANES), row),
        pl.BlockSpec((tm, LANES), row),
        pl.BlockSpec((tm, LANES), row),
    ]
    return pl.pallas_call(
        kern, grid=(nt,), in_specs=in_specs, out_specs=out_specs, out_shape=out_shape,
        compiler_params=pltpu.CompilerParams(vmem_limit_bytes=VMEM_LIMIT),
        name="in_proj",
    )(h2, w, wg2h, wg2l, bg, ikg, ikb, cos_t, sa_t, sb_t)


def _gla_kernel(q_ref, k_ref, g_ref, v_ref, z_ref, ng_ref, o_ref, st_ref, *, n_chunks):
    C = GLA_CHUNK
    st_ref[...] = jnp.zeros_like(st_ref)
    lane = lax.broadcasted_iota(I32, (1, LANES), 1)
    head_lanes = (lane < GLA_DK, lane >= GLA_DK)
    rr = lax.broadcasted_iota(I32, (C, C), 0)
    cc = lax.broadcasted_iota(I32, (C, C), 1)
    causal = rr >= cc
    tril = causal.astype(BF16)
    row = lax.broadcasted_iota(I32, (C, 1), 0)
    ng = ng_ref[...]

    def body(ci, carry):
        r0 = pl.multiple_of(ci * C, C)
        rows = pl.ds(r0, C)
        valid = (r0 + row) >= PAD
        q = q_ref[0, rows, :]
        k = jnp.where(valid, k_ref[0, rows, :], 0.0)
        g = g_ref[0, rows, :]
        g_hi = g.astype(BF16)
        g_lo = (g - g_hi.astype(F32)).astype(BF16)
        b = _dot(tril, g_hi) + _dot(tril, g_lo)
        b_mid = b[C // 2 - 1:C // 2, :]
        b_last = b[C - 1:C, :]
        q_in = (q * jnp.exp(b - b_mid)).astype(BF16)
        k_in = (k * jnp.exp(b_mid - b)).astype(BF16)
        q_st = (q * jnp.exp(b)).astype(BF16)
        k_st = (k * jnp.exp(b_last - b)).astype(BF16)
        decay = jnp.exp(b_last)
        zero = jnp.zeros_like(q_in)
        for h in range(2):
            hl = head_lanes[h]
            cols = slice(h * GLA_DV, (h + 1) * GLA_DV)
            a = _dot_nt(jnp.where(hl, q_in, zero), k_in)
            a = jnp.where(causal, a, 0.0).astype(BF16)
            v = jnp.where(valid, v_ref[0, rows, cols], 0.0).astype(BF16)
            st = st_ref[h]
            o = _dot(a, v) + _dot_nt(jnp.where(hl, q_st, zero), st.astype(BF16))
            st_ref[h] = st * decay + _dot_tn(v, jnp.where(hl, k_st, zero))
            on = o * lax.rsqrt(jnp.mean(o * o, axis=-1, keepdims=True) + LN_EPS) * ng
            z = z_ref[0, rows, cols]
            o_ref[0, rows, cols] = (on * (z * jax.nn.sigmoid(z))).astype(BF16)
        return carry

    lax.fori_loop(0, n_chunks, body, 0)


def _gla(gq, gk, glog, gv, gz, ng, B, LP):
    n_chunks = LP // GLA_CHUNK
    kern = functools.partial(_gla_kernel, n_chunks=n_chunks)
    pair = lambda b_, p: (b_, 0, p)
    return pl.pallas_call(
        kern,
        grid=(B, GLA_HEADS // 2),
        in_specs=[
            pl.BlockSpec((1, LP, LANES), pair),
            pl.BlockSpec((1, LP, LANES), pair),
            pl.BlockSpec((1, LP, LANES), pair),
            pl.BlockSpec((1, LP, 2 * GLA_DV), pair),
            pl.BlockSpec((1, LP, 2 * GLA_DV), pair),
            pl.BlockSpec((1, GLA_DV), lambda b_, p: (0, 0)),
        ],
        out_specs=pl.BlockSpec((1, LP, 2 * GLA_DV), pair),
        out_shape=jax.ShapeDtypeStruct((B, LP, GLA_W), BF16),
        scratch_shapes=[pltpu.VMEM((2, GLA_DV, LANES), F32)],
        compiler_params=pltpu.CompilerParams(vmem_limit_bytes=VMEM_LIMIT),
        name="gla",
    )(gq.reshape(B, LP, 256), gk.reshape(B, LP, 256), glog.reshape(B, LP, 256),
      gv.reshape(B, LP, 512), gz.reshape(B, LP, 512), ng)


def _dsa_kernel(ik_ref, ak_ref, avt_ref, aq_ref, iq_ref, iwt_ref, az_ref, o_ref,
                sc_ref, qz_ref, iqz_ref, m_ref, l_ref, acc_ref, *, topk):
    j = pl.program_id(1)
    nkb = j + 1
    lane = lax.broadcasted_iota(I32, (1, LANES), 1)
    low = lane < DSA_HD
    zero_b = jnp.zeros((LANES, LANES), BF16)
    for m in range(4):
        sl = slice(m * LANES, (m + 1) * LANES)
        qs = aq_ref[0, :, sl]
        qz_ref[2 * m] = jnp.where(low, qs, zero_b)
        qz_ref[2 * m + 1] = jnp.where(low, zero_b, qs)
        iqs = iq_ref[0, :, sl]
        iqz_ref[2 * m] = jnp.where(low, iqs, zero_b)
        iqz_ref[2 * m + 1] = jnp.where(low, zero_b, iqs)
    w = iwt_ref[0]
    t_pos = j * LANES + lane
    s_loc = lax.broadcasted_iota(I32, (LANES, 1), 0)

    def score_block(kb, carry):
        ikb = ik_ref[0, kb]
        acc = jnp.zeros((LANES, LANES), F32)
        for jh in range(IDX_HEADS):
            d = _dot_nt(ikb, iqz_ref[jh])
            acc = acc + jnp.maximum(d, 0.0) * w[jh:jh + 1, :]
        s_pos = kb * LANES + s_loc
        valid = (s_pos <= t_pos) & ((s_pos >= PAD) | (t_pos < PAD))
        sc_ref[kb] = jnp.where(valid, acc, -jnp.inf)
        return carry

    lax.fori_loop(0, nkb, score_block, 0)

    def key_to_float(key):
        bits = jnp.where(key < 0, key ^ jnp.int32(0x7FFFFFFF), key)
        return lax.bitcast_convert_type(bits, F32)

    def enough(cand_key):
        cand = key_to_float(cand_key)

        def cnt_block(kb, cnt):
            return cnt + (sc_ref[kb] >= cand).astype(I32)

        cnt = lax.fori_loop(0, nkb, cnt_block, jnp.zeros((LANES, LANES), I32))
        return jnp.sum(cnt, axis=0, keepdims=True) >= topk

    base = jnp.where(enough(jnp.zeros((1, LANES), I32)), jnp.int32(0), jnp.int32(INT_MIN))

    def bit_step(i, base):
        cand = base | jnp.left_shift(jnp.int32(1), 30 - i)
        return jnp.where(enough(cand), cand, base)

    base = lax.fori_loop(0, 31, bit_step, base)
    thr = jnp.where(base == INT_MIN, jnp.float32(-3.0e38), key_to_float(base))

    m_ref[...] = jnp.full_like(m_ref, -1e30)
    l_ref[...] = jnp.zeros_like(l_ref)
    acc_ref[...] = jnp.zeros_like(acc_ref)

    def attn_block(kb, carry):
        sel = sc_ref[kb] >= thr
        kblk = ak_ref[0, kb]
        vt = avt_ref[0, kb]
        for i in range(DSA_HEADS):
            g = i % 2
            s = jnp.where(sel, _dot_nt(kblk, qz_ref[i]), -jnp.inf)
            m_old = m_ref[i:i + 1, :]
            m_new = jnp.maximum(m_old, jnp.max(s, axis=0, keepdims=True))
            alpha = jnp.exp(m_old - m_new)
            p = jnp.exp(s - m_new)
            l_ref[i:i + 1, :] = alpha * l_ref[i:i + 1, :] + jnp.sum(p, axis=0, keepdims=True)
            acc_ref[i] = alpha * acc_ref[i] + _dot(vt[g * DSA_HD:(g + 1) * DSA_HD, :], p.astype(BF16))
            m_ref[i:i + 1, :] = m_new
        return carry

    lax.fori_loop(0, nkb, attn_block, 0)

    ot = jnp.concatenate(
        [acc_ref[i] / l_ref[i:i + 1, :] for i in range(DSA_HEADS)], axis=0)
    z = az_ref[0]
    o_ref[0] = (ot.T * (z * jax.nn.sigmoid(z))).astype(BF16)


def _dsa(ik2, ak, avt, aq, iq, iwt, az, B, nblk, topk):
    kern = functools.partial(_dsa_kernel, topk=topk)
    whole = lambda b_, j: (b_, 0, 0, 0)
    qblk = lambda b_, j: (b_ * nblk + j, 0, 0)
    return pl.pallas_call(
        kern,
        grid=(B, nblk),
        in_specs=[
            pl.BlockSpec((1, nblk, LANES, LANES), whole),
            pl.BlockSpec((1, nblk, LANES, LANES), whole),
            pl.BlockSpec((1, nblk, LANES, LANES), whole),
            pl.BlockSpec((1, LANES, DSA_W), qblk),
            pl.BlockSpec((1, LANES, IDX_HEADS * IDX_HD), qblk),
            pl.BlockSpec((1, IDX_HEADS, LANES), qblk),
            pl.BlockSpec((1, LANES, DSA_W), qblk),
        ],
        out_specs=pl.BlockSpec((1, LANES, DSA_W), qblk),
        out_shape=jax.ShapeDtypeStruct((B * nblk, LANES, DSA_W), BF16),
        scratch_shapes=[
            pltpu.VMEM((nblk, LANES, LANES), F32),
            pltpu.VMEM((DSA_HEADS, LANES, LANES), BF16),
            pltpu.VMEM((IDX_HEADS, LANES, LANES), BF16),
            pltpu.VMEM((DSA_HEADS, LANES), F32),
            pltpu.VMEM((DSA_HEADS, LANES), F32),
            pltpu.VMEM((DSA_HEADS, DSA_HD, LANES), F32),
        ],
        compiler_params=pltpu.CompilerParams(vmem_limit_bytes=VMEM_LIMIT),
        name="dsa",
    )(ik2.reshape(B, nblk, LANES, LANES), ak.reshape(B, nblk, LANES, LANES),
      avt.reshape(B, nblk, LANES, LANES), aq.reshape(B * nblk, LANES, DSA_W),
      iq.reshape(B * nblk, LANES, IDX_HEADS * IDX_HD), iwt, az.reshape(B * nblk, LANES, DSA_W))


def _out_kernel(mg_ref, md_ref, h_ref, w_ref, g_ref, b_ref, o_ref, *, alpha):
    y = _dot(mg_ref[0], w_ref[:GLA_W, :]) + _dot(md_ref[0], w_ref[GLA_W:, :])
    o_ref[0] = _layer_norm_rows(alpha * h_ref[0] + y, g_ref[...], b_ref[...])


def _out(mg, md, h3, w, g, b, B, nblk, alpha, drop_first):
    nout = nblk - 1 if drop_first else nblk
    off = 1 if drop_first else 0
    src = lambda b_, j: (b_ * nblk + j + off, 0, 0)
    const = lambda b_, j: (0, 0)
    kern = functools.partial(_out_kernel, alpha=alpha)
    return pl.pallas_call(
        kern,
        grid=(B, nout),
        in_specs=[
            pl.BlockSpec((1, LANES, GLA_W), src),
            pl.BlockSpec((1, LANES, DSA_W), src),
            pl.BlockSpec((1, LANES, D_MODEL), src),
            pl.BlockSpec((MIX_W, D_MODEL), const),
            pl.BlockSpec((1, D_MODEL), const),
            pl.BlockSpec((1, D_MODEL), const),
        ],
        out_specs=pl.BlockSpec((1, LANES, D_MODEL), lambda b_, j: (b_ * nout + j, 0, 0)),
        out_shape=jax.ShapeDtypeStruct((B * nout, LANES, D_MODEL), F32),
        name="out_proj_ln",
    )(mg, md, h3, w, g, b)


def _pack_w_in(w):
    splits = (256, 256, 512, GLA_RANK, 512, 512, 128, 128, 512, IDX_HD, IDX_HEADS, 512)
    offs = np.cumsum((0,) + splits)
    gq, gk, gv, glr, gz, aq, ak, av, iq, ik, iw, az = [w[:, offs[i]:offs[i + 1]] for i in range(12)]

    def perm_heads(a):
        return a.reshape(D_MODEL, DSA_HEADS, DSA_HD)[:, HEAD_PERM, :].reshape(D_MODEL, DSA_W)

    pad = jnp.zeros((D_MODEL, LANES - IDX_HD - GLA_RANK - IDX_HEADS), w.dtype)
    packed = jnp.concatenate(
        [gq * (GLA_DK ** -0.5), gk, gv, gz, perm_heads(aq) * (DSA_HD ** -0.5), ak, av,
         iq * (IDX_HD ** -0.5), perm_heads(az), ik, glr, iw, pad], axis=1)
    return packed.astype(BF16)


def _pack_w_out(w):
    dsa = w[GLA_W:].reshape(DSA_HEADS, DSA_HD, D_MODEL)[HEAD_PERM, :, :].reshape(DSA_W, D_MODEL)
    return jnp.concatenate([w[:GLA_W], dsa], axis=0).astype(BF16)


def _rope_lane_tables(B, LP):
    inv = ROPE_THETA ** (-jnp.arange(0, ROPE_DIM, 2, dtype=F32) / ROPE_DIM)
    pos = (jnp.arange(LP, dtype=F32) - PAD)[:, None]
    ang = pos * inv[None, :]
    cos, sin = jnp.cos(ang), jnp.sin(ang)
    ones = jnp.ones((LP, DSA_HD - ROPE_DIM), F32)
    zeros = jnp.zeros((LP, DSA_HD - ROPE_DIM), F32)
    zh = jnp.zeros((LP, ROPE_HALF), F32)
    c = jnp.concatenate([cos, cos, ones], axis=1)
    sa = jnp.concatenate([zh, sin, zeros], axis=1)
    sb = jnp.concatenate([-sin, zh, zeros], axis=1)
    tile = lambda t: jnp.tile(jnp.concatenate([t, t], axis=1), (B, 1))
    return tile(c), tile(sa), tile(sb)


def kernel(x, meta_tokens, ln_in_g, ln_in_b, w_in, gla_wg2, gla_bg, gla_norm_g, idx_k_g, idx_k_b,
           w_out, ln_g, ln_b):
    B, S, D = x.shape
    depth = w_in.shape[0]
    nblk = S // LANES + 1
    LP = nblk * LANES
    R = B * LP
    topk = min(TOPK_MAX, S // 4)
    alpha = (2.0 * depth) ** 0.25
    tm = 256

    meta_pad = jnp.concatenate([jnp.zeros((PAD, D), x.dtype), meta_tokens.astype(x.dtype)], axis=0)
    h = _embed(x, meta_pad, ln_in_g.reshape(1, D), ln_in_b.reshape(1, D))
    cos_t, sa_t, sb_t = _rope_lane_tables(B, LP)

    for i in range(depth):
        w = _pack_w_in(w_in[i])
        wg2 = jnp.zeros((LANES, 256), F32).at[MISC_GLR:MISC_GLR + GLA_RANK].set(gla_wg2[i])
        wg2h = wg2.astype(BF16)
        wg2l = (wg2 - wg2h.astype(F32)).astype(BF16)
        ikg = jnp.zeros((1, LANES), F32).at[0, :IDX_HD].set(idx_k_g[i])
        ikb = jnp.zeros((1, LANES), F32).at[0, :IDX_HD].set(idx_k_b[i])
        (gq, gk, gv, glog, gz, aq, ak, avt, iq, ik2, iwt, az) = _proj(
            h.reshape(R, D), w, wg2h, wg2l, gla_bg[i].reshape(1, 256), ikg, ikb,
            cos_t, sa_t, sb_t, tm)
        mix_gla = _gla(gq, gk, glog, gv, gz, gla_norm_g[i].reshape(1, GLA_DV), B, LP)
        mix_dsa = _dsa(ik2, ak, avt, aq, iq, iwt, az, B, nblk, topk)
        h = _out(mix_gla.reshape(B * nblk, LANES, GLA_W), mix_dsa, h,
                 _pack_w_out(w_out[i]), ln_g[i].reshape(1, D), ln_b[i].reshape(1, D),
                 B, nblk, alpha, drop_first=(i == depth - 1))
    return h.reshape(B, S, D)
```

```python
import functools

import numpy as np
import jax
import jax.numpy as jnp
from jax import lax
from jax.experimental import pallas as pl
from jax.experimental.pallas import tpu as pltpu

F32 = jnp.float32
BF16 = jnp.bfloat16
I32 = jnp.int32

D_MODEL = 1024
N_META = 16
ROPE_THETA = 500000.0
LN_EPS = 1e-5
GLA_HEADS = 4
GLA_DK = 64
GLA_DV = 128
GLA_RANK = 16
GLA_TAU = 16.0
GLA_CHUNK = 64
GLA_W = GLA_HEADS * GLA_DV
DSA_HEADS = 8
DSA_KV_HEADS = 2
DSA_GROUP = DSA_HEADS // DSA_KV_HEADS
DSA_HD = 64
DSA_W = DSA_HEADS * DSA_HD
IDX_HEADS = 8
IDX_HD = 64
TOPK_MAX = 256
ROPE_DIM = DSA_HD // 4
ROPE_HALF = ROPE_DIM // 2
MIX_W = GLA_W + DSA_W

LANES = 128
SUBLANES = 8
ONES_ROWS = 16
PAD = LANES - N_META
INT_MIN = -(2 ** 31)
VMEM_LIMIT = 48 * 1024 * 1024

O_GQ, O_GK, O_GV, O_GZ = 0, 256, 512, 1024
O_AQ, O_AK, O_AV, O_IQ, O_AZ, O_MISC = 1536, 2048, 2176, 2304, 2816, 3328
W_PACKED = 3456
MISC_GLR = IDX_HD
MISC_IW = IDX_HD + GLA_RANK
HEAD_PERM = tuple(m + DSA_GROUP * r for m in range(DSA_GROUP) for r in range(2))


def _dot(a, b):
    return jnp.dot(a, b, preferred_element_type=F32)


def _dot_nt(a, b):
    return lax.dot_general(a, b, (((1,), (1,)), ((), ())), preferred_element_type=F32)


def _dot_tn(a, b):
    return lax.dot_general(a, b, (((0,), (0,)), ((), ())), preferred_element_type=F32)


def _layer_norm_rows(u, g, b):
    mu = jnp.mean(u, axis=-1, keepdims=True)
    d = u - mu
    var = jnp.mean(d * d, axis=-1, keepdims=True)
    return d * lax.rsqrt(var + LN_EPS) * g + b


def _embed_kernel(x_ref, meta_ref, g_ref, b_ref, o_ref):
    j = pl.program_id(1)
    src = jnp.where(j == 0, meta_ref[...], x_ref[0, 0])
    o_ref[0] = _layer_norm_rows(src, g_ref[...], b_ref[...])


def _embed(x, meta_pad, g, b):
    B, S, D = x.shape
    nblk = S // LANES + 1
    x4 = x.reshape(B, S // LANES, LANES, D)
    return pl.pallas_call(
        _embed_kernel,
        grid=(B, nblk),
        in_specs=[
            pl.BlockSpec((1, 1, LANES, D), lambda b_, j: (b_, jnp.maximum(j - 1, 0), 0, 0)),
            pl.BlockSpec((LANES, D), lambda b_, j: (0, 0)),
            pl.BlockSpec((1, D), lambda b_, j: (0, 0)),
            pl.BlockSpec((1, D), lambda b_, j: (0, 0)),
        ],
        out_specs=pl.BlockSpec((1, LANES, D), lambda b_, j: (b_ * nblk + j, 0, 0)),
        out_shape=jax.ShapeDtypeStruct((B * nblk, LANES, D), F32),
        name="embed_ln",
    )(x4, meta_pad, g, b)


def _rope_slab(x, c, sa, sb):
    return x * c + pltpu.roll(x, ROPE_HALF, 1) * sa + pltpu.roll(x, LANES - ROPE_HALF, 1) * sb


def _proj_kernel(h_ref, w_ref, wg2h_ref, wg2l_ref, bg_ref, ikg_ref, ikb_ref,
                 cos_ref, sa_ref, sb_ref,
                 gq_ref, gk_ref, gv_ref, glog_ref, gz_ref, aq_ref, ak_ref, avt_ref,
                 iq_ref, ik_ref, iwt_ref, az_ref, *, tm):
    hb = h_ref[...].astype(BF16)

    def seg(o, w):
        return _dot(hb, w_ref[:, o:o + w])

    gq_ref[...] = seg(O_GQ, 256)
    gk_ref[...] = seg(O_GK, 256)
    gv_ref[...] = seg(O_GV, 512)
    gz_ref[...] = seg(O_GZ, 512)
    az_ref[...] = seg(O_AZ, 512)

    c, sa, sb = cos_ref[...], sa_ref[...], sb_ref[...]
    aq = seg(O_AQ, 512)
    iq = seg(O_IQ, 512)
    for m in range(4):
        sl = slice(m * LANES, (m + 1) * LANES)
        aq_ref[:, sl] = _rope_slab(aq[:, sl], c, sa, sb).astype(BF16)
        iq_ref[:, sl] = _rope_slab(iq[:, sl], c, sa, sb).astype(BF16)
    ak_ref[...] = _rope_slab(seg(O_AK, LANES), c, sa, sb).astype(BF16)

    av = seg(O_AV, LANES)
    misc = seg(O_MISC, LANES)
    misc_t_scale = IDX_HEADS ** -0.5
    for r in range(tm // LANES):
        rows = slice(r * LANES, (r + 1) * LANES)
        avt_ref[r] = av[rows, :].T.astype(BF16)
        iwt_ref[r] = misc[rows, :].T[MISC_IW:MISC_IW + IDX_HEADS, :] * misc_t_scale

    m_hi = misc.astype(BF16)
    m_lo = (misc - m_hi.astype(F32)).astype(BF16)
    xg = (_dot(m_hi, wg2h_ref[...]) + _dot(m_lo, wg2h_ref[...]) + _dot(m_hi, wg2l_ref[...])
          + bg_ref[...])
    glog_ref[...] = (jnp.minimum(xg, 0.0) - jnp.log1p(jnp.exp(-jnp.abs(xg)))) * (1.0 / GLA_TAU)

    lane = lax.broadcasted_iota(I32, (1, LANES), 1)
    is_key = lane < IDX_HD
    mu = jnp.sum(jnp.where(is_key, misc, 0.0), axis=-1, keepdims=True) * (1.0 / IDX_HD)
    d = jnp.where(is_key, misc - mu, 0.0)
    var = jnp.sum(d * d, axis=-1, keepdims=True) * (1.0 / IDX_HD)
    ikn = d * lax.rsqrt(var + LN_EPS) * ikg_ref[...] + ikb_ref[...]
    ikr = _rope_slab(ikn, c, sa, sb)
    ik_ref[...] = jnp.where(is_key, ikr, pltpu.roll(ikr, IDX_HD, 1)).astype(BF16)


def _proj(h2, w, wg2h, wg2l, bg, ikg, ikb, cos_t, sa_t, sb_t, tm):
    R = h2.shape[0]
    nt = R // tm
    nsub = tm // LANES
    row = lambda i: (i, 0)
    const = lambda i: (0, 0)
    kern = functools.partial(_proj_kernel, tm=tm)
    out_shape = (
        jax.ShapeDtypeStruct((R, 256), F32),
        jax.ShapeDtypeStruct((R, 256), F32),
        jax.ShapeDtypeStruct((R, 512), F32),
        jax.ShapeDtypeStruct((R, 256), F32),
        jax.ShapeDtypeStruct((R, 512), F32),
        jax.ShapeDtypeStruct((R, 512), BF16),
        jax.ShapeDtypeStruct((R, LANES), BF16),
        jax.ShapeDtypeStruct((R // LANES, LANES, LANES), BF16),
        jax.ShapeDtypeStruct((R, 512), BF16),
        jax.ShapeDtypeStruct((R, LANES), BF16),
        jax.ShapeDtypeStruct((R // LANES, IDX_HEADS, LANES), F32),
        jax.ShapeDtypeStruct((R, 512), F32),
    )
    out_specs = (
        pl.BlockSpec((tm, 256), row),
        pl.BlockSpec((tm, 256), row),
        pl.BlockSpec((tm, 512), row),
        pl.BlockSpec((tm, 256), row),
        pl.BlockSpec((tm, 512), row),
        pl.BlockSpec((tm, 512), row),
        pl.BlockSpec((tm, LANES), row),
        pl.BlockSpec((nsub, LANES, LANES), lambda i: (i, 0, 0)),
        pl.BlockSpec((tm, 512), row),
        pl.BlockSpec((tm, LANES), row),
        pl.BlockSpec((nsub, IDX_HEADS, LANES), lambda i: (i, 0, 0)),
        pl.BlockSpec((tm, 512), row),
    )
    in_specs = [
        pl.BlockSpec((tm, D_MODEL), row),
        pl.BlockSpec((D_MODEL, W_PACKED), const),
        pl.BlockSpec((LANES, 256), const),
        pl.BlockSpec((LANES, 256), const),
        pl.BlockSpec((1, 256), const),
        pl.BlockSpec((1, LANES), const),
        pl.BlockSpec((1, LANES), const),
        pl.BlockSpec((tm, LANES), row),
        pl.BlockSpec((tm, LANES), row),
        pl.BlockSpec((tm, LANES), row),
    ]
    return pl.pallas_call(
        kern, grid=(nt,), in_specs=in_specs, out_specs=out_specs, out_shape=out_shape,
        compiler_params=pltpu.CompilerParams(vmem_limit_bytes=VMEM_LIMIT),
        name="in_proj",
    )(h2, w, wg2h, wg2l, bg, ikg, ikb, cos_t, sa_t, sb_t)


def _gla_kernel(q_ref, k_ref, g_ref, v_ref, z_ref, ng_ref, o_ref, st_ref, *, n_chunks):
    C = GLA_CHUNK
    st_ref[...] = jnp.zeros_like(st_ref)
    lane = lax.broadcasted_iota(I32, (1, LANES), 1)
    head_lanes = (lane < GLA_DK, lane >= GLA_DK)
    rr = lax.broadcasted_iota(I32, (C, C), 0)
    cc = lax.broadcasted_iota(I32, (C, C), 1)
    causal = rr >= cc
    tril = causal.astype(BF16)
    row = lax.broadcasted_iota(I32, (C, 1), 0)
    ng = ng_ref[...]

    def body(ci, carry):
        r0 = pl.multiple_of(ci * C, C)
        rows = pl.ds(r0, C)
        valid = (r0 + row) >= PAD
        q = q_ref[0, rows, :]
        k = jnp.where(valid, k_ref[0, rows, :], 0.0)
        g = g_ref[0, rows, :]
        g_hi = g.astype(BF16)
        g_lo = (g - g_hi.astype(F32)).astype(BF16)
        b = _dot(tril, g_hi) + _dot(tril, g_lo)
        b_mid = b[C // 2 - 1:C // 2, :]
        b_last = b[C - 1:C, :]
        q_in = (q * jnp.exp(b - b_mid)).astype(BF16)
        k_in = (k * jnp.exp(b_mid - b)).astype(BF16)
        q_st = (q * jnp.exp(b)).astype(BF16)
        k_st = (k * jnp.exp(b_last - b)).astype(BF16)
        decay = jnp.exp(b_last)
        zero = jnp.zeros_like(q_in)
        for h in range(2):
            hl = head_lanes[h]
            cols = slice(h * GLA_DV, (h + 1) * GLA_DV)
            a = _dot_nt(jnp.where(hl, q_in, zero), k_in)
            a = jnp.where(causal, a, 0.0).astype(BF16)
            v = jnp.where(valid, v_ref[0, rows, cols], 0.0).astype(BF16)
            st = st_ref[h]
            o = _dot(a, v) + _dot_nt(jnp.where(hl, q_st, zero), st.astype(BF16))
            st_ref[h] = st * decay + _dot_tn(v, jnp.where(hl, k_st, zero))
            on = o * lax.rsqrt(jnp.mean(o * o, axis=-1, keepdims=True) + LN_EPS) * ng
            z = z_ref[0, rows, cols]
            o_ref[0, rows, cols] = (on * (z * jax.nn.sigmoid(z))).astype(BF16)
        return carry

    lax.fori_loop(0, n_chunks, body, 0)


def _gla(gq, gk, glog, gv, gz, ng, B, LP):
    n_chunks = LP // GLA_CHUNK
    kern = functools.partial(_gla_kernel, n_chunks=n_chunks)
    pair = lambda b_, p: (b_, 0, p)
    return pl.pallas_call(
        kern,
        grid=(B, GLA_HEADS // 2),
        in_specs=[
            pl.BlockSpec((1, LP, LANES), pair),
            pl.BlockSpec((1, LP, LANES), pair),
            pl.BlockSpec((1, LP, LANES), pair),
            pl.BlockSpec((1, LP, 2 * GLA_DV), pair),
            pl.BlockSpec((1, LP, 2 * GLA_DV), pair),
            pl.BlockSpec((1, GLA_DV), lambda b_, p: (0, 0)),
        ],
        out_specs=pl.BlockSpec((1, LP, 2 * GLA_DV), pair),
        out_shape=jax.ShapeDtypeStruct((B, LP, GLA_W), BF16),
        scratch_shapes=[pltpu.VMEM((2, GLA_DV, LANES), F32)],
        compiler_params=pltpu.CompilerParams(vmem_limit_bytes=VMEM_LIMIT),
        name="gla",
    )(gq.reshape(B, LP, 256), gk.reshape(B, LP, 256), glog.reshape(B, LP, 256),
      gv.reshape(B, LP, 512), gz.reshape(B, LP, 512), ng)


def _dsa_kernel(ik_ref, ak_ref, avt_ref, aq_ref, iq_ref, iwt_ref, az_ref, o_ref,
                sc_ref, qz_ref, iqz_ref, w_ref, m_ref, acc_ref, *, topk, nblk):
    j = pl.program_id(1)
    npair = (j + 2) // 2
    KP = 2 * LANES

    def pair_rows(ref, i, axis):
        kb1 = jnp.minimum(2 * i + 1, nblk - 1)
        return jnp.concatenate([ref[0, 2 * i], ref[0, kb1]], axis=axis)

    lane = lax.broadcasted_iota(I32, (1, LANES), 1)
    low = lane < DSA_HD
    zero_b = jnp.zeros((LANES, LANES), BF16)
    for m in range(4):
        sl = slice(m * LANES, (m + 1) * LANES)
        qs = aq_ref[0, :, sl]
        qz_ref[m * LANES:(m + 1) * LANES, :] = jnp.where(low, qs, zero_b)
        qz_ref[(m + 4) * LANES:(m + 5) * LANES, :] = jnp.where(low, zero_b, qs)
        iqs = iq_ref[0, :, sl]
        iqz_ref[(2 * m) * LANES:(2 * m + 1) * LANES, :] = jnp.where(low, iqs, zero_b)
        iqz_ref[(2 * m + 1) * LANES:(2 * m + 2) * LANES, :] = jnp.where(low, zero_b, iqs)
    for h in range(IDX_HEADS):
        w_ref[:, h * LANES:(h + 1) * LANES] = iwt_ref[0, h:h + 1, :]
    t_pos = j * LANES + lane
    s_loc = lax.broadcasted_iota(I32, (KP, 1), 0)

    def score_block(i, carry):
        d = jnp.maximum(_dot_nt(pair_rows(ik_ref, i, 0), iqz_ref[...]), 0.0) * w_ref[...]
        acc = d[:, :LANES]
        for h in range(1, IDX_HEADS):
            acc = acc + d[:, h * LANES:(h + 1) * LANES]
        s_pos = i * KP + s_loc
        valid = (s_pos <= t_pos) & ((s_pos >= PAD) | (t_pos < PAD))
        sc_ref[pl.ds(pl.multiple_of(i * KP, KP), KP), :] = jnp.where(valid, acc, -jnp.inf)
        return carry

    lax.fori_loop(0, npair, score_block, 0)

    def key_to_float(key):
        bits = jnp.where(key < 0, key ^ jnp.int32(0x7FFFFFFF), key)
        return lax.bitcast_convert_type(bits, F32)

    def enough(cand_key):
        cand = key_to_float(cand_key)

        def cnt_block(i, cnt):
            ge = (sc_ref[pl.ds(pl.multiple_of(i * KP, KP), KP), :] >= cand).astype(I32)
            parts = [ge[r:r + SUBLANES, :] for r in range(0, KP, SUBLANES)]
            while len(parts) > 1:
                parts = [parts[k] + parts[k + 1] for k in range(0, len(parts), 2)]
            return cnt + parts[0]

        cnt = lax.fori_loop(0, npair, cnt_block, jnp.zeros((SUBLANES, LANES), I32))
        return jnp.sum(cnt, axis=0, keepdims=True) >= topk

    base = jnp.where(enough(jnp.zeros((1, LANES), I32)), jnp.int32(0), jnp.int32(INT_MIN))

    def bit_step(i, base):
        cand = base | jnp.left_shift(jnp.int32(1), 30 - i)
        return jnp.where(enough(cand), cand, base)

    base = lax.fori_loop(0, 31, bit_step, base)
    thr = jnp.where(base == INT_MIN, jnp.float32(-3.0e38), key_to_float(base))

    m_ref[...] = jnp.full_like(m_ref, -1e30)
    acc_ref[...] = jnp.zeros_like(acc_ref)
    GW = DSA_GROUP * LANES
    ones_rows = jnp.ones((ONES_ROWS, KP), BF16)

    def attn_block(i, carry):
        sel = sc_ref[pl.ds(pl.multiple_of(i * KP, KP), KP), :] >= thr
        sel_all = jnp.concatenate([sel] * DSA_HEADS, axis=1)
        s = jnp.where(sel_all, _dot_nt(pair_rows(ak_ref, i, 0), qz_ref[...]), -jnp.inf)
        m_old = m_ref[...]
        m_new = jnp.maximum(m_old, jnp.max(s, axis=0, keepdims=True))
        alpha = jnp.exp(m_old - m_new)
        pb = jnp.exp(s - m_new).astype(BF16)
        m_ref[...] = m_new
        vt = pair_rows(avt_ref, i, 1)
        for g in range(DSA_KV_HEADS):
            cols = slice(g * GW, (g + 1) * GW)
            vg = jnp.concatenate([vt[g * DSA_HD:(g + 1) * DSA_HD, :], ones_rows], axis=0)
            acc_ref[g] = alpha[:, cols] * acc_ref[g] + _dot(vg, pb[:, cols])
        return carry

    lax.fori_loop(0, npair, attn_block, 0)

    heads = []
    for h in range(DSA_HEADS):
        a = acc_ref[h // DSA_GROUP][:, (h % DSA_GROUP) * LANES:(h % DSA_GROUP + 1) * LANES]
        heads.append(a[:DSA_HD, :] / a[DSA_HD:DSA_HD + 1, :])
    ot = jnp.concatenate(heads, axis=0)
    z = az_ref[0]
    o_ref[0] = (ot.T * (z * jax.nn.sigmoid(z))).astype(BF16)


def _dsa(ik2, ak, avt, aq, iq, iwt, az, B, nblk, topk):
    kern = functools.partial(_dsa_kernel, topk=topk, nblk=nblk)
    whole = lambda b_, j: (b_, 0, 0, 0)
    qblk = lambda b_, j: (b_ * nblk + j, 0, 0)
    return pl.pallas_call(
        kern,
        grid=(B, nblk),
        in_specs=[
            pl.BlockSpec((1, nblk, LANES, LANES), whole),
            pl.BlockSpec((1, nblk, LANES, LANES), whole),
            pl.BlockSpec((1, nblk, LANES, LANES), whole),
            pl.BlockSpec((1, LANES, DSA_W), qblk),
            pl.BlockSpec((1, LANES, IDX_HEADS * IDX_HD), qblk),
            pl.BlockSpec((1, IDX_HEADS, LANES), qblk),
            pl.BlockSpec((1, LANES, DSA_W), qblk),
        ],
        out_specs=pl.BlockSpec((1, LANES, DSA_W), qblk),
        out_shape=jax.ShapeDtypeStruct((B * nblk, LANES, DSA_W), BF16),
        scratch_shapes=[
            pltpu.VMEM(((nblk + 1) // 2 * 2 * LANES, LANES), F32),
            pltpu.VMEM((DSA_HEADS * LANES, LANES), BF16),
            pltpu.VMEM((IDX_HEADS * LANES, LANES), BF16),
            pltpu.VMEM((1, IDX_HEADS * LANES), F32),
            pltpu.VMEM((1, DSA_HEADS * LANES), F32),
            pltpu.VMEM((DSA_KV_HEADS, DSA_HD + ONES_ROWS, DSA_GROUP * LANES), F32),
        ],
        compiler_params=pltpu.CompilerParams(vmem_limit_bytes=VMEM_LIMIT),
        name="dsa",
    )(ik2.reshape(B, nblk, LANES, LANES), ak.reshape(B, nblk, LANES, LANES),
      avt.reshape(B, nblk, LANES, LANES), aq.reshape(B * nblk, LANES, DSA_W),
      iq.reshape(B * nblk, LANES, IDX_HEADS * IDX_HD), iwt, az.reshape(B * nblk, LANES, DSA_W))


def _out_kernel(mg_ref, md_ref, h_ref, w_ref, g_ref, b_ref, o_ref, *, alpha):
    y = _dot(mg_ref[0], w_ref[:GLA_W, :]) + _dot(md_ref[0], w_ref[GLA_W:, :])
    o_ref[0] = _layer_norm_rows(alpha * h_ref[0] + y, g_ref[...], b_ref[...])


def _out(mg, md, h3, w, g, b, B, nblk, alpha, drop_first):
    nout = nblk - 1 if drop_first else nblk
    off = 1 if drop_first else 0
    src = lambda b_, j: (b_ * nblk + j + off, 0, 0)
    const = lambda b_, j: (0, 0)
    kern = functools.partial(_out_kernel, alpha=alpha)
    return pl.pallas_call(
        kern,
        grid=(B, nout),
        in_specs=[
            pl.BlockSpec((1, LANES, GLA_W), src),
            pl.BlockSpec((1, LANES, DSA_W), src),
            pl.BlockSpec((1, LANES, D_MODEL), src),
            pl.BlockSpec((MIX_W, D_MODEL), const),
            pl.BlockSpec((1, D_MODEL), const),
            pl.BlockSpec((1, D_MODEL), const),
        ],
        out_specs=pl.BlockSpec((1, LANES, D_MODEL), lambda b_, j: (b_ * nout + j, 0, 0)),
        out_shape=jax.ShapeDtypeStruct((B * nout, LANES, D_MODEL), F32),
        name="out_proj_ln",
    )(mg, md, h3, w, g, b)


def _pack_w_in(w):
    splits = (256, 256, 512, GLA_RANK, 512, 512, 128, 128, 512, IDX_HD, IDX_HEADS, 512)
    offs = np.cumsum((0,) + splits)
    gq, gk, gv, glr, gz, aq, ak, av, iq, ik, iw, az = [w[:, offs[i]:offs[i + 1]] for i in range(12)]

    def perm_heads(a):
        return a.reshape(D_MODEL, DSA_HEADS, DSA_HD)[:, HEAD_PERM, :].reshape(D_MODEL, DSA_W)

    pad = jnp.zeros((D_MODEL, LANES - IDX_HD - GLA_RANK - IDX_HEADS), w.dtype)
    packed = jnp.concatenate(
        [gq * (GLA_DK ** -0.5), gk, gv, gz, perm_heads(aq) * (DSA_HD ** -0.5), ak, av,
         iq * (IDX_HD ** -0.5), az, ik, glr, iw, pad], axis=1)
    return packed.astype(BF16)


def _rope_lane_tables(B, LP):
    inv = ROPE_THETA ** (-jnp.arange(0, ROPE_DIM, 2, dtype=F32) / ROPE_DIM)
    pos = (jnp.arange(LP, dtype=F32) - PAD)[:, None]
    ang = pos * inv[None, :]
    cos, sin = jnp.cos(ang), jnp.sin(ang)
    ones = jnp.ones((LP, DSA_HD - ROPE_DIM), F32)
    zeros = jnp.zeros((LP, DSA_HD - ROPE_DIM), F32)
    zh = jnp.zeros((LP, ROPE_HALF), F32)
    c = jnp.concatenate([cos, cos, ones], axis=1)
    sa = jnp.concatenate([zh, sin, zeros], axis=1)
    sb = jnp.concatenate([-sin, zh, zeros], axis=1)
    tile = lambda t: jnp.tile(jnp.concatenate([t, t], axis=1), (B, 1))
    return tile(c), tile(sa), tile(sb)


def kernel(x, meta_tokens, ln_in_g, ln_in_b, w_in, gla_wg2, gla_bg, gla_norm_g, idx_k_g, idx_k_b,
           w_out, ln_g, ln_b):
    B, S, D = x.shape
    depth = w_in.shape[0]
    nblk = S // LANES + 1
    LP = nblk * LANES
    R = B * LP
    topk = min(TOPK_MAX, S // 4)
    alpha = (2.0 * depth) ** 0.25
    tm = 256

    meta_pad = jnp.concatenate([jnp.zeros((PAD, D), x.dtype), meta_tokens.astype(x.dtype)], axis=0)
    h = _embed(x, meta_pad, ln_in_g.reshape(1, D), ln_in_b.reshape(1, D))
    cos_t, sa_t, sb_t = _rope_lane_tables(B, LP)

    for i in range(depth):
        w = _pack_w_in(w_in[i])
        wg2 = jnp.zeros((LANES, 256), F32).at[MISC_GLR:MISC_GLR + GLA_RANK].set(gla_wg2[i])
        wg2h = wg2.astype(BF16)
        wg2l = (wg2 - wg2h.astype(F32)).astype(BF16)
        ikg = jnp.zeros((1, LANES), F32).at[0, :IDX_HD].set(idx_k_g[i])
        ikb = jnp.zeros((1, LANES), F32).at[0, :IDX_HD].set(idx_k_b[i])
        (gq, gk, gv, glog, gz, aq, ak, avt, iq, ik2, iwt, az) = _proj(
            h.reshape(R, D), w, wg2h, wg2l, gla_bg[i].reshape(1, 256), ikg, ikb,
            cos_t, sa_t, sb_t, tm)
        mix_gla = _gla(gq, gk, glog, gv, gz, gla_norm_g[i].reshape(1, GLA_DV), B, LP)
        mix_dsa = _dsa(ik2, ak, avt, aq, iq, iwt, az, B, nblk, topk)
        h = _out(mix_gla.reshape(B * nblk, LANES, GLA_W), mix_dsa, h,
                 w_out[i].astype(BF16), ln_g[i].reshape(1, D), ln_b[i].reshape(1, D),
                 B, nblk, alpha, drop_first=(i == depth - 1))
    return h.reshape(B, S, D)
```

```python
import functools

import numpy as np
import jax
import jax.numpy as jnp
from jax import lax
from jax.experimental import pallas as pl
from jax.experimental.pallas import tpu as pltpu

F32 = jnp.float32
BF16 = jnp.bfloat16
I32 = jnp.int32

D_MODEL = 1024
N_META = 16
ROPE_THETA = 500000.0
LN_EPS = 1e-5
GLA_HEADS = 4
GLA_DK = 64
GLA_DV = 128
GLA_RANK = 16
GLA_TAU = 16.0
GLA_CHUNK = 64
GLA_W = GLA_HEADS * GLA_DV
DSA_HEADS = 8
DSA_KV_HEADS = 2
DSA_GROUP = DSA_HEADS // DSA_KV_HEADS
DSA_HD = 64
DSA_W = DSA_HEADS * DSA_HD
IDX_HEADS = 8
IDX_HD = 64
TOPK_MAX = 256
ROPE_DIM = DSA_HD // 4
ROPE_HALF = ROPE_DIM // 2
MIX_W = GLA_W + DSA_W

LANES = 128
SUBLANES = 8
ONES_ROWS = 16
PAD = LANES - N_META
INT_MIN = -(2 ** 31)
VMEM_LIMIT = 48 * 1024 * 1024
PROJ_ROWS = 512
OUT_BLOCKS = 4

O_GQ, O_GK, O_GV, O_GZ = 0, 256, 512, 1024
O_AQ, O_AK, O_AV, O_IQ, O_AZ, O_MISC = 1536, 2048, 2176, 2304, 2816, 3328
W_PACKED = 3456
MISC_GLR = IDX_HD
MISC_IW = IDX_HD + GLA_RANK
HEAD_PERM = tuple(m + DSA_GROUP * r for m in range(DSA_GROUP) for r in range(2))


def _dot(a, b):
    return jnp.dot(a, b, preferred_element_type=F32)


def _dot_nt(a, b):
    return lax.dot_general(a, b, (((1,), (1,)), ((), ())), preferred_element_type=F32)


def _dot_tn(a, b):
    return lax.dot_general(a, b, (((0,), (0,)), ((), ())), preferred_element_type=F32)


def _layer_norm_rows(u, g, b):
    mu = jnp.mean(u, axis=-1, keepdims=True)
    d = u - mu
    var = jnp.mean(d * d, axis=-1, keepdims=True)
    return d * lax.rsqrt(var + LN_EPS) * g + b


def _embed_kernel(x_ref, meta_ref, g_ref, b_ref, o_ref):
    j = pl.program_id(1)
    src = jnp.where(j == 0, meta_ref[...], x_ref[0, 0])
    o_ref[0] = _layer_norm_rows(src, g_ref[...], b_ref[...])


def _embed(x, meta_pad, g, b):
    B, S, D = x.shape
    nblk = S // LANES + 1
    x4 = x.reshape(B, S // LANES, LANES, D)
    return pl.pallas_call(
        _embed_kernel,
        grid=(B, nblk),
        in_specs=[
            pl.BlockSpec((1, 1, LANES, D), lambda b_, j: (b_, jnp.maximum(j - 1, 0), 0, 0)),
            pl.BlockSpec((LANES, D), lambda b_, j: (0, 0)),
            pl.BlockSpec((1, D), lambda b_, j: (0, 0)),
            pl.BlockSpec((1, D), lambda b_, j: (0, 0)),
        ],
        out_specs=pl.BlockSpec((1, LANES, D), lambda b_, j: (b_ * nblk + j, 0, 0)),
        out_shape=jax.ShapeDtypeStruct((B * nblk, LANES, D), F32),
        name="embed_ln",
    )(x4, meta_pad, g, b)


def _rope_slab(x, c, sa, sb):
    return x * c + pltpu.roll(x, ROPE_HALF, 1) * sa + pltpu.roll(x, LANES - ROPE_HALF, 1) * sb


def _proj_kernel(h_ref, w_ref, wg2h_ref, wg2l_ref, bg_ref, ikg_ref, ikb_ref,
                 cos_ref, sa_ref, sb_ref,
                 gq_ref, gk_ref, gv_ref, glog_ref, gz_ref, aq_ref, ak_ref, avt_ref,
                 iq_ref, ik_ref, iwt_ref, az_ref, *, tm):
    hb = h_ref[...].astype(BF16)

    def seg(o, w):
        return _dot(hb, w_ref[:, o:o + w])

    gq_ref[...] = seg(O_GQ, 256)
    gk_ref[...] = seg(O_GK, 256)
    gv_ref[...] = seg(O_GV, 512).astype(BF16)
    gz_ref[...] = seg(O_GZ, 512)
    az_ref[...] = seg(O_AZ, 512)

    c, sa, sb = cos_ref[...], sa_ref[...], sb_ref[...]
    aq = seg(O_AQ, 512)
    iq = seg(O_IQ, 512)
    for m in range(4):
        sl = slice(m * LANES, (m + 1) * LANES)
        aq_ref[:, sl] = _rope_slab(aq[:, sl], c, sa, sb).astype(BF16)
        iq_ref[:, sl] = _rope_slab(iq[:, sl], c, sa, sb).astype(BF16)
    ak_ref[...] = _rope_slab(seg(O_AK, LANES), c, sa, sb).astype(BF16)

    av = seg(O_AV, LANES)
    misc = seg(O_MISC, LANES)
    misc_t_scale = IDX_HEADS ** -0.5
    for r in range(tm // LANES):
        rows = slice(r * LANES, (r + 1) * LANES)
        avt_ref[r] = av[rows, :].T.astype(BF16)
        iwt_ref[r] = misc[rows, :].T[MISC_IW:MISC_IW + IDX_HEADS, :] * misc_t_scale

    m_hi = misc.astype(BF16)
    m_lo = (misc - m_hi.astype(F32)).astype(BF16)
    xg = (_dot(m_hi, wg2h_ref[...]) + _dot(m_lo, wg2h_ref[...]) + _dot(m_hi, wg2l_ref[...])
          + bg_ref[...])
    glog_ref[...] = (jnp.minimum(xg, 0.0) - jnp.log1p(jnp.exp(-jnp.abs(xg)))) * (1.0 / GLA_TAU)

    lane = lax.broadcasted_iota(I32, (1, LANES), 1)
    is_key = lane < IDX_HD
    mu = jnp.sum(jnp.where(is_key, misc, 0.0), axis=-1, keepdims=True) * (1.0 / IDX_HD)
    d = jnp.where(is_key, misc - mu, 0.0)
    var = jnp.sum(d * d, axis=-1, keepdims=True) * (1.0 / IDX_HD)
    ikn = d * lax.rsqrt(var + LN_EPS) * ikg_ref[...] + ikb_ref[...]
    ikr = _rope_slab(ikn, c, sa, sb)
    ik_ref[...] = jnp.where(is_key, ikr, pltpu.roll(ikr, IDX_HD, 1)).astype(BF16)


def _proj(h2, w, wg2h, wg2l, bg, ikg, ikb, cos_t, sa_t, sb_t, tm):
    R = h2.shape[0]
    nt = R // tm
    nsub = tm // LANES
    row = lambda i: (i, 0)
    const = lambda i: (0, 0)
    kern = functools.partial(_proj_kernel, tm=tm)
    out_shape = (
        jax.ShapeDtypeStruct((R, 256), F32),
        jax.ShapeDtypeStruct((R, 256), F32),
        jax.ShapeDtypeStruct((R, 512), BF16),
        jax.ShapeDtypeStruct((R, 256), F32),
        jax.ShapeDtypeStruct((R, 512), F32),
        jax.ShapeDtypeStruct((R, 512), BF16),
        jax.ShapeDtypeStruct((R, LANES), BF16),
        jax.ShapeDtypeStruct((R // LANES, LANES, LANES), BF16),
        jax.ShapeDtypeStruct((R, 512), BF16),
        jax.ShapeDtypeStruct((R, LANES), BF16),
        jax.ShapeDtypeStruct((R // LANES, IDX_HEADS, LANES), F32),
        jax.ShapeDtypeStruct((R, 512), F32),
    )
    out_specs = (
        pl.BlockSpec((tm, 256), row),
        pl.BlockSpec((tm, 256), row),
        pl.BlockSpec((tm, 512), row),
        pl.BlockSpec((tm, 256), row),
        pl.BlockSpec((tm, 512), row),
        pl.BlockSpec((tm, 512), row),
        pl.BlockSpec((tm, LANES), row),
        pl.BlockSpec((nsub, LANES, LANES), lambda i: (i, 0, 0)),
        pl.BlockSpec((tm, 512), row),
        pl.BlockSpec((tm, LANES), row),
        pl.BlockSpec((nsub, IDX_HEADS, LANES), lambda i: (i, 0, 0)),
        pl.BlockSpec((tm, 512), row),
    )
    in_specs = [
        pl.BlockSpec((tm, D_MODEL), row),
        pl.BlockSpec((D_MODEL, W_PACKED), const),
        pl.BlockSpec((LANES, 256), const),
        pl.BlockSpec((LANES, 256), const),
        pl.BlockSpec((1, 256), const),
        pl.BlockSpec((1, LANES), const),
        pl.BlockSpec((1, LANES), const),
        pl.BlockSpec((tm, LANES), row),
        pl.BlockSpec((tm, LANES), row),
        pl.BlockSpec((tm, LANES), row),
    ]
    return pl.pallas_call(
        kern, grid=(nt,), in_specs=in_specs, out_specs=out_specs, out_shape=out_shape,
        compiler_params=pltpu.CompilerParams(vmem_limit_bytes=VMEM_LIMIT),
        name="in_proj",
    )(h2, w, wg2h, wg2l, bg, ikg, ikb, cos_t, sa_t, sb_t)


GLA_NB = 2
GLA_PIECES = 2


def _gla_kernel(q_ref, k_ref, g_ref, v_ref, z_ref, ng_ref, o_ref, st_ref, *, n_chunks):
    C = GLA_CHUNK
    piece = pl.program_id(1)

    @pl.when(piece == 0)
    def _():
        st_ref[...] = jnp.zeros_like(st_ref)

    lane = lax.broadcasted_iota(I32, (1, LANES), 1)
    head_lanes = (lane < GLA_DK, lane >= GLA_DK)
    rr = lax.broadcasted_iota(I32, (C, C), 0)
    cc = lax.broadcasted_iota(I32, (C, C), 1)
    causal = rr >= cc
    tril = causal.astype(BF16)
    row = lax.broadcasted_iota(I32, (C, 1), 0)
    ng = ng_ref[...]
    row0 = piece * (n_chunks * C)

    def body(ci, carry):
        r0 = pl.multiple_of(ci * C, C)
        rows = pl.ds(r0, C)
        valid = (row0 + r0 + row) >= PAD
        pairs = [(bi, pr) for bi in range(GLA_NB) for pr in range(GLA_HEADS // 2)]
        bs = []
        for bi, pr in pairs:
            g = g_ref[bi, rows, pr * LANES:(pr + 1) * LANES]
            g_hi = g.astype(BF16)
            g_lo = (g - g_hi.astype(F32)).astype(BF16)
            bs.append(_dot(tril, g_hi) + _dot(tril, g_lo))
        ops = []
        for (bi, pr), b in zip(pairs, bs):
            pl_ = slice(pr * LANES, (pr + 1) * LANES)
            q = q_ref[bi, rows, pl_]
            k = jnp.where(valid, k_ref[bi, rows, pl_], 0.0)
            b_mid = b[C // 2 - 1:C // 2, :]
            b_last = b[C - 1:C, :]
            q_in = (q * jnp.exp(b - b_mid)).astype(BF16)
            k_in = (k * jnp.exp(b_mid - b)).astype(BF16)
            q_st = (q * jnp.exp(b)).astype(BF16)
            k_st = (k * jnp.exp(b_last - b)).astype(BF16)
            ops.append((q_in, k_in, q_st, k_st, jnp.exp(b_last)))
        heads = [(bi, pr, h) for bi, pr in pairs for h in range(2)]
        zero = jnp.zeros((C, LANES), BF16)
        part = []
        for n, (bi, pr, h) in enumerate(heads):
            q_in, k_in, q_st, k_st, decay = ops[n // 2]
            hl = head_lanes[h]
            head = 2 * pr + h
            cols = slice(head * GLA_DV, (head + 1) * GLA_DV)
            v = jnp.where(valid, v_ref[bi, rows, cols], jnp.zeros((), BF16))
            st = st_ref[bi, head]
            a = _dot_nt(jnp.where(hl, q_in, zero), k_in)
            o_st = _dot_nt(jnp.where(hl, q_st, zero), st.astype(BF16))
            st_ref[bi, head] = st * decay + _dot_tn(v, jnp.where(hl, k_st, zero))
            part.append((a, o_st, v, cols))
        for (bi, pr, h), (a, o_st, v, cols) in zip(heads, part):
            o = _dot(jnp.where(causal, a, 0.0).astype(BF16), v) + o_st
            on = o * lax.rsqrt(jnp.mean(o * o, axis=-1, keepdims=True) + LN_EPS) * ng
            z = z_ref[bi, rows, cols]
            o_ref[bi, rows, cols] = (on * (z * jax.nn.sigmoid(z))).astype(BF16)
        return carry

    lax.fori_loop(0, n_chunks, body, 0)


def _gla(gq, gk, glog, gv, gz, ng, B, LP):
    rows = LP // GLA_PIECES
    kern = functools.partial(_gla_kernel, n_chunks=rows // GLA_CHUNK)
    blk = lambda b_, p: (b_, p, 0)
    return pl.pallas_call(
        kern,
        grid=(B // GLA_NB, GLA_PIECES),
        in_specs=[
            pl.BlockSpec((GLA_NB, rows, 256), blk),
            pl.BlockSpec((GLA_NB, rows, 256), blk),
            pl.BlockSpec((GLA_NB, rows, 256), blk),
            pl.BlockSpec((GLA_NB, rows, GLA_W), blk),
            pl.BlockSpec((GLA_NB, rows, GLA_W), blk),
            pl.BlockSpec((1, GLA_DV), lambda b_, p: (0, 0)),
        ],
        out_specs=pl.BlockSpec((GLA_NB, rows, GLA_W), blk),
        out_shape=jax.ShapeDtypeStruct((B, LP, GLA_W), BF16),
        scratch_shapes=[pltpu.VMEM((GLA_NB, GLA_HEADS, GLA_DV, LANES), F32)],
        compiler_params=pltpu.CompilerParams(vmem_limit_bytes=VMEM_LIMIT),
        name="gla",
    )(gq.reshape(B, LP, 256), gk.reshape(B, LP, 256), glog.reshape(B, LP, 256),
      gv.reshape(B, LP, GLA_W), gz.reshape(B, LP, GLA_W), ng)


def _dsa_kernel(ik_ref, ak_ref, avt_ref, aq_ref, iq_ref, iwt_ref, az_ref, o_ref,
                sc_ref, qz_ref, iqz_ref, w_ref, m_ref, acc_ref, *, topk, nblk):
    j = pl.program_id(1)
    npair = (j + 2) // 2
    KP = 2 * LANES

    def pair_rows(ref, i, axis):
        kb1 = jnp.minimum(2 * i + 1, nblk - 1)
        return jnp.concatenate([ref[0, 2 * i], ref[0, kb1]], axis=axis)

    lane = lax.broadcasted_iota(I32, (1, LANES), 1)
    low = lane < DSA_HD
    zero_b = jnp.zeros((LANES, LANES), BF16)
    for m in range(4):
        sl = slice(m * LANES, (m + 1) * LANES)
        qs = aq_ref[0, :, sl]
        qz_ref[m * LANES:(m + 1) * LANES, :] = jnp.where(low, qs, zero_b)
        qz_ref[(m + 4) * LANES:(m + 5) * LANES, :] = jnp.where(low, zero_b, qs)
        iqs = iq_ref[0, :, sl]
        iqz_ref[(2 * m) * LANES:(2 * m + 1) * LANES, :] = jnp.where(low, iqs, zero_b)
        iqz_ref[(2 * m + 1) * LANES:(2 * m + 2) * LANES, :] = jnp.where(low, zero_b, iqs)
    for h in range(IDX_HEADS):
        w_ref[:, h * LANES:(h + 1) * LANES] = iwt_ref[0, h:h + 1, :]
    t_pos = j * LANES + lane
    s_loc = lax.broadcasted_iota(I32, (KP, 1), 0)

    def score_block(i, carry):
        d = jnp.maximum(_dot_nt(pair_rows(ik_ref, i, 0), iqz_ref[...]), 0.0) * w_ref[...]
        acc = d[:, :LANES]
        for h in range(1, IDX_HEADS):
            acc = acc + d[:, h * LANES:(h + 1) * LANES]
        s_pos = i * KP + s_loc
        valid = (s_pos <= t_pos) & ((s_pos >= PAD) | (t_pos < PAD))
        sc_ref[pl.ds(pl.multiple_of(i * KP, KP), KP), :] = jnp.where(valid, acc, -jnp.inf)
        return carry

    lax.fori_loop(0, npair, score_block, 0)

    def key_to_float(key):
        bits = jnp.where(key < 0, key ^ jnp.int32(0x7FFFFFFF), key)
        return lax.bitcast_convert_type(bits, F32)

    def pair_ds(i):
        return pl.ds(pl.multiple_of(i * KP, KP), KP)

    def count(pred):
        def cnt_block(i, cnt):
            hit = pred(sc_ref[pair_ds(i), :]).astype(I32)
            parts = [hit[r:r + SUBLANES, :] for r in range(0, KP, SUBLANES)]
            while len(parts) > 1:
                parts = [parts[k] + parts[k + 1] for k in range(0, len(parts), 2)]
            return cnt + parts[0]

        cnt = lax.fori_loop(0, npair, cnt_block, jnp.zeros((SUBLANES, LANES), I32))
        return jnp.sum(cnt, axis=0, keepdims=True)

    def count_ge(cand_key):
        cand = key_to_float(cand_key)
        return count(lambda x: x >= cand)

    c0 = count_ge(jnp.zeros((1, LANES), I32))
    base = jnp.where(c0 >= topk, jnp.int32(0), jnp.int32(INT_MIN))
    base_cnt = jnp.where(c0 >= topk, c0, 0)

    def bit_step(i, carry):
        base, base_cnt = carry
        cand = base | jnp.left_shift(jnp.int32(1), 30 - i)
        c = count_ge(cand)
        ok = c >= topk
        return jnp.where(ok, cand, base), jnp.where(ok, c, base_cnt)

    base, base_cnt = lax.fori_loop(0, 31, bit_step, (base, base_cnt))
    thr = jnp.where(base == INT_MIN, jnp.float32(-3.0e38), key_to_float(base))

    @pl.when(jnp.max(base_cnt) > topk)
    def _():
        need = (topk - count(lambda x: x > thr)).astype(F32)
        rr = lax.broadcasted_iota(I32, (KP, KP), 0)
        cc = lax.broadcasted_iota(I32, (KP, KP), 1)
        tri = (rr >= cc).astype(BF16)

        def strike(i, seen):
            x = sc_ref[pair_ds(i), :]
            eq = x == thr
            rank = _dot(tri, jnp.where(eq, 1.0, 0.0).astype(BF16)) + seen
            sc_ref[pair_ds(i), :] = jnp.where(eq & (rank > need), -jnp.inf, x)
            return rank[KP - 1:KP, :]

        lax.fori_loop(0, npair, strike, jnp.zeros((1, LANES), F32))

    m_ref[...] = jnp.full_like(m_ref, -1e30)
    acc_ref[...] = jnp.zeros_like(acc_ref)
    GW = DSA_GROUP * LANES
    ones_rows = jnp.ones((ONES_ROWS, KP), BF16)

    def attn_block(i, carry):
        sel = sc_ref[pl.ds(pl.multiple_of(i * KP, KP), KP), :] >= thr
        sel_all = jnp.concatenate([sel] * DSA_HEADS, axis=1)
        s = jnp.where(sel_all, _dot_nt(pair_rows(ak_ref, i, 0), qz_ref[...]), -jnp.inf)
        m_old = m_ref[...]
        m_new = jnp.maximum(m_old, jnp.max(s, axis=0, keepdims=True))
        alpha = jnp.exp(m_old - m_new)
        pb = jnp.exp(s - m_new).astype(BF16)
        m_ref[...] = m_new
        vt = pair_rows(avt_ref, i, 1)
        for g in range(DSA_KV_HEADS):
            cols = slice(g * GW, (g + 1) * GW)
            vg = jnp.concatenate([vt[g * DSA_HD:(g + 1) * DSA_HD, :], ones_rows], axis=0)
            acc_ref[g] = alpha[:, cols] * acc_ref[g] + _dot(vg, pb[:, cols])
        return carry

    lax.fori_loop(0, npair, attn_block, 0)

    heads = []
    for h in range(DSA_HEADS):
        a = acc_ref[h // DSA_GROUP][:, (h % DSA_GROUP) * LANES:(h % DSA_GROUP + 1) * LANES]
        heads.append(a[:DSA_HD, :] / a[DSA_HD:DSA_HD + 1, :])
    ot = jnp.concatenate(heads, axis=0)
    z = az_ref[0]
    o_ref[0] = (ot.T * (z * jax.nn.sigmoid(z))).astype(BF16)


def _dsa(ik2, ak, avt, aq, iq, iwt, az, B, nblk, topk):
    kern = functools.partial(_dsa_kernel, topk=topk, nblk=nblk)
    whole = lambda b_, j: (b_, 0, 0, 0)
    qblk = lambda b_, j: (b_ * nblk + j, 0, 0)
    return pl.pallas_call(
        kern,
        grid=(B, nblk),
        in_specs=[
            pl.BlockSpec((1, nblk, LANES, LANES), whole),
            pl.BlockSpec((1, nblk, LANES, LANES), whole),
            pl.BlockSpec((1, nblk, LANES, LANES), whole),
            pl.BlockSpec((1, LANES, DSA_W), qblk),
            pl.BlockSpec((1, LANES, IDX_HEADS * IDX_HD), qblk),
            pl.BlockSpec((1, IDX_HEADS, LANES), qblk),
            pl.BlockSpec((1, LANES, DSA_W), qblk),
        ],
        out_specs=pl.BlockSpec((1, LANES, DSA_W), qblk),
        out_shape=jax.ShapeDtypeStruct((B * nblk, LANES, DSA_W), BF16),
        scratch_shapes=[
            pltpu.VMEM(((nblk + 1) // 2 * 2 * LANES, LANES), F32),
            pltpu.VMEM((DSA_HEADS * LANES, LANES), BF16),
            pltpu.VMEM((IDX_HEADS * LANES, LANES), BF16),
            pltpu.VMEM((1, IDX_HEADS * LANES), F32),
            pltpu.VMEM((1, DSA_HEADS * LANES), F32),
            pltpu.VMEM((DSA_KV_HEADS, DSA_HD + ONES_ROWS, DSA_GROUP * LANES), F32),
        ],
        compiler_params=pltpu.CompilerParams(vmem_limit_bytes=VMEM_LIMIT),
        name="dsa",
    )(ik2.reshape(B, nblk, LANES, LANES), ak.reshape(B, nblk, LANES, LANES),
      avt.reshape(B, nblk, LANES, LANES), aq.reshape(B * nblk, LANES, DSA_W),
      iq.reshape(B * nblk, LANES, IDX_HEADS * IDX_HD), iwt, az.reshape(B * nblk, LANES, DSA_W))


def _out_kernel(mg_ref, md_ref, h_ref, w_ref, g_ref, b_ref, o_ref, *, alpha, nb):
    rows = nb * LANES
    y = (_dot(mg_ref[...].reshape(rows, GLA_W), w_ref[:GLA_W, :])
         + _dot(md_ref[...].reshape(rows, DSA_W), w_ref[GLA_W:, :]))
    u = alpha * h_ref[...].reshape(rows, D_MODEL) + y
    o_ref[...] = _layer_norm_rows(u, g_ref[...], b_ref[...]).reshape(nb, LANES, D_MODEL)


def _out(mg, md, h3, w, g, b, B, nblk, alpha, drop_first):
    if drop_first:
        nb, grid, nout = 1, (B, nblk - 1), B * (nblk - 1)
        src = lambda b_, j: (b_ * nblk + j + 1, 0, 0)
        dst = lambda b_, j: (b_ * (nblk - 1) + j, 0, 0)
        const = lambda b_, j: (0, 0)
    else:
        nb = OUT_BLOCKS
        grid, nout = (B * nblk // nb,), B * nblk
        src = dst = lambda i: (i, 0, 0)
        const = lambda i: (0, 0)
    kern = functools.partial(_out_kernel, alpha=alpha, nb=nb)
    return pl.pallas_call(
        kern,
        grid=grid,
        in_specs=[
            pl.BlockSpec((nb, LANES, GLA_W), src),
            pl.BlockSpec((nb, LANES, DSA_W), src),
            pl.BlockSpec((nb, LANES, D_MODEL), src),
            pl.BlockSpec((MIX_W, D_MODEL), const),
            pl.BlockSpec((1, D_MODEL), const),
            pl.BlockSpec((1, D_MODEL), const),
        ],
        out_specs=pl.BlockSpec((nb, LANES, D_MODEL), dst),
        out_shape=jax.ShapeDtypeStruct((nout, LANES, D_MODEL), F32),
        name="out_proj_ln",
    )(mg, md, h3, w, g, b)


def _pack_w_in(w):
    splits = (256, 256, 512, GLA_RANK, 512, 512, 128, 128, 512, IDX_HD, IDX_HEADS, 512)
    offs = np.cumsum((0,) + splits)
    gq, gk, gv, glr, gz, aq, ak, av, iq, ik, iw, az = [w[:, offs[i]:offs[i + 1]] for i in range(12)]

    def perm_heads(a):
        return a.reshape(D_MODEL, DSA_HEADS, DSA_HD)[:, HEAD_PERM, :].reshape(D_MODEL, DSA_W)

    pad = jnp.zeros((D_MODEL, LANES - IDX_HD - GLA_RANK - IDX_HEADS), w.dtype)
    packed = jnp.concatenate(
        [gq * (GLA_DK ** -0.5), gk, gv, gz, perm_heads(aq) * (DSA_HD ** -0.5), ak, av,
         iq * (IDX_HD ** -0.5), az, ik, glr, iw, pad], axis=1)
    return packed.astype(BF16)


def _rope_lane_tables(B, LP):
    inv = ROPE_THETA ** (-jnp.arange(0, ROPE_DIM, 2, dtype=F32) / ROPE_DIM)
    pos = (jnp.arange(LP, dtype=F32) - PAD)[:, None]
    ang = pos * inv[None, :]
    cos, sin = jnp.cos(ang), jnp.sin(ang)
    ones = jnp.ones((LP, DSA_HD - ROPE_DIM), F32)
    zeros = jnp.zeros((LP, DSA_HD - ROPE_DIM), F32)
    zh = jnp.zeros((LP, ROPE_HALF), F32)
    c = jnp.concatenate([cos, cos, ones], axis=1)
    sa = jnp.concatenate([zh, sin, zeros], axis=1)
    sb = jnp.concatenate([-sin, zh, zeros], axis=1)
    tile = lambda t: jnp.tile(jnp.concatenate([t, t], axis=1), (B, 1))
    return tile(c), tile(sa), tile(sb)


def kernel(x, meta_tokens, ln_in_g, ln_in_b, w_in, gla_wg2, gla_bg, gla_norm_g, idx_k_g, idx_k_b,
           w_out, ln_g, ln_b):
    B, S, D = x.shape
    depth = w_in.shape[0]
    nblk = S // LANES + 1
    LP = nblk * LANES
    R = B * LP
    topk = min(TOPK_MAX, S // 4)
    alpha = (2.0 * depth) ** 0.25
    tm = PROJ_ROWS

    meta_pad = jnp.concatenate([jnp.zeros((PAD, D), x.dtype), meta_tokens.astype(x.dtype)], axis=0)
    h = _embed(x, meta_pad, ln_in_g.reshape(1, D), ln_in_b.reshape(1, D))
    cos_t, sa_t, sb_t = _rope_lane_tables(B, LP)

    for i in range(depth):
        w = _pack_w_in(w_in[i])
        wg2 = jnp.zeros((LANES, 256), F32).at[MISC_GLR:MISC_GLR + GLA_RANK].set(gla_wg2[i])
        wg2h = wg2.astype(BF16)
        wg2l = (wg2 - wg2h.astype(F32)).astype(BF16)
        ikg = jnp.zeros((1, LANES), F32).at[0, :IDX_HD].set(idx_k_g[i])
        ikb = jnp.zeros((1, LANES), F32).at[0, :IDX_HD].set(idx_k_b[i])
        (gq, gk, gv, glog, gz, aq, ak, avt, iq, ik2, iwt, az) = _proj(
            h.reshape(R, D), w, wg2h, wg2l, gla_bg[i].reshape(1, 256), ikg, ikb,
            cos_t, sa_t, sb_t, tm)
        mix_gla = _gla(gq, gk, glog, gv, gz, gla_norm_g[i].reshape(1, GLA_DV), B, LP)
        mix_dsa = _dsa(ik2, ak, avt, aq, iq, iwt, az, B, nblk, topk)
        h = _out(mix_gla.reshape(B * nblk, LANES, GLA_W), mix_dsa, h,
                 w_out[i].astype(BF16), ln_g[i].reshape(1, D), ln_b[i].reshape(1, D),
                 B, nblk, alpha, drop_first=(i == depth - 1))
    return h.reshape(B, S, D)
```

```python
import functools

import numpy as np
import jax
import jax.numpy as jnp
from jax import lax
from jax.experimental import pallas as pl
from jax.experimental.pallas import tpu as pltpu

F32 = jnp.float32
BF16 = jnp.bfloat16
I32 = jnp.int32

D_MODEL = 1024
N_META = 16
ROPE_THETA = 500000.0
LN_EPS = 1e-5
GLA_HEADS = 4
GLA_DK = 64
GLA_DV = 128
GLA_RANK = 16
GLA_TAU = 16.0
GLA_CHUNK = 64
GLA_W = GLA_HEADS * GLA_DV
DSA_HEADS = 8
DSA_KV_HEADS = 2
DSA_GROUP = DSA_HEADS // DSA_KV_HEADS
DSA_HD = 64
DSA_W = DSA_HEADS * DSA_HD
IDX_HEADS = 8
IDX_HD = 64
TOPK_MAX = 256
ROPE_DIM = DSA_HD // 4
ROPE_HALF = ROPE_DIM // 2
MIX_W = GLA_W + DSA_W

LANES = 128
SUBLANES = 8
ONES_ROWS = 16
PAD = LANES - N_META
INT_MIN = -(2 ** 31)
VMEM_LIMIT = 48 * 1024 * 1024
PROJ_ROWS = 512
OUT_BLOCKS = 4

O_GQ, O_GK, O_GV, O_GZ = 0, 256, 512, 1024
O_AQ, O_AK, O_AV, O_IQ, O_AZ, O_MISC = 1536, 2048, 2176, 2304, 2816, 3328
W_PACKED = 3456
MISC_GLR = IDX_HD
MISC_IW = IDX_HD + GLA_RANK
HEAD_PERM = tuple(m + DSA_GROUP * r for m in range(DSA_GROUP) for r in range(2))


def _dot(a, b):
    return jnp.dot(a, b, preferred_element_type=F32)


def _dot_nt(a, b):
    return lax.dot_general(a, b, (((1,), (1,)), ((), ())), preferred_element_type=F32)


def _dot_tn(a, b):
    return lax.dot_general(a, b, (((0,), (0,)), ((), ())), preferred_element_type=F32)


def _layer_norm_rows(u, g, b):
    mu = jnp.mean(u, axis=-1, keepdims=True)
    d = u - mu
    var = jnp.mean(d * d, axis=-1, keepdims=True)
    return d * lax.rsqrt(var + LN_EPS) * g + b


def _embed_kernel(x_ref, meta_ref, g_ref, b_ref, o_ref):
    j = pl.program_id(1)
    src = jnp.where(j == 0, meta_ref[...], x_ref[0, 0])
    o_ref[0] = _layer_norm_rows(src, g_ref[...], b_ref[...])


def _embed(x, meta_pad, g, b):
    B, S, D = x.shape
    nblk = S // LANES + 1
    x4 = x.reshape(B, S // LANES, LANES, D)
    return pl.pallas_call(
        _embed_kernel,
        grid=(B, nblk),
        in_specs=[
            pl.BlockSpec((1, 1, LANES, D), lambda b_, j: (b_, jnp.maximum(j - 1, 0), 0, 0)),
            pl.BlockSpec((LANES, D), lambda b_, j: (0, 0)),
            pl.BlockSpec((1, D), lambda b_, j: (0, 0)),
            pl.BlockSpec((1, D), lambda b_, j: (0, 0)),
        ],
        out_specs=pl.BlockSpec((1, LANES, D), lambda b_, j: (b_ * nblk + j, 0, 0)),
        out_shape=jax.ShapeDtypeStruct((B * nblk, LANES, D), F32),
        name="embed_ln",
    )(x4, meta_pad, g, b)


def _rope_slab(x, c, sa, sb):
    return x * c + pltpu.roll(x, ROPE_HALF, 1) * sa + pltpu.roll(x, LANES - ROPE_HALF, 1) * sb


def _proj_kernel(h_ref, w_ref, wg2h_ref, wg2l_ref, bg_ref, ikg_ref, ikb_ref,
                 cos_ref, sa_ref, sb_ref,
                 gq_ref, gk_ref, gv_ref, glog_ref, gz_ref, aq_ref, ak_ref, avt_ref,
                 iq_ref, ik_ref, iwt_ref, az_ref, *, tm):
    hb = h_ref[...].astype(BF16)

    def seg(o, w):
        return _dot(hb, w_ref[:, o:o + w])

    gq_ref[...] = seg(O_GQ, 256)
    gk_ref[...] = seg(O_GK, 256)
    gv_ref[...] = seg(O_GV, 512).astype(BF16)
    gz_ref[...] = seg(O_GZ, 512)
    az_ref[...] = seg(O_AZ, 512)

    c, sa, sb = cos_ref[...], sa_ref[...], sb_ref[...]
    aq = seg(O_AQ, 512)
    iq = seg(O_IQ, 512)
    for m in range(4):
        sl = slice(m * LANES, (m + 1) * LANES)
        aq_ref[:, sl] = _rope_slab(aq[:, sl], c, sa, sb).astype(BF16)
        iq_ref[:, sl] = _rope_slab(iq[:, sl], c, sa, sb).astype(BF16)
    ak_ref[...] = _rope_slab(seg(O_AK, LANES), c, sa, sb).astype(BF16)

    av = seg(O_AV, LANES)
    misc = seg(O_MISC, LANES)
    misc_t_scale = IDX_HEADS ** -0.5
    for r in range(tm // LANES):
        rows = slice(r * LANES, (r + 1) * LANES)
        avt_ref[r] = av[rows, :].T.astype(BF16)
        iwt_ref[r] = misc[rows, :].T[MISC_IW:MISC_IW + IDX_HEADS, :] * misc_t_scale

    m_hi = misc.astype(BF16)
    m_lo = (misc - m_hi.astype(F32)).astype(BF16)
    xg = (_dot(m_hi, wg2h_ref[...]) + _dot(m_lo, wg2h_ref[...]) + _dot(m_hi, wg2l_ref[...])
          + bg_ref[...])
    glog_ref[...] = (jnp.minimum(xg, 0.0) - jnp.log1p(jnp.exp(-jnp.abs(xg)))) * (1.0 / GLA_TAU)

    lane = lax.broadcasted_iota(I32, (1, LANES), 1)
    is_key = lane < IDX_HD
    mu = jnp.sum(jnp.where(is_key, misc, 0.0), axis=-1, keepdims=True) * (1.0 / IDX_HD)
    d = jnp.where(is_key, misc - mu, 0.0)
    var = jnp.sum(d * d, axis=-1, keepdims=True) * (1.0 / IDX_HD)
    ikn = d * lax.rsqrt(var + LN_EPS) * ikg_ref[...] + ikb_ref[...]
    ikr = _rope_slab(ikn, c, sa, sb)
    ik_ref[...] = jnp.where(is_key, ikr, pltpu.roll(ikr, IDX_HD, 1)).astype(BF16)


def _proj(h2, w, wg2h, wg2l, bg, ikg, ikb, cos_t, sa_t, sb_t, tm):
    R = h2.shape[0]
    nt = R // tm
    nsub = tm // LANES
    row = lambda i: (i, 0)
    const = lambda i: (0, 0)
    kern = functools.partial(_proj_kernel, tm=tm)
    out_shape = (
        jax.ShapeDtypeStruct((R, 256), F32),
        jax.ShapeDtypeStruct((R, 256), F32),
        jax.ShapeDtypeStruct((R, 512), BF16),
        jax.ShapeDtypeStruct((R, 256), F32),
        jax.ShapeDtypeStruct((R, 512), F32),
        jax.ShapeDtypeStruct((R, 512), BF16),
        jax.ShapeDtypeStruct((R, LANES), BF16),
        jax.ShapeDtypeStruct((R // LANES, LANES, LANES), BF16),
        jax.ShapeDtypeStruct((R, 512), BF16),
        jax.ShapeDtypeStruct((R, LANES), BF16),
        jax.ShapeDtypeStruct((R // LANES, IDX_HEADS, LANES), F32),
        jax.ShapeDtypeStruct((R, 512), F32),
    )
    out_specs = (
        pl.BlockSpec((tm, 256), row),
        pl.BlockSpec((tm, 256), row),
        pl.BlockSpec((tm, 512), row),
        pl.BlockSpec((tm, 256), row),
        pl.BlockSpec((tm, 512), row),
        pl.BlockSpec((tm, 512), row),
        pl.BlockSpec((tm, LANES), row),
        pl.BlockSpec((nsub, LANES, LANES), lambda i: (i, 0, 0)),
        pl.BlockSpec((tm, 512), row),
        pl.BlockSpec((tm, LANES), row),
        pl.BlockSpec((nsub, IDX_HEADS, LANES), lambda i: (i, 0, 0)),
        pl.BlockSpec((tm, 512), row),
    )
    in_specs = [
        pl.BlockSpec((tm, D_MODEL), row),
        pl.BlockSpec((D_MODEL, W_PACKED), const),
        pl.BlockSpec((LANES, 256), const),
        pl.BlockSpec((LANES, 256), const),
        pl.BlockSpec((1, 256), const),
        pl.BlockSpec((1, LANES), const),
        pl.BlockSpec((1, LANES), const),
        pl.BlockSpec((tm, LANES), row),
        pl.BlockSpec((tm, LANES), row),
        pl.BlockSpec((tm, LANES), row),
    ]
    return pl.pallas_call(
        kern, grid=(nt,), in_specs=in_specs, out_specs=out_specs, out_shape=out_shape,
        compiler_params=pltpu.CompilerParams(vmem_limit_bytes=VMEM_LIMIT),
        name="in_proj",
    )(h2, w, wg2h, wg2l, bg, ikg, ikb, cos_t, sa_t, sb_t)


GLA_NB = 2
GLA_PIECES = 2


def _gla_kernel(q_ref, k_ref, g_ref, v_ref, z_ref, ng_ref, o_ref, st_ref, *, n_chunks):
    C = GLA_CHUNK
    piece = pl.program_id(1)

    @pl.when(piece == 0)
    def _():
        st_ref[...] = jnp.zeros_like(st_ref)

    lane = lax.broadcasted_iota(I32, (1, LANES), 1)
    head_lanes = (lane < GLA_DK, lane >= GLA_DK)
    rr = lax.broadcasted_iota(I32, (C, C), 0)
    cc = lax.broadcasted_iota(I32, (C, C), 1)
    causal = rr >= cc
    tril = causal.astype(BF16)
    row = lax.broadcasted_iota(I32, (C, 1), 0)
    ng = ng_ref[...]
    row0 = piece * (n_chunks * C)

    def body(ci, carry):
        r0 = pl.multiple_of(ci * C, C)
        rows = pl.ds(r0, C)
        valid = (row0 + r0 + row) >= PAD
        pairs = [(bi, pr) for bi in range(GLA_NB) for pr in range(GLA_HEADS // 2)]
        bs = []
        for bi, pr in pairs:
            g = g_ref[bi, rows, pr * LANES:(pr + 1) * LANES]
            g_hi = g.astype(BF16)
            g_lo = (g - g_hi.astype(F32)).astype(BF16)
            bs.append(_dot(tril, g_hi) + _dot(tril, g_lo))
        ops = []
        for (bi, pr), b in zip(pairs, bs):
            pl_ = slice(pr * LANES, (pr + 1) * LANES)
            q = q_ref[bi, rows, pl_]
            k = jnp.where(valid, k_ref[bi, rows, pl_], 0.0)
            b_mid = b[C // 2 - 1:C // 2, :]
            b_last = b[C - 1:C, :]
            q_in = (q * jnp.exp(b - b_mid)).astype(BF16)
            k_in = (k * jnp.exp(b_mid - b)).astype(BF16)
            q_st = (q * jnp.exp(b)).astype(BF16)
            k_st = (k * jnp.exp(b_last - b)).astype(BF16)
            ops.append((q_in, k_in, q_st, k_st, jnp.exp(b_last)))
        heads = [(bi, pr, h) for bi, pr in pairs for h in range(2)]
        zero = jnp.zeros((C, LANES), BF16)
        part = []
        for n, (bi, pr, h) in enumerate(heads):
            q_in, k_in, q_st, k_st, decay = ops[n // 2]
            hl = head_lanes[h]
            head = 2 * pr + h
            cols = slice(head * GLA_DV, (head + 1) * GLA_DV)
            v = jnp.where(valid, v_ref[bi, rows, cols], jnp.zeros((), BF16))
            st = st_ref[bi, head]
            a = _dot_nt(jnp.where(hl, q_in, zero), k_in)
            o_st = _dot_nt(jnp.where(hl, q_st, zero), st.astype(BF16))
            st_ref[bi, head] = st * decay + _dot_tn(v, jnp.where(hl, k_st, zero))
            part.append((a, o_st, v, cols))
        for (bi, pr, h), (a, o_st, v, cols) in zip(heads, part):
            o = _dot(jnp.where(causal, a, 0.0).astype(BF16), v) + o_st
            on = o * lax.rsqrt(jnp.mean(o * o, axis=-1, keepdims=True) + LN_EPS) * ng
            z = z_ref[bi, rows, cols]
            o_ref[bi, rows, cols] = (on * (z * jax.nn.sigmoid(z))).astype(BF16)
        return carry

    lax.fori_loop(0, n_chunks, body, 0)


def _gla(gq, gk, glog, gv, gz, ng, B, LP):
    rows = LP // GLA_PIECES
    kern = functools.partial(_gla_kernel, n_chunks=rows // GLA_CHUNK)
    blk = lambda b_, p: (b_, p, 0)
    return pl.pallas_call(
        kern,
        grid=(B // GLA_NB, GLA_PIECES),
        in_specs=[
            pl.BlockSpec((GLA_NB, rows, 256), blk),
            pl.BlockSpec((GLA_NB, rows, 256), blk),
            pl.BlockSpec((GLA_NB, rows, 256), blk),
            pl.BlockSpec((GLA_NB, rows, GLA_W), blk),
            pl.BlockSpec((GLA_NB, rows, GLA_W), blk),
            pl.BlockSpec((1, GLA_DV), lambda b_, p: (0, 0)),
        ],
        out_specs=pl.BlockSpec((GLA_NB, rows, GLA_W), blk),
        out_shape=jax.ShapeDtypeStruct((B, LP, GLA_W), BF16),
        scratch_shapes=[pltpu.VMEM((GLA_NB, GLA_HEADS, GLA_DV, LANES), F32)],
        compiler_params=pltpu.CompilerParams(vmem_limit_bytes=VMEM_LIMIT),
        name="gla",
    )(gq.reshape(B, LP, 256), gk.reshape(B, LP, 256), glog.reshape(B, LP, 256),
      gv.reshape(B, LP, GLA_W), gz.reshape(B, LP, GLA_W), ng)


def _dsa_kernel(ik_ref, ak_ref, avt_ref, aq_ref, iq_ref, iwt_ref, az_ref, o_ref,
                sc_ref, qz_ref, iqz_ref, w_ref, m_ref, acc_ref, plane_ref, thr_ref, cnt_ref,
                *, topk, nblk):
    j = pl.program_id(1)
    npair = (j + 2) // 2
    KP = 2 * LANES

    def pair_rows(ref, i, axis):
        kb1 = jnp.minimum(2 * i + 1, nblk - 1)
        return jnp.concatenate([ref[0, 2 * i], ref[0, kb1]], axis=axis)

    lane = lax.broadcasted_iota(I32, (1, LANES), 1)
    low = lane < DSA_HD
    zero_b = jnp.zeros((LANES, LANES), BF16)
    for m in range(4):
        sl = slice(m * LANES, (m + 1) * LANES)
        qs = aq_ref[0, :, sl]
        qz_ref[m * LANES:(m + 1) * LANES, :] = jnp.where(low, qs, zero_b)
        qz_ref[(m + 4) * LANES:(m + 5) * LANES, :] = jnp.where(low, zero_b, qs)
        iqs = iq_ref[0, :, sl]
        iqz_ref[(2 * m) * LANES:(2 * m + 1) * LANES, :] = jnp.where(low, iqs, zero_b)
        iqz_ref[(2 * m + 1) * LANES:(2 * m + 2) * LANES, :] = jnp.where(low, zero_b, iqs)
    for h in range(IDX_HEADS):
        w_ref[:, h * LANES:(h + 1) * LANES] = iwt_ref[0, h:h + 1, :]
    t_pos = j * LANES + lane
    s_loc = lax.broadcasted_iota(I32, (KP, 1), 0)

    def score_block(i, carry):
        d = jnp.maximum(_dot_nt(pair_rows(ik_ref, i, 0), iqz_ref[...]), 0.0) * w_ref[...]
        acc = d[:, :LANES]
        for h in range(1, IDX_HEADS):
            acc = acc + d[:, h * LANES:(h + 1) * LANES]
        s_pos = i * KP + s_loc
        valid = (s_pos <= t_pos) & ((s_pos >= PAD) | (t_pos < PAD))
        sc_ref[pl.ds(pl.multiple_of(i * KP, KP), KP), :] = jnp.where(valid, acc, -jnp.inf)
        return carry

    lax.fori_loop(0, npair, score_block, 0)

    def pair_ds(i):
        return pl.ds(pl.multiple_of(i * KP, KP), KP)

    def count(*preds):
        def cnt_block(i, cnts):
            x = sc_ref[pair_ds(i), :]
            out = []
            for pred, cnt in zip(preds, cnts):
                hit = pred(x).astype(I32)
                parts = [hit[r:r + SUBLANES, :] for r in range(0, KP, SUBLANES)]
                while len(parts) > 1:
                    parts = [parts[k] + parts[k + 1] for k in range(0, len(parts), 2)]
                out.append(cnt + parts[0])
            return tuple(out)

        zero = jnp.zeros((SUBLANES, LANES), I32)
        cnts = lax.fori_loop(0, npair, cnt_block, (zero,) * len(preds))
        return tuple(jnp.sum(c, axis=0, keepdims=True) for c in cnts)

    select_all = jnp.float32(-3.0e38)

    @pl.when((pl.program_id(0) == 0) & (j == 0))
    def _():
        plane_ref[...] = jnp.zeros_like(plane_ref)

    def shift_const(x, n):
        return jnp.full(x.shape, n, I32)

    def build_planes(i, carry):
        bits = lax.bitcast_convert_type(sc_ref[pair_ds(i), :], I32)
        u = bits ^ (lax.shift_right_arithmetic(bits, shift_const(bits, 31)) | jnp.int32(INT_MIN))
        a = [u[SUBLANES * v:SUBLANES * (v + 1), :] for v in range(32)]
        step, mask = 16, 0x0000FFFF
        while step:
            for k in range(32):
                if not k & step:
                    t = (a[k] ^ lax.shift_right_logical(a[k + step], shift_const(a[k], step))) & jnp.int32(mask)
                    a[k] = a[k] ^ t
                    a[k + step] = a[k + step] ^ lax.shift_left(t, shift_const(t, step))
            step >>= 1
            mask = (mask ^ (mask << step)) & 0xFFFFFFFF
        for r in range(32):
            plane_ref[i, r] = a[r]
        return carry

    lax.fori_loop(0, npair, build_planes, 0)

    n_pairs_max = plane_ref.shape[0]
    ones_v = jnp.full((SUBLANES, LANES), -1, I32)
    zeros_v = jnp.zeros((SUBLANES, LANES), I32)
    alive0 = tuple(jnp.where(i < npair, ones_v, zeros_v) for i in range(n_pairs_max))

    def radix_step(r, carry):
        alive, k_left, u_thr = carry
        planes = [plane_ref[i, r] for i in range(n_pairs_max)]
        c8 = None
        for i in range(n_pairs_max):
            pc = lax.population_count(alive[i] & planes[i])
            c8 = pc if c8 is None else c8 + pc
        c = jnp.sum(c8, axis=0, keepdims=True)
        take = c >= k_left
        k_left = jnp.where(take, k_left, k_left - c)
        flip = jnp.where(take, jnp.int32(0), jnp.int32(-1))
        alive = tuple(alive[i] & (planes[i] ^ flip) for i in range(n_pairs_max))
        bit = lax.shift_left(jnp.int32(1), 31 - r)
        return alive, k_left, u_thr | jnp.where(take, bit, jnp.int32(0))

    _, _, u_thr = lax.fori_loop(
        0, 32, radix_step,
        (alive0, jnp.full((1, LANES), topk, I32), jnp.zeros((1, LANES), I32)))
    thr_bits = jnp.where(u_thr < 0, u_thr ^ jnp.int32(INT_MIN), ~u_thr)
    thr_fast = lax.bitcast_convert_type(thr_bits, F32)
    thr_fast = jnp.where(thr_fast == -jnp.inf, select_all, thr_fast)

    n_ge, n_gt = count(lambda x: x >= thr_fast, lambda x: x > thr_fast)
    is_all = thr_fast == select_all
    good = is_all | ((n_gt < topk) & (n_ge >= topk))
    thr_ref[...] = thr_fast
    cnt_ref[...] = jnp.where(is_all, 0, n_ge)

    @pl.when(jnp.min(good.astype(I32)) == 0)
    def _():
        def key_to_float(key):
            bits = jnp.where(key < 0, key ^ jnp.int32(0x7FFFFFFF), key)
            return lax.bitcast_convert_type(bits, F32)

        def count_ge(cand_key):
            cand = key_to_float(cand_key)
            return count(lambda x: x >= cand)[0]

        c0 = count_ge(jnp.zeros((1, LANES), I32))
        base = jnp.where(c0 >= topk, jnp.int32(0), jnp.int32(INT_MIN))
        base_cnt = jnp.where(c0 >= topk, c0, 0)

        def bit_step(i, carry):
            base, base_cnt = carry
            cand = base | jnp.left_shift(jnp.int32(1), 30 - i)
            c = count_ge(cand)
            ok = c >= topk
            return jnp.where(ok, cand, base), jnp.where(ok, c, base_cnt)

        base, base_cnt = lax.fori_loop(0, 31, bit_step, (base, base_cnt))
        thr_ref[...] = jnp.where(base == INT_MIN, select_all, key_to_float(base))
        cnt_ref[...] = base_cnt

    thr = thr_ref[...]
    base_cnt = cnt_ref[...]

    @pl.when(jnp.max(base_cnt) > topk)
    def _():
        need = (topk - count(lambda x: x > thr)[0]).astype(F32)
        rr = lax.broadcasted_iota(I32, (KP, KP), 0)
        cc = lax.broadcasted_iota(I32, (KP, KP), 1)
        tri = (rr >= cc).astype(BF16)

        def strike(i, seen):
            x = sc_ref[pair_ds(i), :]
            eq = x == thr
            rank = _dot(tri, jnp.where(eq, 1.0, 0.0).astype(BF16)) + seen
            sc_ref[pair_ds(i), :] = jnp.where(eq & (rank > need), -jnp.inf, x)
            return rank[KP - 1:KP, :]

        lax.fori_loop(0, npair, strike, jnp.zeros((1, LANES), F32))

    m_ref[...] = jnp.full_like(m_ref, -1e30)
    acc_ref[...] = jnp.zeros_like(acc_ref)
    GW = DSA_GROUP * LANES
    ones_rows = jnp.ones((ONES_ROWS, KP), BF16)

    def attn_block(i, carry):
        sel = sc_ref[pl.ds(pl.multiple_of(i * KP, KP), KP), :] >= thr
        sel_all = jnp.concatenate([sel] * DSA_HEADS, axis=1)
        s = jnp.where(sel_all, _dot_nt(pair_rows(ak_ref, i, 0), qz_ref[...]), -jnp.inf)
        m_old = m_ref[...]
        m_new = jnp.maximum(m_old, jnp.max(s, axis=0, keepdims=True))
        alpha = jnp.exp(m_old - m_new)
        pb = jnp.exp(s - m_new).astype(BF16)
        m_ref[...] = m_new
        vt = pair_rows(avt_ref, i, 1)
        for g in range(DSA_KV_HEADS):
            cols = slice(g * GW, (g + 1) * GW)
            vg = jnp.concatenate([vt[g * DSA_HD:(g + 1) * DSA_HD, :], ones_rows], axis=0)
            acc_ref[g] = alpha[:, cols] * acc_ref[g] + _dot(vg, pb[:, cols])
        return carry

    lax.fori_loop(0, npair, attn_block, 0)

    heads = []
    for h in range(DSA_HEADS):
        a = acc_ref[h // DSA_GROUP][:, (h % DSA_GROUP) * LANES:(h % DSA_GROUP + 1) * LANES]
        heads.append(a[:DSA_HD, :] / a[DSA_HD:DSA_HD + 1, :])
    ot = jnp.concatenate(heads, axis=0)
    z = az_ref[0]
    o_ref[0] = (ot.T * (z * jax.nn.sigmoid(z))).astype(BF16)


def _dsa(ik2, ak, avt, aq, iq, iwt, az, B, nblk, topk):
    kern = functools.partial(_dsa_kernel, topk=topk, nblk=nblk)
    whole = lambda b_, j: (b_, 0, 0, 0)
    qblk = lambda b_, j: (b_ * nblk + j, 0, 0)
    return pl.pallas_call(
        kern,
        grid=(B, nblk),
        in_specs=[
            pl.BlockSpec((1, nblk, LANES, LANES), whole),
            pl.BlockSpec((1, nblk, LANES, LANES), whole),
            pl.BlockSpec((1, nblk, LANES, LANES), whole),
            pl.BlockSpec((1, LANES, DSA_W), qblk),
            pl.BlockSpec((1, LANES, IDX_HEADS * IDX_HD), qblk),
            pl.BlockSpec((1, IDX_HEADS, LANES), qblk),
            pl.BlockSpec((1, LANES, DSA_W), qblk),
        ],
        out_specs=pl.BlockSpec((1, LANES, DSA_W), qblk),
        out_shape=jax.ShapeDtypeStruct((B * nblk, LANES, DSA_W), BF16),
        scratch_shapes=[
            pltpu.VMEM(((nblk + 1) // 2 * 2 * LANES, LANES), F32),
            pltpu.VMEM((DSA_HEADS * LANES, LANES), BF16),
            pltpu.VMEM((IDX_HEADS * LANES, LANES), BF16),
            pltpu.VMEM((1, IDX_HEADS * LANES), F32),
            pltpu.VMEM((1, DSA_HEADS * LANES), F32),
            pltpu.VMEM((DSA_KV_HEADS, DSA_HD + ONES_ROWS, DSA_GROUP * LANES), F32),
            pltpu.VMEM(((nblk + 1) // 2, 32, SUBLANES, LANES), I32),
            pltpu.VMEM((1, LANES), F32),
            pltpu.VMEM((1, LANES), I32),
        ],
        compiler_params=pltpu.CompilerParams(vmem_limit_bytes=VMEM_LIMIT),
        name="dsa",
    )(ik2.reshape(B, nblk, LANES, LANES), ak.reshape(B, nblk, LANES, LANES),
      avt.reshape(B, nblk, LANES, LANES), aq.reshape(B * nblk, LANES, DSA_W),
      iq.reshape(B * nblk, LANES, IDX_HEADS * IDX_HD), iwt, az.reshape(B * nblk, LANES, DSA_W))


def _out_kernel(mg_ref, md_ref, h_ref, w_ref, g_ref, b_ref, o_ref, *, alpha, nb):
    rows = nb * LANES
    y = (_dot(mg_ref[...].reshape(rows, GLA_W), w_ref[:GLA_W, :])
         + _dot(md_ref[...].reshape(rows, DSA_W), w_ref[GLA_W:, :]))
    u = alpha * h_ref[...].reshape(rows, D_MODEL) + y
    o_ref[...] = _layer_norm_rows(u, g_ref[...], b_ref[...]).reshape(nb, LANES, D_MODEL)


def _out(mg, md, h3, w, g, b, B, nblk, alpha, drop_first):
    if drop_first:
        nb, grid, nout = 1, (B, nblk - 1), B * (nblk - 1)
        src = lambda b_, j: (b_ * nblk + j + 1, 0, 0)
        dst = lambda b_, j: (b_ * (nblk - 1) + j, 0, 0)
        const = lambda b_, j: (0, 0)
    else:
        nb = OUT_BLOCKS
        grid, nout = (B * nblk // nb,), B * nblk
        src = dst = lambda i: (i, 0, 0)
        const = lambda i: (0, 0)
    kern = functools.partial(_out_kernel, alpha=alpha, nb=nb)
    return pl.pallas_call(
        kern,
        grid=grid,
        in_specs=[
            pl.BlockSpec((nb, LANES, GLA_W), src),
            pl.BlockSpec((nb, LANES, DSA_W), src),
            pl.BlockSpec((nb, LANES, D_MODEL), src),
            pl.BlockSpec((MIX_W, D_MODEL), const),
            pl.BlockSpec((1, D_MODEL), const),
            pl.BlockSpec((1, D_MODEL), const),
        ],
        out_specs=pl.BlockSpec((nb, LANES, D_MODEL), dst),
        out_shape=jax.ShapeDtypeStruct((nout, LANES, D_MODEL), F32),
        name="out_proj_ln",
    )(mg, md, h3, w, g, b)


def _pack_w_in(w):
    splits = (256, 256, 512, GLA_RANK, 512, 512, 128, 128, 512, IDX_HD, IDX_HEADS, 512)
    offs = np.cumsum((0,) + splits)
    gq, gk, gv, glr, gz, aq, ak, av, iq, ik, iw, az = [w[:, offs[i]:offs[i + 1]] for i in range(12)]

    def perm_heads(a):
        return a.reshape(D_MODEL, DSA_HEADS, DSA_HD)[:, HEAD_PERM, :].reshape(D_MODEL, DSA_W)

    pad = jnp.zeros((D_MODEL, LANES - IDX_HD - GLA_RANK - IDX_HEADS), w.dtype)
    packed = jnp.concatenate(
        [gq * (GLA_DK ** -0.5), gk, gv, gz, perm_heads(aq) * (DSA_HD ** -0.5), ak, av,
         iq * (IDX_HD ** -0.5), az, ik, glr, iw, pad], axis=1)
    return packed.astype(BF16)


def _rope_lane_tables(B, LP):
    inv = ROPE_THETA ** (-jnp.arange(0, ROPE_DIM, 2, dtype=F32) / ROPE_DIM)
    pos = (jnp.arange(LP, dtype=F32) - PAD)[:, None]
    ang = pos * inv[None, :]
    cos, sin = jnp.cos(ang), jnp.sin(ang)
    ones = jnp.ones((LP, DSA_HD - ROPE_DIM), F32)
    zeros = jnp.zeros((LP, DSA_HD - ROPE_DIM), F32)
    zh = jnp.zeros((LP, ROPE_HALF), F32)
    c = jnp.concatenate([cos, cos, ones], axis=1)
    sa = jnp.concatenate([zh, sin, zeros], axis=1)
    sb = jnp.concatenate([-sin, zh, zeros], axis=1)
    tile = lambda t: jnp.tile(jnp.concatenate([t, t], axis=1), (B, 1))
    return tile(c), tile(sa), tile(sb)


def kernel(x, meta_tokens, ln_in_g, ln_in_b, w_in, gla_wg2, gla_bg, gla_norm_g, idx_k_g, idx_k_b,
           w_out, ln_g, ln_b):
    B, S, D = x.shape
    depth = w_in.shape[0]
    nblk = S // LANES + 1
    LP = nblk * LANES
    R = B * LP
    topk = min(TOPK_MAX, S // 4)
    alpha = (2.0 * depth) ** 0.25
    tm = PROJ_ROWS

    meta_pad = jnp.concatenate([jnp.zeros((PAD, D), x.dtype), meta_tokens.astype(x.dtype)], axis=0)
    h = _embed(x, meta_pad, ln_in_g.reshape(1, D), ln_in_b.reshape(1, D))
    cos_t, sa_t, sb_t = _rope_lane_tables(B, LP)

    for i in range(depth):
        w = _pack_w_in(w_in[i])
        wg2 = jnp.zeros((LANES, 256), F32).at[MISC_GLR:MISC_GLR + GLA_RANK].set(gla_wg2[i])
        wg2h = wg2.astype(BF16)
        wg2l = (wg2 - wg2h.astype(F32)).astype(BF16)
        ikg = jnp.zeros((1, LANES), F32).at[0, :IDX_HD].set(idx_k_g[i])
        ikb = jnp.zeros((1, LANES), F32).at[0, :IDX_HD].set(idx_k_b[i])
        (gq, gk, gv, glog, gz, aq, ak, avt, iq, ik2, iwt, az) = _proj(
            h.reshape(R, D), w, wg2h, wg2l, gla_bg[i].reshape(1, 256), ikg, ikb,
            cos_t, sa_t, sb_t, tm)
        mix_gla = _gla(gq, gk, glog, gv, gz, gla_norm_g[i].reshape(1, GLA_DV), B, LP)
        mix_dsa = _dsa(ik2, ak, avt, aq, iq, iwt, az, B, nblk, topk)
        h = _out(mix_gla.reshape(B * nblk, LANES, GLA_W), mix_dsa, h,
                 w_out[i].astype(BF16), ln_g[i].reshape(1, D), ln_b[i].reshape(1, D),
                 B, nblk, alpha, drop_first=(i == depth - 1))
    return h.reshape(B, S, D)
```

```python
import functools

import numpy as np
import jax
import jax.numpy as jnp
from jax import lax
from jax.experimental import pallas as pl
from jax.experimental.pallas import tpu as pltpu

F32 = jnp.float32
BF16 = jnp.bfloat16
I32 = jnp.int32

D_MODEL = 1024
N_META = 16
ROPE_THETA = 500000.0
LN_EPS = 1e-5
GLA_HEADS = 4
GLA_DK = 64
GLA_DV = 128
GLA_RANK = 16
GLA_TAU = 16.0
GLA_CHUNK = 64
GLA_W = GLA_HEADS * GLA_DV
DSA_HEADS = 8
DSA_KV_HEADS = 2
DSA_GROUP = DSA_HEADS // DSA_KV_HEADS
DSA_HD = 64
DSA_W = DSA_HEADS * DSA_HD
IDX_HEADS = 8
IDX_HD = 64
TOPK_MAX = 256
ROPE_DIM = DSA_HD // 4
ROPE_HALF = ROPE_DIM // 2
MIX_W = GLA_W + DSA_W

LANES = 128
SUBLANES = 8
ONES_ROWS = 16
PAD = LANES - N_META
INT_MIN = -(2 ** 31)
M_INIT = -(2.0 ** 100)
VMEM_LIMIT = 48 * 1024 * 1024
PROJ_ROWS = 512
OUT_BLOCKS = 4

O_GQ, O_GK, O_GV, O_GZ = 0, 256, 512, 1024
O_AQ, O_AK, O_AV, O_IQ, O_AZ, O_MISC = 1536, 2048, 2176, 2304, 2816, 3328
W_PACKED = 3456
MISC_GLR = IDX_HD
MISC_IW = IDX_HD + GLA_RANK
HEAD_PERM = tuple(m + DSA_GROUP * r for m in range(DSA_GROUP) for r in range(2))


def _dot(a, b):
    return jnp.dot(a, b, preferred_element_type=F32)


def _dot_nt(a, b):
    return lax.dot_general(a, b, (((1,), (1,)), ((), ())), preferred_element_type=F32)


def _dot_tn(a, b):
    return lax.dot_general(a, b, (((0,), (0,)), ((), ())), preferred_element_type=F32)


def _layer_norm_rows(u, g, b):
    mu = jnp.mean(u, axis=-1, keepdims=True)
    d = u - mu
    var = jnp.mean(d * d, axis=-1, keepdims=True)
    return d * lax.rsqrt(var + LN_EPS) * g + b


def _embed_kernel(x_ref, meta_ref, g_ref, b_ref, o_ref):
    j = pl.program_id(1)
    src = jnp.where(j == 0, meta_ref[...], x_ref[0, 0])
    o_ref[0] = _layer_norm_rows(src, g_ref[...], b_ref[...])


def _embed(x, meta_pad, g, b):
    B, S, D = x.shape
    nblk = S // LANES + 1
    x4 = x.reshape(B, S // LANES, LANES, D)
    return pl.pallas_call(
        _embed_kernel,
        grid=(B, nblk),
        in_specs=[
            pl.BlockSpec((1, 1, LANES, D), lambda b_, j: (b_, jnp.maximum(j - 1, 0), 0, 0)),
            pl.BlockSpec((LANES, D), lambda b_, j: (0, 0)),
            pl.BlockSpec((1, D), lambda b_, j: (0, 0)),
            pl.BlockSpec((1, D), lambda b_, j: (0, 0)),
        ],
        out_specs=pl.BlockSpec((1, LANES, D), lambda b_, j: (b_ * nblk + j, 0, 0)),
        out_shape=jax.ShapeDtypeStruct((B * nblk, LANES, D), F32),
        name="embed_ln",
    )(x4, meta_pad, g, b)


def _rope_slab(x, c, sa, sb):
    return x * c + pltpu.roll(x, ROPE_HALF, 1) * sa + pltpu.roll(x, LANES - ROPE_HALF, 1) * sb


def _proj_kernel(h_ref, w_ref, wg2h_ref, wg2l_ref, bg_ref, ikg_ref, ikb_ref,
                 cos_ref, sa_ref, sb_ref,
                 gq_ref, gk_ref, gv_ref, glog_ref, gz_ref, aq_ref, ak_ref, avt_ref,
                 iq_ref, ik_ref, iwt_ref, az_ref, *, tm):
    hb = h_ref[...].astype(BF16)

    def seg(o, w):
        return _dot(hb, w_ref[:, o:o + w])

    c, sa, sb = cos_ref[...], sa_ref[...], sb_ref[...]
    misc = seg(O_MISC, LANES)
    av = seg(O_AV, LANES)
    ak = seg(O_AK, LANES)
    aq = seg(O_AQ, 512)

    misc_t_scale = IDX_HEADS ** -0.5
    for r in range(tm // LANES):
        rows = slice(r * LANES, (r + 1) * LANES)
        avt_ref[r] = av[rows, :].T.astype(BF16)
        iwt_ref[r] = misc[rows, :].T[MISC_IW:MISC_IW + IDX_HEADS, :] * misc_t_scale
    ak_ref[...] = _rope_slab(ak, c, sa, sb).astype(BF16)
    iq = seg(O_IQ, 512)

    m_hi = misc.astype(BF16)
    m_lo = (misc - m_hi.astype(F32)).astype(BF16)
    xg = (_dot(m_hi, wg2h_ref[...]) + _dot(m_lo, wg2h_ref[...]) + _dot(m_hi, wg2l_ref[...])
          + bg_ref[...])
    gq_ref[...] = seg(O_GQ, 256)
    glog_ref[...] = (jnp.minimum(xg, 0.0) - jnp.log1p(jnp.exp(-jnp.abs(xg)))) * (1.0 / GLA_TAU)

    lane = lax.broadcasted_iota(I32, (1, LANES), 1)
    is_key = lane < IDX_HD
    mu = jnp.sum(jnp.where(is_key, misc, 0.0), axis=-1, keepdims=True) * (1.0 / IDX_HD)
    d = jnp.where(is_key, misc - mu, 0.0)
    var = jnp.sum(d * d, axis=-1, keepdims=True) * (1.0 / IDX_HD)
    ikn = d * lax.rsqrt(var + LN_EPS) * ikg_ref[...] + ikb_ref[...]
    ikr = _rope_slab(ikn, c, sa, sb)
    gk_ref[...] = seg(O_GK, 256)
    ik_ref[...] = jnp.where(is_key, ikr, pltpu.roll(ikr, IDX_HD, 1)).astype(BF16)

    gv = seg(O_GV, 512)
    for m in range(4):
        sl = slice(m * LANES, (m + 1) * LANES)
        aq_ref[:, sl] = _rope_slab(aq[:, sl], c, sa, sb).astype(BF16)
    gv_ref[...] = gv.astype(BF16)
    gz = seg(O_GZ, 512)
    for m in range(4):
        sl = slice(m * LANES, (m + 1) * LANES)
        iq_ref[:, sl] = _rope_slab(iq[:, sl], c, sa, sb).astype(BF16)
    gz_ref[...] = gz
    az_ref[...] = seg(O_AZ, 512)


def _proj(h2, w, wg2h, wg2l, bg, ikg, ikb, cos_t, sa_t, sb_t, tm):
    R = h2.shape[0]
    nt = R // tm
    nsub = tm // LANES
    row = lambda i: (i, 0)
    const = lambda i: (0, 0)
    kern = functools.partial(_proj_kernel, tm=tm)
    out_shape = (
        jax.ShapeDtypeStruct((R, 256), F32),
        jax.ShapeDtypeStruct((R, 256), F32),
        jax.ShapeDtypeStruct((R, 512), BF16),
        jax.ShapeDtypeStruct((R, 256), F32),
        jax.ShapeDtypeStruct((R, 512), F32),
        jax.ShapeDtypeStruct((R, 512), BF16),
        jax.ShapeDtypeStruct((R, LANES), BF16),
        jax.ShapeDtypeStruct((R // LANES, LANES, LANES), BF16),
        jax.ShapeDtypeStruct((R, 512), BF16),
        jax.ShapeDtypeStruct((R, LANES), BF16),
        jax.ShapeDtypeStruct((R // LANES, IDX_HEADS, LANES), F32),
        jax.ShapeDtypeStruct((R, 512), F32),
    )
    out_specs = (
        pl.BlockSpec((tm, 256), row),
        pl.BlockSpec((tm, 256), row),
        pl.BlockSpec((tm, 512), row),
        pl.BlockSpec((tm, 256), row),
        pl.BlockSpec((tm, 512), row),
        pl.BlockSpec((tm, 512), row),
        pl.BlockSpec((tm, LANES), row),
        pl.BlockSpec((nsub, LANES, LANES), lambda i: (i, 0, 0)),
        pl.BlockSpec((tm, 512), row),
        pl.BlockSpec((tm, LANES), row),
        pl.BlockSpec((nsub, IDX_HEADS, LANES), lambda i: (i, 0, 0)),
        pl.BlockSpec((tm, 512), row),
    )
    in_specs = [
        pl.BlockSpec((tm, D_MODEL), row),
        pl.BlockSpec((D_MODEL, W_PACKED), const),
        pl.BlockSpec((LANES, 256), const),
        pl.BlockSpec((LANES, 256), const),
        pl.BlockSpec((1, 256), const),
        pl.BlockSpec((1, LANES), const),
        pl.BlockSpec((1, LANES), const),
        pl.BlockSpec((tm, LANES), row),
        pl.BlockSpec((tm, LANES), row),
        pl.BlockSpec((tm, LANES), row),
    ]
    return pl.pallas_call(
        kern, grid=(nt,), in_specs=in_specs, out_specs=out_specs, out_shape=out_shape,
        compiler_params=pltpu.CompilerParams(vmem_limit_bytes=VMEM_LIMIT),
        name="in_proj",
    )(h2, w, wg2h, wg2l, bg, ikg, ikb, cos_t, sa_t, sb_t)


GLA_NB = 2
GLA_PIECES = 2


def _gla_kernel(q_ref, k_ref, g_ref, v_ref, z_ref, ng_ref, o_ref, st_ref, *, n_chunks):
    C = GLA_CHUNK
    piece = pl.program_id(1)

    @pl.when(piece == 0)
    def _():
        st_ref[...] = jnp.zeros_like(st_ref)

    lane = lax.broadcasted_iota(I32, (1, LANES), 1)
    head_lanes = (lane < GLA_DK, lane >= GLA_DK)
    rr = lax.broadcasted_iota(I32, (C, C), 0)
    cc = lax.broadcasted_iota(I32, (C, C), 1)
    causal = rr >= cc
    tril = causal.astype(BF16)
    row = lax.broadcasted_iota(I32, (C, 1), 0)
    ng = ng_ref[...]
    row0 = piece * (n_chunks * C)

    def body(ci, carry):
        r0 = pl.multiple_of(ci * C, C)
        rows = pl.ds(r0, C)
        valid = (row0 + r0 + row) >= PAD
        pairs = [(bi, pr) for bi in range(GLA_NB) for pr in range(GLA_HEADS // 2)]
        bs = []
        for bi, pr in pairs:
            g = g_ref[bi, rows, pr * LANES:(pr + 1) * LANES]
            g_hi = g.astype(BF16)
            g_lo = (g - g_hi.astype(F32)).astype(BF16)
            bs.append(_dot(tril, g_hi) + _dot(tril, g_lo))
        ops = []
        for (bi, pr), b in zip(pairs, bs):
            pl_ = slice(pr * LANES, (pr + 1) * LANES)
            q = q_ref[bi, rows, pl_]
            k = jnp.where(valid, k_ref[bi, rows, pl_], 0.0)
            b_mid = b[C // 2 - 1:C // 2, :]
            b_last = b[C - 1:C, :]
            q_in = (q * jnp.exp(b - b_mid)).astype(BF16)
            k_in = (k * jnp.exp(b_mid - b)).astype(BF16)
            q_st = (q * jnp.exp(b)).astype(BF16)
            k_st = (k * jnp.exp(b_last - b)).astype(BF16)
            ops.append((q_in, k_in, q_st, k_st, jnp.exp(b_last)))
        heads = [(bi, pr, h) for bi, pr in pairs for h in range(2)]
        zero = jnp.zeros((C, LANES), BF16)
        part = []
        for n, (bi, pr, h) in enumerate(heads):
            q_in, k_in, q_st, k_st, decay = ops[n // 2]
            hl = head_lanes[h]
            head = 2 * pr + h
            cols = slice(head * GLA_DV, (head + 1) * GLA_DV)
            v = jnp.where(valid, v_ref[bi, rows, cols], jnp.zeros((), BF16))
            st = st_ref[bi, head]
            a = _dot_nt(jnp.where(hl, q_in, zero), k_in)
            o_st = _dot_nt(jnp.where(hl, q_st, zero), st.astype(BF16))
            st_ref[bi, head] = st * decay + _dot_tn(v, jnp.where(hl, k_st, zero))
            part.append((a, o_st, v, cols))
        for (bi, pr, h), (a, o_st, v, cols) in zip(heads, part):
            o = _dot(jnp.where(causal, a, 0.0).astype(BF16), v) + o_st
            on = o * lax.rsqrt(jnp.mean(o * o, axis=-1, keepdims=True) + LN_EPS) * ng
            z = z_ref[bi, rows, cols]
            o_ref[bi, rows, cols] = (on * (z * jax.nn.sigmoid(z))).astype(BF16)
        return carry

    lax.fori_loop(0, n_chunks, body, 0)


def _gla(gq, gk, glog, gv, gz, ng, B, LP):
    rows = LP // GLA_PIECES
    kern = functools.partial(_gla_kernel, n_chunks=rows // GLA_CHUNK)
    blk = lambda b_, p: (b_, p, 0)
    return pl.pallas_call(
        kern,
        grid=(B // GLA_NB, GLA_PIECES),
        in_specs=[
            pl.BlockSpec((GLA_NB, rows, 256), blk),
            pl.BlockSpec((GLA_NB, rows, 256), blk),
            pl.BlockSpec((GLA_NB, rows, 256), blk),
            pl.BlockSpec((GLA_NB, rows, GLA_W), blk),
            pl.BlockSpec((GLA_NB, rows, GLA_W), blk),
            pl.BlockSpec((1, GLA_DV), lambda b_, p: (0, 0)),
        ],
        out_specs=pl.BlockSpec((GLA_NB, rows, GLA_W), blk),
        out_shape=jax.ShapeDtypeStruct((B, LP, GLA_W), BF16),
        scratch_shapes=[pltpu.VMEM((GLA_NB, GLA_HEADS, GLA_DV, LANES), F32)],
        compiler_params=pltpu.CompilerParams(vmem_limit_bytes=VMEM_LIMIT),
        name="gla",
    )(gq.reshape(B, LP, 256), gk.reshape(B, LP, 256), glog.reshape(B, LP, 256),
      gv.reshape(B, LP, GLA_W), gz.reshape(B, LP, GLA_W), ng)


def _dsa_kernel(ik_ref, ak_ref, avt_ref, aq_ref, iq_ref, iwt_ref, az_ref, o_ref,
                sc_ref, qz_ref, iqz_ref, w_ref, m_ref, acc_ref, plane_ref, thr_ref, cnt_ref,
                *, topk, nblk):
    j = pl.program_id(1)
    npair = (j + 2) // 2
    ntrip = (npair + 1) // 2
    KP = 2 * LANES

    def pair_rows(ref, i, axis):
        kb0 = jnp.minimum(2 * i, nblk - 1)
        kb1 = jnp.minimum(2 * i + 1, nblk - 1)
        return jnp.concatenate([ref[0, kb0], ref[0, kb1]], axis=axis)

    lane = lax.broadcasted_iota(I32, (1, LANES), 1)
    low = lane < DSA_HD
    zero_b = jnp.zeros((LANES, LANES), BF16)
    for m in range(4):
        sl = slice(m * LANES, (m + 1) * LANES)
        qs = aq_ref[0, :, sl]
        qz_ref[m * LANES:(m + 1) * LANES, :] = jnp.where(low, qs, zero_b)
        qz_ref[(m + 4) * LANES:(m + 5) * LANES, :] = jnp.where(low, zero_b, qs)
        iqs = iq_ref[0, :, sl]
        iqz_ref[(2 * m) * LANES:(2 * m + 1) * LANES, :] = jnp.where(low, iqs, zero_b)
        iqz_ref[(2 * m + 1) * LANES:(2 * m + 2) * LANES, :] = jnp.where(low, zero_b, iqs)
    for h in range(IDX_HEADS):
        w_ref[:, h * LANES:(h + 1) * LANES] = iwt_ref[0, h:h + 1, :]
    t_pos = j * LANES + lane
    s_loc = lax.broadcasted_iota(I32, (KP, 1), 0)

    def pair_ds(i):
        return pl.ds(pl.multiple_of(i * KP, KP), KP)

    def score_trip(u, carry):
        pairs = (2 * u, 2 * u + 1)
        dots = [_dot_nt(pair_rows(ik_ref, i, 0), iqz_ref[...]) for i in pairs]
        for i, d in zip(pairs, dots):
            d = jnp.maximum(d, 0.0) * w_ref[...]
            acc = d[:, :LANES]
            for h in range(1, IDX_HEADS):
                acc = acc + d[:, h * LANES:(h + 1) * LANES]
            s_pos = i * KP + s_loc
            valid = (s_pos <= t_pos) & ((s_pos >= PAD) | (t_pos < PAD))
            sc_ref[pair_ds(i), :] = jnp.where(valid, acc, -jnp.inf)
        return carry

    lax.fori_loop(0, ntrip, score_trip, 0)


    def count(*preds):
        def cnt_block(i, cnts):
            x = sc_ref[pair_ds(i), :]
            out = []
            for pred, cnt in zip(preds, cnts):
                hit = pred(x).astype(I32)
                parts = [hit[r:r + SUBLANES, :] for r in range(0, KP, SUBLANES)]
                while len(parts) > 1:
                    parts = [parts[k] + parts[k + 1] for k in range(0, len(parts), 2)]
                out.append(cnt + parts[0])
            return tuple(out)

        zero = jnp.zeros((SUBLANES, LANES), I32)
        cnts = lax.fori_loop(0, npair, cnt_block, (zero,) * len(preds))
        return tuple(jnp.sum(c, axis=0, keepdims=True) for c in cnts)

    select_all = jnp.float32(-3.0e38)

    @pl.when((pl.program_id(0) == 0) & (j == 0))
    def _():
        plane_ref[...] = jnp.zeros_like(plane_ref)

    def shift_const(x, n):
        return jnp.full(x.shape, n, I32)

    def build_planes(i, carry):
        bits = lax.bitcast_convert_type(sc_ref[pair_ds(i), :], I32)
        u = bits ^ (lax.shift_right_arithmetic(bits, shift_const(bits, 31)) | jnp.int32(INT_MIN))
        a = [u[SUBLANES * v:SUBLANES * (v + 1), :] for v in range(32)]
        step, mask = 16, 0x0000FFFF
        while step:
            for k in range(32):
                if not k & step:
                    t = (a[k] ^ lax.shift_right_logical(a[k + step], shift_const(a[k], step))) & jnp.int32(mask)
                    a[k] = a[k] ^ t
                    a[k + step] = a[k + step] ^ lax.shift_left(t, shift_const(t, step))
            step >>= 1
            mask = (mask ^ (mask << step)) & 0xFFFFFFFF
        for r in range(32):
            plane_ref[i, r] = a[r]
        return carry

    lax.fori_loop(0, npair, build_planes, 0)

    n_pairs_max = plane_ref.shape[0]
    ones_v = jnp.full((SUBLANES, LANES), -1, I32)
    zeros_v = jnp.zeros((SUBLANES, LANES), I32)
    alive0 = tuple(jnp.where(i < npair, ones_v, zeros_v) for i in range(n_pairs_max))

    def radix_step(r, carry):
        alive, k_left, u_thr = carry
        planes = [plane_ref[i, r] for i in range(n_pairs_max)]
        c8 = None
        for i in range(n_pairs_max):
            pc = lax.population_count(alive[i] & planes[i])
            c8 = pc if c8 is None else c8 + pc
        c = jnp.sum(c8, axis=0, keepdims=True)
        take = c >= k_left
        k_left = jnp.where(take, k_left, k_left - c)
        flip = jnp.where(take, jnp.int32(0), jnp.int32(-1))
        alive = tuple(alive[i] & (planes[i] ^ flip) for i in range(n_pairs_max))
        bit = lax.shift_left(jnp.int32(1), 31 - r)
        return alive, k_left, u_thr | jnp.where(take, bit, jnp.int32(0))

    _, _, u_thr = lax.fori_loop(
        0, 32, radix_step,
        (alive0, jnp.full((1, LANES), topk, I32), jnp.zeros((1, LANES), I32)))
    thr_bits = jnp.where(u_thr < 0, u_thr ^ jnp.int32(INT_MIN), ~u_thr)
    thr_fast = lax.bitcast_convert_type(thr_bits, F32)
    thr_fast = jnp.where(thr_fast == -jnp.inf, select_all, thr_fast)

    n_ge, n_gt = count(lambda x: x >= thr_fast, lambda x: x > thr_fast)
    is_all = thr_fast == select_all
    good = is_all | ((n_gt < topk) & (n_ge >= topk))
    thr_ref[...] = thr_fast
    cnt_ref[...] = jnp.where(is_all, 0, n_ge)

    @pl.when(jnp.min(good.astype(I32)) == 0)
    def _():
        def key_to_float(key):
            bits = jnp.where(key < 0, key ^ jnp.int32(0x7FFFFFFF), key)
            return lax.bitcast_convert_type(bits, F32)

        def count_ge(cand_key):
            cand = key_to_float(cand_key)
            return count(lambda x: x >= cand)[0]

        c0 = count_ge(jnp.zeros((1, LANES), I32))
        base = jnp.where(c0 >= topk, jnp.int32(0), jnp.int32(INT_MIN))
        base_cnt = jnp.where(c0 >= topk, c0, 0)

        def bit_step(i, carry):
            base, base_cnt = carry
            cand = base | jnp.left_shift(jnp.int32(1), 30 - i)
            c = count_ge(cand)
            ok = c >= topk
            return jnp.where(ok, cand, base), jnp.where(ok, c, base_cnt)

        base, base_cnt = lax.fori_loop(0, 31, bit_step, (base, base_cnt))
        thr_ref[...] = jnp.where(base == INT_MIN, select_all, key_to_float(base))
        cnt_ref[...] = base_cnt

    thr = thr_ref[...]
    base_cnt = cnt_ref[...]

    @pl.when(jnp.max(base_cnt) > topk)
    def _():
        need = (topk - count(lambda x: x > thr)[0]).astype(F32)
        rr = lax.broadcasted_iota(I32, (KP, KP), 0)
        cc = lax.broadcasted_iota(I32, (KP, KP), 1)
        tri = (rr >= cc).astype(BF16)

        def strike(i, seen):
            x = sc_ref[pair_ds(i), :]
            eq = x == thr
            rank = _dot(tri, jnp.where(eq, 1.0, 0.0).astype(BF16)) + seen
            sc_ref[pair_ds(i), :] = jnp.where(eq & (rank > need), -jnp.inf, x)
            return rank[KP - 1:KP, :]

        lax.fori_loop(0, npair, strike, jnp.zeros((1, LANES), F32))

    m_ref[...] = jnp.full_like(m_ref, M_INIT)
    acc_ref[...] = jnp.zeros_like(acc_ref)
    GW = DSA_GROUP * LANES
    ones_rows = jnp.ones((ONES_ROWS, KP), BF16)

    def attn_trip(u, carry):
        pairs = (2 * u, 2 * u + 1)
        logits = [_dot_nt(pair_rows(ak_ref, i, 0), qz_ref[...]).astype(BF16) for i in pairs]
        for i, s in zip(pairs, logits):
            bias = jnp.where(sc_ref[pair_ds(i), :] >= thr, 0.0, -jnp.inf).astype(BF16)
            sb = s + jnp.concatenate([bias] * DSA_HEADS, axis=1)
            m_old = m_ref[...]
            m_new = jnp.maximum(m_old, jnp.max(sb, axis=0, keepdims=True).astype(F32))
            alpha = jnp.exp(m_old - m_new)
            pb = jnp.exp(sb - m_new.astype(BF16))
            m_ref[...] = m_new
            vt = pair_rows(avt_ref, i, 1)
            for g in range(DSA_KV_HEADS):
                cols = slice(g * GW, (g + 1) * GW)
                vg = jnp.concatenate([vt[g * DSA_HD:(g + 1) * DSA_HD, :], ones_rows], axis=0)
                acc_ref[g] = alpha[:, cols] * acc_ref[g] + _dot(vg, pb[:, cols])
        return carry

    lax.fori_loop(0, ntrip, attn_trip, 0)

    heads = []
    for h in range(DSA_HEADS):
        a = acc_ref[h // DSA_GROUP][:, (h % DSA_GROUP) * LANES:(h % DSA_GROUP + 1) * LANES]
        heads.append(a[:DSA_HD, :] / a[DSA_HD:DSA_HD + 1, :])
    ot = jnp.concatenate(heads, axis=0)
    z = az_ref[0]
    o_ref[0] = (ot.T * (z * jax.nn.sigmoid(z))).astype(BF16)


def _dsa(ik2, ak, avt, aq, iq, iwt, az, B, nblk, topk):
    kern = functools.partial(_dsa_kernel, topk=topk, nblk=nblk)
    whole = lambda b_, j: (b_, 0, 0, 0)
    qblk = lambda b_, j: (b_ * nblk + j, 0, 0)
    return pl.pallas_call(
        kern,
        grid=(B, nblk),
        in_specs=[
            pl.BlockSpec((1, nblk, LANES, LANES), whole),
            pl.BlockSpec((1, nblk, LANES, LANES), whole),
            pl.BlockSpec((1, nblk, LANES, LANES), whole),
            pl.BlockSpec((1, LANES, DSA_W), qblk),
            pl.BlockSpec((1, LANES, IDX_HEADS * IDX_HD), qblk),
            pl.BlockSpec((1, IDX_HEADS, LANES), qblk),
            pl.BlockSpec((1, LANES, DSA_W), qblk),
        ],
        out_specs=pl.BlockSpec((1, LANES, DSA_W), qblk),
        out_shape=jax.ShapeDtypeStruct((B * nblk, LANES, DSA_W), BF16),
        scratch_shapes=[
            pltpu.VMEM(((nblk + 3) // 4 * 4 * LANES, LANES), F32),
            pltpu.VMEM((DSA_HEADS * LANES, LANES), BF16),
            pltpu.VMEM((IDX_HEADS * LANES, LANES), BF16),
            pltpu.VMEM((1, IDX_HEADS * LANES), F32),
            pltpu.VMEM((1, DSA_HEADS * LANES), F32),
            pltpu.VMEM((DSA_KV_HEADS, DSA_HD + ONES_ROWS, DSA_GROUP * LANES), F32),
            pltpu.VMEM(((nblk + 1) // 2, 32, SUBLANES, LANES), I32),
            pltpu.VMEM((1, LANES), F32),
            pltpu.VMEM((1, LANES), I32),
        ],
        compiler_params=pltpu.CompilerParams(vmem_limit_bytes=VMEM_LIMIT),
        name="dsa",
    )(ik2.reshape(B, nblk, LANES, LANES), ak.reshape(B, nblk, LANES, LANES),
      avt.reshape(B, nblk, LANES, LANES), aq.reshape(B * nblk, LANES, DSA_W),
      iq.reshape(B * nblk, LANES, IDX_HEADS * IDX_HD), iwt, az.reshape(B * nblk, LANES, DSA_W))


def _out_kernel(mg_ref, md_ref, h_ref, w_ref, g_ref, b_ref, o_ref, *, alpha, nb):
    rows = nb * LANES
    y = (_dot(mg_ref[...].reshape(rows, GLA_W), w_ref[:GLA_W, :])
         + _dot(md_ref[...].reshape(rows, DSA_W), w_ref[GLA_W:, :]))
    u = alpha * h_ref[...].reshape(rows, D_MODEL) + y
    o_ref[...] = _layer_norm_rows(u, g_ref[...], b_ref[...]).reshape(nb, LANES, D_MODEL)


def _out(mg, md, h3, w, g, b, B, nblk, alpha, drop_first):
    if drop_first:
        nb, grid, nout = 1, (B, nblk - 1), B * (nblk - 1)
        src = lambda b_, j: (b_ * nblk + j + 1, 0, 0)
        dst = lambda b_, j: (b_ * (nblk - 1) + j, 0, 0)
        const = lambda b_, j: (0, 0)
    else:
        nb = OUT_BLOCKS
        grid, nout = (B * nblk // nb,), B * nblk
        src = dst = lambda i: (i, 0, 0)
        const = lambda i: (0, 0)
    kern = functools.partial(_out_kernel, alpha=alpha, nb=nb)
    return pl.pallas_call(
        kern,
        grid=grid,
        in_specs=[
            pl.BlockSpec((nb, LANES, GLA_W), src),
            pl.BlockSpec((nb, LANES, DSA_W), src),
            pl.BlockSpec((nb, LANES, D_MODEL), src),
            pl.BlockSpec((MIX_W, D_MODEL), const),
            pl.BlockSpec((1, D_MODEL), const),
            pl.BlockSpec((1, D_MODEL), const),
        ],
        out_specs=pl.BlockSpec((nb, LANES, D_MODEL), dst),
        out_shape=jax.ShapeDtypeStruct((nout, LANES, D_MODEL), F32),
        name="out_proj_ln",
    )(mg, md, h3, w, g, b)


def _pack_w_in(w):
    splits = (256, 256, 512, GLA_RANK, 512, 512, 128, 128, 512, IDX_HD, IDX_HEADS, 512)
    offs = np.cumsum((0,) + splits)
    gq, gk, gv, glr, gz, aq, ak, av, iq, ik, iw, az = [w[:, offs[i]:offs[i + 1]] for i in range(12)]

    def perm_heads(a):
        return a.reshape(D_MODEL, DSA_HEADS, DSA_HD)[:, HEAD_PERM, :].reshape(D_MODEL, DSA_W)

    pad = jnp.zeros((D_MODEL, LANES - IDX_HD - GLA_RANK - IDX_HEADS), w.dtype)
    packed = jnp.concatenate(
        [gq * (GLA_DK ** -0.5), gk, gv, gz, perm_heads(aq) * (DSA_HD ** -0.5), ak, av,
         iq * (IDX_HD ** -0.5), az, ik, glr, iw, pad], axis=1)
    return packed.astype(BF16)


def _rope_lane_tables(B, LP):
    inv = ROPE_THETA ** (-jnp.arange(0, ROPE_DIM, 2, dtype=F32) / ROPE_DIM)
    pos = (jnp.arange(LP, dtype=F32) - PAD)[:, None]
    ang = pos * inv[None, :]
    cos, sin = jnp.cos(ang), jnp.sin(ang)
    ones = jnp.ones((LP, DSA_HD - ROPE_DIM), F32)
    zeros = jnp.zeros((LP, DSA_HD - ROPE_DIM), F32)
    zh = jnp.zeros((LP, ROPE_HALF), F32)
    c = jnp.concatenate([cos, cos, ones], axis=1)
    sa = jnp.concatenate([zh, sin, zeros], axis=1)
    sb = jnp.concatenate([-sin, zh, zeros], axis=1)
    tile = lambda t: jnp.tile(jnp.concatenate([t, t], axis=1), (B, 1))
    return tile(c), tile(sa), tile(sb)


def kernel(x, meta_tokens, ln_in_g, ln_in_b, w_in, gla_wg2, gla_bg, gla_norm_g, idx_k_g, idx_k_b,
           w_out, ln_g, ln_b):
    B, S, D = x.shape
    depth = w_in.shape[0]
    nblk = S // LANES + 1
    LP = nblk * LANES
    R = B * LP
    topk = min(TOPK_MAX, S // 4)
    alpha = (2.0 * depth) ** 0.25
    tm = PROJ_ROWS

    meta_pad = jnp.concatenate([jnp.zeros((PAD, D), x.dtype), meta_tokens.astype(x.dtype)], axis=0)
    h = _embed(x, meta_pad, ln_in_g.reshape(1, D), ln_in_b.reshape(1, D))
    cos_t, sa_t, sb_t = _rope_lane_tables(B, LP)

    for i in range(depth):
        w = _pack_w_in(w_in[i])
        wg2 = jnp.zeros((LANES, 256), F32).at[MISC_GLR:MISC_GLR + GLA_RANK].set(gla_wg2[i])
        wg2h = wg2.astype(BF16)
        wg2l = (wg2 - wg2h.astype(F32)).astype(BF16)
        ikg = jnp.zeros((1, LANES), F32).at[0, :IDX_HD].set(idx_k_g[i])
        ikb = jnp.zeros((1, LANES), F32).at[0, :IDX_HD].set(idx_k_b[i])
        (gq, gk, gv, glog, gz, aq, ak, avt, iq, ik2, iwt, az) = _proj(
            h.reshape(R, D), w, wg2h, wg2l, gla_bg[i].reshape(1, 256), ikg, ikb,
            cos_t, sa_t, sb_t, tm)
        mix_gla = _gla(gq, gk, glog, gv, gz, gla_norm_g[i].reshape(1, GLA_DV), B, LP)
        mix_dsa = _dsa(ik2, ak, avt, aq, iq, iwt, az, B, nblk, topk)
        h = _out(mix_gla.reshape(B * nblk, LANES, GLA_W), mix_dsa, h,
                 w_out[i].astype(BF16), ln_g[i].reshape(1, D), ln_b[i].reshape(1, D),
                 B, nblk, alpha, drop_first=(i == depth - 1))
    return h.reshape(B, S, D)
```

```python
import functools
import math

import numpy as np
import jax
import jax.numpy as jnp
from jax import lax
from jax.experimental import pallas as pl
from jax.experimental.pallas import tpu as pltpu

F32 = jnp.float32
BF16 = jnp.bfloat16
I32 = jnp.int32

D_MODEL = 1024
N_META = 16
ROPE_THETA = 500000.0
LN_EPS = 1e-5
GLA_HEADS = 4
GLA_DK = 64
GLA_DV = 128
GLA_RANK = 16
GLA_TAU = 16.0
GLA_CHUNK = 64
GLA_W = GLA_HEADS * GLA_DV
DSA_HEADS = 8
DSA_KV_HEADS = 2
DSA_GROUP = DSA_HEADS // DSA_KV_HEADS
DSA_HD = 64
DSA_W = DSA_HEADS * DSA_HD
IDX_HEADS = 8
IDX_HD = 64
TOPK_MAX = 256
ROPE_DIM = DSA_HD // 4
ROPE_HALF = ROPE_DIM // 2
MIX_W = GLA_W + DSA_W

LANES = 128
SUBLANES = 8
ONES_ROWS = 16
PAD = LANES - N_META
INT_MIN = -(2 ** 31)
M_INIT = -(2.0 ** 100)
VMEM_LIMIT = 48 * 1024 * 1024
PROJ_ROWS = 512
OUT_ROWS = 512

O_GQ, O_GK, O_GV, O_GZ = 0, 256, 512, 1024
O_AQ, O_AK, O_AV, O_IQ, O_AZ, O_MISC = 1536, 2048, 2176, 2304, 2816, 3328
W_PACKED = 3456
MISC_GLR = IDX_HD
MISC_IW = IDX_HD + GLA_RANK
HEAD_PERM = tuple(m + DSA_GROUP * r for m in range(DSA_GROUP) for r in range(2))


def _dot(a, b):
    return jnp.dot(a, b, preferred_element_type=F32)


def _dot_nt(a, b):
    return lax.dot_general(a, b, (((1,), (1,)), ((), ())), preferred_element_type=F32)


def _dot_tn(a, b):
    return lax.dot_general(a, b, (((0,), (0,)), ((), ())), preferred_element_type=F32)


def _layer_norm_rows(u, g, b):
    mu = jnp.mean(u, axis=-1, keepdims=True)
    d = u - mu
    var = jnp.mean(d * d, axis=-1, keepdims=True)
    return d * lax.rsqrt(var + LN_EPS) * g + b


def _embed_kernel(x_ref, meta_ref, g_ref, b_ref, o_ref):
    g, b = g_ref[...], b_ref[...]
    o_ref[0] = _layer_norm_rows(meta_ref[...], g, b)
    for c in range(x_ref.shape[1]):
        o_ref[c + 1] = _layer_norm_rows(x_ref[0, c], g, b)


def _embed(x, meta_pad, g, b):
    B, S, D = x.shape
    nblk = S // LANES + 1
    x4 = x.reshape(B, S // LANES, LANES, D)
    return pl.pallas_call(
        _embed_kernel,
        grid=(B,),
        in_specs=[
            pl.BlockSpec((1, S // LANES, LANES, D), lambda b_: (b_, 0, 0, 0)),
            pl.BlockSpec((LANES, D), lambda b_: (0, 0)),
            pl.BlockSpec((1, D), lambda b_: (0, 0)),
            pl.BlockSpec((1, D), lambda b_: (0, 0)),
        ],
        out_specs=pl.BlockSpec((nblk, LANES, D), lambda b_: (b_, 0, 0)),
        out_shape=jax.ShapeDtypeStruct((B * nblk, LANES, D), F32),
        compiler_params=pltpu.CompilerParams(vmem_limit_bytes=VMEM_LIMIT),
        name="embed_ln",
    )(x4, meta_pad, g, b)


def _rope_slab(x, c, sa, sb):
    return x * c + pltpu.roll(x, ROPE_HALF, 1) * sa + pltpu.roll(x, LANES - ROPE_HALF, 1) * sb


def _proj_kernel(h_ref, w_ref, wg2h_ref, wg2l_ref, bg_ref, ikg_ref, ikb_ref,
                 cos_ref, sa_ref, sb_ref,
                 gq_ref, gk_ref, gv_ref, glog_ref, gz_ref, aq_ref, ak_ref, avt_ref,
                 iq_ref, ik_ref, iwt_ref, az_ref, *, tm):
    hb = h_ref[...].astype(BF16)

    def seg(o, w):
        return _dot(hb, w_ref[:, o:o + w])

    c, sa, sb = cos_ref[...], sa_ref[...], sb_ref[...]
    misc = seg(O_MISC, LANES)
    av = seg(O_AV, LANES)
    ak = seg(O_AK, LANES)
    aq = seg(O_AQ, 512)

    misc_t_scale = IDX_HEADS ** -0.5
    for r in range(tm // LANES):
        rows = slice(r * LANES, (r + 1) * LANES)
        avt_ref[r] = av[rows, :].T.astype(BF16)
        iwt_ref[r] = misc[rows, :].T[MISC_IW:MISC_IW + IDX_HEADS, :] * misc_t_scale
    ak_ref[...] = _rope_slab(ak, c, sa, sb).astype(BF16)
    iq = seg(O_IQ, 512)

    m_hi = misc.astype(BF16)
    m_lo = (misc - m_hi.astype(F32)).astype(BF16)
    xg = (_dot(m_hi, wg2h_ref[...]) + _dot(m_lo, wg2h_ref[...]) + _dot(m_hi, wg2l_ref[...])
          + bg_ref[...])
    gq_ref[...] = seg(O_GQ, 256)
    glog_ref[...] = (jnp.minimum(xg, 0.0) - jnp.log1p(jnp.exp(-jnp.abs(xg)))) * (1.0 / GLA_TAU)

    lane = lax.broadcasted_iota(I32, (1, LANES), 1)
    is_key = lane < IDX_HD
    mu = jnp.sum(jnp.where(is_key, misc, 0.0), axis=-1, keepdims=True) * (1.0 / IDX_HD)
    d = jnp.where(is_key, misc - mu, 0.0)
    var = jnp.sum(d * d, axis=-1, keepdims=True) * (1.0 / IDX_HD)
    ikn = d * lax.rsqrt(var + LN_EPS) * ikg_ref[...] + ikb_ref[...]
    ikr = _rope_slab(ikn, c, sa, sb)
    gk_ref[...] = seg(O_GK, 256)
    ik_ref[...] = jnp.where(is_key, ikr, pltpu.roll(ikr, IDX_HD, 1)).astype(BF16)

    gv = seg(O_GV, 512)
    for m in range(4):
        sl = slice(m * LANES, (m + 1) * LANES)
        aq_ref[:, sl] = _rope_slab(aq[:, sl], c, sa, sb).astype(BF16)
    gv_ref[...] = gv.astype(BF16)
    gz = seg(O_GZ, 512)
    for m in range(4):
        sl = slice(m * LANES, (m + 1) * LANES)
        iq_ref[:, sl] = _rope_slab(iq[:, sl], c, sa, sb).astype(BF16)
    gz_ref[...] = gz
    az_ref[...] = seg(O_AZ, 512)


def _proj(h2, w, wg2h, wg2l, bg, ikg, ikb, cos_t, sa_t, sb_t, tm):
    R = h2.shape[0]
    nt = R // tm
    nsub = tm // LANES
    row = lambda i: (i, 0)
    const = lambda i: (0, 0)
    table_tiles = cos_t.shape[0] // tm
    table_row = lambda i: (i % table_tiles, 0)
    kern = functools.partial(_proj_kernel, tm=tm)
    out_shape = (
        jax.ShapeDtypeStruct((R, 256), F32),
        jax.ShapeDtypeStruct((R, 256), F32),
        jax.ShapeDtypeStruct((R, 512), BF16),
        jax.ShapeDtypeStruct((R, 256), F32),
        jax.ShapeDtypeStruct((R, 512), F32),
        jax.ShapeDtypeStruct((R, 512), BF16),
        jax.ShapeDtypeStruct((R, LANES), BF16),
        jax.ShapeDtypeStruct((R // LANES, LANES, LANES), BF16),
        jax.ShapeDtypeStruct((R, 512), BF16),
        jax.ShapeDtypeStruct((R, LANES), BF16),
        jax.ShapeDtypeStruct((R // LANES, IDX_HEADS, LANES), F32),
        jax.ShapeDtypeStruct((R, 512), F32),
    )
    out_specs = (
        pl.BlockSpec((tm, 256), row),
        pl.BlockSpec((tm, 256), row),
        pl.BlockSpec((tm, 512), row),
        pl.BlockSpec((tm, 256), row),
        pl.BlockSpec((tm, 512), row),
        pl.BlockSpec((tm, 512), row),
        pl.BlockSpec((tm, LANES), row),
        pl.BlockSpec((nsub, LANES, LANES), lambda i: (i, 0, 0)),
        pl.BlockSpec((tm, 512), row),
        pl.BlockSpec((tm, LANES), row),
        pl.BlockSpec((nsub, IDX_HEADS, LANES), lambda i: (i, 0, 0)),
        pl.BlockSpec((tm, 512), row),
    )
    in_specs = [
        pl.BlockSpec((tm, D_MODEL), row),
        pl.BlockSpec((D_MODEL, W_PACKED), const),
        pl.BlockSpec((LANES, 256), const),
        pl.BlockSpec((LANES, 256), const),
        pl.BlockSpec((1, 256), const),
        pl.BlockSpec((1, LANES), const),
        pl.BlockSpec((1, LANES), const),
        pl.BlockSpec((tm, LANES), table_row),
        pl.BlockSpec((tm, LANES), table_row),
        pl.BlockSpec((tm, LANES), table_row),
    ]
    return pl.pallas_call(
        kern, grid=(nt,), in_specs=in_specs, out_specs=out_specs, out_shape=out_shape,
        compiler_params=pltpu.CompilerParams(vmem_limit_bytes=VMEM_LIMIT),
        name="in_proj",
    )(h2, w, wg2h, wg2l, bg, ikg, ikb, cos_t, sa_t, sb_t)


GLA_NB = 2
GLA_PIECES = 2


def _gla_kernel(q_ref, k_ref, g_ref, v_ref, z_ref, ng_ref, o_ref, st_ref, *, n_chunks):
    C = GLA_CHUNK
    piece = pl.program_id(1)

    @pl.when(piece == 0)
    def _():
        st_ref[...] = jnp.zeros_like(st_ref)

    lane = lax.broadcasted_iota(I32, (1, LANES), 1)
    head_lanes = (lane < GLA_DK, lane >= GLA_DK)
    rr = lax.broadcasted_iota(I32, (C, C), 0)
    cc = lax.broadcasted_iota(I32, (C, C), 1)
    causal = rr >= cc
    tril = causal.astype(BF16)
    row = lax.broadcasted_iota(I32, (C, 1), 0)
    ng = ng_ref[...]
    row0 = piece * (n_chunks * C)

    def body(ci, carry):
        r0 = pl.multiple_of(ci * C, C)
        rows = pl.ds(r0, C)
        valid = (row0 + r0 + row) >= PAD
        pairs = [(bi, pr) for bi in range(GLA_NB) for pr in range(GLA_HEADS // 2)]
        bs = []
        for bi, pr in pairs:
            g = g_ref[bi, rows, pr * LANES:(pr + 1) * LANES]
            g_hi = g.astype(BF16)
            g_lo = (g - g_hi.astype(F32)).astype(BF16)
            bs.append(_dot(tril, g_hi) + _dot(tril, g_lo))
        ops = []
        for (bi, pr), b in zip(pairs, bs):
            pl_ = slice(pr * LANES, (pr + 1) * LANES)
            q = q_ref[bi, rows, pl_]
            k = jnp.where(valid, k_ref[bi, rows, pl_], 0.0)
            b_mid = b[C // 2 - 1:C // 2, :]
            b_last = b[C - 1:C, :]
            q_in = (q * jnp.exp(b - b_mid)).astype(BF16)
            k_in = (k * jnp.exp(b_mid - b)).astype(BF16)
            q_st = (q * jnp.exp(b)).astype(BF16)
            k_st = (k * jnp.exp(b_last - b)).astype(BF16)
            ops.append((q_in, k_in, q_st, k_st, jnp.exp(b_last)))
        heads = [(bi, pr, h) for bi, pr in pairs for h in range(2)]
        zero = jnp.zeros((C, LANES), BF16)
        part = []
        for n, (bi, pr, h) in enumerate(heads):
            q_in, k_in, q_st, k_st, decay = ops[n // 2]
            hl = head_lanes[h]
            head = 2 * pr + h
            cols = slice(head * GLA_DV, (head + 1) * GLA_DV)
            v = jnp.where(valid, v_ref[bi, rows, cols], jnp.zeros((), BF16))
            st = st_ref[bi, head]
            a = _dot_nt(jnp.where(hl, q_in, zero), k_in)
            o_st = _dot_nt(jnp.where(hl, q_st, zero), st.astype(BF16))
            st_ref[bi, head] = st * decay + _dot_tn(v, jnp.where(hl, k_st, zero))
            part.append((a, o_st, v, cols))
        for (bi, pr, h), (a, o_st, v, cols) in zip(heads, part):
            o = _dot(jnp.where(causal, a, 0.0).astype(BF16), v) + o_st
            on = o * lax.rsqrt(jnp.mean(o * o, axis=-1, keepdims=True) + LN_EPS) * ng
            z = z_ref[bi, rows, cols]
            o_ref[bi, rows, cols] = (on * (z * jax.nn.sigmoid(z))).astype(BF16)
        return carry

    lax.fori_loop(0, n_chunks, body, 0)


def _gla(gq, gk, glog, gv, gz, ng, B, LP):
    rows = LP // GLA_PIECES
    kern = functools.partial(_gla_kernel, n_chunks=rows // GLA_CHUNK)
    blk = lambda b_, p: (b_, p, 0)
    return pl.pallas_call(
        kern,
        grid=(B // GLA_NB, GLA_PIECES),
        in_specs=[
            pl.BlockSpec((GLA_NB, rows, 256), blk),
            pl.BlockSpec((GLA_NB, rows, 256), blk),
            pl.BlockSpec((GLA_NB, rows, 256), blk),
            pl.BlockSpec((GLA_NB, rows, GLA_W), blk),
            pl.BlockSpec((GLA_NB, rows, GLA_W), blk),
            pl.BlockSpec((1, GLA_DV), lambda b_, p: (0, 0)),
        ],
        out_specs=pl.BlockSpec((GLA_NB, rows, GLA_W), blk),
        out_shape=jax.ShapeDtypeStruct((B, LP, GLA_W), BF16),
        scratch_shapes=[pltpu.VMEM((GLA_NB, GLA_HEADS, GLA_DV, LANES), F32)],
        compiler_params=pltpu.CompilerParams(vmem_limit_bytes=VMEM_LIMIT),
        name="gla",
    )(gq.reshape(B, LP, 256), gk.reshape(B, LP, 256), glog.reshape(B, LP, 256),
      gv.reshape(B, LP, GLA_W), gz.reshape(B, LP, GLA_W), ng)


def _dsa_kernel(ik_ref, ak_ref, avt_ref, aq_ref, iq_ref, iwt_ref, az_ref, o_ref,
                sc_ref, qz_ref, iqz_ref, w_ref, m_ref, acc_ref, plane_ref, thr_ref, cnt_ref,
                *, topk, nblk):
    j = pl.program_id(1)
    npair = (j + 2) // 2
    ntrip = (npair + 1) // 2
    KP = 2 * LANES

    def pair_rows(ref, i, axis):
        kb0 = jnp.minimum(2 * i, nblk - 1)
        kb1 = jnp.minimum(2 * i + 1, nblk - 1)
        return jnp.concatenate([ref[0, kb0], ref[0, kb1]], axis=axis)

    lane = lax.broadcasted_iota(I32, (1, LANES), 1)
    low = lane < DSA_HD
    zero_b = jnp.zeros((LANES, LANES), BF16)
    for m in range(4):
        sl = slice(m * LANES, (m + 1) * LANES)
        qs = aq_ref[0, :, sl]
        qz_ref[m * LANES:(m + 1) * LANES, :] = jnp.where(low, qs, zero_b)
        qz_ref[(m + 4) * LANES:(m + 5) * LANES, :] = jnp.where(low, zero_b, qs)
        iqs = iq_ref[0, :, sl]
        iqz_ref[(2 * m) * LANES:(2 * m + 1) * LANES, :] = jnp.where(low, iqs, zero_b)
        iqz_ref[(2 * m + 1) * LANES:(2 * m + 2) * LANES, :] = jnp.where(low, zero_b, iqs)
    for h in range(IDX_HEADS):
        w_ref[:, h * LANES:(h + 1) * LANES] = iwt_ref[0, h:h + 1, :]
    t_pos = j * LANES + lane
    s_loc = lax.broadcasted_iota(I32, (KP, 1), 0)

    def pair_ds(i):
        return pl.ds(pl.multiple_of(i * KP, KP), KP)

    def score_trip(u, carry):
        pairs = (2 * u, 2 * u + 1)
        dots = [_dot_nt(pair_rows(ik_ref, i, 0), iqz_ref[...]) for i in pairs]
        for i, d in zip(pairs, dots):
            d = jnp.maximum(d, 0.0) * w_ref[...]
            acc = d[:, :LANES]
            for h in range(1, IDX_HEADS):
                acc = acc + d[:, h * LANES:(h + 1) * LANES]
            s_pos = i * KP + s_loc
            valid = (s_pos <= t_pos) & ((s_pos >= PAD) | (t_pos < PAD))
            sc_ref[pair_ds(i), :] = jnp.where(valid, acc, -jnp.inf)
        return carry

    lax.fori_loop(0, ntrip, score_trip, 0)


    def count(*preds):
        def cnt_block(i, cnts):
            x = sc_ref[pair_ds(i), :]
            out = []
            for pred, cnt in zip(preds, cnts):
                hit = pred(x).astype(I32)
                parts = [hit[r:r + SUBLANES, :] for r in range(0, KP, SUBLANES)]
                while len(parts) > 1:
                    parts = [parts[k] + parts[k + 1] for k in range(0, len(parts), 2)]
                out.append(cnt + parts[0])
            return tuple(out)

        zero = jnp.zeros((SUBLANES, LANES), I32)
        cnts = lax.fori_loop(0, npair, cnt_block, (zero,) * len(preds))
        return tuple(jnp.sum(c, axis=0, keepdims=True) for c in cnts)

    select_all = jnp.float32(-3.0e38)

    @pl.when((pl.program_id(0) == 0) & (j == 0))
    def _():
        plane_ref[...] = jnp.zeros_like(plane_ref)

    def shift_const(x, n):
        return jnp.full(x.shape, n, I32)

    def build_planes(i, carry):
        bits = lax.bitcast_convert_type(sc_ref[pair_ds(i), :], I32)
        u = bits ^ (lax.shift_right_arithmetic(bits, shift_const(bits, 31)) | jnp.int32(INT_MIN))
        a = [u[SUBLANES * v:SUBLANES * (v + 1), :] for v in range(32)]
        step, mask = 16, 0x0000FFFF
        while step:
            for k in range(32):
                if not k & step:
                    t = (a[k] ^ lax.shift_right_logical(a[k + step], shift_const(a[k], step))) & jnp.int32(mask)
                    a[k] = a[k] ^ t
                    a[k + step] = a[k + step] ^ lax.shift_left(t, shift_const(t, step))
            step >>= 1
            mask = (mask ^ (mask << step)) & 0xFFFFFFFF
        for r in range(32):
            plane_ref[i, r] = a[r]
        return carry

    lax.fori_loop(0, npair, build_planes, 0)

    n_pairs_max = plane_ref.shape[0]
    ones_v = jnp.full((SUBLANES, LANES), -1, I32)
    zeros_v = jnp.zeros((SUBLANES, LANES), I32)
    alive0 = tuple(jnp.where(i < npair, ones_v, zeros_v) for i in range(n_pairs_max))

    def radix_step(t, carry):
        alive, k_left, u_thr = carry
        hi = [plane_ref[i, 2 * t] for i in range(n_pairs_max)]
        lo = [plane_ref[i, 2 * t + 1] for i in range(n_pairs_max)]
        c11 = c1x = c01 = zeros_v
        for i in range(n_pairs_max):
            a1 = alive[i] & hi[i]
            a0 = alive[i] ^ a1
            c11 = c11 + lax.population_count(a1 & lo[i])
            c1x = c1x + lax.population_count(a1)
            c01 = c01 + lax.population_count(a0 & lo[i])
        c11 = jnp.sum(c11, axis=0, keepdims=True)
        c1x = jnp.sum(c1x, axis=0, keepdims=True)
        c01 = jnp.sum(c01, axis=0, keepdims=True)
        in3 = k_left <= c11
        in32 = k_left <= c1x
        in321 = k_left <= c1x + c01
        bit_hi = in32
        bit_lo = in3 | (~in32 & in321)
        k_left = jnp.where(in3, k_left,
                           jnp.where(in32, k_left - c11,
                                     jnp.where(in321, k_left - c1x, k_left - c1x - c01)))
        f_hi = jnp.where(bit_hi, jnp.int32(0), jnp.int32(-1))
        f_lo = jnp.where(bit_lo, jnp.int32(0), jnp.int32(-1))
        alive = tuple(alive[i] & (hi[i] ^ f_hi) & (lo[i] ^ f_lo) for i in range(n_pairs_max))
        sh = 30 - 2 * t
        digit = jnp.where(bit_hi, jnp.int32(2), jnp.int32(0)) | jnp.where(bit_lo, jnp.int32(1), jnp.int32(0))
        return alive, k_left, u_thr | lax.shift_left(digit, jnp.full(digit.shape, sh, I32))

    _, _, u_thr = lax.fori_loop(
        0, 16, radix_step,
        (alive0, jnp.full((1, LANES), topk, I32), jnp.zeros((1, LANES), I32)))
    thr_bits = jnp.where(u_thr < 0, u_thr ^ jnp.int32(INT_MIN), ~u_thr)
    thr_fast = lax.bitcast_convert_type(thr_bits, F32)
    thr_fast = jnp.where(thr_fast == -jnp.inf, select_all, thr_fast)

    n_ge, n_gt = count(lambda x: x >= thr_fast, lambda x: x > thr_fast)
    is_all = thr_fast == select_all
    good = is_all | ((n_gt < topk) & (n_ge >= topk))
    thr_ref[...] = thr_fast
    cnt_ref[...] = jnp.where(is_all, 0, n_ge)

    @pl.when(jnp.min(good.astype(I32)) == 0)
    def _():
        def key_to_float(key):
            bits = jnp.where(key < 0, key ^ jnp.int32(0x7FFFFFFF), key)
            return lax.bitcast_convert_type(bits, F32)

        def count_ge(cand_key):
            cand = key_to_float(cand_key)
            return count(lambda x: x >= cand)[0]

        c0 = count_ge(jnp.zeros((1, LANES), I32))
        base = jnp.where(c0 >= topk, jnp.int32(0), jnp.int32(INT_MIN))
        base_cnt = jnp.where(c0 >= topk, c0, 0)

        def bit_step(i, carry):
            base, base_cnt = carry
            cand = base | jnp.left_shift(jnp.int32(1), 30 - i)
            c = count_ge(cand)
            ok = c >= topk
            return jnp.where(ok, cand, base), jnp.where(ok, c, base_cnt)

        base, base_cnt = lax.fori_loop(0, 31, bit_step, (base, base_cnt))
        thr_ref[...] = jnp.where(base == INT_MIN, select_all, key_to_float(base))
        cnt_ref[...] = base_cnt

    thr = thr_ref[...]
    base_cnt = cnt_ref[...]

    @pl.when(jnp.max(base_cnt) > topk)
    def _():
        need = (topk - count(lambda x: x > thr)[0]).astype(F32)
        rr = lax.broadcasted_iota(I32, (KP, KP), 0)
        cc = lax.broadcasted_iota(I32, (KP, KP), 1)
        tri = (rr >= cc).astype(BF16)

        def strike(i, seen):
            x = sc_ref[pair_ds(i), :]
            eq = x == thr
            rank = _dot(tri, jnp.where(eq, 1.0, 0.0).astype(BF16)) + seen
            sc_ref[pair_ds(i), :] = jnp.where(eq & (rank > need), -jnp.inf, x)
            return rank[KP - 1:KP, :]

        lax.fori_loop(0, npair, strike, jnp.zeros((1, LANES), F32))

    m_ref[...] = jnp.full_like(m_ref, M_INIT)
    acc_ref[...] = jnp.zeros_like(acc_ref)
    GW = DSA_GROUP * LANES
    ones_rows = jnp.ones((ONES_ROWS, KP), BF16)

    def attn_trip(u, carry):
        pairs = (2 * u, 2 * u + 1)
        logits = [_dot_nt(pair_rows(ak_ref, i, 0), qz_ref[...]).astype(BF16) for i in pairs]
        for i, s in zip(pairs, logits):
            bias = jnp.where(sc_ref[pair_ds(i), :] >= thr, 0.0, -jnp.inf).astype(BF16)
            sb = s + jnp.concatenate([bias] * DSA_HEADS, axis=1)
            m_old = m_ref[...]
            m_new = jnp.maximum(m_old, jnp.max(sb, axis=0, keepdims=True).astype(F32))
            alpha = jnp.exp(m_old - m_new)
            pb = jnp.exp(sb - m_new.astype(BF16))
            m_ref[...] = m_new
            vt = pair_rows(avt_ref, i, 1)
            for g in range(DSA_KV_HEADS):
                cols = slice(g * GW, (g + 1) * GW)
                vg = jnp.concatenate([vt[g * DSA_HD:(g + 1) * DSA_HD, :], ones_rows], axis=0)
                acc_ref[g] = alpha[:, cols] * acc_ref[g] + _dot(vg, pb[:, cols])
        return carry

    lax.fori_loop(0, ntrip, attn_trip, 0)

    heads = []
    for h in range(DSA_HEADS):
        a = acc_ref[h // DSA_GROUP][:, (h % DSA_GROUP) * LANES:(h % DSA_GROUP + 1) * LANES]
        heads.append(a[:DSA_HD, :] / a[DSA_HD:DSA_HD + 1, :])
    ot = jnp.concatenate(heads, axis=0)
    z = az_ref[0]
    o_ref[0] = (ot.T * (z * jax.nn.sigmoid(z))).astype(BF16)


def _dsa(ik2, ak, avt, aq, iq, iwt, az, B, nblk, topk):
    kern = functools.partial(_dsa_kernel, topk=topk, nblk=nblk)
    whole = lambda b_, j: (b_, 0, 0, 0)
    qblk = lambda b_, j: (b_ * nblk + j, 0, 0)
    return pl.pallas_call(
        kern,
        grid=(B, nblk),
        in_specs=[
            pl.BlockSpec((1, nblk, LANES, LANES), whole),
            pl.BlockSpec((1, nblk, LANES, LANES), whole),
            pl.BlockSpec((1, nblk, LANES, LANES), whole),
            pl.BlockSpec((1, LANES, DSA_W), qblk),
            pl.BlockSpec((1, LANES, IDX_HEADS * IDX_HD), qblk),
            pl.BlockSpec((1, IDX_HEADS, LANES), qblk),
            pl.BlockSpec((1, LANES, DSA_W), qblk),
        ],
        out_specs=pl.BlockSpec((1, LANES, DSA_W), qblk),
        out_shape=jax.ShapeDtypeStruct((B * nblk, LANES, DSA_W), BF16),
        scratch_shapes=[
            pltpu.VMEM(((nblk + 3) // 4 * 4 * LANES, LANES), F32),
            pltpu.VMEM((DSA_HEADS * LANES, LANES), BF16),
            pltpu.VMEM((IDX_HEADS * LANES, LANES), BF16),
            pltpu.VMEM((1, IDX_HEADS * LANES), F32),
            pltpu.VMEM((1, DSA_HEADS * LANES), F32),
            pltpu.VMEM((DSA_KV_HEADS, DSA_HD + ONES_ROWS, DSA_GROUP * LANES), F32),
            pltpu.VMEM(((nblk + 1) // 2, 32, SUBLANES, LANES), I32),
            pltpu.VMEM((1, LANES), F32),
            pltpu.VMEM((1, LANES), I32),
        ],
        compiler_params=pltpu.CompilerParams(vmem_limit_bytes=VMEM_LIMIT),
        name="dsa",
    )(ik2.reshape(B, nblk, LANES, LANES), ak.reshape(B, nblk, LANES, LANES),
      avt.reshape(B, nblk, LANES, LANES), aq.reshape(B * nblk, LANES, DSA_W),
      iq.reshape(B * nblk, LANES, IDX_HEADS * IDX_HD), iwt, az.reshape(B * nblk, LANES, DSA_W))


def _out_kernel(mg_ref, md_ref, h_ref, w_ref, g_ref, b_ref, o_ref, *, alpha):
    y = _dot(mg_ref[0], w_ref[:GLA_W, :]) + _dot(md_ref[0], w_ref[GLA_W:, :])
    o_ref[0] = _layer_norm_rows(alpha * h_ref[0] + y, g_ref[...], b_ref[...])


def _out(mg, md, h, w, g, b, B, LP, alpha, drop_filler):
    rows = OUT_ROWS
    if drop_filler:
        S = LP - LANES
        grid, nout = (B, S // rows), S
        src_blk = lambda width: (pl.Element(1), pl.Element(rows), pl.Element(width))
        src = lambda b_, j: (b_, pl.multiple_of(LANES + j * rows, LANES), 0)
    else:
        grid, nout = (B * LP // rows,), LP
        src_blk = lambda width: (1, rows, width)
        mg, md, h = (a.reshape(1, B * LP, a.shape[-1]) for a in (mg, md, h))
        src = lambda i: (0, i, 0)
    dst = (lambda b_, j: (b_, j, 0)) if drop_filler else (lambda i: (0, i, 0))
    const = (lambda b_, j: (0, 0)) if drop_filler else (lambda i: (0, 0))
    out_shape = (B, nout, D_MODEL) if drop_filler else (1, B * LP, D_MODEL)
    kern = functools.partial(_out_kernel, alpha=alpha)
    return pl.pallas_call(
        kern,
        grid=grid,
        in_specs=[
            pl.BlockSpec(src_blk(GLA_W), src),
            pl.BlockSpec(src_blk(DSA_W), src),
            pl.BlockSpec(src_blk(D_MODEL), src),
            pl.BlockSpec((MIX_W, D_MODEL), const),
            pl.BlockSpec((1, D_MODEL), const),
            pl.BlockSpec((1, D_MODEL), const),
        ],
        out_specs=pl.BlockSpec((1, rows, D_MODEL), dst),
        out_shape=jax.ShapeDtypeStruct(out_shape, F32),
        name="out_proj_ln",
    )(mg, md, h, w, g, b)


def _pack_w_in(w):
    splits = (256, 256, 512, GLA_RANK, 512, 512, 128, 128, 512, IDX_HD, IDX_HEADS, 512)
    offs = np.cumsum((0,) + splits)
    gq, gk, gv, glr, gz, aq, ak, av, iq, ik, iw, az = [w[:, offs[i]:offs[i + 1]] for i in range(12)]

    def perm_heads(a):
        return a.reshape(D_MODEL, DSA_HEADS, DSA_HD)[:, HEAD_PERM, :].reshape(D_MODEL, DSA_W)

    pad = jnp.zeros((D_MODEL, LANES - IDX_HD - GLA_RANK - IDX_HEADS), w.dtype)
    packed = jnp.concatenate(
        [gq * (GLA_DK ** -0.5), gk, gv, gz, perm_heads(aq) * (DSA_HD ** -0.5), ak, av,
         iq * (IDX_HD ** -0.5), az, ik, glr, iw, pad], axis=1)
    return packed.astype(BF16)


def _rope_lane_tables(reps, LP):
    inv = ROPE_THETA ** (-jnp.arange(0, ROPE_DIM, 2, dtype=F32) / ROPE_DIM)
    pos = (jnp.arange(LP, dtype=F32) - PAD)[:, None]
    ang = pos * inv[None, :]
    cos, sin = jnp.cos(ang), jnp.sin(ang)
    ones = jnp.ones((LP, DSA_HD - ROPE_DIM), F32)
    zeros = jnp.zeros((LP, DSA_HD - ROPE_DIM), F32)
    zh = jnp.zeros((LP, ROPE_HALF), F32)
    c = jnp.concatenate([cos, cos, ones], axis=1)
    sa = jnp.concatenate([zh, sin, zeros], axis=1)
    sb = jnp.concatenate([-sin, zh, zeros], axis=1)
    tile = lambda t: jnp.tile(jnp.concatenate([t, t], axis=1), (reps, 1))
    return tile(c), tile(sa), tile(sb)


def kernel(x, meta_tokens, ln_in_g, ln_in_b, w_in, gla_wg2, gla_bg, gla_norm_g, idx_k_g, idx_k_b,
           w_out, ln_g, ln_b):
    B, S, D = x.shape
    depth = w_in.shape[0]
    nblk = S // LANES + 1
    LP = nblk * LANES
    R = B * LP
    topk = min(TOPK_MAX, S // 4)
    alpha = (2.0 * depth) ** 0.25
    tm = PROJ_ROWS

    meta_pad = jnp.concatenate([jnp.zeros((PAD, D), x.dtype), meta_tokens.astype(x.dtype)], axis=0)
    h = _embed(x, meta_pad, ln_in_g.reshape(1, D), ln_in_b.reshape(1, D))
    cos_t, sa_t, sb_t = _rope_lane_tables(tm // math.gcd(tm, LP), LP)

    for i in range(depth):
        w = _pack_w_in(w_in[i])
        wg2 = jnp.zeros((LANES, 256), F32).at[MISC_GLR:MISC_GLR + GLA_RANK].set(gla_wg2[i])
        wg2h = wg2.astype(BF16)
        wg2l = (wg2 - wg2h.astype(F32)).astype(BF16)
        ikg = jnp.zeros((1, LANES), F32).at[0, :IDX_HD].set(idx_k_g[i])
        ikb = jnp.zeros((1, LANES), F32).at[0, :IDX_HD].set(idx_k_b[i])
        (gq, gk, gv, glog, gz, aq, ak, avt, iq, ik2, iwt, az) = _proj(
            h.reshape(R, D), w, wg2h, wg2l, gla_bg[i].reshape(1, 256), ikg, ikb,
            cos_t, sa_t, sb_t, tm)
        mix_gla = _gla(gq, gk, glog, gv, gz, gla_norm_g[i].reshape(1, GLA_DV), B, LP)
        mix_dsa = _dsa(ik2, ak, avt, aq, iq, iwt, az, B, nblk, topk)
        h = _out(mix_gla, mix_dsa.reshape(B, LP, DSA_W), h.reshape(B, LP, D),
                 w_out[i].astype(BF16), ln_g[i].reshape(1, D), ln_b[i].reshape(1, D),
                 B, LP, alpha, drop_filler=(i == depth - 1))
    return h.reshape(B, S, D)
```

```python
import functools
import math

import numpy as np
import jax
import jax.numpy as jnp
from jax import lax
from jax.experimental import pallas as pl
from jax.experimental.pallas import tpu as pltpu

F32 = jnp.float32
BF16 = jnp.bfloat16
I32 = jnp.int32

D_MODEL = 1024
N_META = 16
ROPE_THETA = 500000.0
LN_EPS = 1e-5
GLA_HEADS = 4
GLA_DK = 64
GLA_DV = 128
GLA_RANK = 16
GLA_TAU = 16.0
GLA_CHUNK = 64
GLA_W = GLA_HEADS * GLA_DV
DSA_HEADS = 8
DSA_KV_HEADS = 2
DSA_GROUP = DSA_HEADS // DSA_KV_HEADS
DSA_HD = 64
DSA_W = DSA_HEADS * DSA_HD
IDX_HEADS = 8
IDX_HD = 64
TOPK_MAX = 256
ROPE_DIM = DSA_HD // 4
ROPE_HALF = ROPE_DIM // 2
MIX_W = GLA_W + DSA_W

LANES = 128
SUBLANES = 8
ONES_ROWS = 16
PAD = LANES - N_META
INT_MIN = -(2 ** 31)
M_INIT = -(2.0 ** 100)
VMEM_LIMIT = 48 * 1024 * 1024
PROJ_ROWS = 512
OUT_ROWS = 512

O_GQ, O_GK, O_GV, O_GZ = 0, 256, 512, 1024
O_AQ, O_AK, O_AV, O_IQ, O_AZ, O_MISC = 1536, 2048, 2176, 2304, 2816, 3328
W_PACKED = 3456
MISC_GLR = IDX_HD
MISC_IW = IDX_HD + GLA_RANK
HEAD_PERM = tuple(m + DSA_GROUP * r for m in range(DSA_GROUP) for r in range(2))


def _dot(a, b):
    return jnp.dot(a, b, preferred_element_type=F32)


def _dot_nt(a, b):
    return lax.dot_general(a, b, (((1,), (1,)), ((), ())), preferred_element_type=F32)


def _dot_tn(a, b):
    return lax.dot_general(a, b, (((0,), (0,)), ((), ())), preferred_element_type=F32)


def _layer_norm_rows(u, g, b):
    mu = jnp.mean(u, axis=-1, keepdims=True)
    d = u - mu
    var = jnp.mean(d * d, axis=-1, keepdims=True)
    return d * lax.rsqrt(var + LN_EPS) * g + b


def _embed_kernel(x_ref, meta_ref, g_ref, b_ref, o_ref):
    g, b = g_ref[...], b_ref[...]
    o_ref[0] = _layer_norm_rows(meta_ref[...], g, b)
    for c in range(x_ref.shape[1]):
        o_ref[c + 1] = _layer_norm_rows(x_ref[0, c], g, b)


def _embed(x, meta_pad, g, b):
    B, S, D = x.shape
    nblk = S // LANES + 1
    x4 = x.reshape(B, S // LANES, LANES, D)
    return pl.pallas_call(
        _embed_kernel,
        grid=(B,),
        in_specs=[
            pl.BlockSpec((1, S // LANES, LANES, D), lambda b_: (b_, 0, 0, 0)),
            pl.BlockSpec((LANES, D), lambda b_: (0, 0)),
            pl.BlockSpec((1, D), lambda b_: (0, 0)),
            pl.BlockSpec((1, D), lambda b_: (0, 0)),
        ],
        out_specs=pl.BlockSpec((nblk, LANES, D), lambda b_: (b_, 0, 0)),
        out_shape=jax.ShapeDtypeStruct((B * nblk, LANES, D), F32),
        compiler_params=pltpu.CompilerParams(vmem_limit_bytes=VMEM_LIMIT),
        name="embed_ln",
    )(x4, meta_pad, g, b)


def _rope_slab(x, c, sa, sb):
    return x * c + pltpu.roll(x, ROPE_HALF, 1) * sa + pltpu.roll(x, LANES - ROPE_HALF, 1) * sb


def _proj_kernel(h_ref, w_ref, wg2h_ref, wg2l_ref, bg_ref, ikg_ref, ikb_ref,
                 cos_ref, sa_ref, sb_ref,
                 gq_ref, gk_ref, gv_ref, glog_ref, gz_ref, aq_ref, ak_ref, avt_ref,
                 iq_ref, ik_ref, iwt_ref, az_ref, *, tm):
    hb = h_ref[...].astype(BF16)

    def seg(o, w):
        return _dot(hb, w_ref[:, o:o + w])

    c, sa, sb = cos_ref[...], sa_ref[...], sb_ref[...]
    misc = seg(O_MISC, LANES)
    av = seg(O_AV, LANES)
    ak = seg(O_AK, LANES)
    aq = seg(O_AQ, 512)

    misc_t_scale = IDX_HEADS ** -0.5
    for r in range(tm // LANES):
        rows = slice(r * LANES, (r + 1) * LANES)
        avt_ref[r] = av[rows, :].T.astype(BF16)
        iwt_ref[r] = misc[rows, :].T[MISC_IW:MISC_IW + IDX_HEADS, :] * misc_t_scale
    ak_ref[...] = _rope_slab(ak, c, sa, sb).astype(BF16)
    iq = seg(O_IQ, 512)

    m_hi = misc.astype(BF16)
    m_lo = (misc - m_hi.astype(F32)).astype(BF16)
    xg = (_dot(m_hi, wg2h_ref[...]) + _dot(m_lo, wg2h_ref[...]) + _dot(m_hi, wg2l_ref[...])
          + bg_ref[...])
    gq_ref[...] = seg(O_GQ, 256)
    glog_ref[...] = (jnp.minimum(xg, 0.0) - jnp.log1p(jnp.exp(-jnp.abs(xg)))) * (1.0 / GLA_TAU)

    lane = lax.broadcasted_iota(I32, (1, LANES), 1)
    is_key = lane < IDX_HD
    mu = jnp.sum(jnp.where(is_key, misc, 0.0), axis=-1, keepdims=True) * (1.0 / IDX_HD)
    d = jnp.where(is_key, misc - mu, 0.0)
    var = jnp.sum(d * d, axis=-1, keepdims=True) * (1.0 / IDX_HD)
    ikn = d * lax.rsqrt(var + LN_EPS) * ikg_ref[...] + ikb_ref[...]
    ikr = _rope_slab(ikn, c, sa, sb)
    gk_ref[...] = seg(O_GK, 256)
    ik_ref[...] = jnp.where(is_key, ikr, pltpu.roll(ikr, IDX_HD, 1)).astype(BF16)

    gv = seg(O_GV, 512)
    for m in range(4):
        sl = slice(m * LANES, (m + 1) * LANES)
        aq_ref[:, sl] = _rope_slab(aq[:, sl], c, sa, sb).astype(BF16)
    gv_ref[...] = gv.astype(BF16)
    gz = seg(O_GZ, 512)
    for m in range(4):
        sl = slice(m * LANES, (m + 1) * LANES)
        iq_ref[:, sl] = _rope_slab(iq[:, sl], c, sa, sb).astype(BF16)
    gz_ref[...] = gz
    az_ref[...] = seg(O_AZ, 512)


def _proj(h2, w, wg2h, wg2l, bg, ikg, ikb, cos_t, sa_t, sb_t, tm):
    R = h2.shape[0]
    nt = R // tm
    nsub = tm // LANES
    row = lambda i: (i, 0)
    const = lambda i: (0, 0)
    table_tiles = cos_t.shape[0] // tm
    table_row = lambda i: (i % table_tiles, 0)
    kern = functools.partial(_proj_kernel, tm=tm)
    out_shape = (
        jax.ShapeDtypeStruct((R, 256), F32),
        jax.ShapeDtypeStruct((R, 256), F32),
        jax.ShapeDtypeStruct((R, 512), BF16),
        jax.ShapeDtypeStruct((R, 256), F32),
        jax.ShapeDtypeStruct((R, 512), F32),
        jax.ShapeDtypeStruct((R, 512), BF16),
        jax.ShapeDtypeStruct((R, LANES), BF16),
        jax.ShapeDtypeStruct((R // LANES, LANES, LANES), BF16),
        jax.ShapeDtypeStruct((R, 512), BF16),
        jax.ShapeDtypeStruct((R, LANES), BF16),
        jax.ShapeDtypeStruct((R // LANES, IDX_HEADS, LANES), F32),
        jax.ShapeDtypeStruct((R, 512), F32),
    )
    out_specs = (
        pl.BlockSpec((tm, 256), row),
        pl.BlockSpec((tm, 256), row),
        pl.BlockSpec((tm, 512), row),
        pl.BlockSpec((tm, 256), row),
        pl.BlockSpec((tm, 512), row),
        pl.BlockSpec((tm, 512), row),
        pl.BlockSpec((tm, LANES), row),
        pl.BlockSpec((nsub, LANES, LANES), lambda i: (i, 0, 0)),
        pl.BlockSpec((tm, 512), row),
        pl.BlockSpec((tm, LANES), row),
        pl.BlockSpec((nsub, IDX_HEADS, LANES), lambda i: (i, 0, 0)),
        pl.BlockSpec((tm, 512), row),
    )
    in_specs = [
        pl.BlockSpec((tm, D_MODEL), row),
        pl.BlockSpec((D_MODEL, W_PACKED), const),
        pl.BlockSpec((LANES, 256), const),
        pl.BlockSpec((LANES, 256), const),
        pl.BlockSpec((1, 256), const),
        pl.BlockSpec((1, LANES), const),
        pl.BlockSpec((1, LANES), const),
        pl.BlockSpec((tm, LANES), table_row),
        pl.BlockSpec((tm, LANES), table_row),
        pl.BlockSpec((tm, LANES), table_row),
    ]
    return pl.pallas_call(
        kern, grid=(nt,), in_specs=in_specs, out_specs=out_specs, out_shape=out_shape,
        compiler_params=pltpu.CompilerParams(vmem_limit_bytes=VMEM_LIMIT),
        name="in_proj",
    )(h2, w, wg2h, wg2l, bg, ikg, ikb, cos_t, sa_t, sb_t)


GLA_NB = 8
GLA_PIECES = 17


def _gla_kernel(q_ref, k_ref, g_ref, v_ref, z_ref, ng_ref, o_ref, st_ref, *, n_chunks):
    C = GLA_CHUNK
    piece = pl.program_id(1)

    @pl.when(piece == 0)
    def _():
        st_ref[...] = jnp.zeros_like(st_ref)

    lane = lax.broadcasted_iota(I32, (1, LANES), 1)
    head_lanes = (lane < GLA_DK, lane >= GLA_DK)
    rr = lax.broadcasted_iota(I32, (C, C), 0)
    cc = lax.broadcasted_iota(I32, (C, C), 1)
    causal = rr >= cc
    tril = causal.astype(BF16)
    row = lax.broadcasted_iota(I32, (C, 1), 0)
    ng = ng_ref[...]
    row0 = piece * (n_chunks * C)

    def body(ci, carry):
        r0 = pl.multiple_of(ci * C, C)
        rows = pl.ds(r0, C)
        valid = (row0 + r0 + row) >= PAD
        pairs = [(bi, pr) for bi in range(GLA_NB) for pr in range(GLA_HEADS // 2)]
        bs = []
        for bi, pr in pairs:
            g = g_ref[bi, rows, pr * LANES:(pr + 1) * LANES]
            g_hi = g.astype(BF16)
            g_lo = (g - g_hi.astype(F32)).astype(BF16)
            bs.append(_dot(tril, g_hi) + _dot(tril, g_lo))
        ops = []
        for (bi, pr), b in zip(pairs, bs):
            pl_ = slice(pr * LANES, (pr + 1) * LANES)
            q = q_ref[bi, rows, pl_]
            k = jnp.where(valid, k_ref[bi, rows, pl_], 0.0)
            b_mid = b[C // 2 - 1:C // 2, :]
            b_last = b[C - 1:C, :]
            q_in = (q * jnp.exp(b - b_mid)).astype(BF16)
            k_in = (k * jnp.exp(b_mid - b)).astype(BF16)
            q_st = (q * jnp.exp(b)).astype(BF16)
            k_st = (k * jnp.exp(b_last - b)).astype(BF16)
            ops.append((q_in, k_in, q_st, k_st, jnp.exp(b_last)))
        heads = [(bi, pr, h) for bi, pr in pairs for h in range(2)]
        zero = jnp.zeros((C, LANES), BF16)
        part = []
        for n, (bi, pr, h) in enumerate(heads):
            q_in, k_in, q_st, k_st, decay = ops[n // 2]
            hl = head_lanes[h]
            head = 2 * pr + h
            cols = slice(head * GLA_DV, (head + 1) * GLA_DV)
            v = jnp.where(valid, v_ref[bi, rows, cols], jnp.zeros((), BF16))
            st = st_ref[bi, head]
            a = _dot_nt(jnp.where(hl, q_in, zero), k_in)
            o_st = _dot_nt(jnp.where(hl, q_st, zero), st.astype(BF16))
            st_ref[bi, head] = st * decay + _dot_tn(v, jnp.where(hl, k_st, zero))
            part.append((a, o_st, v, cols))
        for (bi, pr, h), (a, o_st, v, cols) in zip(heads, part):
            o = _dot(jnp.where(causal, a, 0.0).astype(BF16), v) + o_st
            on = o * lax.rsqrt(jnp.mean(o * o, axis=-1, keepdims=True) + LN_EPS) * ng
            z = z_ref[bi, rows, cols]
            o_ref[bi, rows, cols] = (on * (z * jax.nn.sigmoid(z))).astype(BF16)
        return carry

    lax.fori_loop(0, n_chunks, body, 0)


def _gla(gq, gk, glog, gv, gz, ng, B, LP):
    rows = LP // GLA_PIECES
    kern = functools.partial(_gla_kernel, n_chunks=rows // GLA_CHUNK)
    blk = lambda b_, p: (b_, p, 0)
    return pl.pallas_call(
        kern,
        grid=(B // GLA_NB, GLA_PIECES),
        in_specs=[
            pl.BlockSpec((GLA_NB, rows, 256), blk),
            pl.BlockSpec((GLA_NB, rows, 256), blk),
            pl.BlockSpec((GLA_NB, rows, 256), blk),
            pl.BlockSpec((GLA_NB, rows, GLA_W), blk),
            pl.BlockSpec((GLA_NB, rows, GLA_W), blk),
            pl.BlockSpec((1, GLA_DV), lambda b_, p: (0, 0)),
        ],
        out_specs=pl.BlockSpec((GLA_NB, rows, GLA_W), blk),
        out_shape=jax.ShapeDtypeStruct((B, LP, GLA_W), BF16),
        scratch_shapes=[pltpu.VMEM((GLA_NB, GLA_HEADS, GLA_DV, LANES), F32)],
        compiler_params=pltpu.CompilerParams(vmem_limit_bytes=VMEM_LIMIT),
        name="gla",
    )(gq.reshape(B, LP, 256), gk.reshape(B, LP, 256), glog.reshape(B, LP, 256),
      gv.reshape(B, LP, GLA_W), gz.reshape(B, LP, GLA_W), ng)


def _dsa_kernel(ik_ref, ak_ref, avt_ref, aq_ref, iq_ref, iwt_ref, az_ref, o_ref,
                sc_ref, qz_ref, iqz_ref, w_ref, m_ref, acc_ref, plane_ref, thr_ref, cnt_ref,
                *, topk, nblk):
    j = pl.program_id(1)
    npair = (j + 2) // 2
    ntrip = (npair + 1) // 2
    KP = 2 * LANES

    def pair_rows(ref, i, axis):
        kb0 = jnp.minimum(2 * i, nblk - 1)
        kb1 = jnp.minimum(2 * i + 1, nblk - 1)
        return jnp.concatenate([ref[0, kb0], ref[0, kb1]], axis=axis)

    lane = lax.broadcasted_iota(I32, (1, LANES), 1)
    low = lane < DSA_HD
    zero_b = jnp.zeros((LANES, LANES), BF16)
    for m in range(4):
        sl = slice(m * LANES, (m + 1) * LANES)
        qs = aq_ref[0, :, sl]
        qz_ref[m * LANES:(m + 1) * LANES, :] = jnp.where(low, qs, zero_b)
        qz_ref[(m + 4) * LANES:(m + 5) * LANES, :] = jnp.where(low, zero_b, qs)
        iqs = iq_ref[0, :, sl]
        iqz_ref[(2 * m) * LANES:(2 * m + 1) * LANES, :] = jnp.where(low, iqs, zero_b)
        iqz_ref[(2 * m + 1) * LANES:(2 * m + 2) * LANES, :] = jnp.where(low, zero_b, iqs)
    for h in range(IDX_HEADS):
        w_ref[:, h * LANES:(h + 1) * LANES] = iwt_ref[0, h:h + 1, :]
    t_pos = j * LANES + lane
    s_loc = lax.broadcasted_iota(I32, (KP, 1), 0)

    def pair_ds(i):
        return pl.ds(pl.multiple_of(i * KP, KP), KP)

    def score_trip(u, carry):
        pairs = (2 * u, 2 * u + 1)
        dots = [_dot_nt(pair_rows(ik_ref, i, 0), iqz_ref[...]) for i in pairs]
        for i, d in zip(pairs, dots):
            d = jnp.maximum(d, 0.0) * w_ref[...]
            acc = d[:, :LANES]
            for h in range(1, IDX_HEADS):
                acc = acc + d[:, h * LANES:(h + 1) * LANES]
            s_pos = i * KP + s_loc
            valid = (s_pos <= t_pos) & ((s_pos >= PAD) | (t_pos < PAD))
            sc_ref[pair_ds(i), :] = jnp.where(valid, acc, -jnp.inf)
        return carry

    lax.fori_loop(0, ntrip, score_trip, 0)


    def count(*preds):
        def cnt_block(i, cnts):
            x = sc_ref[pair_ds(i), :]
            out = []
            for pred, cnt in zip(preds, cnts):
                hit = pred(x).astype(I32)
                parts = [hit[r:r + SUBLANES, :] for r in range(0, KP, SUBLANES)]
                while len(parts) > 1:
                    parts = [parts[k] + parts[k + 1] for k in range(0, len(parts), 2)]
                out.append(cnt + parts[0])
            return tuple(out)

        zero = jnp.zeros((SUBLANES, LANES), I32)
        cnts = lax.fori_loop(0, npair, cnt_block, (zero,) * len(preds))
        return tuple(jnp.sum(c, axis=0, keepdims=True) for c in cnts)

    select_all = jnp.float32(-3.0e38)

    @pl.when((pl.program_id(0) == 0) & (j == 0))
    def _():
        plane_ref[...] = jnp.zeros_like(plane_ref)

    def shift_const(x, n):
        return jnp.full(x.shape, n, I32)

    def build_planes(i, carry):
        bits = lax.bitcast_convert_type(sc_ref[pair_ds(i), :], I32)
        u = bits ^ (lax.shift_right_arithmetic(bits, shift_const(bits, 31)) | jnp.int32(INT_MIN))
        a = [u[SUBLANES * v:SUBLANES * (v + 1), :] for v in range(32)]
        step, mask = 16, 0x0000FFFF
        while step:
            for k in range(32):
                if not k & step:
                    t = (a[k] ^ lax.shift_right_logical(a[k + step], shift_const(a[k], step))) & jnp.int32(mask)
                    a[k] = a[k] ^ t
                    a[k + step] = a[k + step] ^ lax.shift_left(t, shift_const(t, step))
            step >>= 1
            mask = (mask ^ (mask << step)) & 0xFFFFFFFF
        for r in range(32):
            plane_ref[i, r] = a[r]
        return carry

    lax.fori_loop(0, npair, build_planes, 0)

    n_pairs_max = plane_ref.shape[0]
    ones_v = jnp.full((SUBLANES, LANES), -1, I32)
    zeros_v = jnp.zeros((SUBLANES, LANES), I32)
    alive0 = tuple(jnp.where(i < npair, ones_v, zeros_v) for i in range(n_pairs_max))

    def radix_step(t, carry):
        alive, k_left, u_thr = carry
        hi = [plane_ref[i, 2 * t] for i in range(n_pairs_max)]
        lo = [plane_ref[i, 2 * t + 1] for i in range(n_pairs_max)]
        c11 = c1x = c01 = zeros_v
        for i in range(n_pairs_max):
            a1 = alive[i] & hi[i]
            a0 = alive[i] ^ a1
            c11 = c11 + lax.population_count(a1 & lo[i])
            c1x = c1x + lax.population_count(a1)
            c01 = c01 + lax.population_count(a0 & lo[i])
        c11 = jnp.sum(c11, axis=0, keepdims=True)
        c1x = jnp.sum(c1x, axis=0, keepdims=True)
        c01 = jnp.sum(c01, axis=0, keepdims=True)
        in3 = k_left <= c11
        in32 = k_left <= c1x
        in321 = k_left <= c1x + c01
        bit_hi = in32
        bit_lo = in3 | (~in32 & in321)
        k_left = jnp.where(in3, k_left,
                           jnp.where(in32, k_left - c11,
                                     jnp.where(in321, k_left - c1x, k_left - c1x - c01)))
        f_hi = jnp.where(bit_hi, jnp.int32(0), jnp.int32(-1))
        f_lo = jnp.where(bit_lo, jnp.int32(0), jnp.int32(-1))
        alive = tuple(alive[i] & (hi[i] ^ f_hi) & (lo[i] ^ f_lo) for i in range(n_pairs_max))
        sh = 30 - 2 * t
        digit = jnp.where(bit_hi, jnp.int32(2), jnp.int32(0)) | jnp.where(bit_lo, jnp.int32(1), jnp.int32(0))
        return alive, k_left, u_thr | lax.shift_left(digit, jnp.full(digit.shape, sh, I32))

    _, _, u_thr = lax.fori_loop(
        0, 16, radix_step,
        (alive0, jnp.full((1, LANES), topk, I32), jnp.zeros((1, LANES), I32)))
    thr_bits = jnp.where(u_thr < 0, u_thr ^ jnp.int32(INT_MIN), ~u_thr)
    thr_fast = lax.bitcast_convert_type(thr_bits, F32)
    thr_fast = jnp.where(thr_fast == -jnp.inf, select_all, thr_fast)

    n_ge, n_gt = count(lambda x: x >= thr_fast, lambda x: x > thr_fast)
    is_all = thr_fast == select_all
    good = is_all | ((n_gt < topk) & (n_ge >= topk))
    thr_ref[...] = thr_fast
    cnt_ref[...] = jnp.where(is_all, 0, n_ge)

    @pl.when(jnp.min(good.astype(I32)) == 0)
    def _():
        def key_to_float(key):
            bits = jnp.where(key < 0, key ^ jnp.int32(0x7FFFFFFF), key)
            return lax.bitcast_convert_type(bits, F32)

        def count_ge(cand_key):
            cand = key_to_float(cand_key)
            return count(lambda x: x >= cand)[0]

        c0 = count_ge(jnp.zeros((1, LANES), I32))
        base = jnp.where(c0 >= topk, jnp.int32(0), jnp.int32(INT_MIN))
        base_cnt = jnp.where(c0 >= topk, c0, 0)

        def bit_step(i, carry):
            base, base_cnt = carry
            cand = base | jnp.left_shift(jnp.int32(1), 30 - i)
            c = count_ge(cand)
            ok = c >= topk
            return jnp.where(ok, cand, base), jnp.where(ok, c, base_cnt)

        base, base_cnt = lax.fori_loop(0, 31, bit_step, (base, base_cnt))
        thr_ref[...] = jnp.where(base == INT_MIN, select_all, key_to_float(base))
        cnt_ref[...] = base_cnt

    thr = thr_ref[...]
    base_cnt = cnt_ref[...]

    @pl.when(jnp.max(base_cnt) > topk)
    def _():
        need = (topk - count(lambda x: x > thr)[0]).astype(F32)
        rr = lax.broadcasted_iota(I32, (KP, KP), 0)
        cc = lax.broadcasted_iota(I32, (KP, KP), 1)
        tri = (rr >= cc).astype(BF16)

        def strike(i, seen):
            x = sc_ref[pair_ds(i), :]
            eq = x == thr
            rank = _dot(tri, jnp.where(eq, 1.0, 0.0).astype(BF16)) + seen
            sc_ref[pair_ds(i), :] = jnp.where(eq & (rank > need), -jnp.inf, x)
            return rank[KP - 1:KP, :]

        lax.fori_loop(0, npair, strike, jnp.zeros((1, LANES), F32))

    m_ref[...] = jnp.full_like(m_ref, M_INIT)
    acc_ref[...] = jnp.zeros_like(acc_ref)
    GW = DSA_GROUP * LANES
    ones_rows = jnp.ones((ONES_ROWS, KP), BF16)

    def attn_trip(u, carry):
        pairs = (2 * u, 2 * u + 1)
        logits = [_dot_nt(pair_rows(ak_ref, i, 0), qz_ref[...]).astype(BF16) for i in pairs]
        for i, s in zip(pairs, logits):
            bias = jnp.where(sc_ref[pair_ds(i), :] >= thr, 0.0, -jnp.inf).astype(BF16)
            sb = s + jnp.concatenate([bias] * DSA_HEADS, axis=1)
            m_old = m_ref[...]
            m_new = jnp.maximum(m_old, jnp.max(sb, axis=0, keepdims=True).astype(F32))
            alpha = jnp.exp(m_old - m_new)
            pb = jnp.exp(sb - m_new.astype(BF16))
            m_ref[...] = m_new
            vt = pair_rows(avt_ref, i, 1)
            for g in range(DSA_KV_HEADS):
                cols = slice(g * GW, (g + 1) * GW)
                vg = jnp.concatenate([vt[g * DSA_HD:(g + 1) * DSA_HD, :], ones_rows], axis=0)
                acc_ref[g] = alpha[:, cols] * acc_ref[g] + _dot(vg, pb[:, cols])
        return carry

    lax.fori_loop(0, ntrip, attn_trip, 0)

    heads = []
    for h in range(DSA_HEADS):
        a = acc_ref[h // DSA_GROUP][:, (h % DSA_GROUP) * LANES:(h % DSA_GROUP + 1) * LANES]
        heads.append(a[:DSA_HD, :] / a[DSA_HD:DSA_HD + 1, :])
    ot = jnp.concatenate(heads, axis=0)
    z = az_ref[0]
    o_ref[0] = (ot.T * (z * jax.nn.sigmoid(z))).astype(BF16)


def _dsa(ik2, ak, avt, aq, iq, iwt, az, B, nblk, topk):
    kern = functools.partial(_dsa_kernel, topk=topk, nblk=nblk)
    whole = lambda b_, j: (b_, 0, 0, 0)
    qblk = lambda b_, j: (b_ * nblk + j, 0, 0)
    return pl.pallas_call(
        kern,
        grid=(B, nblk),
        in_specs=[
            pl.BlockSpec((1, nblk, LANES, LANES), whole),
            pl.BlockSpec((1, nblk, LANES, LANES), whole),
            pl.BlockSpec((1, nblk, LANES, LANES), whole),
            pl.BlockSpec((1, LANES, DSA_W), qblk),
            pl.BlockSpec((1, LANES, IDX_HEADS * IDX_HD), qblk),
            pl.BlockSpec((1, IDX_HEADS, LANES), qblk),
            pl.BlockSpec((1, LANES, DSA_W), qblk),
        ],
        out_specs=pl.BlockSpec((1, LANES, DSA_W), qblk),
        out_shape=jax.ShapeDtypeStruct((B * nblk, LANES, DSA_W), BF16),
        scratch_shapes=[
            pltpu.VMEM(((nblk + 3) // 4 * 4 * LANES, LANES), F32),
            pltpu.VMEM((DSA_HEADS * LANES, LANES), BF16),
            pltpu.VMEM((IDX_HEADS * LANES, LANES), BF16),
            pltpu.VMEM((1, IDX_HEADS * LANES), F32),
            pltpu.VMEM((1, DSA_HEADS * LANES), F32),
            pltpu.VMEM((DSA_KV_HEADS, DSA_HD + ONES_ROWS, DSA_GROUP * LANES), F32),
            pltpu.VMEM(((nblk + 1) // 2, 32, SUBLANES, LANES), I32),
            pltpu.VMEM((1, LANES), F32),
            pltpu.VMEM((1, LANES), I32),
        ],
        compiler_params=pltpu.CompilerParams(vmem_limit_bytes=VMEM_LIMIT),
        name="dsa",
    )(ik2.reshape(B, nblk, LANES, LANES), ak.reshape(B, nblk, LANES, LANES),
      avt.reshape(B, nblk, LANES, LANES), aq.reshape(B * nblk, LANES, DSA_W),
      iq.reshape(B * nblk, LANES, IDX_HEADS * IDX_HD), iwt, az.reshape(B * nblk, LANES, DSA_W))


def _out_kernel(mg_ref, md_ref, h_ref, w_ref, g_ref, b_ref, o_ref, *, alpha):
    y = _dot(mg_ref[0], w_ref[:GLA_W, :]) + _dot(md_ref[0], w_ref[GLA_W:, :])
    o_ref[0] = _layer_norm_rows(alpha * h_ref[0] + y, g_ref[...], b_ref[...])


def _out(mg, md, h, w, g, b, B, LP, alpha, drop_filler):
    rows = OUT_ROWS
    if drop_filler:
        S = LP - LANES
        grid, nout = (B, S // rows), S
        src_blk = lambda width: (pl.Element(1), pl.Element(rows), pl.Element(width))
        src = lambda b_, j: (b_, pl.multiple_of(LANES + j * rows, LANES), 0)
    else:
        grid, nout = (B * LP // rows,), LP
        src_blk = lambda width: (1, rows, width)
        mg, md, h = (a.reshape(1, B * LP, a.shape[-1]) for a in (mg, md, h))
        src = lambda i: (0, i, 0)
    dst = (lambda b_, j: (b_, j, 0)) if drop_filler else (lambda i: (0, i, 0))
    const = (lambda b_, j: (0, 0)) if drop_filler else (lambda i: (0, 0))
    out_shape = (B, nout, D_MODEL) if drop_filler else (1, B * LP, D_MODEL)
    kern = functools.partial(_out_kernel, alpha=alpha)
    return pl.pallas_call(
        kern,
        grid=grid,
        in_specs=[
            pl.BlockSpec(src_blk(GLA_W), src),
            pl.BlockSpec(src_blk(DSA_W), src),
            pl.BlockSpec(src_blk(D_MODEL), src),
            pl.BlockSpec((MIX_W, D_MODEL), const),
            pl.BlockSpec((1, D_MODEL), const),
            pl.BlockSpec((1, D_MODEL), const),
        ],
        out_specs=pl.BlockSpec((1, rows, D_MODEL), dst),
        out_shape=jax.ShapeDtypeStruct(out_shape, F32),
        name="out_proj_ln",
    )(mg, md, h, w, g, b)


def _pack_w_in(w):
    splits = (256, 256, 512, GLA_RANK, 512, 512, 128, 128, 512, IDX_HD, IDX_HEADS, 512)
    offs = np.cumsum((0,) + splits)
    gq, gk, gv, glr, gz, aq, ak, av, iq, ik, iw, az = [w[:, offs[i]:offs[i + 1]] for i in range(12)]

    def perm_heads(a):
        return a.reshape(D_MODEL, DSA_HEADS, DSA_HD)[:, HEAD_PERM, :].reshape(D_MODEL, DSA_W)

    pad = jnp.zeros((D_MODEL, LANES - IDX_HD - GLA_RANK - IDX_HEADS), w.dtype)
    packed = jnp.concatenate(
        [gq * (GLA_DK ** -0.5), gk, gv, gz, perm_heads(aq) * (DSA_HD ** -0.5), ak, av,
         iq * (IDX_HD ** -0.5), az, ik, glr, iw, pad], axis=1)
    return packed.astype(BF16)


def _rope_lane_tables(reps, LP):
    inv = ROPE_THETA ** (-jnp.arange(0, ROPE_DIM, 2, dtype=F32) / ROPE_DIM)
    pos = (jnp.arange(LP, dtype=F32) - PAD)[:, None]
    ang = pos * inv[None, :]
    cos, sin = jnp.cos(ang), jnp.sin(ang)
    ones = jnp.ones((LP, DSA_HD - ROPE_DIM), F32)
    zeros = jnp.zeros((LP, DSA_HD - ROPE_DIM), F32)
    zh = jnp.zeros((LP, ROPE_HALF), F32)
    c = jnp.concatenate([cos, cos, ones], axis=1)
    sa = jnp.concatenate([zh, sin, zeros], axis=1)
    sb = jnp.concatenate([-sin, zh, zeros], axis=1)
    tile = lambda t: jnp.tile(jnp.concatenate([t, t], axis=1), (reps, 1))
    return tile(c), tile(sa), tile(sb)


def kernel(x, meta_tokens, ln_in_g, ln_in_b, w_in, gla_wg2, gla_bg, gla_norm_g, idx_k_g, idx_k_b,
           w_out, ln_g, ln_b):
    B, S, D = x.shape
    depth = w_in.shape[0]
    nblk = S // LANES + 1
    LP = nblk * LANES
    R = B * LP
    topk = min(TOPK_MAX, S // 4)
    alpha = (2.0 * depth) ** 0.25
    tm = PROJ_ROWS

    meta_pad = jnp.concatenate([jnp.zeros((PAD, D), x.dtype), meta_tokens.astype(x.dtype)], axis=0)
    h = _embed(x, meta_pad, ln_in_g.reshape(1, D), ln_in_b.reshape(1, D))
    cos_t, sa_t, sb_t = _rope_lane_tables(tm // math.gcd(tm, LP), LP)

    for i in range(depth):
        w = _pack_w_in(w_in[i])
        wg2 = jnp.zeros((LANES, 256), F32).at[MISC_GLR:MISC_GLR + GLA_RANK].set(gla_wg2[i])
        wg2h = wg2.astype(BF16)
        wg2l = (wg2 - wg2h.astype(F32)).astype(BF16)
        ikg = jnp.zeros((1, LANES), F32).at[0, :IDX_HD].set(idx_k_g[i])
        ikb = jnp.zeros((1, LANES), F32).at[0, :IDX_HD].set(idx_k_b[i])
        (gq, gk, gv, glog, gz, aq, ak, avt, iq, ik2, iwt, az) = _proj(
            h.reshape(R, D), w, wg2h, wg2l, gla_bg[i].reshape(1, 256), ikg, ikb,
            cos_t, sa_t, sb_t, tm)
        mix_gla = _gla(gq, gk, glog, gv, gz, gla_norm_g[i].reshape(1, GLA_DV), B, LP)
        mix_dsa = _dsa(ik2, ak, avt, aq, iq, iwt, az, B, nblk, topk)
        h = _out(mix_gla, mix_dsa.reshape(B, LP, DSA_W), h.reshape(B, LP, D),
                 w_out[i].astype(BF16), ln_g[i].reshape(1, D), ln_b[i].reshape(1, D),
                 B, LP, alpha, drop_filler=(i == depth - 1))
    return h.reshape(B, S, D)
```

```python
import functools
import math

import numpy as np
import jax
import jax.numpy as jnp
from jax import lax
from jax.experimental import pallas as pl
from jax.experimental.pallas import tpu as pltpu

F32 = jnp.float32
BF16 = jnp.bfloat16
I32 = jnp.int32

D_MODEL = 1024
N_META = 16
ROPE_THETA = 500000.0
LN_EPS = 1e-5
GLA_HEADS = 4
GLA_DK = 64
GLA_DV = 128
GLA_RANK = 16
GLA_TAU = 16.0
GLA_CHUNK = 64
GLA_W = GLA_HEADS * GLA_DV
DSA_HEADS = 8
DSA_KV_HEADS = 2
DSA_GROUP = DSA_HEADS // DSA_KV_HEADS
DSA_HD = 64
DSA_W = DSA_HEADS * DSA_HD
IDX_HEADS = 8
IDX_HD = 64
TOPK_MAX = 256
ROPE_DIM = DSA_HD // 4
ROPE_HALF = ROPE_DIM // 2
MIX_W = GLA_W + DSA_W

LANES = 128
SUBLANES = 8
ONES_ROWS = 16
PAD = LANES - N_META
INT_MIN = -(2 ** 31)
M_INIT = -(2.0 ** 100)
VMEM_LIMIT = 48 * 1024 * 1024
PROJ_ROWS = 512
OUT_ROWS = 512

O_GQ, O_GK, O_GV, O_GZ = 0, 256, 512, 1024
O_AQ, O_AK, O_AV, O_IQ, O_AZ, O_MISC = 1536, 2048, 2176, 2304, 2816, 3328
W_PACKED = 3456
MISC_GLR = IDX_HD
MISC_IW = IDX_HD + GLA_RANK
HEAD_PERM = tuple(m + DSA_GROUP * r for m in range(DSA_GROUP) for r in range(2))


def _dot(a, b):
    return jnp.dot(a, b, preferred_element_type=F32)


def _dot_nt(a, b):
    return lax.dot_general(a, b, (((1,), (1,)), ((), ())), preferred_element_type=F32)


def _dot_tn(a, b):
    return lax.dot_general(a, b, (((0,), (0,)), ((), ())), preferred_element_type=F32)


def _layer_norm_rows(u, g, b):
    mu = jnp.mean(u, axis=-1, keepdims=True)
    d = u - mu
    var = jnp.mean(d * d, axis=-1, keepdims=True)
    return d * lax.rsqrt(var + LN_EPS) * g + b


def _embed_kernel(x_ref, meta_ref, g_ref, b_ref, o_ref):
    g, b = g_ref[...], b_ref[...]
    o_ref[0] = _layer_norm_rows(meta_ref[...], g, b)
    for c in range(x_ref.shape[1]):
        o_ref[c + 1] = _layer_norm_rows(x_ref[0, c], g, b)


def _embed(x, meta_pad, g, b):
    B, S, D = x.shape
    nblk = S // LANES + 1
    x4 = x.reshape(B, S // LANES, LANES, D)
    return pl.pallas_call(
        _embed_kernel,
        grid=(B,),
        in_specs=[
            pl.BlockSpec((1, S // LANES, LANES, D), lambda b_: (b_, 0, 0, 0)),
            pl.BlockSpec((LANES, D), lambda b_: (0, 0)),
            pl.BlockSpec((1, D), lambda b_: (0, 0)),
            pl.BlockSpec((1, D), lambda b_: (0, 0)),
        ],
        out_specs=pl.BlockSpec((nblk, LANES, D), lambda b_: (b_, 0, 0)),
        out_shape=jax.ShapeDtypeStruct((B * nblk, LANES, D), F32),
        compiler_params=pltpu.CompilerParams(vmem_limit_bytes=VMEM_LIMIT),
        name="embed_ln",
    )(x4, meta_pad, g, b)


def _rope_slab(x, c, sa, sb):
    return x * c + pltpu.roll(x, ROPE_HALF, 1) * sa + pltpu.roll(x, LANES - ROPE_HALF, 1) * sb


def _proj_kernel(h_ref, w_ref, wg2h_ref, wg2l_ref, bg_ref, ikg_ref, ikb_ref,
                 cos_ref, sa_ref, sb_ref,
                 gq_ref, gk_ref, gv_ref, glog_ref, gz_ref, aq_ref, ak_ref, avt_ref,
                 iq_ref, ik_ref, iwt_ref, az_ref, *, tm):
    hb = h_ref[...].astype(BF16)

    def seg(o, w):
        return _dot(hb, w_ref[:, o:o + w])

    c, sa, sb = cos_ref[...], sa_ref[...], sb_ref[...]
    misc = seg(O_MISC, LANES)
    av = seg(O_AV, LANES)
    ak = seg(O_AK, LANES)
    aq = seg(O_AQ, 512)

    misc_t_scale = IDX_HEADS ** -0.5
    for r in range(tm // LANES):
        rows = slice(r * LANES, (r + 1) * LANES)
        avt_ref[r] = av[rows, :].T.astype(BF16)
        iwt_ref[r] = misc[rows, :].T[MISC_IW:MISC_IW + IDX_HEADS, :] * misc_t_scale
    ak_ref[...] = _rope_slab(ak, c, sa, sb).astype(BF16)
    iq = seg(O_IQ, 512)

    m_hi = misc.astype(BF16)
    m_lo = (misc - m_hi.astype(F32)).astype(BF16)
    xg = (_dot(m_hi, wg2h_ref[...]) + _dot(m_lo, wg2h_ref[...]) + _dot(m_hi, wg2l_ref[...])
          + bg_ref[...])
    gq_ref[...] = seg(O_GQ, 256)
    glog_ref[...] = (jnp.minimum(xg, 0.0) - jnp.log1p(jnp.exp(-jnp.abs(xg)))) * (1.0 / GLA_TAU)

    lane = lax.broadcasted_iota(I32, (1, LANES), 1)
    is_key = lane < IDX_HD
    mu = jnp.sum(jnp.where(is_key, misc, 0.0), axis=-1, keepdims=True) * (1.0 / IDX_HD)
    d = jnp.where(is_key, misc - mu, 0.0)
    var = jnp.sum(d * d, axis=-1, keepdims=True) * (1.0 / IDX_HD)
    ikn = d * lax.rsqrt(var + LN_EPS) * ikg_ref[...] + ikb_ref[...]
    ikr = _rope_slab(ikn, c, sa, sb)
    gk_ref[...] = seg(O_GK, 256)
    ik_ref[...] = jnp.where(is_key, ikr, pltpu.roll(ikr, IDX_HD, 1)).astype(BF16)

    gv = seg(O_GV, 512)
    for m in range(4):
        sl = slice(m * LANES, (m + 1) * LANES)
        aq_ref[:, sl] = _rope_slab(aq[:, sl], c, sa, sb).astype(BF16)
    gv_ref[...] = gv.astype(BF16)
    gz = seg(O_GZ, 512)
    for m in range(4):
        sl = slice(m * LANES, (m + 1) * LANES)
        iq_ref[:, sl] = _rope_slab(iq[:, sl], c, sa, sb).astype(BF16)
    gz_ref[...] = gz
    az_ref[...] = seg(O_AZ, 512)


def _proj(h2, w, wg2h, wg2l, bg, ikg, ikb, cos_t, sa_t, sb_t, tm):
    R = h2.shape[0]
    nt = R // tm
    nsub = tm // LANES
    row = lambda i: (i, 0)
    const = lambda i: (0, 0)
    table_tiles = cos_t.shape[0] // tm
    table_row = lambda i: (i % table_tiles, 0)
    kern = functools.partial(_proj_kernel, tm=tm)
    out_shape = (
        jax.ShapeDtypeStruct((R, 256), F32),
        jax.ShapeDtypeStruct((R, 256), F32),
        jax.ShapeDtypeStruct((R, 512), BF16),
        jax.ShapeDtypeStruct((R, 256), F32),
        jax.ShapeDtypeStruct((R, 512), F32),
        jax.ShapeDtypeStruct((R, 512), BF16),
        jax.ShapeDtypeStruct((R, LANES), BF16),
        jax.ShapeDtypeStruct((R // LANES, LANES, LANES), BF16),
        jax.ShapeDtypeStruct((R, 512), BF16),
        jax.ShapeDtypeStruct((R, LANES), BF16),
        jax.ShapeDtypeStruct((R // LANES, IDX_HEADS, LANES), F32),
        jax.ShapeDtypeStruct((R, 512), F32),
    )
    out_specs = (
        pl.BlockSpec((tm, 256), row),
        pl.BlockSpec((tm, 256), row),
        pl.BlockSpec((tm, 512), row),
        pl.BlockSpec((tm, 256), row),
        pl.BlockSpec((tm, 512), row),
        pl.BlockSpec((tm, 512), row),
        pl.BlockSpec((tm, LANES), row),
        pl.BlockSpec((nsub, LANES, LANES), lambda i: (i, 0, 0)),
        pl.BlockSpec((tm, 512), row),
        pl.BlockSpec((tm, LANES), row),
        pl.BlockSpec((nsub, IDX_HEADS, LANES), lambda i: (i, 0, 0)),
        pl.BlockSpec((tm, 512), row),
    )
    in_specs = [
        pl.BlockSpec((tm, D_MODEL), row),
        pl.BlockSpec((D_MODEL, W_PACKED), const),
        pl.BlockSpec((LANES, 256), const),
        pl.BlockSpec((LANES, 256), const),
        pl.BlockSpec((1, 256), const),
        pl.BlockSpec((1, LANES), const),
        pl.BlockSpec((1, LANES), const),
        pl.BlockSpec((tm, LANES), table_row),
        pl.BlockSpec((tm, LANES), table_row),
        pl.BlockSpec((tm, LANES), table_row),
    ]
    return pl.pallas_call(
        kern, grid=(nt,), in_specs=in_specs, out_specs=out_specs, out_shape=out_shape,
        compiler_params=pltpu.CompilerParams(vmem_limit_bytes=VMEM_LIMIT),
        name="in_proj",
    )(h2, w, wg2h, wg2l, bg, ikg, ikb, cos_t, sa_t, sb_t)


GLA_NB = 8
GLA_PIECES = 17


def _gla_kernel(q_ref, k_ref, g_ref, v_ref, z_ref, ng_ref, o_ref, st_ref, *, n_chunks):
    C = GLA_CHUNK
    piece = pl.program_id(1)

    @pl.when(piece == 0)
    def _():
        st_ref[...] = jnp.zeros_like(st_ref)

    lane = lax.broadcasted_iota(I32, (1, LANES), 1)
    head_lanes = (lane < GLA_DK, lane >= GLA_DK)
    rr = lax.broadcasted_iota(I32, (C, C), 0)
    cc = lax.broadcasted_iota(I32, (C, C), 1)
    causal = rr >= cc
    tril = causal.astype(BF16)
    row = lax.broadcasted_iota(I32, (C, 1), 0)
    ng = ng_ref[...]
    row0 = piece * (n_chunks * C)

    def body(ci, carry):
        r0 = pl.multiple_of(ci * C, C)
        rows = pl.ds(r0, C)
        valid = (row0 + r0 + row) >= PAD
        pairs = [(bi, pr) for bi in range(GLA_NB) for pr in range(GLA_HEADS // 2)]
        bs = []
        for bi, pr in pairs:
            g = g_ref[bi, rows, pr * LANES:(pr + 1) * LANES]
            g_hi = g.astype(BF16)
            g_lo = (g - g_hi.astype(F32)).astype(BF16)
            bs.append(_dot(tril, g_hi) + _dot(tril, g_lo))
        ops = []
        for (bi, pr), b in zip(pairs, bs):
            pl_ = slice(pr * LANES, (pr + 1) * LANES)
            q = q_ref[bi, rows, pl_]
            k = jnp.where(valid, k_ref[bi, rows, pl_], 0.0)
            b_mid = b[C // 2 - 1:C // 2, :]
            b_last = b[C - 1:C, :]
            q_in = (q * jnp.exp(b - b_mid)).astype(BF16)
            k_in = (k * jnp.exp(b_mid - b)).astype(BF16)
            q_st = (q * jnp.exp(b)).astype(BF16)
            k_st = (k * jnp.exp(b_last - b)).astype(BF16)
            ops.append((q_in, k_in, q_st, k_st, jnp.exp(b_last)))
        heads = [(bi, pr, h) for bi, pr in pairs for h in range(2)]
        zero = jnp.zeros((C, LANES), BF16)
        part = []
        for n, (bi, pr, h) in enumerate(heads):
            q_in, k_in, q_st, k_st, decay = ops[n // 2]
            hl = head_lanes[h]
            head = 2 * pr + h
            cols = slice(head * GLA_DV, (head + 1) * GLA_DV)
            v = jnp.where(valid, v_ref[bi, rows, cols], jnp.zeros((), BF16))
            st = st_ref[bi, head]
            a = _dot_nt(jnp.where(hl, q_in, zero), k_in)
            o_st = _dot_nt(jnp.where(hl, q_st, zero), st.astype(BF16))
            st_ref[bi, head] = st * decay + _dot_tn(v, jnp.where(hl, k_st, zero))
            part.append((a, o_st, v, cols))
        for (bi, pr, h), (a, o_st, v, cols) in zip(heads, part):
            o = _dot(jnp.where(causal, a, 0.0).astype(BF16), v) + o_st
            on = o * lax.rsqrt(jnp.mean(o * o, axis=-1, keepdims=True) + LN_EPS) * ng
            z = z_ref[bi, rows, cols]
            o_ref[bi, rows, cols] = (on * (z * jax.nn.sigmoid(z))).astype(BF16)
        return carry

    lax.fori_loop(0, n_chunks, body, 0)


def _gla(gq, gk, glog, gv, gz, ng, B, LP):
    rows = LP // GLA_PIECES
    kern = functools.partial(_gla_kernel, n_chunks=rows // GLA_CHUNK)
    blk = lambda b_, p: (b_, p, 0)
    return pl.pallas_call(
        kern,
        grid=(B // GLA_NB, GLA_PIECES),
        in_specs=[
            pl.BlockSpec((GLA_NB, rows, 256), blk),
            pl.BlockSpec((GLA_NB, rows, 256), blk),
            pl.BlockSpec((GLA_NB, rows, 256), blk),
            pl.BlockSpec((GLA_NB, rows, GLA_W), blk),
            pl.BlockSpec((GLA_NB, rows, GLA_W), blk),
            pl.BlockSpec((1, GLA_DV), lambda b_, p: (0, 0)),
        ],
        out_specs=pl.BlockSpec((GLA_NB, rows, GLA_W), blk),
        out_shape=jax.ShapeDtypeStruct((B, LP, GLA_W), BF16),
        scratch_shapes=[pltpu.VMEM((GLA_NB, GLA_HEADS, GLA_DV, LANES), F32)],
        compiler_params=pltpu.CompilerParams(vmem_limit_bytes=VMEM_LIMIT),
        name="gla",
    )(gq.reshape(B, LP, 256), gk.reshape(B, LP, 256), glog.reshape(B, LP, 256),
      gv.reshape(B, LP, GLA_W), gz.reshape(B, LP, GLA_W), ng)


DSA_NB = 2


def _dsa_kernel(ik_ref, ak_ref, avt_ref, aq_ref, iq_ref, iwt_ref, az_ref, o_ref,
                sc_ref, qz_ref, iqz_ref, w_ref, m_ref, acc_ref, plane_ref, thr_ref, cnt_ref,
                *, topk, nblk):
    j = pl.program_id(1)
    npair = (j + 2) // 2
    ntrip = (npair + 1) // 2
    KP = 2 * LANES
    batch = range(DSA_NB)

    def pair_rows(ref, n, i, axis):
        kb0 = jnp.minimum(2 * i, nblk - 1)
        kb1 = jnp.minimum(2 * i + 1, nblk - 1)
        return jnp.concatenate([ref[n, kb0], ref[n, kb1]], axis=axis)

    def pair_ds(i):
        return pl.ds(pl.multiple_of(i * KP, KP), KP)

    lane = lax.broadcasted_iota(I32, (1, LANES), 1)
    low = lane < DSA_HD
    zero_b = jnp.zeros((LANES, LANES), BF16)
    for n in batch:
        for m in range(4):
            sl = slice(m * LANES, (m + 1) * LANES)
            qs = aq_ref[n, 0, :, sl]
            qz_ref[n, m * LANES:(m + 1) * LANES, :] = jnp.where(low, qs, zero_b)
            qz_ref[n, (m + 4) * LANES:(m + 5) * LANES, :] = jnp.where(low, zero_b, qs)
            iqs = iq_ref[n, 0, :, sl]
            iqz_ref[n, (2 * m) * LANES:(2 * m + 1) * LANES, :] = jnp.where(low, iqs, zero_b)
            iqz_ref[n, (2 * m + 1) * LANES:(2 * m + 2) * LANES, :] = jnp.where(low, zero_b, iqs)
        for h in range(IDX_HEADS):
            w_ref[n, :, h * LANES:(h + 1) * LANES] = iwt_ref[n, 0, h:h + 1, :]
    t_pos = j * LANES + lane
    s_loc = lax.broadcasted_iota(I32, (KP, 1), 0)

    def score_trip(u, carry):
        work = [(n, i) for i in (2 * u, 2 * u + 1) for n in batch]
        dots = [_dot_nt(pair_rows(ik_ref, n, i, 0), iqz_ref[n]) for n, i in work]
        for (n, i), d in zip(work, dots):
            d = jnp.maximum(d, 0.0) * w_ref[n]
            acc = d[:, :LANES]
            for h in range(1, IDX_HEADS):
                acc = acc + d[:, h * LANES:(h + 1) * LANES]
            s_pos = i * KP + s_loc
            valid = (s_pos <= t_pos) & ((s_pos >= PAD) | (t_pos < PAD))
            sc_ref[n, pair_ds(i), :] = jnp.where(valid, acc, -jnp.inf)
        return carry

    lax.fori_loop(0, ntrip, score_trip, 0)

    def count(preds):
        flat = [(n, p) for n in batch for p in preds[n]]

        def cnt_block(i, cnts):
            xs = [sc_ref[n, pair_ds(i), :] for n in batch]
            out = []
            for (n, pred), cnt in zip(flat, cnts):
                hit = pred(xs[n]).astype(I32)
                parts = [hit[r:r + SUBLANES, :] for r in range(0, KP, SUBLANES)]
                while len(parts) > 1:
                    parts = [parts[k] + parts[k + 1] for k in range(0, len(parts), 2)]
                out.append(cnt + parts[0])
            return tuple(out)

        zero = jnp.zeros((SUBLANES, LANES), I32)
        cnts = lax.fori_loop(0, npair, cnt_block, (zero,) * len(flat))
        sums = [jnp.sum(c, axis=0, keepdims=True) for c in cnts]
        per = len(flat) // DSA_NB
        return [tuple(sums[n * per:(n + 1) * per]) for n in batch]

    select_all = jnp.float32(-3.0e38)

    @pl.when((pl.program_id(0) == 0) & (j == 0))
    def _():
        plane_ref[...] = jnp.zeros_like(plane_ref)

    def shift_const(x, n):
        return jnp.full(x.shape, n, I32)

    def build_planes(i, carry):
        for n in batch:
            bits = lax.bitcast_convert_type(sc_ref[n, pair_ds(i), :], I32)
            u = bits ^ (lax.shift_right_arithmetic(bits, shift_const(bits, 31)) | jnp.int32(INT_MIN))
            a = [u[SUBLANES * v:SUBLANES * (v + 1), :] for v in range(32)]
            step, mask = 16, 0x0000FFFF
            while step:
                for k in range(32):
                    if not k & step:
                        t = (a[k] ^ lax.shift_right_logical(a[k + step], shift_const(a[k], step))) & jnp.int32(mask)
                        a[k] = a[k] ^ t
                        a[k + step] = a[k + step] ^ lax.shift_left(t, shift_const(t, step))
                step >>= 1
                mask = (mask ^ (mask << step)) & 0xFFFFFFFF
            for r in range(32):
                plane_ref[n, i, r] = a[r]
        return carry

    lax.fori_loop(0, npair, build_planes, 0)

    n_pairs_max = plane_ref.shape[1]
    ones_v = jnp.full((SUBLANES, LANES), -1, I32)
    zeros_v = jnp.zeros((SUBLANES, LANES), I32)
    alive0 = tuple(tuple(jnp.where(i < npair, ones_v, zeros_v) for i in range(n_pairs_max)) for n in batch)

    def radix_step(t, carry):
        alive_all, k_all, u_all = carry
        out_alive, out_k, out_u = [], [], []
        counts = []
        for n in batch:
            alive = alive_all[n]
            hi = [plane_ref[n, i, 2 * t] for i in range(n_pairs_max)]
            lo = [plane_ref[n, i, 2 * t + 1] for i in range(n_pairs_max)]
            c11 = c1x = c01 = zeros_v
            for i in range(n_pairs_max):
                a1 = alive[i] & hi[i]
                a0 = alive[i] ^ a1
                c11 = c11 + lax.population_count(a1 & lo[i])
                c1x = c1x + lax.population_count(a1)
                c01 = c01 + lax.population_count(a0 & lo[i])
            counts.append((hi, lo, c11, c1x, c01))
        for n in batch:
            hi, lo, c11, c1x, c01 = counts[n]
            alive, k_left = alive_all[n], k_all[n]
            c11 = jnp.sum(c11, axis=0, keepdims=True)
            c1x = jnp.sum(c1x, axis=0, keepdims=True)
            c01 = jnp.sum(c01, axis=0, keepdims=True)
            in3 = k_left <= c11
            in32 = k_left <= c1x
            in321 = k_left <= c1x + c01
            bit_hi = in32
            bit_lo = in3 | (~in32 & in321)
            k_left = jnp.where(in3, k_left,
                               jnp.where(in32, k_left - c11,
                                         jnp.where(in321, k_left - c1x, k_left - c1x - c01)))
            f_hi = jnp.where(bit_hi, jnp.int32(0), jnp.int32(-1))
            f_lo = jnp.where(bit_lo, jnp.int32(0), jnp.int32(-1))
            out_alive.append(tuple(alive[i] & (hi[i] ^ f_hi) & (lo[i] ^ f_lo) for i in range(n_pairs_max)))
            digit = jnp.where(bit_hi, jnp.int32(2), jnp.int32(0)) | jnp.where(bit_lo, jnp.int32(1), jnp.int32(0))
            out_k.append(k_left)
            out_u.append(u_all[n] | lax.shift_left(digit, jnp.full(digit.shape, 30 - 2 * t, I32)))
        return tuple(out_alive), tuple(out_k), tuple(out_u)

    _, _, u_thr = lax.fori_loop(
        0, 16, radix_step,
        (alive0, (jnp.full((1, LANES), topk, I32),) * DSA_NB, (jnp.zeros((1, LANES), I32),) * DSA_NB))
    thr_fast = []
    for n in batch:
        thr_bits = jnp.where(u_thr[n] < 0, u_thr[n] ^ jnp.int32(INT_MIN), ~u_thr[n])
        tf = lax.bitcast_convert_type(thr_bits, F32)
        thr_fast.append(jnp.where(tf == -jnp.inf, select_all, tf))

    def ge_gt(n):
        return (lambda x: x >= thr_fast[n], lambda x: x > thr_fast[n])

    n_bad = jnp.int32(0)
    for n, (n_ge, n_gt) in zip(batch, count([ge_gt(n) for n in batch])):
        is_all = thr_fast[n] == select_all
        good = is_all | ((n_gt < topk) & (n_ge >= topk))
        thr_ref[n] = thr_fast[n]
        cnt_ref[n] = jnp.where(is_all, 0, n_ge)
        n_bad = n_bad + jnp.sum(jnp.where(good, 0, 1))

    @pl.when(n_bad > 0)
    def _():
        def key_to_float(key):
            bits = jnp.where(key < 0, key ^ jnp.int32(0x7FFFFFFF), key)
            return lax.bitcast_convert_type(bits, F32)

        def count_ge(cand_keys):
            cands = [key_to_float(c) for c in cand_keys]
            return [c[0] for c in count([((lambda x, n=n: x >= cands[n]),) for n in batch])]

        c0 = count_ge([jnp.zeros((1, LANES), I32)] * DSA_NB)
        base = tuple(jnp.where(c >= topk, jnp.int32(0), jnp.int32(INT_MIN)) for c in c0)
        base_cnt = tuple(jnp.where(c >= topk, c, 0) for c in c0)

        def bit_step(i, carry):
            base, base_cnt = carry
            cand = [b | jnp.left_shift(jnp.int32(1), 30 - i) for b in base]
            cs = count_ge(cand)
            ok = [c >= topk for c in cs]
            return (tuple(jnp.where(ok[n], cand[n], base[n]) for n in batch),
                    tuple(jnp.where(ok[n], cs[n], base_cnt[n]) for n in batch))

        base, base_cnt = lax.fori_loop(0, 31, bit_step, (base, base_cnt))
        for n in batch:
            thr_ref[n] = jnp.where(base[n] == INT_MIN, select_all, key_to_float(base[n]))
            cnt_ref[n] = base_cnt[n]

    thr = [thr_ref[n] for n in batch]
    base_cnt = [cnt_ref[n] for n in batch]

    most = base_cnt[0]
    for n in batch[1:]:
        most = jnp.maximum(most, base_cnt[n])

    @pl.when(jnp.max(most) > topk)
    def _():
        n_gt = count([((lambda x, n=n: x > thr[n]),) for n in batch])
        rr = lax.broadcasted_iota(I32, (KP, KP), 0)
        cc = lax.broadcasted_iota(I32, (KP, KP), 1)
        tri = (rr >= cc).astype(BF16)
        for n in batch:
            need = (topk - n_gt[n][0]).astype(F32)

            def strike(i, seen, n=n, need=need):
                x = sc_ref[n, pair_ds(i), :]
                eq = x == thr[n]
                rank = _dot(tri, jnp.where(eq, 1.0, 0.0).astype(BF16)) + seen
                sc_ref[n, pair_ds(i), :] = jnp.where(eq & (rank > need), -jnp.inf, x)
                return rank[KP - 1:KP, :]

            lax.fori_loop(0, npair, strike, jnp.zeros((1, LANES), F32))

    m_ref[...] = jnp.full_like(m_ref, M_INIT)
    acc_ref[...] = jnp.zeros_like(acc_ref)
    GW = DSA_GROUP * LANES
    ones_rows = jnp.ones((ONES_ROWS, KP), BF16)

    def attn_trip(u, carry):
        work = [(n, i) for i in (2 * u, 2 * u + 1) for n in batch]
        logits = [_dot_nt(pair_rows(ak_ref, n, i, 0), qz_ref[n]).astype(BF16) for n, i in work]
        for (n, i), s in zip(work, logits):
            bias = jnp.where(sc_ref[n, pair_ds(i), :] >= thr[n], 0.0, -jnp.inf).astype(BF16)
            sb = s + jnp.concatenate([bias] * DSA_HEADS, axis=1)
            m_old = m_ref[n]
            m_new = jnp.maximum(m_old, jnp.max(sb, axis=0, keepdims=True).astype(F32))
            alpha = jnp.exp(m_old - m_new)
            pb = jnp.exp(sb - m_new.astype(BF16))
            m_ref[n] = m_new
            vt = pair_rows(avt_ref, n, i, 1)
            for g in range(DSA_KV_HEADS):
                cols = slice(g * GW, (g + 1) * GW)
                vg = jnp.concatenate([vt[g * DSA_HD:(g + 1) * DSA_HD, :], ones_rows], axis=0)
                acc_ref[n, g] = alpha[:, cols] * acc_ref[n, g] + _dot(vg, pb[:, cols])
        return carry

    lax.fori_loop(0, ntrip, attn_trip, 0)

    for n in batch:
        heads = []
        for h in range(DSA_HEADS):
            a = acc_ref[n, h // DSA_GROUP][:, (h % DSA_GROUP) * LANES:(h % DSA_GROUP + 1) * LANES]
            heads.append(a[:DSA_HD, :] / a[DSA_HD:DSA_HD + 1, :])
        ot = jnp.concatenate(heads, axis=0)
        z = az_ref[n, 0]
        o_ref[n, 0] = (ot.T * (z * jax.nn.sigmoid(z))).astype(BF16)


def _dsa(ik2, ak, avt, aq, iq, iwt, az, B, nblk, topk):
    kern = functools.partial(_dsa_kernel, topk=topk, nblk=nblk)
    nb = DSA_NB
    whole = lambda b_, j: (b_, 0, 0, 0)
    qblk = lambda b_, j: (b_, j, 0, 0)
    by_block = lambda a: a.reshape(B, nblk, -1, a.shape[-1])
    out = pl.pallas_call(
        kern,
        grid=(B // nb, nblk),
        in_specs=[
            pl.BlockSpec((nb, nblk, LANES, LANES), whole),
            pl.BlockSpec((nb, nblk, LANES, LANES), whole),
            pl.BlockSpec((nb, nblk, LANES, LANES), whole),
            pl.BlockSpec((nb, 1, LANES, DSA_W), qblk),
            pl.BlockSpec((nb, 1, LANES, IDX_HEADS * IDX_HD), qblk),
            pl.BlockSpec((nb, 1, IDX_HEADS, LANES), qblk),
            pl.BlockSpec((nb, 1, LANES, DSA_W), qblk),
        ],
        out_specs=pl.BlockSpec((nb, 1, LANES, DSA_W), qblk),
        out_shape=jax.ShapeDtypeStruct((B, nblk, LANES, DSA_W), BF16),
        scratch_shapes=[
            pltpu.VMEM((nb, (nblk + 3) // 4 * 4 * LANES, LANES), F32),
            pltpu.VMEM((nb, DSA_HEADS * LANES, LANES), BF16),
            pltpu.VMEM((nb, IDX_HEADS * LANES, LANES), BF16),
            pltpu.VMEM((nb, 1, IDX_HEADS * LANES), F32),
            pltpu.VMEM((nb, 1, DSA_HEADS * LANES), F32),
            pltpu.VMEM((nb, DSA_KV_HEADS, DSA_HD + ONES_ROWS, DSA_GROUP * LANES), F32),
            pltpu.VMEM((nb, (nblk + 1) // 2, 32, SUBLANES, LANES), I32),
            pltpu.VMEM((nb, 1, LANES), F32),
            pltpu.VMEM((nb, 1, LANES), I32),
        ],
        compiler_params=pltpu.CompilerParams(vmem_limit_bytes=VMEM_LIMIT),
        name="dsa",
    )(by_block(ik2), by_block(ak), by_block(avt), by_block(aq), by_block(iq), by_block(iwt), by_block(az))
    return out.reshape(B * nblk, LANES, DSA_W)


def _out_kernel(mg_ref, md_ref, h_ref, w_ref, g_ref, b_ref, o_ref, *, alpha):
    y = _dot(mg_ref[0], w_ref[:GLA_W, :]) + _dot(md_ref[0], w_ref[GLA_W:, :])
    o_ref[0] = _layer_norm_rows(alpha * h_ref[0] + y, g_ref[...], b_ref[...])


def _out(mg, md, h, w, g, b, B, LP, alpha, drop_filler):
    rows = OUT_ROWS
    if drop_filler:
        S = LP - LANES
        grid, nout = (B, S // rows), S
        src_blk = lambda width: (pl.Element(1), pl.Element(rows), pl.Element(width))
        src = lambda b_, j: (b_, pl.multiple_of(LANES + j * rows, LANES), 0)
    else:
        grid, nout = (B * LP // rows,), LP
        src_blk = lambda width: (1, rows, width)
        mg, md, h = (a.reshape(1, B * LP, a.shape[-1]) for a in (mg, md, h))
        src = lambda i: (0, i, 0)
    dst = (lambda b_, j: (b_, j, 0)) if drop_filler else (lambda i: (0, i, 0))
    const = (lambda b_, j: (0, 0)) if drop_filler else (lambda i: (0, 0))
    out_shape = (B, nout, D_MODEL) if drop_filler else (1, B * LP, D_MODEL)
    kern = functools.partial(_out_kernel, alpha=alpha)
    return pl.pallas_call(
        kern,
        grid=grid,
        in_specs=[
            pl.BlockSpec(src_blk(GLA_W), src),
            pl.BlockSpec(src_blk(DSA_W), src),
            pl.BlockSpec(src_blk(D_MODEL), src),
            pl.BlockSpec((MIX_W, D_MODEL), const),
            pl.BlockSpec((1, D_MODEL), const),
            pl.BlockSpec((1, D_MODEL), const),
        ],
        out_specs=pl.BlockSpec((1, rows, D_MODEL), dst),
        out_shape=jax.ShapeDtypeStruct(out_shape, F32),
        name="out_proj_ln",
    )(mg, md, h, w, g, b)


def _pack_w_in(w):
    splits = (256, 256, 512, GLA_RANK, 512, 512, 128, 128, 512, IDX_HD, IDX_HEADS, 512)
    offs = np.cumsum((0,) + splits)
    gq, gk, gv, glr, gz, aq, ak, av, iq, ik, iw, az = [w[:, offs[i]:offs[i + 1]] for i in range(12)]

    def perm_heads(a):
        return a.reshape(D_MODEL, DSA_HEADS, DSA_HD)[:, HEAD_PERM, :].reshape(D_MODEL, DSA_W)

    pad = jnp.zeros((D_MODEL, LANES - IDX_HD - GLA_RANK - IDX_HEADS), w.dtype)
    packed = jnp.concatenate(
        [gq * (GLA_DK ** -0.5), gk, gv, gz, perm_heads(aq) * (DSA_HD ** -0.5), ak, av,
         iq * (IDX_HD ** -0.5), az, ik, glr, iw, pad], axis=1)
    return packed.astype(BF16)


def _rope_lane_tables(reps, LP):
    inv = ROPE_THETA ** (-jnp.arange(0, ROPE_DIM, 2, dtype=F32) / ROPE_DIM)
    pos = (jnp.arange(LP, dtype=F32) - PAD)[:, None]
    ang = pos * inv[None, :]
    cos, sin = jnp.cos(ang), jnp.sin(ang)
    ones = jnp.ones((LP, DSA_HD - ROPE_DIM), F32)
    zeros = jnp.zeros((LP, DSA_HD - ROPE_DIM), F32)
    zh = jnp.zeros((LP, ROPE_HALF), F32)
    c = jnp.concatenate([cos, cos, ones], axis=1)
    sa = jnp.concatenate([zh, sin, zeros], axis=1)
    sb = jnp.concatenate([-sin, zh, zeros], axis=1)
    tile = lambda t: jnp.tile(jnp.concatenate([t, t], axis=1), (reps, 1))
    return tile(c), tile(sa), tile(sb)


def kernel(x, meta_tokens, ln_in_g, ln_in_b, w_in, gla_wg2, gla_bg, gla_norm_g, idx_k_g, idx_k_b,
           w_out, ln_g, ln_b):
    B, S, D = x.shape
    depth = w_in.shape[0]
    nblk = S // LANES + 1
    LP = nblk * LANES
    R = B * LP
    topk = min(TOPK_MAX, S // 4)
    alpha = (2.0 * depth) ** 0.25
    tm = PROJ_ROWS

    meta_pad = jnp.concatenate([jnp.zeros((PAD, D), x.dtype), meta_tokens.astype(x.dtype)], axis=0)
    h = _embed(x, meta_pad, ln_in_g.reshape(1, D), ln_in_b.reshape(1, D))
    cos_t, sa_t, sb_t = _rope_lane_tables(tm // math.gcd(tm, LP), LP)

    for i in range(depth):
        w = _pack_w_in(w_in[i])
        wg2 = jnp.zeros((LANES, 256), F32).at[MISC_GLR:MISC_GLR + GLA_RANK].set(gla_wg2[i])
        wg2h = wg2.astype(BF16)
        wg2l = (wg2 - wg2h.astype(F32)).astype(BF16)
        ikg = jnp.zeros((1, LANES), F32).at[0, :IDX_HD].set(idx_k_g[i])
        ikb = jnp.zeros((1, LANES), F32).at[0, :IDX_HD].set(idx_k_b[i])
        (gq, gk, gv, glog, gz, aq, ak, avt, iq, ik2, iwt, az) = _proj(
            h.reshape(R, D), w, wg2h, wg2l, gla_bg[i].reshape(1, 256), ikg, ikb,
            cos_t, sa_t, sb_t, tm)
        mix_gla = _gla(gq, gk, glog, gv, gz, gla_norm_g[i].reshape(1, GLA_DV), B, LP)
        mix_dsa = _dsa(ik2, ak, avt, aq, iq, iwt, az, B, nblk, topk)
        h = _out(mix_gla, mix_dsa.reshape(B, LP, DSA_W), h.reshape(B, LP, D),
                 w_out[i].astype(BF16), ln_g[i].reshape(1, D), ln_b[i].reshape(1, D),
                 B, LP, alpha, drop_filler=(i == depth - 1))
    return h.reshape(B, S, D)
```

```python
import functools
import math

import numpy as np
import jax
import jax.numpy as jnp
from jax import lax
from jax.experimental import pallas as pl
from jax.experimental.pallas import tpu as pltpu

F32 = jnp.float32
BF16 = jnp.bfloat16
I32 = jnp.int32

D_MODEL = 1024
N_META = 16
ROPE_THETA = 500000.0
LN_EPS = 1e-5
GLA_HEADS = 4
GLA_DK = 64
GLA_DV = 128
GLA_RANK = 16
GLA_TAU = 16.0
GLA_CHUNK = 64
GLA_W = GLA_HEADS * GLA_DV
DSA_HEADS = 8
DSA_KV_HEADS = 2
DSA_GROUP = DSA_HEADS // DSA_KV_HEADS
DSA_HD = 64
DSA_W = DSA_HEADS * DSA_HD
IDX_HEADS = 8
IDX_HD = 64
TOPK_MAX = 256
ROPE_DIM = DSA_HD // 4
ROPE_HALF = ROPE_DIM // 2
MIX_W = GLA_W + DSA_W

LANES = 128
SUBLANES = 8
ONES_ROWS = 16
PAD = LANES - N_META
INT_MIN = -(2 ** 31)
M_INIT = -(2.0 ** 100)
VMEM_LIMIT = 48 * 1024 * 1024
PROJ_ROWS = 512
OUT_ROWS = 512

O_GQ, O_GK, O_GV, O_GZ = 0, 256, 512, 1024
O_AQ, O_AK, O_AV, O_IQ, O_AZ, O_MISC = 1536, 2048, 2176, 2304, 2816, 3328
W_PACKED = 3456
MISC_GLR = IDX_HD
MISC_IW = IDX_HD + GLA_RANK
HEAD_PERM = tuple(m + DSA_GROUP * r for m in range(DSA_GROUP) for r in range(2))


def _dot(a, b):
    return jnp.dot(a, b, preferred_element_type=F32)


def _dot_nt(a, b):
    return lax.dot_general(a, b, (((1,), (1,)), ((), ())), preferred_element_type=F32)


def _dot_tn(a, b):
    return lax.dot_general(a, b, (((0,), (0,)), ((), ())), preferred_element_type=F32)


def _layer_norm_rows(u, g, b):
    mu = jnp.mean(u, axis=-1, keepdims=True)
    d = u - mu
    var = jnp.mean(d * d, axis=-1, keepdims=True)
    return d * lax.rsqrt(var + LN_EPS) * g + b


def _embed_kernel(x_ref, meta_ref, g_ref, b_ref, o_ref):
    g, b = g_ref[...], b_ref[...]
    o_ref[0] = _layer_norm_rows(meta_ref[...], g, b)
    for c in range(x_ref.shape[1]):
        o_ref[c + 1] = _layer_norm_rows(x_ref[0, c], g, b)


def _embed(x, meta_pad, g, b):
    B, S, D = x.shape
    nblk = S // LANES + 1
    x4 = x.reshape(B, S // LANES, LANES, D)
    return pl.pallas_call(
        _embed_kernel,
        grid=(B,),
        in_specs=[
            pl.BlockSpec((1, S // LANES, LANES, D), lambda b_: (b_, 0, 0, 0)),
            pl.BlockSpec((LANES, D), lambda b_: (0, 0)),
            pl.BlockSpec((1, D), lambda b_: (0, 0)),
            pl.BlockSpec((1, D), lambda b_: (0, 0)),
        ],
        out_specs=pl.BlockSpec((nblk, LANES, D), lambda b_: (b_, 0, 0)),
        out_shape=jax.ShapeDtypeStruct((B * nblk, LANES, D), F32),
        compiler_params=pltpu.CompilerParams(vmem_limit_bytes=VMEM_LIMIT),
        name="embed_ln",
    )(x4, meta_pad, g, b)


def _rope_slab(x, c, sa, sb):
    return x * c + pltpu.roll(x, ROPE_HALF, 1) * sa + pltpu.roll(x, LANES - ROPE_HALF, 1) * sb


def _proj_kernel(h_ref, w_ref, wg2h_ref, wg2l_ref, bg_ref, ikg_ref, ikb_ref,
                 cos_ref, sa_ref, sb_ref,
                 gq_ref, gk_ref, gv_ref, glog_ref, gz_ref, aq_ref, ak_ref, avt_ref,
                 iq_ref, ik_ref, iwt_ref, az_ref, *, tm):
    hb = h_ref[...].astype(BF16)

    def seg(o, w):
        return _dot(hb, w_ref[:, o:o + w])

    c, sa, sb = cos_ref[...], sa_ref[...], sb_ref[...]
    misc = seg(O_MISC, LANES)
    av = seg(O_AV, LANES)
    ak = seg(O_AK, LANES)
    aq = seg(O_AQ, 512)

    misc_t_scale = IDX_HEADS ** -0.5
    for r in range(tm // LANES):
        rows = slice(r * LANES, (r + 1) * LANES)
        avt_ref[r] = av[rows, :].T.astype(BF16)
        iwt_ref[r] = misc[rows, :].T[MISC_IW:MISC_IW + IDX_HEADS, :] * misc_t_scale
    ak_ref[...] = _rope_slab(ak, c, sa, sb).astype(BF16)
    iq = seg(O_IQ, 512)

    m_hi = misc.astype(BF16)
    m_lo = (misc - m_hi.astype(F32)).astype(BF16)
    xg = (_dot(m_hi, wg2h_ref[...]) + _dot(m_lo, wg2h_ref[...]) + _dot(m_hi, wg2l_ref[...])
          + bg_ref[...])
    gq_ref[...] = seg(O_GQ, 256)
    glog_ref[...] = (jnp.minimum(xg, 0.0) - jnp.log1p(jnp.exp(-jnp.abs(xg)))) * (1.0 / GLA_TAU)

    lane = lax.broadcasted_iota(I32, (1, LANES), 1)
    is_key = lane < IDX_HD
    mu = jnp.sum(jnp.where(is_key, misc, 0.0), axis=-1, keepdims=True) * (1.0 / IDX_HD)
    d = jnp.where(is_key, misc - mu, 0.0)
    var = jnp.sum(d * d, axis=-1, keepdims=True) * (1.0 / IDX_HD)
    ikn = d * lax.rsqrt(var + LN_EPS) * ikg_ref[...] + ikb_ref[...]
    ikr = _rope_slab(ikn, c, sa, sb)
    gk_ref[...] = seg(O_GK, 256)
    ik_ref[...] = jnp.where(is_key, ikr, pltpu.roll(ikr, IDX_HD, 1)).astype(BF16)

    gv = seg(O_GV, 512)
    for m in range(4):
        sl = slice(m * LANES, (m + 1) * LANES)
        aq_ref[:, sl] = _rope_slab(aq[:, sl], c, sa, sb).astype(BF16)
    gv_ref[...] = gv.astype(BF16)
    gz = seg(O_GZ, 512)
    for m in range(4):
        sl = slice(m * LANES, (m + 1) * LANES)
        iq_ref[:, sl] = _rope_slab(iq[:, sl], c, sa, sb).astype(BF16)
    gz_ref[...] = gz
    az_ref[...] = seg(O_AZ, 512)


def _proj(h2, w, wg2h, wg2l, bg, ikg, ikb, cos_t, sa_t, sb_t, tm):
    R = h2.shape[0]
    nt = R // tm
    nsub = tm // LANES
    row = lambda i: (i, 0)
    const = lambda i: (0, 0)
    table_tiles = cos_t.shape[0] // tm
    table_row = lambda i: (i % table_tiles, 0)
    kern = functools.partial(_proj_kernel, tm=tm)
    out_shape = (
        jax.ShapeDtypeStruct((R, 256), F32),
        jax.ShapeDtypeStruct((R, 256), F32),
        jax.ShapeDtypeStruct((R, 512), BF16),
        jax.ShapeDtypeStruct((R, 256), F32),
        jax.ShapeDtypeStruct((R, 512), F32),
        jax.ShapeDtypeStruct((R, 512), BF16),
        jax.ShapeDtypeStruct((R, LANES), BF16),
        jax.ShapeDtypeStruct((R // LANES, LANES, LANES), BF16),
        jax.ShapeDtypeStruct((R, 512), BF16),
        jax.ShapeDtypeStruct((R, LANES), BF16),
        jax.ShapeDtypeStruct((R // LANES, IDX_HEADS, LANES), F32),
        jax.ShapeDtypeStruct((R, 512), F32),
    )
    out_specs = (
        pl.BlockSpec((tm, 256), row),
        pl.BlockSpec((tm, 256), row),
        pl.BlockSpec((tm, 512), row),
        pl.BlockSpec((tm, 256), row),
        pl.BlockSpec((tm, 512), row),
        pl.BlockSpec((tm, 512), row),
        pl.BlockSpec((tm, LANES), row),
        pl.BlockSpec((nsub, LANES, LANES), lambda i: (i, 0, 0)),
        pl.BlockSpec((tm, 512), row),
        pl.BlockSpec((tm, LANES), row),
        pl.BlockSpec((nsub, IDX_HEADS, LANES), lambda i: (i, 0, 0)),
        pl.BlockSpec((tm, 512), row),
    )
    in_specs = [
        pl.BlockSpec((tm, D_MODEL), row),
        pl.BlockSpec((D_MODEL, W_PACKED), const),
        pl.BlockSpec((LANES, 256), const),
        pl.BlockSpec((LANES, 256), const),
        pl.BlockSpec((1, 256), const),
        pl.BlockSpec((1, LANES), const),
        pl.BlockSpec((1, LANES), const),
        pl.BlockSpec((tm, LANES), table_row),
        pl.BlockSpec((tm, LANES), table_row),
        pl.BlockSpec((tm, LANES), table_row),
    ]
    return pl.pallas_call(
        kern, grid=(nt,), in_specs=in_specs, out_specs=out_specs, out_shape=out_shape,
        compiler_params=pltpu.CompilerParams(vmem_limit_bytes=VMEM_LIMIT),
        name="in_proj",
    )(h2, w, wg2h, wg2l, bg, ikg, ikb, cos_t, sa_t, sb_t)


GLA_NB = 8
GLA_PIECES = 17


def _gla_kernel(q_ref, k_ref, g_ref, v_ref, z_ref, ng_ref, o_ref, st_ref, *, n_chunks):
    C = GLA_CHUNK
    piece = pl.program_id(1)

    @pl.when(piece == 0)
    def _():
        st_ref[...] = jnp.zeros_like(st_ref)

    lane = lax.broadcasted_iota(I32, (1, LANES), 1)
    head_lanes = (lane < GLA_DK, lane >= GLA_DK)
    rr = lax.broadcasted_iota(I32, (C, C), 0)
    cc = lax.broadcasted_iota(I32, (C, C), 1)
    causal = rr >= cc
    tril = causal.astype(BF16)
    row = lax.broadcasted_iota(I32, (C, 1), 0)
    ng = ng_ref[...]
    row0 = piece * (n_chunks * C)

    def body(ci, carry):
        r0 = pl.multiple_of(ci * C, C)
        rows = pl.ds(r0, C)
        valid = (row0 + r0 + row) >= PAD
        pairs = [(bi, pr) for bi in range(GLA_NB) for pr in range(GLA_HEADS // 2)]
        bs = []
        for bi, pr in pairs:
            g = g_ref[bi, rows, pr * LANES:(pr + 1) * LANES]
            g_hi = g.astype(BF16)
            g_lo = (g - g_hi.astype(F32)).astype(BF16)
            bs.append(_dot(tril, g_hi) + _dot(tril, g_lo))
        ops = []
        for (bi, pr), b in zip(pairs, bs):
            pl_ = slice(pr * LANES, (pr + 1) * LANES)
            q = q_ref[bi, rows, pl_]
            k = jnp.where(valid, k_ref[bi, rows, pl_], 0.0)
            b_mid = b[C // 2 - 1:C // 2, :]
            b_last = b[C - 1:C, :]
            q_in = (q * jnp.exp(b - b_mid)).astype(BF16)
            k_in = (k * jnp.exp(b_mid - b)).astype(BF16)
            q_st = (q * jnp.exp(b)).astype(BF16)
            k_st = (k * jnp.exp(b_last - b)).astype(BF16)
            ops.append((q_in, k_in, q_st, k_st, jnp.exp(b_last)))
        heads = [(bi, pr, h) for bi, pr in pairs for h in range(2)]
        zero = jnp.zeros((C, LANES), BF16)
        part = []
        for n, (bi, pr, h) in enumerate(heads):
            q_in, k_in, q_st, k_st, decay = ops[n // 2]
            hl = head_lanes[h]
            head = 2 * pr + h
            cols = slice(head * GLA_DV, (head + 1) * GLA_DV)
            v = jnp.where(valid, v_ref[bi, rows, cols], jnp.zeros((), BF16))
            st = st_ref[bi, head]
            a = _dot_nt(jnp.where(hl, q_in, zero), k_in)
            o_st = _dot_nt(jnp.where(hl, q_st, zero), st.astype(BF16))
            st_ref[bi, head] = st * decay + _dot_tn(v, jnp.where(hl, k_st, zero))
            part.append((a, o_st, v, cols))
        for (bi, pr, h), (a, o_st, v, cols) in zip(heads, part):
            o = _dot(jnp.where(causal, a, 0.0).astype(BF16), v) + o_st
            on = o * lax.rsqrt(jnp.mean(o * o, axis=-1, keepdims=True) + LN_EPS) * ng
            z = z_ref[bi, rows, cols]
            o_ref[bi, rows, cols] = (on * (z * jax.nn.sigmoid(z))).astype(BF16)
        return carry

    lax.fori_loop(0, n_chunks, body, 0)


def _gla(gq, gk, glog, gv, gz, ng, B, LP):
    rows = LP // GLA_PIECES
    kern = functools.partial(_gla_kernel, n_chunks=rows // GLA_CHUNK)
    blk = lambda b_, p: (b_, p, 0)
    return pl.pallas_call(
        kern,
        grid=(B // GLA_NB, GLA_PIECES),
        in_specs=[
            pl.BlockSpec((GLA_NB, rows, 256), blk),
            pl.BlockSpec((GLA_NB, rows, 256), blk),
            pl.BlockSpec((GLA_NB, rows, 256), blk),
            pl.BlockSpec((GLA_NB, rows, GLA_W), blk),
            pl.BlockSpec((GLA_NB, rows, GLA_W), blk),
            pl.BlockSpec((1, GLA_DV), lambda b_, p: (0, 0)),
        ],
        out_specs=pl.BlockSpec((GLA_NB, rows, GLA_W), blk),
        out_shape=jax.ShapeDtypeStruct((B, LP, GLA_W), BF16),
        scratch_shapes=[pltpu.VMEM((GLA_NB, GLA_HEADS, GLA_DV, LANES), F32)],
        compiler_params=pltpu.CompilerParams(vmem_limit_bytes=VMEM_LIMIT),
        name="gla",
    )(gq.reshape(B, LP, 256), gk.reshape(B, LP, 256), glog.reshape(B, LP, 256),
      gv.reshape(B, LP, GLA_W), gz.reshape(B, LP, GLA_W), ng)


DSA_NB = 2


def _dsa_kernel(ik_ref, ak_ref, avt_ref, aq_ref, iq_ref, iwt_ref, az_ref, o_ref,
                sc_ref, qz_ref, iqz_ref, w_ref, m_ref, acc_ref, plane_ref, thr_ref, cnt_ref,
                *, topk, nblk):
    j = pl.program_id(1)
    npair = (j + 2) // 2
    KP = 2 * LANES
    batch = range(DSA_NB)

    def pair_rows(ref, n, i, axis):
        kb1 = jnp.minimum(2 * i + 1, nblk - 1)
        return jnp.concatenate([ref[n, 2 * i], ref[n, kb1]], axis=axis)

    def pair_ds(i):
        return pl.ds(pl.multiple_of(i * KP, KP), KP)

    lane = lax.broadcasted_iota(I32, (1, LANES), 1)
    low = lane < DSA_HD
    zero_b = jnp.zeros((LANES, LANES), BF16)
    for n in batch:
        for m in range(4):
            sl = slice(m * LANES, (m + 1) * LANES)
            qs = aq_ref[n, 0, :, sl]
            qz_ref[n, m * LANES:(m + 1) * LANES, :] = jnp.where(low, qs, zero_b)
            qz_ref[n, (m + 4) * LANES:(m + 5) * LANES, :] = jnp.where(low, zero_b, qs)
            iqs = iq_ref[n, 0, :, sl]
            iqz_ref[n, (2 * m) * LANES:(2 * m + 1) * LANES, :] = jnp.where(low, iqs, zero_b)
            iqz_ref[n, (2 * m + 1) * LANES:(2 * m + 2) * LANES, :] = jnp.where(low, zero_b, iqs)
        for h in range(IDX_HEADS):
            w_ref[n, :, h * LANES:(h + 1) * LANES] = iwt_ref[n, 0, h:h + 1, :]
    t_pos = j * LANES + lane
    s_loc = lax.broadcasted_iota(I32, (KP, 1), 0)

    def score_pairs(pairs):
        work = [(n, i) for i in pairs for n in batch]
        dots = [_dot_nt(pair_rows(ik_ref, n, i, 0), iqz_ref[n]) for n, i in work]
        for (n, i), d in zip(work, dots):
            d = jnp.maximum(d, 0.0) * w_ref[n]
            acc = d[:, :LANES]
            for h in range(1, IDX_HEADS):
                acc = acc + d[:, h * LANES:(h + 1) * LANES]
            s_pos = i * KP + s_loc
            valid = (s_pos <= t_pos) & ((s_pos >= PAD) | (t_pos < PAD))
            sc_ref[n, pair_ds(i), :] = jnp.where(valid, acc, -jnp.inf)

    def two_pair_trips(body):
        def trip(u, carry):
            body((2 * u, 2 * u + 1))
            return carry

        lax.fori_loop(0, npair // 2, trip, 0)

        @pl.when(npair % 2 == 1)
        def _():
            body((npair - 1,))

    two_pair_trips(score_pairs)

    def count(preds):
        flat = [(n, p) for n in batch for p in preds[n]]

        def cnt_block(i, cnts):
            xs = [sc_ref[n, pair_ds(i), :] for n in batch]
            out = []
            for (n, pred), cnt in zip(flat, cnts):
                hit = pred(xs[n]).astype(I32)
                parts = [hit[r:r + SUBLANES, :] for r in range(0, KP, SUBLANES)]
                while len(parts) > 1:
                    parts = [parts[k] + parts[k + 1] for k in range(0, len(parts), 2)]
                out.append(cnt + parts[0])
            return tuple(out)

        zero = jnp.zeros((SUBLANES, LANES), I32)
        cnts = lax.fori_loop(0, npair, cnt_block, (zero,) * len(flat))
        sums = [jnp.sum(c, axis=0, keepdims=True) for c in cnts]
        per = len(flat) // DSA_NB
        return [tuple(sums[n * per:(n + 1) * per]) for n in batch]

    select_all = jnp.float32(-3.0e38)

    @pl.when((pl.program_id(0) == 0) & (j == 0))
    def _():
        plane_ref[...] = jnp.zeros_like(plane_ref)

    def shift_const(x, n):
        return jnp.full(x.shape, n, I32)

    def build_planes(i, carry):
        for n in batch:
            bits = lax.bitcast_convert_type(sc_ref[n, pair_ds(i), :], I32)
            u = bits ^ (lax.shift_right_arithmetic(bits, shift_const(bits, 31)) | jnp.int32(INT_MIN))
            a = [u[SUBLANES * v:SUBLANES * (v + 1), :] for v in range(32)]
            step, mask = 16, 0x0000FFFF
            while step:
                for k in range(32):
                    if not k & step:
                        t = (a[k] ^ lax.shift_right_logical(a[k + step], shift_const(a[k], step))) & jnp.int32(mask)
                        a[k] = a[k] ^ t
                        a[k + step] = a[k + step] ^ lax.shift_left(t, shift_const(t, step))
                step >>= 1
                mask = (mask ^ (mask << step)) & 0xFFFFFFFF
            for r in range(32):
                plane_ref[n, i, r] = a[r]
        return carry

    lax.fori_loop(0, npair, build_planes, 0)

    n_pairs_max = plane_ref.shape[1]
    ones_v = jnp.full((SUBLANES, LANES), -1, I32)
    zeros_v = jnp.zeros((SUBLANES, LANES), I32)
    alive0 = tuple(tuple(jnp.where(i < npair, ones_v, zeros_v) for i in range(n_pairs_max)) for n in batch)

    def radix_step(t, carry):
        alive_all, k_all, u_all = carry
        out_alive, out_k, out_u = [], [], []
        counts = []
        for n in batch:
            alive = alive_all[n]
            hi = [plane_ref[n, i, 2 * t] for i in range(n_pairs_max)]
            lo = [plane_ref[n, i, 2 * t + 1] for i in range(n_pairs_max)]
            c11 = c1x = c01 = zeros_v
            for i in range(n_pairs_max):
                a1 = alive[i] & hi[i]
                a0 = alive[i] ^ a1
                c11 = c11 + lax.population_count(a1 & lo[i])
                c1x = c1x + lax.population_count(a1)
                c01 = c01 + lax.population_count(a0 & lo[i])
            counts.append((hi, lo, c11, c1x, c01))
        for n in batch:
            hi, lo, c11, c1x, c01 = counts[n]
            alive, k_left = alive_all[n], k_all[n]
            c11 = jnp.sum(c11, axis=0, keepdims=True)
            c1x = jnp.sum(c1x, axis=0, keepdims=True)
            c01 = jnp.sum(c01, axis=0, keepdims=True)
            in3 = k_left <= c11
            in32 = k_left <= c1x
            in321 = k_left <= c1x + c01
            bit_hi = in32
            bit_lo = in3 | (~in32 & in321)
            k_left = jnp.where(in3, k_left,
                               jnp.where(in32, k_left - c11,
                                         jnp.where(in321, k_left - c1x, k_left - c1x - c01)))
            f_hi = jnp.where(bit_hi, jnp.int32(0), jnp.int32(-1))
            f_lo = jnp.where(bit_lo, jnp.int32(0), jnp.int32(-1))
            out_alive.append(tuple(alive[i] & (hi[i] ^ f_hi) & (lo[i] ^ f_lo) for i in range(n_pairs_max)))
            digit = jnp.where(bit_hi, jnp.int32(2), jnp.int32(0)) | jnp.where(bit_lo, jnp.int32(1), jnp.int32(0))
            out_k.append(k_left)
            out_u.append(u_all[n] | lax.shift_left(digit, jnp.full(digit.shape, 30 - 2 * t, I32)))
        return tuple(out_alive), tuple(out_k), tuple(out_u)

    _, _, u_thr = lax.fori_loop(
        0, 16, radix_step,
        (alive0, (jnp.full((1, LANES), topk, I32),) * DSA_NB, (jnp.zeros((1, LANES), I32),) * DSA_NB))
    thr_fast = []
    for n in batch:
        thr_bits = jnp.where(u_thr[n] < 0, u_thr[n] ^ jnp.int32(INT_MIN), ~u_thr[n])
        tf = lax.bitcast_convert_type(thr_bits, F32)
        thr_fast.append(jnp.where(tf == -jnp.inf, select_all, tf))

    def ge_gt(n):
        return (lambda x: x >= thr_fast[n], lambda x: x > thr_fast[n])

    n_bad = jnp.int32(0)
    for n, (n_ge, n_gt) in zip(batch, count([ge_gt(n) for n in batch])):
        is_all = thr_fast[n] == select_all
        good = is_all | ((n_gt < topk) & (n_ge >= topk))
        thr_ref[n] = thr_fast[n]
        cnt_ref[n] = jnp.where(is_all, 0, n_ge)
        n_bad = n_bad + jnp.sum(jnp.where(good, 0, 1))

    @pl.when(n_bad > 0)
    def _():
        def key_to_float(key):
            bits = jnp.where(key < 0, key ^ jnp.int32(0x7FFFFFFF), key)
            return lax.bitcast_convert_type(bits, F32)

        def count_ge(cand_keys):
            cands = [key_to_float(c) for c in cand_keys]
            return [c[0] for c in count([((lambda x, n=n: x >= cands[n]),) for n in batch])]

        c0 = count_ge([jnp.zeros((1, LANES), I32)] * DSA_NB)
        base = tuple(jnp.where(c >= topk, jnp.int32(0), jnp.int32(INT_MIN)) for c in c0)
        base_cnt = tuple(jnp.where(c >= topk, c, 0) for c in c0)

        def bit_step(i, carry):
            base, base_cnt = carry
            cand = [b | jnp.left_shift(jnp.int32(1), 30 - i) for b in base]
            cs = count_ge(cand)
            ok = [c >= topk for c in cs]
            return (tuple(jnp.where(ok[n], cand[n], base[n]) for n in batch),
                    tuple(jnp.where(ok[n], cs[n], base_cnt[n]) for n in batch))

        base, base_cnt = lax.fori_loop(0, 31, bit_step, (base, base_cnt))
        for n in batch:
            thr_ref[n] = jnp.where(base[n] == INT_MIN, select_all, key_to_float(base[n]))
            cnt_ref[n] = base_cnt[n]

    thr = [thr_ref[n] for n in batch]
    base_cnt = [cnt_ref[n] for n in batch]

    most = base_cnt[0]
    for n in batch[1:]:
        most = jnp.maximum(most, base_cnt[n])

    @pl.when(jnp.max(most) > topk)
    def _():
        n_gt = count([((lambda x, n=n: x > thr[n]),) for n in batch])
        rr = lax.broadcasted_iota(I32, (KP, KP), 0)
        cc = lax.broadcasted_iota(I32, (KP, KP), 1)
        tri = (rr >= cc).astype(BF16)
        for n in batch:
            need = (topk - n_gt[n][0]).astype(F32)

            def strike(i, seen, n=n, need=need):
                x = sc_ref[n, pair_ds(i), :]
                eq = x == thr[n]
                rank = _dot(tri, jnp.where(eq, 1.0, 0.0).astype(BF16)) + seen
                sc_ref[n, pair_ds(i), :] = jnp.where(eq & (rank > need), -jnp.inf, x)
                return rank[KP - 1:KP, :]

            lax.fori_loop(0, npair, strike, jnp.zeros((1, LANES), F32))

    m_ref[...] = jnp.full_like(m_ref, M_INIT)
    acc_ref[...] = jnp.zeros_like(acc_ref)
    GW = DSA_GROUP * LANES
    ones_rows = jnp.ones((ONES_ROWS, KP), BF16)

    def attn_pairs(pairs):
        work = [(n, i) for i in pairs for n in batch]
        logits = [_dot_nt(pair_rows(ak_ref, n, i, 0), qz_ref[n]).astype(BF16) for n, i in work]
        for (n, i), s in zip(work, logits):
            bias = jnp.where(sc_ref[n, pair_ds(i), :] >= thr[n], 0.0, -jnp.inf).astype(BF16)
            sb = s + jnp.concatenate([bias] * DSA_HEADS, axis=1)
            m_old = m_ref[n]
            m_new = jnp.maximum(m_old, jnp.max(sb, axis=0, keepdims=True).astype(F32))
            alpha = jnp.exp(m_old - m_new)
            pb = jnp.exp(sb - m_new.astype(BF16))
            m_ref[n] = m_new
            vt = pair_rows(avt_ref, n, i, 1)
            for g in range(DSA_KV_HEADS):
                cols = slice(g * GW, (g + 1) * GW)
                vg = jnp.concatenate([vt[g * DSA_HD:(g + 1) * DSA_HD, :], ones_rows], axis=0)
                acc_ref[n, g] = alpha[:, cols] * acc_ref[n, g] + _dot(vg, pb[:, cols])

    two_pair_trips(attn_pairs)

    for n in batch:
        heads = []
        for h in range(DSA_HEADS):
            a = acc_ref[n, h // DSA_GROUP][:, (h % DSA_GROUP) * LANES:(h % DSA_GROUP + 1) * LANES]
            heads.append(a[:DSA_HD, :] / a[DSA_HD:DSA_HD + 1, :])
        ot = jnp.concatenate(heads, axis=0)
        z = az_ref[n, 0]
        o_ref[n, 0] = (ot.T * (z * jax.nn.sigmoid(z))).astype(BF16)


def _dsa(ik2, ak, avt, aq, iq, iwt, az, B, nblk, topk):
    kern = functools.partial(_dsa_kernel, topk=topk, nblk=nblk)
    nb = DSA_NB
    whole = lambda b_, j: (b_, 0, 0, 0)
    qblk = lambda b_, j: (b_, j, 0, 0)
    by_block = lambda a: a.reshape(B, nblk, -1, a.shape[-1])
    out = pl.pallas_call(
        kern,
        grid=(B // nb, nblk),
        in_specs=[
            pl.BlockSpec((nb, nblk, LANES, LANES), whole),
            pl.BlockSpec((nb, nblk, LANES, LANES), whole),
            pl.BlockSpec((nb, nblk, LANES, LANES), whole),
            pl.BlockSpec((nb, 1, LANES, DSA_W), qblk),
            pl.BlockSpec((nb, 1, LANES, IDX_HEADS * IDX_HD), qblk),
            pl.BlockSpec((nb, 1, IDX_HEADS, LANES), qblk),
            pl.BlockSpec((nb, 1, LANES, DSA_W), qblk),
        ],
        out_specs=pl.BlockSpec((nb, 1, LANES, DSA_W), qblk),
        out_shape=jax.ShapeDtypeStruct((B, nblk, LANES, DSA_W), BF16),
        scratch_shapes=[
            pltpu.VMEM((nb, (nblk + 1) // 2 * 2 * LANES, LANES), F32),
            pltpu.VMEM((nb, DSA_HEADS * LANES, LANES), BF16),
            pltpu.VMEM((nb, IDX_HEADS * LANES, LANES), BF16),
            pltpu.VMEM((nb, 1, IDX_HEADS * LANES), F32),
            pltpu.VMEM((nb, 1, DSA_HEADS * LANES), F32),
            pltpu.VMEM((nb, DSA_KV_HEADS, DSA_HD + ONES_ROWS, DSA_GROUP * LANES), F32),
            pltpu.VMEM((nb, (nblk + 1) // 2, 32, SUBLANES, LANES), I32),
            pltpu.VMEM((nb, 1, LANES), F32),
            pltpu.VMEM((nb, 1, LANES), I32),
        ],
        compiler_params=pltpu.CompilerParams(vmem_limit_bytes=VMEM_LIMIT),
        name="dsa",
    )(by_block(ik2), by_block(ak), by_block(avt), by_block(aq), by_block(iq), by_block(iwt), by_block(az))
    return out.reshape(B * nblk, LANES, DSA_W)


def _out_kernel(mg_ref, md_ref, h_ref, w_ref, g_ref, b_ref, o_ref, *, alpha):
    y = _dot(mg_ref[0], w_ref[:GLA_W, :]) + _dot(md_ref[0], w_ref[GLA_W:, :])
    o_ref[0] = _layer_norm_rows(alpha * h_ref[0] + y, g_ref[...], b_ref[...])


def _out(mg, md, h, w, g, b, B, LP, alpha, drop_filler):
    rows = OUT_ROWS
    if drop_filler:
        S = LP - LANES
        grid, nout = (B, S // rows), S
        src_blk = lambda width: (pl.Element(1), pl.Element(rows), pl.Element(width))
        src = lambda b_, j: (b_, pl.multiple_of(LANES + j * rows, LANES), 0)
    else:
        grid, nout = (B * LP // rows,), LP
        src_blk = lambda width: (1, rows, width)
        mg, md, h = (a.reshape(1, B * LP, a.shape[-1]) for a in (mg, md, h))
        src = lambda i: (0, i, 0)
    dst = (lambda b_, j: (b_, j, 0)) if drop_filler else (lambda i: (0, i, 0))
    const = (lambda b_, j: (0, 0)) if drop_filler else (lambda i: (0, 0))
    out_shape = (B, nout, D_MODEL) if drop_filler else (1, B * LP, D_MODEL)
    kern = functools.partial(_out_kernel, alpha=alpha)
    return pl.pallas_call(
        kern,
        grid=grid,
        in_specs=[
            pl.BlockSpec(src_blk(GLA_W), src),
            pl.BlockSpec(src_blk(DSA_W), src),
            pl.BlockSpec(src_blk(D_MODEL), src),
            pl.BlockSpec((MIX_W, D_MODEL), const),
            pl.BlockSpec((1, D_MODEL), const),
            pl.BlockSpec((1, D_MODEL), const),
        ],
        out_specs=pl.BlockSpec((1, rows, D_MODEL), dst),
        out_shape=jax.ShapeDtypeStruct(out_shape, F32),
        name="out_proj_ln",
    )(mg, md, h, w, g, b)


def _pack_w_in(w):
    splits = (256, 256, 512, GLA_RANK, 512, 512, 128, 128, 512, IDX_HD, IDX_HEADS, 512)
    offs = np.cumsum((0,) + splits)
    gq, gk, gv, glr, gz, aq, ak, av, iq, ik, iw, az = [w[:, offs[i]:offs[i + 1]] for i in range(12)]

    def perm_heads(a):
        return a.reshape(D_MODEL, DSA_HEADS, DSA_HD)[:, HEAD_PERM, :].reshape(D_MODEL, DSA_W)

    pad = jnp.zeros((D_MODEL, LANES - IDX_HD - GLA_RANK - IDX_HEADS), w.dtype)
    packed = jnp.concatenate(
        [gq * (GLA_DK ** -0.5), gk, gv, gz, perm_heads(aq) * (DSA_HD ** -0.5), ak, av,
         iq * (IDX_HD ** -0.5), az, ik, glr, iw, pad], axis=1)
    return packed.astype(BF16)


def _rope_lane_tables(reps, LP):
    inv = ROPE_THETA ** (-jnp.arange(0, ROPE_DIM, 2, dtype=F32) / ROPE_DIM)
    pos = (jnp.arange(LP, dtype=F32) - PAD)[:, None]
    ang = pos * inv[None, :]
    cos, sin = jnp.cos(ang), jnp.sin(ang)
    ones = jnp.ones((LP, DSA_HD - ROPE_DIM), F32)
    zeros = jnp.zeros((LP, DSA_HD - ROPE_DIM), F32)
    zh = jnp.zeros((LP, ROPE_HALF), F32)
    c = jnp.concatenate([cos, cos, ones], axis=1)
    sa = jnp.concatenate([zh, sin, zeros], axis=1)
    sb = jnp.concatenate([-sin, zh, zeros], axis=1)
    tile = lambda t: jnp.tile(jnp.concatenate([t, t], axis=1), (reps, 1))
    return tile(c), tile(sa), tile(sb)


def kernel(x, meta_tokens, ln_in_g, ln_in_b, w_in, gla_wg2, gla_bg, gla_norm_g, idx_k_g, idx_k_b,
           w_out, ln_g, ln_b):
    B, S, D = x.shape
    depth = w_in.shape[0]
    nblk = S // LANES + 1
    LP = nblk * LANES
    R = B * LP
    topk = min(TOPK_MAX, S // 4)
    alpha = (2.0 * depth) ** 0.25
    tm = PROJ_ROWS

    meta_pad = jnp.concatenate([jnp.zeros((PAD, D), x.dtype), meta_tokens.astype(x.dtype)], axis=0)
    h = _embed(x, meta_pad, ln_in_g.reshape(1, D), ln_in_b.reshape(1, D))
    cos_t, sa_t, sb_t = _rope_lane_tables(tm // math.gcd(tm, LP), LP)

    for i in range(depth):
        w = _pack_w_in(w_in[i])
        wg2 = jnp.zeros((LANES, 256), F32).at[MISC_GLR:MISC_GLR + GLA_RANK].set(gla_wg2[i])
        wg2h = wg2.astype(BF16)
        wg2l = (wg2 - wg2h.astype(F32)).astype(BF16)
        ikg = jnp.zeros((1, LANES), F32).at[0, :IDX_HD].set(idx_k_g[i])
        ikb = jnp.zeros((1, LANES), F32).at[0, :IDX_HD].set(idx_k_b[i])
        (gq, gk, gv, glog, gz, aq, ak, avt, iq, ik2, iwt, az) = _proj(
            h.reshape(R, D), w, wg2h, wg2l, gla_bg[i].reshape(1, 256), ikg, ikb,
            cos_t, sa_t, sb_t, tm)
        mix_gla = _gla(gq, gk, glog, gv, gz, gla_norm_g[i].reshape(1, GLA_DV), B, LP)
        mix_dsa = _dsa(ik2, ak, avt, aq, iq, iwt, az, B, nblk, topk)
        h = _out(mix_gla, mix_dsa.reshape(B, LP, DSA_W), h.reshape(B, LP, D),
                 w_out[i].astype(BF16), ln_g[i].reshape(1, D), ln_b[i].reshape(1, D),
                 B, LP, alpha, drop_filler=(i == depth - 1))
    return h.reshape(B, S, D)
```

```python
import functools
import math

import numpy as np
import jax
import jax.numpy as jnp
from jax import lax
from jax.experimental import pallas as pl
from jax.experimental.pallas import tpu as pltpu

F32 = jnp.float32
BF16 = jnp.bfloat16
I32 = jnp.int32

D_MODEL = 1024
N_META = 16
ROPE_THETA = 500000.0
LN_EPS = 1e-5
GLA_HEADS = 4
GLA_DK = 64
GLA_DV = 128
GLA_RANK = 16
GLA_TAU = 16.0
GLA_CHUNK = 64
GLA_W = GLA_HEADS * GLA_DV
DSA_HEADS = 8
DSA_KV_HEADS = 2
DSA_GROUP = DSA_HEADS // DSA_KV_HEADS
DSA_HD = 64
DSA_W = DSA_HEADS * DSA_HD
IDX_HEADS = 8
IDX_HD = 64
TOPK_MAX = 256
ROPE_DIM = DSA_HD // 4
ROPE_HALF = ROPE_DIM // 2
MIX_W = GLA_W + DSA_W

LANES = 128
SUBLANES = 8
ONES_ROWS = 16
PAD = LANES - N_META
INT_MIN = -(2 ** 31)
M_INIT = -(2.0 ** 100)
VMEM_LIMIT = 48 * 1024 * 1024
PROJ_ROWS = 512
OUT_ROWS = 512

O_GQ, O_GK, O_GV, O_GZ = 0, 256, 512, 1024
O_AQ, O_AK, O_AV, O_IQ, O_AZ, O_MISC = 1536, 2048, 2176, 2304, 2816, 3328
W_PACKED = 3456
MISC_GLR = IDX_HD
MISC_IW = IDX_HD + GLA_RANK
HEAD_PERM = tuple(m + DSA_GROUP * r for m in range(DSA_GROUP) for r in range(2))


def _dot(a, b):
    return jnp.dot(a, b, preferred_element_type=F32)


def _dot_nt(a, b):
    return lax.dot_general(a, b, (((1,), (1,)), ((), ())), preferred_element_type=F32)


def _dot_tn(a, b):
    return lax.dot_general(a, b, (((0,), (0,)), ((), ())), preferred_element_type=F32)


def _layer_norm_rows(u, g, b):
    mu = jnp.mean(u, axis=-1, keepdims=True)
    d = u - mu
    var = jnp.mean(d * d, axis=-1, keepdims=True)
    return d * lax.rsqrt(var + LN_EPS) * g + b


def _embed_kernel(x_ref, meta_ref, g_ref, b_ref, o_ref):
    g, b = g_ref[...], b_ref[...]
    o_ref[0] = _layer_norm_rows(meta_ref[...], g, b)
    for c in range(x_ref.shape[1]):
        o_ref[c + 1] = _layer_norm_rows(x_ref[0, c], g, b)


def _embed(x, meta_pad, g, b):
    B, S, D = x.shape
    nblk = S // LANES + 1
    x4 = x.reshape(B, S // LANES, LANES, D)
    return pl.pallas_call(
        _embed_kernel,
        grid=(B,),
        in_specs=[
            pl.BlockSpec((1, S // LANES, LANES, D), lambda b_: (b_, 0, 0, 0)),
            pl.BlockSpec((LANES, D), lambda b_: (0, 0)),
            pl.BlockSpec((1, D), lambda b_: (0, 0)),
            pl.BlockSpec((1, D), lambda b_: (0, 0)),
        ],
        out_specs=pl.BlockSpec((nblk, LANES, D), lambda b_: (b_, 0, 0)),
        out_shape=jax.ShapeDtypeStruct((B * nblk, LANES, D), F32),
        compiler_params=pltpu.CompilerParams(vmem_limit_bytes=VMEM_LIMIT),
        name="embed_ln",
    )(x4, meta_pad, g, b)


def _rope_slab(x, c, sa, sb):
    return x * c + pltpu.roll(x, ROPE_HALF, 1) * sa + pltpu.roll(x, LANES - ROPE_HALF, 1) * sb


def _proj_kernel(h_ref, w_ref, wg2h_ref, wg2l_ref, bg_ref, ikg_ref, ikb_ref,
                 cos_ref, sa_ref, sb_ref,
                 gq_ref, gk_ref, gv_ref, glog_ref, gz_ref, aq_ref, ak_ref, avt_ref,
                 iq_ref, ik_ref, iwt_ref, az_ref, *, tm):
    hb = h_ref[...].astype(BF16)

    def seg(o, w):
        return _dot(hb, w_ref[:, o:o + w])

    c, sa, sb = cos_ref[...], sa_ref[...], sb_ref[...]
    misc = seg(O_MISC, LANES)
    av = seg(O_AV, LANES)
    ak = seg(O_AK, LANES)
    aq = seg(O_AQ, 512)

    misc_t_scale = IDX_HEADS ** -0.5
    for r in range(tm // LANES):
        rows = slice(r * LANES, (r + 1) * LANES)
        avt_ref[r] = av[rows, :].T.astype(BF16)
        iwt_ref[r] = misc[rows, :].T[MISC_IW:MISC_IW + IDX_HEADS, :] * misc_t_scale
    ak_ref[...] = _rope_slab(ak, c, sa, sb).astype(BF16)
    iq = seg(O_IQ, 512)

    m_hi = misc.astype(BF16)
    m_lo = (misc - m_hi.astype(F32)).astype(BF16)
    xg = (_dot(m_hi, wg2h_ref[...]) + _dot(m_lo, wg2h_ref[...]) + _dot(m_hi, wg2l_ref[...])
          + bg_ref[...])
    gq_ref[...] = seg(O_GQ, 256)
    glog_ref[...] = (jnp.minimum(xg, 0.0) - jnp.log1p(jnp.exp(-jnp.abs(xg)))) * (1.0 / GLA_TAU)

    lane = lax.broadcasted_iota(I32, (1, LANES), 1)
    is_key = lane < IDX_HD
    mu = jnp.sum(jnp.where(is_key, misc, 0.0), axis=-1, keepdims=True) * (1.0 / IDX_HD)
    d = jnp.where(is_key, misc - mu, 0.0)
    var = jnp.sum(d * d, axis=-1, keepdims=True) * (1.0 / IDX_HD)
    ikn = d * lax.rsqrt(var + LN_EPS) * ikg_ref[...] + ikb_ref[...]
    ikr = _rope_slab(ikn, c, sa, sb)
    gk_ref[...] = seg(O_GK, 256)
    ik_ref[...] = jnp.where(is_key, ikr, pltpu.roll(ikr, IDX_HD, 1)).astype(BF16)

    gv = seg(O_GV, 512)
    for m in range(4):
        sl = slice(m * LANES, (m + 1) * LANES)
        aq_ref[:, sl] = _rope_slab(aq[:, sl], c, sa, sb).astype(BF16)
    gv_ref[...] = gv.astype(BF16)
    gz = seg(O_GZ, 512)
    for m in range(4):
        sl = slice(m * LANES, (m + 1) * LANES)
        iq_ref[:, sl] = _rope_slab(iq[:, sl], c, sa, sb).astype(BF16)
    gz_ref[...] = gz
    az_ref[...] = seg(O_AZ, 512)


def _proj(h2, w, wg2h, wg2l, bg, ikg, ikb, cos_t, sa_t, sb_t, tm):
    R = h2.shape[0]
    nt = R // tm
    nsub = tm // LANES
    row = lambda i: (i, 0)
    const = lambda i: (0, 0)
    table_tiles = cos_t.shape[0] // tm
    table_row = lambda i: (i % table_tiles, 0)
    kern = functools.partial(_proj_kernel, tm=tm)
    out_shape = (
        jax.ShapeDtypeStruct((R, 256), F32),
        jax.ShapeDtypeStruct((R, 256), F32),
        jax.ShapeDtypeStruct((R, 512), BF16),
        jax.ShapeDtypeStruct((R, 256), F32),
        jax.ShapeDtypeStruct((R, 512), F32),
        jax.ShapeDtypeStruct((R, 512), BF16),
        jax.ShapeDtypeStruct((R, LANES), BF16),
        jax.ShapeDtypeStruct((R // LANES, LANES, LANES), BF16),
        jax.ShapeDtypeStruct((R, 512), BF16),
        jax.ShapeDtypeStruct((R, LANES), BF16),
        jax.ShapeDtypeStruct((R // LANES, IDX_HEADS, LANES), F32),
        jax.ShapeDtypeStruct((R, 512), F32),
    )
    out_specs = (
        pl.BlockSpec((tm, 256), row),
        pl.BlockSpec((tm, 256), row),
        pl.BlockSpec((tm, 512), row),
        pl.BlockSpec((tm, 256), row),
        pl.BlockSpec((tm, 512), row),
        pl.BlockSpec((tm, 512), row),
        pl.BlockSpec((tm, LANES), row),
        pl.BlockSpec((nsub, LANES, LANES), lambda i: (i, 0, 0)),
        pl.BlockSpec((tm, 512), row),
        pl.BlockSpec((tm, LANES), row),
        pl.BlockSpec((nsub, IDX_HEADS, LANES), lambda i: (i, 0, 0)),
        pl.BlockSpec((tm, 512), row),
    )
    in_specs = [
        pl.BlockSpec((tm, D_MODEL), row),
        pl.BlockSpec((D_MODEL, W_PACKED), const),
        pl.BlockSpec((LANES, 256), const),
        pl.BlockSpec((LANES, 256), const),
        pl.BlockSpec((1, 256), const),
        pl.BlockSpec((1, LANES), const),
        pl.BlockSpec((1, LANES), const),
        pl.BlockSpec((tm, LANES), table_row),
        pl.BlockSpec((tm, LANES), table_row),
        pl.BlockSpec((tm, LANES), table_row),
    ]
    return pl.pallas_call(
        kern, grid=(nt,), in_specs=in_specs, out_specs=out_specs, out_shape=out_shape,
        compiler_params=pltpu.CompilerParams(vmem_limit_bytes=VMEM_LIMIT),
        name="in_proj",
    )(h2, w, wg2h, wg2l, bg, ikg, ikb, cos_t, sa_t, sb_t)


GLA_NB = 8
GLA_PIECES = 17


def _gla_kernel(q_ref, k_ref, g_ref, v_ref, z_ref, ng_ref, o_ref, st_ref, *, n_chunks):
    C = GLA_CHUNK
    piece = pl.program_id(1)

    @pl.when(piece == 0)
    def _():
        st_ref[...] = jnp.zeros_like(st_ref)

    lane = lax.broadcasted_iota(I32, (1, LANES), 1)
    head_lanes = (lane < GLA_DK, lane >= GLA_DK)
    rr = lax.broadcasted_iota(I32, (C, C), 0)
    cc = lax.broadcasted_iota(I32, (C, C), 1)
    causal = rr >= cc
    tril = causal.astype(BF16)
    row = lax.broadcasted_iota(I32, (C, 1), 0)
    ng = ng_ref[...]
    row0 = piece * (n_chunks * C)

    def body(ci, carry):
        r0 = pl.multiple_of(ci * C, C)
        rows = pl.ds(r0, C)
        valid = (row0 + r0 + row) >= PAD
        pairs = [(bi, pr) for bi in range(GLA_NB) for pr in range(GLA_HEADS // 2)]
        bs = []
        for bi, pr in pairs:
            g = g_ref[bi, rows, pr * LANES:(pr + 1) * LANES]
            g_hi = g.astype(BF16)
            g_lo = (g - g_hi.astype(F32)).astype(BF16)
            bs.append(_dot(tril, g_hi) + _dot(tril, g_lo))
        ops = []
        for (bi, pr), b in zip(pairs, bs):
            pl_ = slice(pr * LANES, (pr + 1) * LANES)
            q = q_ref[bi, rows, pl_]
            k = jnp.where(valid, k_ref[bi, rows, pl_], 0.0)
            b_mid = b[C // 2 - 1:C // 2, :]
            b_last = b[C - 1:C, :]
            q_in = (q * jnp.exp(b - b_mid)).astype(BF16)
            k_in = (k * jnp.exp(b_mid - b)).astype(BF16)
            q_st = (q * jnp.exp(b)).astype(BF16)
            k_st = (k * jnp.exp(b_last - b)).astype(BF16)
            ops.append((q_in, k_in, q_st, k_st, jnp.exp(b_last)))
        heads = [(bi, pr, h) for bi, pr in pairs for h in range(2)]
        zero = jnp.zeros((C, LANES), BF16)
        part = []
        for n, (bi, pr, h) in enumerate(heads):
            q_in, k_in, q_st, k_st, decay = ops[n // 2]
            hl = head_lanes[h]
            head = 2 * pr + h
            cols = slice(head * GLA_DV, (head + 1) * GLA_DV)
            v = jnp.where(valid, v_ref[bi, rows, cols], jnp.zeros((), BF16))
            st = st_ref[bi, head]
            a = _dot_nt(jnp.where(hl, q_in, zero), k_in)
            o_st = _dot_nt(jnp.where(hl, q_st, zero), st.astype(BF16))
            st_ref[bi, head] = st * decay + _dot_tn(v, jnp.where(hl, k_st, zero))
            part.append((a, o_st, v, cols))
        for (bi, pr, h), (a, o_st, v, cols) in zip(heads, part):
            o = _dot(jnp.where(causal, a, 0.0).astype(BF16), v) + o_st
            on = o * lax.rsqrt(jnp.mean(o * o, axis=-1, keepdims=True) + LN_EPS) * ng
            z = z_ref[bi, rows, cols]
            o_ref[bi, rows, cols] = (on * (z * jax.nn.sigmoid(z))).astype(BF16)
        return carry

    lax.fori_loop(0, n_chunks, body, 0)


def _gla(gq, gk, glog, gv, gz, ng, B, LP):
    rows = LP // GLA_PIECES
    kern = functools.partial(_gla_kernel, n_chunks=rows // GLA_CHUNK)
    blk = lambda b_, p: (b_, p, 0)
    return pl.pallas_call(
        kern,
        grid=(B // GLA_NB, GLA_PIECES),
        in_specs=[
            pl.BlockSpec((GLA_NB, rows, 256), blk),
            pl.BlockSpec((GLA_NB, rows, 256), blk),
            pl.BlockSpec((GLA_NB, rows, 256), blk),
            pl.BlockSpec((GLA_NB, rows, GLA_W), blk),
            pl.BlockSpec((GLA_NB, rows, GLA_W), blk),
            pl.BlockSpec((1, GLA_DV), lambda b_, p: (0, 0)),
        ],
        out_specs=pl.BlockSpec((GLA_NB, rows, GLA_W), blk),
        out_shape=jax.ShapeDtypeStruct((B, LP, GLA_W), BF16),
        scratch_shapes=[pltpu.VMEM((GLA_NB, GLA_HEADS, GLA_DV, LANES), F32)],
        compiler_params=pltpu.CompilerParams(vmem_limit_bytes=VMEM_LIMIT),
        name="gla",
    )(gq.reshape(B, LP, 256), gk.reshape(B, LP, 256), glog.reshape(B, LP, 256),
      gv.reshape(B, LP, GLA_W), gz.reshape(B, LP, GLA_W), ng)


DSA_NB = 2


def _dsa_kernel(ik_ref, ak_ref, avt_ref, aq_ref, iq_ref, iwt_ref, az_ref, o_ref,
                sc_ref, qz_ref, iqz_ref, w_ref, m_ref, acc_ref, plane_ref, thr_ref, cnt_ref,
                *, topk, nblk):
    j = pl.program_id(1)
    npair = (j + 2) // 2
    KP = 2 * LANES
    batch = range(DSA_NB)

    def pair_rows(ref, n, i, axis):
        kb1 = jnp.minimum(2 * i + 1, nblk - 1)
        return jnp.concatenate([ref[n, 2 * i], ref[n, kb1]], axis=axis)

    def pair_ds(i):
        return pl.ds(pl.multiple_of(i * KP, KP), KP)

    lane = lax.broadcasted_iota(I32, (1, LANES), 1)
    low = lane < DSA_HD
    zero_b = jnp.zeros((LANES, LANES), BF16)
    for n in batch:
        for m in range(4):
            sl = slice(m * LANES, (m + 1) * LANES)
            qs = aq_ref[n, 0, :, sl]
            qz_ref[n, m * LANES:(m + 1) * LANES, :] = jnp.where(low, qs, zero_b)
            qz_ref[n, (m + 4) * LANES:(m + 5) * LANES, :] = jnp.where(low, zero_b, qs)
            iqs = iq_ref[n, 0, :, sl]
            iqz_ref[n, (2 * m) * LANES:(2 * m + 1) * LANES, :] = jnp.where(low, iqs, zero_b)
            iqz_ref[n, (2 * m + 1) * LANES:(2 * m + 2) * LANES, :] = jnp.where(low, zero_b, iqs)
        for h in range(IDX_HEADS):
            w_ref[n, :, h * LANES:(h + 1) * LANES] = iwt_ref[n, 0, h:h + 1, :]
    t_pos = j * LANES + lane
    s_loc = lax.broadcasted_iota(I32, (KP, 1), 0)

    @pl.when((pl.program_id(0) == 0) & (j == 0))
    def _():
        plane_ref[...] = jnp.zeros_like(plane_ref)

    def shift_const(x, n):
        return jnp.full(x.shape, n, I32)

    def store_planes(n, i, scores):
        bits = lax.bitcast_convert_type(scores, I32)
        u = bits ^ (lax.shift_right_arithmetic(bits, shift_const(bits, 31)) | jnp.int32(INT_MIN))
        a = [u[SUBLANES * v:SUBLANES * (v + 1), :] for v in range(32)]
        step, mask = 16, 0x0000FFFF
        while step:
            for k in range(32):
                if not k & step:
                    t = (a[k] ^ lax.shift_right_logical(a[k + step], shift_const(a[k], step))) & jnp.int32(mask)
                    a[k] = a[k] ^ t
                    a[k + step] = a[k + step] ^ lax.shift_left(t, shift_const(t, step))
            step >>= 1
            mask = (mask ^ (mask << step)) & 0xFFFFFFFF
        for r in range(32):
            plane_ref[n, i, r] = a[r]

    def score_pairs(pairs):
        work = [(n, i) for i in pairs for n in batch]
        dots = [_dot_nt(pair_rows(ik_ref, n, i, 0), iqz_ref[n]) for n, i in work]
        for (n, i), d in zip(work, dots):
            d = jnp.maximum(d, 0.0) * w_ref[n]
            acc = d[:, :LANES]
            for h in range(1, IDX_HEADS):
                acc = acc + d[:, h * LANES:(h + 1) * LANES]
            s_pos = i * KP + s_loc
            valid = (s_pos <= t_pos) & ((s_pos >= PAD) | (t_pos < PAD))
            scores = jnp.where(valid, acc, -jnp.inf)
            sc_ref[n, pair_ds(i), :] = scores
            store_planes(n, i, scores)

    def two_pair_trips(body):
        def trip(u, carry):
            body((2 * u, 2 * u + 1))
            return carry

        lax.fori_loop(0, npair // 2, trip, 0)

        @pl.when(npair % 2 == 1)
        def _():
            body((npair - 1,))

    two_pair_trips(score_pairs)

    def count(preds):
        flat = [(n, p) for n in batch for p in preds[n]]

        def cnt_block(i, cnts):
            xs = [sc_ref[n, pair_ds(i), :] for n in batch]
            out = []
            for (n, pred), cnt in zip(flat, cnts):
                hit = pred(xs[n]).astype(I32)
                parts = [hit[r:r + SUBLANES, :] for r in range(0, KP, SUBLANES)]
                while len(parts) > 1:
                    parts = [parts[k] + parts[k + 1] for k in range(0, len(parts), 2)]
                out.append(cnt + parts[0])
            return tuple(out)

        zero = jnp.zeros((SUBLANES, LANES), I32)
        cnts = lax.fori_loop(0, npair, cnt_block, (zero,) * len(flat))
        sums = [jnp.sum(c, axis=0, keepdims=True) for c in cnts]
        per = len(flat) // DSA_NB
        return [tuple(sums[n * per:(n + 1) * per]) for n in batch]

    select_all = jnp.float32(-3.0e38)

    ones_v = jnp.full((SUBLANES, LANES), -1, I32)
    zeros_v = jnp.zeros((SUBLANES, LANES), I32)

    def radix_select(n_pairs):
        alive0 = tuple(tuple(jnp.where(i < npair, ones_v, zeros_v) for i in range(n_pairs)) for n in batch)

        def radix_step(t, carry):
            alive_all, k_all, u_all = carry
            out_alive, out_k, out_u = [], [], []
            counts = []
            for n in batch:
                alive = alive_all[n]
                hi = [plane_ref[n, i, 2 * t] for i in range(n_pairs)]
                lo = [plane_ref[n, i, 2 * t + 1] for i in range(n_pairs)]
                c11 = c1x = c01 = zeros_v
                for i in range(n_pairs):
                    a1 = alive[i] & hi[i]
                    a0 = alive[i] ^ a1
                    c11 = c11 + lax.population_count(a1 & lo[i])
                    c1x = c1x + lax.population_count(a1)
                    c01 = c01 + lax.population_count(a0 & lo[i])
                counts.append((hi, lo, c11, c1x, c01))
            for n in batch:
                hi, lo, c11, c1x, c01 = counts[n]
                alive, k_left = alive_all[n], k_all[n]
                c11 = jnp.sum(c11, axis=0, keepdims=True)
                c1x = jnp.sum(c1x, axis=0, keepdims=True)
                c01 = jnp.sum(c01, axis=0, keepdims=True)
                in3 = k_left <= c11
                in32 = k_left <= c1x
                in321 = k_left <= c1x + c01
                bit_hi = in32
                bit_lo = in3 | (~in32 & in321)
                k_left = jnp.where(in3, k_left,
                                   jnp.where(in32, k_left - c11,
                                             jnp.where(in321, k_left - c1x, k_left - c1x - c01)))
                f_hi = jnp.where(bit_hi, jnp.int32(0), jnp.int32(-1))
                f_lo = jnp.where(bit_lo, jnp.int32(0), jnp.int32(-1))
                out_alive.append(tuple(alive[i] & (hi[i] ^ f_hi) & (lo[i] ^ f_lo) for i in range(n_pairs)))
                digit = jnp.where(bit_hi, jnp.int32(2), jnp.int32(0)) | jnp.where(bit_lo, jnp.int32(1), jnp.int32(0))
                out_k.append(k_left)
                out_u.append(u_all[n] | lax.shift_left(digit, jnp.full(digit.shape, 30 - 2 * t, I32)))
            return tuple(out_alive), tuple(out_k), tuple(out_u)

        _, _, u_thr = lax.fori_loop(
            0, 16, radix_step,
            (alive0, (jnp.full((1, LANES), topk, I32),) * DSA_NB, (jnp.zeros((1, LANES), I32),) * DSA_NB))
        for n in batch:
            thr_bits = jnp.where(u_thr[n] < 0, u_thr[n] ^ jnp.int32(INT_MIN), ~u_thr[n])
            tf = lax.bitcast_convert_type(thr_bits, F32)
            thr_ref[n] = jnp.where(tf == -jnp.inf, select_all, tf)

    n_pairs_max = plane_ref.shape[1]
    n_pairs_few = (n_pairs_max + 1) // 2

    @pl.when(npair <= n_pairs_few)
    def _():
        radix_select(n_pairs_few)

    @pl.when(npair > n_pairs_few)
    def _():
        radix_select(n_pairs_max)

    thr_fast = [thr_ref[n] for n in batch]

    def ge_gt(n):
        return (lambda x: x >= thr_fast[n], lambda x: x > thr_fast[n])

    n_bad = jnp.int32(0)
    for n, (n_ge, n_gt) in zip(batch, count([ge_gt(n) for n in batch])):
        is_all = thr_fast[n] == select_all
        good = is_all | ((n_gt < topk) & (n_ge >= topk))
        cnt_ref[n] = jnp.where(is_all, 0, n_ge)
        n_bad = n_bad + jnp.sum(jnp.where(good, 0, 1))

    @pl.when(n_bad > 0)
    def _():
        def key_to_float(key):
            bits = jnp.where(key < 0, key ^ jnp.int32(0x7FFFFFFF), key)
            return lax.bitcast_convert_type(bits, F32)

        def count_ge(cand_keys):
            cands = [key_to_float(c) for c in cand_keys]
            return [c[0] for c in count([((lambda x, n=n: x >= cands[n]),) for n in batch])]

        c0 = count_ge([jnp.zeros((1, LANES), I32)] * DSA_NB)
        base = tuple(jnp.where(c >= topk, jnp.int32(0), jnp.int32(INT_MIN)) for c in c0)
        base_cnt = tuple(jnp.where(c >= topk, c, 0) for c in c0)

        def bit_step(i, carry):
            base, base_cnt = carry
            cand = [b | jnp.left_shift(jnp.int32(1), 30 - i) for b in base]
            cs = count_ge(cand)
            ok = [c >= topk for c in cs]
            return (tuple(jnp.where(ok[n], cand[n], base[n]) for n in batch),
                    tuple(jnp.where(ok[n], cs[n], base_cnt[n]) for n in batch))

        base, base_cnt = lax.fori_loop(0, 31, bit_step, (base, base_cnt))
        for n in batch:
            thr_ref[n] = jnp.where(base[n] == INT_MIN, select_all, key_to_float(base[n]))
            cnt_ref[n] = base_cnt[n]

    thr = [thr_ref[n] for n in batch]
    base_cnt = [cnt_ref[n] for n in batch]

    most = base_cnt[0]
    for n in batch[1:]:
        most = jnp.maximum(most, base_cnt[n])

    @pl.when(jnp.max(most) > topk)
    def _():
        n_gt = count([((lambda x, n=n: x > thr[n]),) for n in batch])
        rr = lax.broadcasted_iota(I32, (KP, KP), 0)
        cc = lax.broadcasted_iota(I32, (KP, KP), 1)
        tri = (rr >= cc).astype(BF16)
        for n in batch:
            need = (topk - n_gt[n][0]).astype(F32)

            def strike(i, seen, n=n, need=need):
                x = sc_ref[n, pair_ds(i), :]
                eq = x == thr[n]
                rank = _dot(tri, jnp.where(eq, 1.0, 0.0).astype(BF16)) + seen
                sc_ref[n, pair_ds(i), :] = jnp.where(eq & (rank > need), -jnp.inf, x)
                return rank[KP - 1:KP, :]

            lax.fori_loop(0, npair, strike, jnp.zeros((1, LANES), F32))

    m_ref[...] = jnp.full_like(m_ref, M_INIT)
    acc_ref[...] = jnp.zeros_like(acc_ref)
    GW = DSA_GROUP * LANES
    ones_rows = jnp.ones((ONES_ROWS, KP), BF16)

    def attn_pairs(pairs):
        work = [(n, i) for i in pairs for n in batch]
        logits = [_dot_nt(pair_rows(ak_ref, n, i, 0), qz_ref[n]).astype(BF16) for n, i in work]
        for (n, i), s in zip(work, logits):
            bias = jnp.where(sc_ref[n, pair_ds(i), :] >= thr[n], 0.0, -jnp.inf).astype(BF16)
            sb = s + jnp.concatenate([bias] * DSA_HEADS, axis=1)
            m_old = m_ref[n]
            m_new = jnp.maximum(m_old, jnp.max(sb, axis=0, keepdims=True).astype(F32))
            alpha = jnp.exp(m_old - m_new)
            pb = jnp.exp(sb - m_new.astype(BF16))
            m_ref[n] = m_new
            vt = pair_rows(avt_ref, n, i, 1)
            for g in range(DSA_KV_HEADS):
                cols = slice(g * GW, (g + 1) * GW)
                vg = jnp.concatenate([vt[g * DSA_HD:(g + 1) * DSA_HD, :], ones_rows], axis=0)
                acc_ref[n, g] = alpha[:, cols] * acc_ref[n, g] + _dot(vg, pb[:, cols])

    two_pair_trips(attn_pairs)

    for n in batch:
        heads = []
        for h in range(DSA_HEADS):
            a = acc_ref[n, h // DSA_GROUP][:, (h % DSA_GROUP) * LANES:(h % DSA_GROUP + 1) * LANES]
            heads.append(a[:DSA_HD, :] / a[DSA_HD:DSA_HD + 1, :])
        ot = jnp.concatenate(heads, axis=0)
        z = az_ref[n, 0]
        o_ref[n, 0] = (ot.T * (z * jax.nn.sigmoid(z))).astype(BF16)


def _dsa(ik2, ak, avt, aq, iq, iwt, az, B, nblk, topk):
    kern = functools.partial(_dsa_kernel, topk=topk, nblk=nblk)
    nb = DSA_NB
    whole = lambda b_, j: (b_, 0, 0, 0)
    qblk = lambda b_, j: (b_, j, 0, 0)
    by_block = lambda a: a.reshape(B, nblk, -1, a.shape[-1])
    out = pl.pallas_call(
        kern,
        grid=(B // nb, nblk),
        in_specs=[
            pl.BlockSpec((nb, nblk, LANES, LANES), whole),
            pl.BlockSpec((nb, nblk, LANES, LANES), whole),
            pl.BlockSpec((nb, nblk, LANES, LANES), whole),
            pl.BlockSpec((nb, 1, LANES, DSA_W), qblk),
            pl.BlockSpec((nb, 1, LANES, IDX_HEADS * IDX_HD), qblk),
            pl.BlockSpec((nb, 1, IDX_HEADS, LANES), qblk),
            pl.BlockSpec((nb, 1, LANES, DSA_W), qblk),
        ],
        out_specs=pl.BlockSpec((nb, 1, LANES, DSA_W), qblk),
        out_shape=jax.ShapeDtypeStruct((B, nblk, LANES, DSA_W), BF16),
        scratch_shapes=[
            pltpu.VMEM((nb, (nblk + 1) // 2 * 2 * LANES, LANES), F32),
            pltpu.VMEM((nb, DSA_HEADS * LANES, LANES), BF16),
            pltpu.VMEM((nb, IDX_HEADS * LANES, LANES), BF16),
            pltpu.VMEM((nb, 1, IDX_HEADS * LANES), F32),
            pltpu.VMEM((nb, 1, DSA_HEADS * LANES), F32),
            pltpu.VMEM((nb, DSA_KV_HEADS, DSA_HD + ONES_ROWS, DSA_GROUP * LANES), F32),
            pltpu.VMEM((nb, (nblk + 1) // 2, 32, SUBLANES, LANES), I32),
            pltpu.VMEM((nb, 1, LANES), F32),
            pltpu.VMEM((nb, 1, LANES), I32),
        ],
        compiler_params=pltpu.CompilerParams(vmem_limit_bytes=VMEM_LIMIT),
        name="dsa",
    )(by_block(ik2), by_block(ak), by_block(avt), by_block(aq), by_block(iq), by_block(iwt), by_block(az))
    return out.reshape(B * nblk, LANES, DSA_W)


def _out_kernel(mg_ref, md_ref, h_ref, w_ref, g_ref, b_ref, o_ref, *, alpha):
    y = _dot(mg_ref[0], w_ref[:GLA_W, :]) + _dot(md_ref[0], w_ref[GLA_W:, :])
    o_ref[0] = _layer_norm_rows(alpha * h_ref[0] + y, g_ref[...], b_ref[...])


def _out(mg, md, h, w, g, b, B, LP, alpha, drop_filler):
    rows = OUT_ROWS
    if drop_filler:
        S = LP - LANES
        grid, nout = (B, S // rows), S
        src_blk = lambda width: (pl.Element(1), pl.Element(rows), pl.Element(width))
        src = lambda b_, j: (b_, pl.multiple_of(LANES + j * rows, LANES), 0)
    else:
        grid, nout = (B * LP // rows,), LP
        src_blk = lambda width: (1, rows, width)
        mg, md, h = (a.reshape(1, B * LP, a.shape[-1]) for a in (mg, md, h))
        src = lambda i: (0, i, 0)
    dst = (lambda b_, j: (b_, j, 0)) if drop_filler else (lambda i: (0, i, 0))
    const = (lambda b_, j: (0, 0)) if drop_filler else (lambda i: (0, 0))
    out_shape = (B, nout, D_MODEL) if drop_filler else (1, B * LP, D_MODEL)
    kern = functools.partial(_out_kernel, alpha=alpha)
    return pl.pallas_call(
        kern,
        grid=grid,
        in_specs=[
            pl.BlockSpec(src_blk(GLA_W), src),
            pl.BlockSpec(src_blk(DSA_W), src),
            pl.BlockSpec(src_blk(D_MODEL), src),
            pl.BlockSpec((MIX_W, D_MODEL), const),
            pl.BlockSpec((1, D_MODEL), const),
            pl.BlockSpec((1, D_MODEL), const),
        ],
        out_specs=pl.BlockSpec((1, rows, D_MODEL), dst),
        out_shape=jax.ShapeDtypeStruct(out_shape, F32),
        name="out_proj_ln",
    )(mg, md, h, w, g, b)


def _pack_w_in(w):
    splits = (256, 256, 512, GLA_RANK, 512, 512, 128, 128, 512, IDX_HD, IDX_HEADS, 512)
    offs = np.cumsum((0,) + splits)
    gq, gk, gv, glr, gz, aq, ak, av, iq, ik, iw, az = [w[:, offs[i]:offs[i + 1]] for i in range(12)]

    def perm_heads(a):
        return a.reshape(D_MODEL, DSA_HEADS, DSA_HD)[:, HEAD_PERM, :].reshape(D_MODEL, DSA_W)

    pad = jnp.zeros((D_MODEL, LANES - IDX_HD - GLA_RANK - IDX_HEADS), w.dtype)
    packed = jnp.concatenate(
        [gq * (GLA_DK ** -0.5), gk, gv, gz, perm_heads(aq) * (DSA_HD ** -0.5), ak, av,
         iq * (IDX_HD ** -0.5), az, ik, glr, iw, pad], axis=1)
    return packed.astype(BF16)


def _rope_lane_tables(reps, LP):
    inv = ROPE_THETA ** (-jnp.arange(0, ROPE_DIM, 2, dtype=F32) / ROPE_DIM)
    pos = (jnp.arange(LP, dtype=F32) - PAD)[:, None]
    ang = pos * inv[None, :]
    cos, sin = jnp.cos(ang), jnp.sin(ang)
    ones = jnp.ones((LP, DSA_HD - ROPE_DIM), F32)
    zeros = jnp.zeros((LP, DSA_HD - ROPE_DIM), F32)
    zh = jnp.zeros((LP, ROPE_HALF), F32)
    c = jnp.concatenate([cos, cos, ones], axis=1)
    sa = jnp.concatenate([zh, sin, zeros], axis=1)
    sb = jnp.concatenate([-sin, zh, zeros], axis=1)
    tile = lambda t: jnp.tile(jnp.concatenate([t, t], axis=1), (reps, 1))
    return tile(c), tile(sa), tile(sb)


def kernel(x, meta_tokens, ln_in_g, ln_in_b, w_in, gla_wg2, gla_bg, gla_norm_g, idx_k_g, idx_k_b,
           w_out, ln_g, ln_b):
    B, S, D = x.shape
    depth = w_in.shape[0]
    nblk = S // LANES + 1
    LP = nblk * LANES
    R = B * LP
    topk = min(TOPK_MAX, S // 4)
    alpha = (2.0 * depth) ** 0.25
    tm = PROJ_ROWS

    meta_pad = jnp.concatenate([jnp.zeros((PAD, D), x.dtype), meta_tokens.astype(x.dtype)], axis=0)
    h = _embed(x, meta_pad, ln_in_g.reshape(1, D), ln_in_b.reshape(1, D))
    cos_t, sa_t, sb_t = _rope_lane_tables(tm // math.gcd(tm, LP), LP)

    for i in range(depth):
        w = _pack_w_in(w_in[i])
        wg2 = jnp.zeros((LANES, 256), F32).at[MISC_GLR:MISC_GLR + GLA_RANK].set(gla_wg2[i])
        wg2h = wg2.astype(BF16)
        wg2l = (wg2 - wg2h.astype(F32)).astype(BF16)
        ikg = jnp.zeros((1, LANES), F32).at[0, :IDX_HD].set(idx_k_g[i])
        ikb = jnp.zeros((1, LANES), F32).at[0, :IDX_HD].set(idx_k_b[i])
        (gq, gk, gv, glog, gz, aq, ak, avt, iq, ik2, iwt, az) = _proj(
            h.reshape(R, D), w, wg2h, wg2l, gla_bg[i].reshape(1, 256), ikg, ikb,
            cos_t, sa_t, sb_t, tm)
        mix_gla = _gla(gq, gk, glog, gv, gz, gla_norm_g[i].reshape(1, GLA_DV), B, LP)
        mix_dsa = _dsa(ik2, ak, avt, aq, iq, iwt, az, B, nblk, topk)
        h = _out(mix_gla, mix_dsa.reshape(B, LP, DSA_W), h.reshape(B, LP, D),
                 w_out[i].astype(BF16), ln_g[i].reshape(1, D), ln_b[i].reshape(1, D),
                 B, LP, alpha, drop_filler=(i == depth - 1))
    return h.reshape(B, S, D)
```

```python
import functools
import math

import numpy as np
import jax
import jax.numpy as jnp
from jax import lax
from jax.experimental import pallas as pl
from jax.experimental.pallas import tpu as pltpu

F32 = jnp.float32
BF16 = jnp.bfloat16
I32 = jnp.int32

D_MODEL = 1024
N_META = 16
ROPE_THETA = 500000.0
LN_EPS = 1e-5
GLA_HEADS = 4
GLA_DK = 64
GLA_DV = 128
GLA_RANK = 16
GLA_TAU = 16.0
GLA_CHUNK = 64
GLA_W = GLA_HEADS * GLA_DV
DSA_HEADS = 8
DSA_KV_HEADS = 2
DSA_GROUP = DSA_HEADS // DSA_KV_HEADS
DSA_HD = 64
DSA_W = DSA_HEADS * DSA_HD
IDX_HEADS = 8
IDX_HD = 64
TOPK_MAX = 256
ROPE_DIM = DSA_HD // 4
ROPE_HALF = ROPE_DIM // 2
MIX_W = GLA_W + DSA_W

LANES = 128
SUBLANES = 8
ONES_ROWS = 16
PAD = LANES - N_META
INT_MIN = -(2 ** 31)
M_INIT = -(2.0 ** 100)
VMEM_LIMIT = 48 * 1024 * 1024
PROJ_ROWS = 512
OUT_ROWS = 512

O_GQ, O_GK, O_GV, O_GZ = 0, 256, 512, 1024
O_AQ, O_AK, O_AV, O_IQ, O_AZ, O_MISC = 1536, 2048, 2176, 2304, 2816, 3328
W_PACKED = 3456
MISC_GLR = IDX_HD
MISC_IW = IDX_HD + GLA_RANK


def _dot(a, b):
    return jnp.dot(a, b, preferred_element_type=F32)


def _dot_nt(a, b):
    return lax.dot_general(a, b, (((1,), (1,)), ((), ())), preferred_element_type=F32)


def _dot_tn(a, b):
    return lax.dot_general(a, b, (((0,), (0,)), ((), ())), preferred_element_type=F32)


def _layer_norm_rows(u, g, b):
    mu = jnp.mean(u, axis=-1, keepdims=True)
    d = u - mu
    var = jnp.mean(d * d, axis=-1, keepdims=True)
    return d * lax.rsqrt(var + LN_EPS) * g + b


def _embed_kernel(x_ref, meta_ref, g_ref, b_ref, o_ref):
    g, b = g_ref[...], b_ref[...]
    o_ref[0] = _layer_norm_rows(meta_ref[...], g, b)
    for c in range(x_ref.shape[1]):
        o_ref[c + 1] = _layer_norm_rows(x_ref[0, c], g, b)


def _embed(x, meta_pad, g, b):
    B, S, D = x.shape
    nblk = S // LANES + 1
    x4 = x.reshape(B, S // LANES, LANES, D)
    return pl.pallas_call(
        _embed_kernel,
        grid=(B,),
        in_specs=[
            pl.BlockSpec((1, S // LANES, LANES, D), lambda b_: (b_, 0, 0, 0)),
            pl.BlockSpec((LANES, D), lambda b_: (0, 0)),
            pl.BlockSpec((1, D), lambda b_: (0, 0)),
            pl.BlockSpec((1, D), lambda b_: (0, 0)),
        ],
        out_specs=pl.BlockSpec((nblk, LANES, D), lambda b_: (b_, 0, 0)),
        out_shape=jax.ShapeDtypeStruct((B * nblk, LANES, D), F32),
        compiler_params=pltpu.CompilerParams(vmem_limit_bytes=VMEM_LIMIT),
        name="embed_ln",
    )(x4, meta_pad, g, b)


def _rope_slab(x, c, sa, sb):
    return x * c + pltpu.roll(x, ROPE_HALF, 1) * sa + pltpu.roll(x, LANES - ROPE_HALF, 1) * sb


def _proj_kernel(h_ref, w_ref, wg2h_ref, wg2l_ref, bg_ref, ikg_ref, ikb_ref,
                 cos_ref, sa_ref, sb_ref,
                 gq_ref, gk_ref, gv_ref, glog_ref, gz_ref, aq_ref, ak_ref, avt_ref,
                 iq_ref, ik_ref, iwt_ref, az_ref, *, tm):
    hb = h_ref[...].astype(BF16)

    def seg(o, w):
        return _dot(hb, w_ref[:, o:o + w])

    c, sa, sb = cos_ref[...], sa_ref[...], sb_ref[...]
    misc = seg(O_MISC, LANES)
    av = seg(O_AV, LANES)
    ak = seg(O_AK, LANES)
    aq = seg(O_AQ, 512)

    misc_t_scale = IDX_HEADS ** -0.5
    for r in range(tm // LANES):
        rows = slice(r * LANES, (r + 1) * LANES)
        avt_ref[r] = av[rows, :].T.astype(BF16)
        iwt_ref[r] = misc[rows, :].T[MISC_IW:MISC_IW + IDX_HEADS, :] * misc_t_scale
    ak_ref[...] = _rope_slab(ak, c, sa, sb).astype(BF16)
    iq = seg(O_IQ, 512)

    m_hi = misc.astype(BF16)
    m_lo = (misc - m_hi.astype(F32)).astype(BF16)
    xg = (_dot(m_hi, wg2h_ref[...]) + _dot(m_lo, wg2h_ref[...]) + _dot(m_hi, wg2l_ref[...])
          + bg_ref[...])
    gq_ref[...] = seg(O_GQ, 256)
    glog_ref[...] = (jnp.minimum(xg, 0.0) - jnp.log1p(jnp.exp(-jnp.abs(xg)))) * (1.0 / GLA_TAU)

    lane = lax.broadcasted_iota(I32, (1, LANES), 1)
    is_key = lane < IDX_HD
    mu = jnp.sum(jnp.where(is_key, misc, 0.0), axis=-1, keepdims=True) * (1.0 / IDX_HD)
    d = jnp.where(is_key, misc - mu, 0.0)
    var = jnp.sum(d * d, axis=-1, keepdims=True) * (1.0 / IDX_HD)
    ikn = d * lax.rsqrt(var + LN_EPS) * ikg_ref[...] + ikb_ref[...]
    ikr = _rope_slab(ikn, c, sa, sb)
    gk_ref[...] = seg(O_GK, 256)
    ik_ref[...] = jnp.where(is_key, ikr, pltpu.roll(ikr, IDX_HD, 1)).astype(BF16)

    gv = seg(O_GV, 512)
    for m in range(4):
        sl = slice(m * LANES, (m + 1) * LANES)
        aq_ref[:, sl] = _rope_slab(aq[:, sl], c, sa, sb).astype(BF16)
    gv_ref[...] = gv.astype(BF16)
    gz = seg(O_GZ, 512)
    for m in range(4):
        sl = slice(m * LANES, (m + 1) * LANES)
        iq_ref[:, sl] = _rope_slab(iq[:, sl], c, sa, sb).astype(BF16)
    gz_ref[...] = gz
    az_ref[...] = seg(O_AZ, 512)


def _proj(h2, w, wg2h, wg2l, bg, ikg, ikb, cos_t, sa_t, sb_t, tm):
    R = h2.shape[0]
    nt = R // tm
    nsub = tm // LANES
    row = lambda i: (i, 0)
    const = lambda i: (0, 0)
    table_tiles = cos_t.shape[0] // tm
    table_row = lambda i: (i % table_tiles, 0)
    kern = functools.partial(_proj_kernel, tm=tm)
    out_shape = (
        jax.ShapeDtypeStruct((R, 256), F32),
        jax.ShapeDtypeStruct((R, 256), F32),
        jax.ShapeDtypeStruct((R, 512), BF16),
        jax.ShapeDtypeStruct((R, 256), F32),
        jax.ShapeDtypeStruct((R, 512), F32),
        jax.ShapeDtypeStruct((R, 512), BF16),
        jax.ShapeDtypeStruct((R, LANES), BF16),
        jax.ShapeDtypeStruct((R // LANES, LANES, LANES), BF16),
        jax.ShapeDtypeStruct((R, 512), BF16),
        jax.ShapeDtypeStruct((R, LANES), BF16),
        jax.ShapeDtypeStruct((R // LANES, IDX_HEADS, LANES), F32),
        jax.ShapeDtypeStruct((R, 512), F32),
    )
    out_specs = (
        pl.BlockSpec((tm, 256), row),
        pl.BlockSpec((tm, 256), row),
        pl.BlockSpec((tm, 512), row),
        pl.BlockSpec((tm, 256), row),
        pl.BlockSpec((tm, 512), row),
        pl.BlockSpec((tm, 512), row),
        pl.BlockSpec((tm, LANES), row),
        pl.BlockSpec((nsub, LANES, LANES), lambda i: (i, 0, 0)),
        pl.BlockSpec((tm, 512), row),
        pl.BlockSpec((tm, LANES), row),
        pl.BlockSpec((nsub, IDX_HEADS, LANES), lambda i: (i, 0, 0)),
        pl.BlockSpec((tm, 512), row),
    )
    in_specs = [
        pl.BlockSpec((tm, D_MODEL), row),
        pl.BlockSpec((D_MODEL, W_PACKED), const),
        pl.BlockSpec((LANES, 256), const),
        pl.BlockSpec((LANES, 256), const),
        pl.BlockSpec((1, 256), const),
        pl.BlockSpec((1, LANES), const),
        pl.BlockSpec((1, LANES), const),
        pl.BlockSpec((tm, LANES), table_row),
        pl.BlockSpec((tm, LANES), table_row),
        pl.BlockSpec((tm, LANES), table_row),
    ]
    return pl.pallas_call(
        kern, grid=(nt,), in_specs=in_specs, out_specs=out_specs, out_shape=out_shape,
        compiler_params=pltpu.CompilerParams(vmem_limit_bytes=VMEM_LIMIT),
        name="in_proj",
    )(h2, w, wg2h, wg2l, bg, ikg, ikb, cos_t, sa_t, sb_t)


GLA_NB = 8
GLA_PIECES = 17


def _gla_kernel(q_ref, k_ref, g_ref, v_ref, z_ref, ng_ref, o_ref, st_ref, *, n_chunks):
    C = GLA_CHUNK
    piece = pl.program_id(1)

    @pl.when(piece == 0)
    def _():
        st_ref[...] = jnp.zeros_like(st_ref)

    lane = lax.broadcasted_iota(I32, (1, LANES), 1)
    head_lanes = (lane < GLA_DK, lane >= GLA_DK)
    rr = lax.broadcasted_iota(I32, (C, C), 0)
    cc = lax.broadcasted_iota(I32, (C, C), 1)
    causal = rr >= cc
    tril = causal.astype(BF16)
    row = lax.broadcasted_iota(I32, (C, 1), 0)
    ng = ng_ref[...]
    row0 = piece * (n_chunks * C)

    def body(ci, carry):
        r0 = pl.multiple_of(ci * C, C)
        rows = pl.ds(r0, C)
        valid = (row0 + r0 + row) >= PAD
        pairs = [(bi, pr) for bi in range(GLA_NB) for pr in range(GLA_HEADS // 2)]
        bs = []
        for bi, pr in pairs:
            g = g_ref[bi, rows, pr * LANES:(pr + 1) * LANES]
            g_hi = g.astype(BF16)
            g_lo = (g - g_hi.astype(F32)).astype(BF16)
            bs.append(_dot(tril, g_hi) + _dot(tril, g_lo))
        ops = []
        for (bi, pr), b in zip(pairs, bs):
            pl_ = slice(pr * LANES, (pr + 1) * LANES)
            q = q_ref[bi, rows, pl_]
            k = jnp.where(valid, k_ref[bi, rows, pl_], 0.0)
            b_mid = b[C // 2 - 1:C // 2, :]
            b_last = b[C - 1:C, :]
            q_in = (q * jnp.exp(b - b_mid)).astype(BF16)
            k_in = (k * jnp.exp(b_mid - b)).astype(BF16)
            q_st = (q * jnp.exp(b)).astype(BF16)
            k_st = (k * jnp.exp(b_last - b)).astype(BF16)
            ops.append((q_in, k_in, q_st, k_st, jnp.exp(b_last)))
        heads = [(bi, pr, h) for bi, pr in pairs for h in range(2)]
        zero = jnp.zeros((C, LANES), BF16)
        part = []
        for n, (bi, pr, h) in enumerate(heads):
            q_in, k_in, q_st, k_st, decay = ops[n // 2]
            hl = head_lanes[h]
            head = 2 * pr + h
            cols = slice(head * GLA_DV, (head + 1) * GLA_DV)
            v = jnp.where(valid, v_ref[bi, rows, cols], jnp.zeros((), BF16))
            st = st_ref[bi, head]
            a = _dot_nt(jnp.where(hl, q_in, zero), k_in)
            o_st = _dot_nt(jnp.where(hl, q_st, zero), st.astype(BF16))
            st_ref[bi, head] = st * decay + _dot_tn(v, jnp.where(hl, k_st, zero))
            part.append((a, o_st, v, cols))
        for (bi, pr, h), (a, o_st, v, cols) in zip(heads, part):
            o = _dot(jnp.where(causal, a, 0.0).astype(BF16), v) + o_st
            on = o * lax.rsqrt(jnp.mean(o * o, axis=-1, keepdims=True) + LN_EPS) * ng
            z = z_ref[bi, rows, cols]
            o_ref[bi, rows, cols] = (on * (z * jax.nn.sigmoid(z))).astype(BF16)
        return carry

    lax.fori_loop(0, n_chunks, body, 0)


def _gla(gq, gk, glog, gv, gz, ng, B, LP):
    rows = LP // GLA_PIECES
    kern = functools.partial(_gla_kernel, n_chunks=rows // GLA_CHUNK)
    blk = lambda b_, p: (b_, p, 0)
    return pl.pallas_call(
        kern,
        grid=(B // GLA_NB, GLA_PIECES),
        in_specs=[
            pl.BlockSpec((GLA_NB, rows, 256), blk),
            pl.BlockSpec((GLA_NB, rows, 256), blk),
            pl.BlockSpec((GLA_NB, rows, 256), blk),
            pl.BlockSpec((GLA_NB, rows, GLA_W), blk),
            pl.BlockSpec((GLA_NB, rows, GLA_W), blk),
            pl.BlockSpec((1, GLA_DV), lambda b_, p: (0, 0)),
        ],
        out_specs=pl.BlockSpec((GLA_NB, rows, GLA_W), blk),
        out_shape=jax.ShapeDtypeStruct((B, LP, GLA_W), BF16),
        scratch_shapes=[pltpu.VMEM((GLA_NB, GLA_HEADS, GLA_DV, LANES), F32)],
        compiler_params=pltpu.CompilerParams(vmem_limit_bytes=VMEM_LIMIT),
        name="gla",
    )(gq.reshape(B, LP, 256), gk.reshape(B, LP, 256), glog.reshape(B, LP, 256),
      gv.reshape(B, LP, GLA_W), gz.reshape(B, LP, GLA_W), ng)


DSA_NB = 2


def _dsa_kernel(ik_ref, ak_ref, avt_ref, aq_ref, iq_ref, iwt_ref, az_ref, o_ref,
                sc_ref, qz_ref, iqz_ref, w_ref, m_ref, acc_ref, plane_ref, thr_ref, cnt_ref,
                *, topk, nblk):
    j = pl.program_id(1)
    npair = (j + 2) // 2
    KP = 2 * LANES
    batch = range(DSA_NB)

    def pair_rows(ref, n, i, axis):
        kb1 = jnp.minimum(2 * i + 1, nblk - 1)
        return jnp.concatenate([ref[n, 2 * i], ref[n, kb1]], axis=axis)

    def pair_ds(i):
        return pl.ds(pl.multiple_of(i * KP, KP), KP)

    lane = lax.broadcasted_iota(I32, (1, LANES), 1)
    low = lane < DSA_HD
    zero_b = jnp.zeros((LANES, LANES), BF16)
    for n in batch:
        for m in range(4):
            sl = slice(m * LANES, (m + 1) * LANES)
            qs = aq_ref[n, 0, :, sl]
            qz_ref[n, m * LANES:(m + 1) * LANES, :] = jnp.where(low, qs, zero_b)
            qz_ref[n, (m + 4) * LANES:(m + 5) * LANES, :] = jnp.where(low, zero_b, qs)
            iqs = iq_ref[n, 0, :, sl]
            iqz_ref[n, (2 * m) * LANES:(2 * m + 1) * LANES, :] = jnp.where(low, iqs, zero_b)
            iqz_ref[n, (2 * m + 1) * LANES:(2 * m + 2) * LANES, :] = jnp.where(low, zero_b, iqs)
        for h in range(IDX_HEADS):
            w_ref[n, :, h * LANES:(h + 1) * LANES] = iwt_ref[n, 0, h:h + 1, :]
    t_pos = j * LANES + lane
    s_loc = lax.broadcasted_iota(I32, (KP, 1), 0)

    @pl.when((pl.program_id(0) == 0) & (j == 0))
    def _():
        plane_ref[...] = jnp.zeros_like(plane_ref)

    def shift_const(x, n):
        return jnp.full(x.shape, n, I32)

    def store_planes(n, i, scores):
        bits = lax.bitcast_convert_type(scores, I32)
        u = bits ^ (lax.shift_right_arithmetic(bits, shift_const(bits, 31)) | jnp.int32(INT_MIN))
        a = [u[SUBLANES * v:SUBLANES * (v + 1), :] for v in range(32)]
        step, mask = 16, 0x0000FFFF
        while step:
            for k in range(32):
                if not k & step:
                    t = (a[k] ^ lax.shift_right_logical(a[k + step], shift_const(a[k], step))) & jnp.int32(mask)
                    a[k] = a[k] ^ t
                    a[k + step] = a[k + step] ^ lax.shift_left(t, shift_const(t, step))
            step >>= 1
            mask = (mask ^ (mask << step)) & 0xFFFFFFFF
        for r in range(32):
            plane_ref[n, i, r] = a[r]

    def score_pairs(pairs):
        work = [(n, i) for i in pairs for n in batch]
        dots = [_dot_nt(pair_rows(ik_ref, n, i, 0), iqz_ref[n]) for n, i in work]
        for (n, i), d in zip(work, dots):
            d = jnp.maximum(d, 0.0) * w_ref[n]
            acc = d[:, :LANES]
            for h in range(1, IDX_HEADS):
                acc = acc + d[:, h * LANES:(h + 1) * LANES]
            s_pos = i * KP + s_loc
            valid = (s_pos <= t_pos) & ((s_pos >= PAD) | (t_pos < PAD))
            scores = jnp.where(valid, acc, -jnp.inf)
            sc_ref[n, pair_ds(i), :] = scores
            store_planes(n, i, scores)

    def two_pair_trips(body):
        def trip(u, carry):
            body((2 * u, 2 * u + 1))
            return carry

        lax.fori_loop(0, npair // 2, trip, 0)

        @pl.when(npair % 2 == 1)
        def _():
            body((npair - 1,))

    two_pair_trips(score_pairs)

    def count(preds):
        flat = [(n, p) for n in batch for p in preds[n]]

        def cnt_block(i, cnts):
            xs = [sc_ref[n, pair_ds(i), :] for n in batch]
            out = []
            for (n, pred), cnt in zip(flat, cnts):
                hit = pred(xs[n]).astype(I32)
                parts = [hit[r:r + SUBLANES, :] for r in range(0, KP, SUBLANES)]
                while len(parts) > 1:
                    parts = [parts[k] + parts[k + 1] for k in range(0, len(parts), 2)]
                out.append(cnt + parts[0])
            return tuple(out)

        zero = jnp.zeros((SUBLANES, LANES), I32)
        cnts = lax.fori_loop(0, npair, cnt_block, (zero,) * len(flat))
        sums = [jnp.sum(c, axis=0, keepdims=True) for c in cnts]
        per = len(flat) // DSA_NB
        return [tuple(sums[n * per:(n + 1) * per]) for n in batch]

    select_all = jnp.float32(jnp.finfo(F32).min)

    ones_v = jnp.full((SUBLANES, LANES), -1, I32)
    zeros_v = jnp.zeros((SUBLANES, LANES), I32)

    def radix_select(n_pairs):
        alive0 = tuple(tuple(jnp.where(i < npair, ones_v, zeros_v) for i in range(n_pairs)) for n in batch)

        def radix_step(t, carry):
            alive_all, k_all, u_all = carry
            out_alive, out_k, out_u = [], [], []
            counts = []
            for n in batch:
                alive = alive_all[n]
                hi = [plane_ref[n, i, 2 * t] for i in range(n_pairs)]
                lo = [plane_ref[n, i, 2 * t + 1] for i in range(n_pairs)]
                c11 = c1x = c01 = zeros_v
                for i in range(n_pairs):
                    a1 = alive[i] & hi[i]
                    a0 = alive[i] ^ a1
                    c11 = c11 + lax.population_count(a1 & lo[i])
                    c1x = c1x + lax.population_count(a1)
                    c01 = c01 + lax.population_count(a0 & lo[i])
                counts.append((hi, lo, c11, c1x, c01))
            for n in batch:
                hi, lo, c11, c1x, c01 = counts[n]
                alive, k_left = alive_all[n], k_all[n]
                c11 = jnp.sum(c11, axis=0, keepdims=True)
                c1x = jnp.sum(c1x, axis=0, keepdims=True)
                c01 = jnp.sum(c01, axis=0, keepdims=True)
                in3 = k_left <= c11
                in32 = k_left <= c1x
                in321 = k_left <= c1x + c01
                bit_hi = in32
                bit_lo = in3 | (~in32 & in321)
                k_left = jnp.where(in3, k_left,
                                   jnp.where(in32, k_left - c11,
                                             jnp.where(in321, k_left - c1x, k_left - c1x - c01)))
                f_hi = jnp.where(bit_hi, jnp.int32(0), jnp.int32(-1))
                f_lo = jnp.where(bit_lo, jnp.int32(0), jnp.int32(-1))
                out_alive.append(tuple(alive[i] & (hi[i] ^ f_hi) & (lo[i] ^ f_lo) for i in range(n_pairs)))
                digit = jnp.where(bit_hi, jnp.int32(2), jnp.int32(0)) | jnp.where(bit_lo, jnp.int32(1), jnp.int32(0))
                out_k.append(k_left)
                out_u.append(u_all[n] | lax.shift_left(digit, jnp.full(digit.shape, 30 - 2 * t, I32)))
            return tuple(out_alive), tuple(out_k), tuple(out_u)

        _, _, u_thr = lax.fori_loop(
            0, 16, radix_step,
            (alive0, (jnp.full((1, LANES), topk, I32),) * DSA_NB, (jnp.zeros((1, LANES), I32),) * DSA_NB))
        for n in batch:
            thr_bits = jnp.where(u_thr[n] < 0, u_thr[n] ^ jnp.int32(INT_MIN), ~u_thr[n])
            tf = lax.bitcast_convert_type(thr_bits, F32)
            thr_ref[n] = jnp.where(tf == -jnp.inf, select_all, tf)

    n_pairs_max = plane_ref.shape[1]
    n_pairs_few = (n_pairs_max + 1) // 2

    @pl.when(npair <= n_pairs_few)
    def _():
        radix_select(n_pairs_few)

    @pl.when(npair > n_pairs_few)
    def _():
        radix_select(n_pairs_max)

    thr_fast = [thr_ref[n] for n in batch]

    def ge_gt(n):
        return (lambda x: x >= thr_fast[n], lambda x: x > thr_fast[n])

    n_bad = jnp.int32(0)
    for n, (n_ge, n_gt) in zip(batch, count([ge_gt(n) for n in batch])):
        is_all = thr_fast[n] == select_all
        good = is_all | ((n_gt < topk) & (n_ge >= topk))
        cnt_ref[n] = jnp.where(is_all, 0, n_ge)
        n_bad = n_bad + jnp.sum(jnp.where(good, 0, 1))

    @pl.when(n_bad > 0)
    def _():
        def key_to_float(key):
            bits = jnp.where(key < 0, key ^ jnp.int32(0x7FFFFFFF), key)
            return lax.bitcast_convert_type(bits, F32)

        def count_ge(cand_keys):
            cands = [key_to_float(c) for c in cand_keys]
            return [c[0] for c in count([((lambda x, n=n: x >= cands[n]),) for n in batch])]

        c0 = count_ge([jnp.zeros((1, LANES), I32)] * DSA_NB)
        base = tuple(jnp.where(c >= topk, jnp.int32(0), jnp.int32(INT_MIN)) for c in c0)
        base_cnt = tuple(jnp.where(c >= topk, c, 0) for c in c0)

        def bit_step(i, carry):
            base, base_cnt = carry
            cand = [b | jnp.left_shift(jnp.int32(1), 30 - i) for b in base]
            cs = count_ge(cand)
            ok = [c >= topk for c in cs]
            return (tuple(jnp.where(ok[n], cand[n], base[n]) for n in batch),
                    tuple(jnp.where(ok[n], cs[n], base_cnt[n]) for n in batch))

        base, base_cnt = lax.fori_loop(0, 31, bit_step, (base, base_cnt))
        for n in batch:
            thr_ref[n] = jnp.where(base[n] == INT_MIN, select_all, key_to_float(base[n]))
            cnt_ref[n] = base_cnt[n]

    thr = [thr_ref[n] for n in batch]
    base_cnt = [cnt_ref[n] for n in batch]

    most = base_cnt[0]
    for n in batch[1:]:
        most = jnp.maximum(most, base_cnt[n])

    @pl.when(jnp.max(most) > topk)
    def _():
        n_gt = count([((lambda x, n=n: x > thr[n]),) for n in batch])
        rr = lax.broadcasted_iota(I32, (KP, KP), 0)
        cc = lax.broadcasted_iota(I32, (KP, KP), 1)
        tri = (rr >= cc).astype(BF16)
        for n in batch:
            need = (topk - n_gt[n][0]).astype(F32)

            def strike(i, seen, n=n, need=need):
                x = sc_ref[n, pair_ds(i), :]
                eq = x == thr[n]
                rank = _dot(tri, jnp.where(eq, 1.0, 0.0).astype(BF16)) + seen
                sc_ref[n, pair_ds(i), :] = jnp.where(eq & (rank > need), -jnp.inf, x)
                return rank[KP - 1:KP, :]

            lax.fori_loop(0, npair, strike, jnp.zeros((1, LANES), F32))

    m_ref[...] = jnp.full_like(m_ref, M_INIT)
    acc_ref[...] = jnp.zeros_like(acc_ref)
    GW = DSA_GROUP * LANES
    ones_rows = jnp.ones((ONES_ROWS, KP), BF16)

    def attn_pairs(pairs):
        work = [(n, i) for i in pairs for n in batch]
        logits = [_dot_nt(pair_rows(ak_ref, n, i, 0), qz_ref[n]).astype(BF16) for n, i in work]
        for (n, i), s in zip(work, logits):
            bias = jnp.where(sc_ref[n, pair_ds(i), :] >= thr[n], 0.0, -jnp.inf).astype(BF16)
            sb = s + jnp.concatenate([bias] * DSA_HEADS, axis=1)
            m_old = m_ref[n]
            m_new = jnp.maximum(m_old, jnp.max(sb, axis=0, keepdims=True).astype(F32))
            alpha = jnp.exp(m_old - m_new)
            pb = jnp.exp(sb - m_new.astype(BF16))
            m_ref[n] = m_new
            vt = pair_rows(avt_ref, n, i, 1)
            for g in range(DSA_KV_HEADS):
                cols = slice(g * GW, (g + 1) * GW)
                vg = jnp.concatenate([vt[g * DSA_HD:(g + 1) * DSA_HD, :], ones_rows], axis=0)
                acc_ref[n, g] = alpha[:, cols] * acc_ref[n, g] + _dot(vg, pb[:, cols])

    two_pair_trips(attn_pairs)

    for n in batch:
        heads = []
        for h in range(DSA_HEADS):
            a = acc_ref[n, h // DSA_GROUP][:, (h % DSA_GROUP) * LANES:(h % DSA_GROUP + 1) * LANES]
            heads.append(a[:DSA_HD, :] / a[DSA_HD:DSA_HD + 1, :])
        ot = jnp.concatenate(heads, axis=0)
        z = az_ref[n, 0]
        o_ref[n, 0] = (ot.T * (z * jax.nn.sigmoid(z))).astype(BF16)


def _dsa(ik2, ak, avt, aq, iq, iwt, az, B, nblk, topk):
    kern = functools.partial(_dsa_kernel, topk=topk, nblk=nblk)
    nb = DSA_NB
    whole = lambda b_, j: (b_, 0, 0, 0)
    qblk = lambda b_, j: (b_, j, 0, 0)
    by_block = lambda a: a.reshape(B, nblk, -1, a.shape[-1])
    out = pl.pallas_call(
        kern,
        grid=(B // nb, nblk),
        in_specs=[
            pl.BlockSpec((nb, nblk, LANES, LANES), whole),
            pl.BlockSpec((nb, nblk, LANES, LANES), whole),
            pl.BlockSpec((nb, nblk, LANES, LANES), whole),
            pl.BlockSpec((nb, 1, LANES, DSA_W), qblk),
            pl.BlockSpec((nb, 1, LANES, IDX_HEADS * IDX_HD), qblk),
            pl.BlockSpec((nb, 1, IDX_HEADS, LANES), qblk),
            pl.BlockSpec((nb, 1, LANES, DSA_W), qblk),
        ],
        out_specs=pl.BlockSpec((nb, 1, LANES, DSA_W), qblk),
        out_shape=jax.ShapeDtypeStruct((B, nblk, LANES, DSA_W), BF16),
        scratch_shapes=[
            pltpu.VMEM((nb, (nblk + 1) // 2 * 2 * LANES, LANES), F32),
            pltpu.VMEM((nb, DSA_HEADS * LANES, LANES), BF16),
            pltpu.VMEM((nb, IDX_HEADS * LANES, LANES), BF16),
            pltpu.VMEM((nb, 1, IDX_HEADS * LANES), F32),
            pltpu.VMEM((nb, 1, DSA_HEADS * LANES), F32),
            pltpu.VMEM((nb, DSA_KV_HEADS, DSA_HD + ONES_ROWS, DSA_GROUP * LANES), F32),
            pltpu.VMEM((nb, (nblk + 1) // 2, 32, SUBLANES, LANES), I32),
            pltpu.VMEM((nb, 1, LANES), F32),
            pltpu.VMEM((nb, 1, LANES), I32),
        ],
        compiler_params=pltpu.CompilerParams(vmem_limit_bytes=VMEM_LIMIT),
        name="dsa",
    )(by_block(ik2), by_block(ak), by_block(avt), by_block(aq), by_block(iq), by_block(iwt), by_block(az))
    return out.reshape(B * nblk, LANES, DSA_W)


def _out_kernel(mg_ref, md_ref, h_ref, w_ref, g_ref, b_ref, o_ref, *, alpha):
    y = _dot(mg_ref[0], w_ref[:GLA_W, :]) + _dot(md_ref[0], w_ref[GLA_W:, :])
    o_ref[0] = _layer_norm_rows(alpha * h_ref[0] + y, g_ref[...], b_ref[...])


def _out(mg, md, h, w, g, b, B, LP, alpha, drop_filler):
    rows = OUT_ROWS
    if drop_filler:
        S = LP - LANES
        grid, nout = (B, S // rows), S
        src_blk = lambda width: (pl.Element(1), pl.Element(rows), pl.Element(width))
        src = lambda b_, j: (b_, pl.multiple_of(LANES + j * rows, LANES), 0)
    else:
        grid, nout = (B * LP // rows,), LP
        src_blk = lambda width: (1, rows, width)
        mg, md, h = (a.reshape(1, B * LP, a.shape[-1]) for a in (mg, md, h))
        src = lambda i: (0, i, 0)
    dst = (lambda b_, j: (b_, j, 0)) if drop_filler else (lambda i: (0, i, 0))
    const = (lambda b_, j: (0, 0)) if drop_filler else (lambda i: (0, 0))
    out_shape = (B, nout, D_MODEL) if drop_filler else (1, B * LP, D_MODEL)
    kern = functools.partial(_out_kernel, alpha=alpha)
    return pl.pallas_call(
        kern,
        grid=grid,
        in_specs=[
            pl.BlockSpec(src_blk(GLA_W), src),
            pl.BlockSpec(src_blk(DSA_W), src),
            pl.BlockSpec(src_blk(D_MODEL), src),
            pl.BlockSpec((MIX_W, D_MODEL), const),
            pl.BlockSpec((1, D_MODEL), const),
            pl.BlockSpec((1, D_MODEL), const),
        ],
        out_specs=pl.BlockSpec((1, rows, D_MODEL), dst),
        out_shape=jax.ShapeDtypeStruct(out_shape, F32),
        name="out_proj_ln",
    )(mg, md, h, w, g, b)


def _pack_w_in(w):
    splits = (256, 256, 512, GLA_RANK, 512, 512, 128, 128, 512, IDX_HD, IDX_HEADS, 512)
    offs = np.cumsum((0,) + splits)
    gq, gk, gv, glr, gz, aq, ak, av, iq, ik, iw, az = [w[:, offs[i]:offs[i + 1]] for i in range(12)]

    def perm_heads(a):
        a = a.reshape(D_MODEL, DSA_KV_HEADS, DSA_GROUP, DSA_HD)
        return a.transpose(0, 2, 1, 3).reshape(D_MODEL, DSA_W)

    pad = jnp.zeros((D_MODEL, LANES - IDX_HD - GLA_RANK - IDX_HEADS), w.dtype)
    pieces = [gq * (GLA_DK ** -0.5), gk, gv, gz, perm_heads(aq) * (DSA_HD ** -0.5), ak, av,
              iq * (IDX_HD ** -0.5), az, ik, glr, iw, pad]
    return jnp.concatenate([p.astype(BF16) for p in pieces], axis=1)


def _rope_lane_tables(reps, LP):
    inv = ROPE_THETA ** (-jnp.arange(0, ROPE_DIM, 2, dtype=F32) / ROPE_DIM)
    pos = (jnp.arange(LP, dtype=F32) - PAD)[:, None]
    ang = pos * inv[None, :]
    cos, sin = jnp.cos(ang), jnp.sin(ang)
    ones = jnp.ones((LP, DSA_HD - ROPE_DIM), F32)
    zeros = jnp.zeros((LP, DSA_HD - ROPE_DIM), F32)
    zh = jnp.zeros((LP, ROPE_HALF), F32)
    c = jnp.concatenate([cos, cos, ones], axis=1)
    sa = jnp.concatenate([zh, sin, zeros], axis=1)
    sb = jnp.concatenate([-sin, zh, zeros], axis=1)
    tile = lambda t: jnp.tile(jnp.concatenate([t, t], axis=1), (reps, 1))
    return tile(c), tile(sa), tile(sb)


def kernel(x, meta_tokens, ln_in_g, ln_in_b, w_in, gla_wg2, gla_bg, gla_norm_g, idx_k_g, idx_k_b,
           w_out, ln_g, ln_b):
    B, S, D = x.shape
    depth = w_in.shape[0]
    nblk = S // LANES + 1
    LP = nblk * LANES
    R = B * LP
    topk = min(TOPK_MAX, S // 4)
    alpha = (2.0 * depth) ** 0.25
    tm = PROJ_ROWS

    meta_pad = jnp.concatenate([jnp.zeros((PAD, D), x.dtype), meta_tokens.astype(x.dtype)], axis=0)
    h = _embed(x, meta_pad, ln_in_g.reshape(1, D), ln_in_b.reshape(1, D))
    cos_t, sa_t, sb_t = _rope_lane_tables(tm // math.gcd(tm, LP), LP)

    for i in range(depth):
        w = _pack_w_in(w_in[i])
        wg2 = jnp.zeros((LANES, 256), F32).at[MISC_GLR:MISC_GLR + GLA_RANK].set(gla_wg2[i])
        wg2h = wg2.astype(BF16)
        wg2l = (wg2 - wg2h.astype(F32)).astype(BF16)
        ikg = jnp.zeros((1, LANES), F32).at[0, :IDX_HD].set(idx_k_g[i])
        ikb = jnp.zeros((1, LANES), F32).at[0, :IDX_HD].set(idx_k_b[i])
        (gq, gk, gv, glog, gz, aq, ak, avt, iq, ik2, iwt, az) = _proj(
            h.reshape(R, D), w, wg2h, wg2l, gla_bg[i].reshape(1, 256), ikg, ikb,
            cos_t, sa_t, sb_t, tm)
        mix_gla = _gla(gq, gk, glog, gv, gz, gla_norm_g[i].reshape(1, GLA_DV), B, LP)
        mix_dsa = _dsa(ik2, ak, avt, aq, iq, iwt, az, B, nblk, topk)
        h = _out(mix_gla, mix_dsa.reshape(B, LP, DSA_W), h.reshape(B, LP, D),
                 w_out[i].astype(BF16), ln_g[i].reshape(1, D), ln_b[i].reshape(1, D),
                 B, LP, alpha, drop_filler=(i == depth - 1))
    return h.reshape(B, S, D)
```

```python
import functools
import math

import numpy as np
import jax
import jax.numpy as jnp
from jax import lax
from jax.experimental import pallas as pl
from jax.experimental.pallas import tpu as pltpu

F32 = jnp.float32
BF16 = jnp.bfloat16
I32 = jnp.int32

D_MODEL = 1024
N_META = 16
ROPE_THETA = 500000.0
LN_EPS = 1e-5
GLA_HEADS = 4
GLA_DK = 64
GLA_DV = 128
GLA_RANK = 16
GLA_TAU = 16.0
GLA_CHUNK = 64
GLA_W = GLA_HEADS * GLA_DV
DSA_HEADS = 8
DSA_KV_HEADS = 2
DSA_GROUP = DSA_HEADS // DSA_KV_HEADS
DSA_HD = 64
DSA_W = DSA_HEADS * DSA_HD
IDX_HEADS = 8
IDX_HD = 64
TOPK_MAX = 256
ROPE_DIM = DSA_HD // 4
ROPE_HALF = ROPE_DIM // 2
MIX_W = GLA_W + DSA_W

LANES = 128
SUBLANES = 8
ONES_ROWS = 16
PAD = LANES - N_META
INT_MIN = -(2 ** 31)
M_INIT = -(2.0 ** 100)
VMEM_LIMIT = 48 * 1024 * 1024
PROJ_ROWS = 512
OUT_ROWS = 512

O_GQ, O_GK, O_GV, O_GZ = 0, 256, 512, 1024
O_AQ, O_AK, O_AV, O_IQ, O_AZ, O_MISC = 1536, 2048, 2176, 2304, 2816, 3328
W_PACKED = 3456
MISC_GLR = IDX_HD
MISC_IW = IDX_HD + GLA_RANK


def _dot(a, b):
    return jnp.dot(a, b, preferred_element_type=F32)


def _dot_nt(a, b):
    return lax.dot_general(a, b, (((1,), (1,)), ((), ())), preferred_element_type=F32)


def _dot_tn(a, b):
    return lax.dot_general(a, b, (((0,), (0,)), ((), ())), preferred_element_type=F32)


def _layer_norm_rows(u, g, b):
    mu = jnp.mean(u, axis=-1, keepdims=True)
    d = u - mu
    var = jnp.mean(d * d, axis=-1, keepdims=True)
    return d * lax.rsqrt(var + LN_EPS) * g + b


def _embed_kernel(x_ref, meta_ref, g_ref, b_ref, o_ref):
    g, b = g_ref[...], b_ref[...]
    o_ref[0] = _layer_norm_rows(meta_ref[...], g, b)
    for c in range(x_ref.shape[1]):
        o_ref[c + 1] = _layer_norm_rows(x_ref[0, c], g, b)


def _embed(x, meta_pad, g, b):
    B, S, D = x.shape
    nblk = S // LANES + 1
    x4 = x.reshape(B, S // LANES, LANES, D)
    return pl.pallas_call(
        _embed_kernel,
        grid=(B,),
        in_specs=[
            pl.BlockSpec((1, S // LANES, LANES, D), lambda b_: (b_, 0, 0, 0)),
            pl.BlockSpec((LANES, D), lambda b_: (0, 0)),
            pl.BlockSpec((1, D), lambda b_: (0, 0)),
            pl.BlockSpec((1, D), lambda b_: (0, 0)),
        ],
        out_specs=pl.BlockSpec((nblk, LANES, D), lambda b_: (b_, 0, 0)),
        out_shape=jax.ShapeDtypeStruct((B * nblk, LANES, D), F32),
        compiler_params=pltpu.CompilerParams(vmem_limit_bytes=VMEM_LIMIT),
        name="embed_ln",
    )(x4, meta_pad, g, b)


def _rope_slab(x, c, sa, sb):
    return x * c + pltpu.roll(x, ROPE_HALF, 1) * sa + pltpu.roll(x, LANES - ROPE_HALF, 1) * sb


def _proj_kernel(h_ref, w_ref, wg2h_ref, wg2l_ref, bg_ref, ikg_ref, ikb_ref,
                 cos_ref, sa_ref, sb_ref,
                 gq_ref, gk_ref, gv_ref, glog_ref, gz_ref, aq_ref, ak_ref, avt_ref,
                 iq_ref, ik_ref, iwt_ref, az_ref, *, tm):
    hb = h_ref[...].astype(BF16)

    def seg(o, w):
        return _dot(hb, w_ref[:, o:o + w])

    c, sa, sb = cos_ref[...], sa_ref[...], sb_ref[...]
    misc = seg(O_MISC, LANES)
    av = seg(O_AV, LANES)
    ak = seg(O_AK, LANES)
    aq = seg(O_AQ, 512)

    misc_t_scale = IDX_HEADS ** -0.5
    for r in range(tm // LANES):
        rows = slice(r * LANES, (r + 1) * LANES)
        avt_ref[r] = av[rows, :].T.astype(BF16)
        iwt_ref[r] = misc[rows, :].T[MISC_IW:MISC_IW + IDX_HEADS, :] * misc_t_scale
    ak_ref[...] = _rope_slab(ak, c, sa, sb).astype(BF16)
    iq = seg(O_IQ, 512)

    m_hi = misc.astype(BF16)
    m_lo = (misc - m_hi.astype(F32)).astype(BF16)
    xg = (_dot(m_hi, wg2h_ref[...]) + _dot(m_lo, wg2h_ref[...]) + _dot(m_hi, wg2l_ref[...])
          + bg_ref[...])
    gq_ref[...] = seg(O_GQ, 256)
    glog_ref[...] = (jnp.minimum(xg, 0.0) - jnp.log1p(jnp.exp(-jnp.abs(xg)))) * (1.0 / GLA_TAU)

    lane = lax.broadcasted_iota(I32, (1, LANES), 1)
    is_key = lane < IDX_HD
    mu = jnp.sum(jnp.where(is_key, misc, 0.0), axis=-1, keepdims=True) * (1.0 / IDX_HD)
    d = jnp.where(is_key, misc - mu, 0.0)
    var = jnp.sum(d * d, axis=-1, keepdims=True) * (1.0 / IDX_HD)
    ikn = d * lax.rsqrt(var + LN_EPS) * ikg_ref[...] + ikb_ref[...]
    ikr = _rope_slab(ikn, c, sa, sb)
    gk_ref[...] = seg(O_GK, 256)
    ik_ref[...] = jnp.where(is_key, ikr, pltpu.roll(ikr, IDX_HD, 1)).astype(BF16)

    gv = seg(O_GV, 512)
    for m in range(4):
        sl = slice(m * LANES, (m + 1) * LANES)
        aq_ref[:, sl] = _rope_slab(aq[:, sl], c, sa, sb).astype(BF16)
    gv_ref[...] = gv.astype(BF16)
    gz = seg(O_GZ, 512)
    for m in range(4):
        sl = slice(m * LANES, (m + 1) * LANES)
        iq_ref[:, sl] = _rope_slab(iq[:, sl], c, sa, sb).astype(BF16)
    gz_ref[...] = gz
    az_ref[...] = seg(O_AZ, 512)


def _proj(h2, w, wg2h, wg2l, bg, ikg, ikb, cos_t, sa_t, sb_t, tm):
    R = h2.shape[0]
    nt = R // tm
    nsub = tm // LANES
    row = lambda i: (i, 0)
    const = lambda i: (0, 0)
    table_tiles = cos_t.shape[0] // tm
    table_row = lambda i: (i % table_tiles, 0)
    kern = functools.partial(_proj_kernel, tm=tm)
    out_shape = (
        jax.ShapeDtypeStruct((R, 256), F32),
        jax.ShapeDtypeStruct((R, 256), F32),
        jax.ShapeDtypeStruct((R, 512), BF16),
        jax.ShapeDtypeStruct((R, 256), F32),
        jax.ShapeDtypeStruct((R, 512), F32),
        jax.ShapeDtypeStruct((R, 512), BF16),
        jax.ShapeDtypeStruct((R, LANES), BF16),
        jax.ShapeDtypeStruct((R // LANES, LANES, LANES), BF16),
        jax.ShapeDtypeStruct((R, 512), BF16),
        jax.ShapeDtypeStruct((R, LANES), BF16),
        jax.ShapeDtypeStruct((R // LANES, IDX_HEADS, LANES), F32),
        jax.ShapeDtypeStruct((R, 512), F32),
    )
    out_specs = (
        pl.BlockSpec((tm, 256), row),
        pl.BlockSpec((tm, 256), row),
        pl.BlockSpec((tm, 512), row),
        pl.BlockSpec((tm, 256), row),
        pl.BlockSpec((tm, 512), row),
        pl.BlockSpec((tm, 512), row),
        pl.BlockSpec((tm, LANES), row),
        pl.BlockSpec((nsub, LANES, LANES), lambda i: (i, 0, 0)),
        pl.BlockSpec((tm, 512), row),
        pl.BlockSpec((tm, LANES), row),
        pl.BlockSpec((nsub, IDX_HEADS, LANES), lambda i: (i, 0, 0)),
        pl.BlockSpec((tm, 512), row),
    )
    in_specs = [
        pl.BlockSpec((tm, D_MODEL), row),
        pl.BlockSpec((D_MODEL, W_PACKED), const),
        pl.BlockSpec((LANES, 256), const),
        pl.BlockSpec((LANES, 256), const),
        pl.BlockSpec((1, 256), const),
        pl.BlockSpec((1, LANES), const),
        pl.BlockSpec((1, LANES), const),
        pl.BlockSpec((tm, LANES), table_row),
        pl.BlockSpec((tm, LANES), table_row),
        pl.BlockSpec((tm, LANES), table_row),
    ]
    return pl.pallas_call(
        kern, grid=(nt,), in_specs=in_specs, out_specs=out_specs, out_shape=out_shape,
        compiler_params=pltpu.CompilerParams(vmem_limit_bytes=VMEM_LIMIT),
        name="in_proj",
    )(h2, w, wg2h, wg2l, bg, ikg, ikb, cos_t, sa_t, sb_t)


GLA_NB = 8
GLA_PIECES = 17
GLA_SAFE_EXP = 80.0


def _gla_kernel(q_ref, k_ref, g_ref, v_ref, z_ref, ng_ref, o_ref, st_ref, a_ref, tmp_ref,
                *, n_chunks):
    C = GLA_CHUNK
    piece = pl.program_id(1)

    @pl.when(piece == 0)
    def _():
        st_ref[...] = jnp.zeros_like(st_ref)

    lane = lax.broadcasted_iota(I32, (1, LANES), 1)
    head_lanes = (lane < GLA_DK, lane >= GLA_DK)
    rr = lax.broadcasted_iota(I32, (C, C), 0)
    cc = lax.broadcasted_iota(I32, (C, C), 1)
    causal = rr >= cc
    tril = causal.astype(BF16)
    row = lax.broadcasted_iota(I32, (C, 1), 0)
    ng = ng_ref[...]
    row0 = piece * (n_chunks * C)

    def body(ci, carry):
        r0 = pl.multiple_of(ci * C, C)
        rows = pl.ds(r0, C)
        valid = (row0 + r0 + row) >= PAD
        pairs = [(bi, pr) for bi in range(GLA_NB) for pr in range(GLA_HEADS // 2)]

        def cum_decay(bi, pr):
            g = g_ref[bi, rows, pr * LANES:(pr + 1) * LANES]
            g_hi = g.astype(BF16)
            g_lo = (g - g_hi.astype(F32)).astype(BF16)
            return _dot(tril, g_hi) + _dot(tril, g_lo)

        def load_qk(bi, pr):
            pl_ = slice(pr * LANES, (pr + 1) * LANES)
            return q_ref[bi, rows, pl_], jnp.where(valid, k_ref[bi, rows, pl_], 0.0)

        bs = [cum_decay(bi, pr) for bi, pr in pairs]
        ops = []
        span = jnp.zeros((C, LANES), F32)
        for (bi, pr), b in zip(pairs, bs):
            q, k = load_qk(bi, pr)
            b_mid = b[C // 2 - 1:C // 2, :]
            b_last = b[C - 1:C, :]
            d_mid = b - b_mid
            span = jnp.maximum(span, jnp.abs(d_mid))
            q_in = (q * jnp.exp(d_mid)).astype(BF16)
            k_in = (k * jnp.exp(-d_mid)).astype(BF16)
            q_st = (q * jnp.exp(b)).astype(BF16)
            k_st = (k * jnp.exp(b_last - b)).astype(BF16)
            ops.append((q_in, k_in, q_st, k_st, jnp.exp(b_last)))
        heads = [(bi, pr, h) for bi, pr in pairs for h in range(2)]
        zero = jnp.zeros((C, LANES), BF16)
        part = []
        for n, (bi, pr, h) in enumerate(heads):
            q_in, k_in, q_st, k_st, decay = ops[n // 2]
            hl = head_lanes[h]
            head = 2 * pr + h
            cols = slice(head * GLA_DV, (head + 1) * GLA_DV)
            v = jnp.where(valid, v_ref[bi, rows, cols], jnp.zeros((), BF16))
            st = st_ref[bi, head]
            a_ref[n] = jnp.where(causal, _dot_nt(jnp.where(hl, q_in, zero), k_in), 0.0)
            o_st = _dot_nt(jnp.where(hl, q_st, zero), st.astype(BF16))
            st_ref[bi, head] = st * decay + _dot_tn(v, jnp.where(hl, k_st, zero))
            part.append((o_st, v, cols))

        @pl.when(jnp.max(span) > GLA_SAFE_EXP)
        def _():
            key_lane = lax.broadcasted_iota(I32, (1, C), 1)
            for p, (bi, pr) in enumerate(pairs):
                q, k = load_qk(bi, pr)
                b = cum_decay(bi, pr)
                tmp_ref[0] = b
                tmp_ref[1] = k

                def key_column(s, acc, q=q, b=b):
                    b_s = tmp_ref[0, pl.ds(s, 1), :]
                    k_s = tmp_ref[1, pl.ds(s, 1), :]
                    w = q * k_s * jnp.exp(jnp.where(row >= s, b - b_s, -jnp.inf))
                    hit = (key_lane == s).astype(F32)
                    return tuple(
                        acc[h] + jnp.sum(jnp.where(head_lanes[h], w, 0.0), axis=-1, keepdims=True) * hit
                        for h in range(2))

                a0, a1 = lax.fori_loop(0, C, key_column, (jnp.zeros((C, C), F32),) * 2)
                a_ref[2 * p] = a0
                a_ref[2 * p + 1] = a1

        for n, ((bi, pr, h), (o_st, v, cols)) in enumerate(zip(heads, part)):
            o = _dot(a_ref[n].astype(BF16), v) + o_st
            on = o * lax.rsqrt(jnp.mean(o * o, axis=-1, keepdims=True) + LN_EPS) * ng
            z = z_ref[bi, rows, cols]
            o_ref[bi, rows, cols] = (on * (z * jax.nn.sigmoid(z))).astype(BF16)
        return carry

    lax.fori_loop(0, n_chunks, body, 0)


def _gla(gq, gk, glog, gv, gz, ng, B, LP):
    rows = LP // GLA_PIECES
    kern = functools.partial(_gla_kernel, n_chunks=rows // GLA_CHUNK)
    blk = lambda b_, p: (b_, p, 0)
    return pl.pallas_call(
        kern,
        grid=(B // GLA_NB, GLA_PIECES),
        in_specs=[
            pl.BlockSpec((GLA_NB, rows, 256), blk),
            pl.BlockSpec((GLA_NB, rows, 256), blk),
            pl.BlockSpec((GLA_NB, rows, 256), blk),
            pl.BlockSpec((GLA_NB, rows, GLA_W), blk),
            pl.BlockSpec((GLA_NB, rows, GLA_W), blk),
            pl.BlockSpec((1, GLA_DV), lambda b_, p: (0, 0)),
        ],
        out_specs=pl.BlockSpec((GLA_NB, rows, GLA_W), blk),
        out_shape=jax.ShapeDtypeStruct((B, LP, GLA_W), BF16),
        scratch_shapes=[
            pltpu.VMEM((GLA_NB, GLA_HEADS, GLA_DV, LANES), F32),
            pltpu.VMEM((GLA_NB * GLA_HEADS, GLA_CHUNK, GLA_CHUNK), F32),
            pltpu.VMEM((2, GLA_CHUNK, LANES), F32),
        ],
        compiler_params=pltpu.CompilerParams(vmem_limit_bytes=VMEM_LIMIT),
        name="gla",
    )(gq.reshape(B, LP, 256), gk.reshape(B, LP, 256), glog.reshape(B, LP, 256),
      gv.reshape(B, LP, GLA_W), gz.reshape(B, LP, GLA_W), ng)


DSA_NB = 2


def _dsa_kernel(ik_ref, ak_ref, avt_ref, aq_ref, iq_ref, iwt_ref, az_ref, o_ref,
                sc_ref, qz_ref, iqz_ref, w_ref, m_ref, acc_ref, plane_ref, thr_ref, cnt_ref,
                *, topk, nblk):
    j = pl.program_id(1)
    npair = (j + 2) // 2
    KP = 2 * LANES
    batch = range(DSA_NB)

    def pair_rows(ref, n, i, axis):
        kb1 = jnp.minimum(2 * i + 1, nblk - 1)
        return jnp.concatenate([ref[n, 2 * i], ref[n, kb1]], axis=axis)

    def pair_ds(i):
        return pl.ds(pl.multiple_of(i * KP, KP), KP)

    lane = lax.broadcasted_iota(I32, (1, LANES), 1)
    low = lane < DSA_HD
    zero_b = jnp.zeros((LANES, LANES), BF16)
    for n in batch:
        for m in range(4):
            sl = slice(m * LANES, (m + 1) * LANES)
            qs = aq_ref[n, 0, :, sl]
            qz_ref[n, m * LANES:(m + 1) * LANES, :] = jnp.where(low, qs, zero_b)
            qz_ref[n, (m + 4) * LANES:(m + 5) * LANES, :] = jnp.where(low, zero_b, qs)
            iqs = iq_ref[n, 0, :, sl]
            iqz_ref[n, (2 * m) * LANES:(2 * m + 1) * LANES, :] = jnp.where(low, iqs, zero_b)
            iqz_ref[n, (2 * m + 1) * LANES:(2 * m + 2) * LANES, :] = jnp.where(low, zero_b, iqs)
        for h in range(IDX_HEADS):
            w_ref[n, :, h * LANES:(h + 1) * LANES] = iwt_ref[n, 0, h:h + 1, :]
    t_pos = j * LANES + lane
    s_loc = lax.broadcasted_iota(I32, (KP, 1), 0)

    @pl.when((pl.program_id(0) == 0) & (j == 0))
    def _():
        plane_ref[...] = jnp.zeros_like(plane_ref)

    def shift_const(x, n):
        return jnp.full(x.shape, n, I32)

    def store_planes(n, i, scores):
        bits = lax.bitcast_convert_type(scores, I32)
        u = bits ^ (lax.shift_right_arithmetic(bits, shift_const(bits, 31)) | jnp.int32(INT_MIN))
        a = [u[SUBLANES * v:SUBLANES * (v + 1), :] for v in range(32)]
        step, mask = 16, 0x0000FFFF
        while step:
            for k in range(32):
                if not k & step:
                    t = (a[k] ^ lax.shift_right_logical(a[k + step], shift_const(a[k], step))) & jnp.int32(mask)
                    a[k] = a[k] ^ t
                    a[k + step] = a[k + step] ^ lax.shift_left(t, shift_const(t, step))
            step >>= 1
            mask = (mask ^ (mask << step)) & 0xFFFFFFFF
        for r in range(32):
            plane_ref[n, i, r] = a[r]

    def score_pairs(pairs):
        work = [(n, i) for i in pairs for n in batch]
        dots = [_dot_nt(pair_rows(ik_ref, n, i, 0), iqz_ref[n]) for n, i in work]
        for (n, i), d in zip(work, dots):
            d = jnp.maximum(d, 0.0) * w_ref[n]
            acc = d[:, :LANES]
            for h in range(1, IDX_HEADS):
                acc = acc + d[:, h * LANES:(h + 1) * LANES]
            s_pos = i * KP + s_loc
            valid = (s_pos <= t_pos) & ((s_pos >= PAD) | (t_pos < PAD))
            scores = jnp.where(valid, acc, -jnp.inf)
            sc_ref[n, pair_ds(i), :] = scores
            store_planes(n, i, scores)

    def two_pair_trips(body):
        def trip(u, carry):
            body((2 * u, 2 * u + 1))
            return carry

        lax.fori_loop(0, npair // 2, trip, 0)

        @pl.when(npair % 2 == 1)
        def _():
            body((npair - 1,))

    two_pair_trips(score_pairs)

    def count(preds):
        flat = [(n, p) for n in batch for p in preds[n]]

        def cnt_block(i, cnts):
            xs = [sc_ref[n, pair_ds(i), :] for n in batch]
            out = []
            for (n, pred), cnt in zip(flat, cnts):
                hit = pred(xs[n]).astype(I32)
                parts = [hit[r:r + SUBLANES, :] for r in range(0, KP, SUBLANES)]
                while len(parts) > 1:
                    parts = [parts[k] + parts[k + 1] for k in range(0, len(parts), 2)]
                out.append(cnt + parts[0])
            return tuple(out)

        zero = jnp.zeros((SUBLANES, LANES), I32)
        cnts = lax.fori_loop(0, npair, cnt_block, (zero,) * len(flat))
        sums = [jnp.sum(c, axis=0, keepdims=True) for c in cnts]
        per = len(flat) // DSA_NB
        return [tuple(sums[n * per:(n + 1) * per]) for n in batch]

    select_all = jnp.float32(jnp.finfo(F32).min)

    ones_v = jnp.full((SUBLANES, LANES), -1, I32)
    zeros_v = jnp.zeros((SUBLANES, LANES), I32)

    def radix_select(n_pairs):
        alive0 = tuple(tuple(jnp.where(i < npair, ones_v, zeros_v) for i in range(n_pairs)) for n in batch)

        def radix_step(t, carry):
            alive_all, k_all, u_all = carry
            out_alive, out_k, out_u = [], [], []
            counts = []
            for n in batch:
                alive = alive_all[n]
                hi = [plane_ref[n, i, 2 * t] for i in range(n_pairs)]
                lo = [plane_ref[n, i, 2 * t + 1] for i in range(n_pairs)]
                c11 = c1x = c01 = zeros_v
                for i in range(n_pairs):
                    a1 = alive[i] & hi[i]
                    a0 = alive[i] ^ a1
                    c11 = c11 + lax.population_count(a1 & lo[i])
                    c1x = c1x + lax.population_count(a1)
                    c01 = c01 + lax.population_count(a0 & lo[i])
                counts.append((hi, lo, c11, c1x, c01))
            for n in batch:
                hi, lo, c11, c1x, c01 = counts[n]
                alive, k_left = alive_all[n], k_all[n]
                c11 = jnp.sum(c11, axis=0, keepdims=True)
                c1x = jnp.sum(c1x, axis=0, keepdims=True)
                c01 = jnp.sum(c01, axis=0, keepdims=True)
                in3 = k_left <= c11
                in32 = k_left <= c1x
                in321 = k_left <= c1x + c01
                bit_hi = in32
                bit_lo = in3 | (~in32 & in321)
                k_left = jnp.where(in3, k_left,
                                   jnp.where(in32, k_left - c11,
                                             jnp.where(in321, k_left - c1x, k_left - c1x - c01)))
                f_hi = jnp.where(bit_hi, jnp.int32(0), jnp.int32(-1))
                f_lo = jnp.where(bit_lo, jnp.int32(0), jnp.int32(-1))
                out_alive.append(tuple(alive[i] & (hi[i] ^ f_hi) & (lo[i] ^ f_lo) for i in range(n_pairs)))
                digit = jnp.where(bit_hi, jnp.int32(2), jnp.int32(0)) | jnp.where(bit_lo, jnp.int32(1), jnp.int32(0))
                out_k.append(k_left)
                out_u.append(u_all[n] | lax.shift_left(digit, jnp.full(digit.shape, 30 - 2 * t, I32)))
            return tuple(out_alive), tuple(out_k), tuple(out_u)

        _, _, u_thr = lax.fori_loop(
            0, 16, radix_step,
            (alive0, (jnp.full((1, LANES), topk, I32),) * DSA_NB, (jnp.zeros((1, LANES), I32),) * DSA_NB))
        for n in batch:
            thr_bits = jnp.where(u_thr[n] < 0, u_thr[n] ^ jnp.int32(INT_MIN), ~u_thr[n])
            tf = lax.bitcast_convert_type(thr_bits, F32)
            thr_ref[n] = jnp.where(tf == -jnp.inf, select_all, tf)

    n_pairs_max = plane_ref.shape[1]
    n_pairs_few = (n_pairs_max + 1) // 2

    @pl.when(npair <= n_pairs_few)
    def _():
        radix_select(n_pairs_few)

    @pl.when(npair > n_pairs_few)
    def _():
        radix_select(n_pairs_max)

    thr_fast = [thr_ref[n] for n in batch]

    def ge_gt(n):
        return (lambda x: x >= thr_fast[n], lambda x: x > thr_fast[n])

    n_bad = jnp.int32(0)
    for n, (n_ge, n_gt) in zip(batch, count([ge_gt(n) for n in batch])):
        is_all = thr_fast[n] == select_all
        good = is_all | ((n_gt < topk) & (n_ge >= topk))
        cnt_ref[n] = jnp.where(is_all, 0, n_ge)
        n_bad = n_bad + jnp.sum(jnp.where(good, 0, 1))

    @pl.when(n_bad > 0)
    def _():
        def key_to_float(key):
            bits = jnp.where(key < 0, key ^ jnp.int32(0x7FFFFFFF), key)
            return lax.bitcast_convert_type(bits, F32)

        def count_ge(cand_keys):
            cands = [key_to_float(c) for c in cand_keys]
            return [c[0] for c in count([((lambda x, n=n: x >= cands[n]),) for n in batch])]

        c0 = count_ge([jnp.zeros((1, LANES), I32)] * DSA_NB)
        base = tuple(jnp.where(c >= topk, jnp.int32(0), jnp.int32(INT_MIN)) for c in c0)
        base_cnt = tuple(jnp.where(c >= topk, c, 0) for c in c0)

        def bit_step(i, carry):
            base, base_cnt = carry
            cand = [b | jnp.left_shift(jnp.int32(1), 30 - i) for b in base]
            cs = count_ge(cand)
            ok = [c >= topk for c in cs]
            return (tuple(jnp.where(ok[n], cand[n], base[n]) for n in batch),
                    tuple(jnp.where(ok[n], cs[n], base_cnt[n]) for n in batch))

        base, base_cnt = lax.fori_loop(0, 31, bit_step, (base, base_cnt))
        for n in batch:
            thr_ref[n] = jnp.where(base[n] == INT_MIN, select_all, key_to_float(base[n]))
            cnt_ref[n] = base_cnt[n]

    thr = [thr_ref[n] for n in batch]
    base_cnt = [cnt_ref[n] for n in batch]

    most = base_cnt[0]
    for n in batch[1:]:
        most = jnp.maximum(most, base_cnt[n])

    @pl.when(jnp.max(most) > topk)
    def _():
        n_gt = count([((lambda x, n=n: x > thr[n]),) for n in batch])
        rr = lax.broadcasted_iota(I32, (KP, KP), 0)
        cc = lax.broadcasted_iota(I32, (KP, KP), 1)
        tri = (rr >= cc).astype(BF16)
        for n in batch:
            need = (topk - n_gt[n][0]).astype(F32)

            def strike(i, seen, n=n, need=need):
                x = sc_ref[n, pair_ds(i), :]
                eq = x == thr[n]
                rank = _dot(tri, jnp.where(eq, 1.0, 0.0).astype(BF16)) + seen
                sc_ref[n, pair_ds(i), :] = jnp.where(eq & (rank > need), -jnp.inf, x)
                return rank[KP - 1:KP, :]

            lax.fori_loop(0, npair, strike, jnp.zeros((1, LANES), F32))

    m_ref[...] = jnp.full_like(m_ref, M_INIT)
    acc_ref[...] = jnp.zeros_like(acc_ref)
    GW = DSA_GROUP * LANES
    ones_rows = jnp.ones((ONES_ROWS, KP), BF16)

    def attn_pairs(pairs):
        work = [(n, i) for i in pairs for n in batch]
        logits = [_dot_nt(pair_rows(ak_ref, n, i, 0), qz_ref[n]).astype(BF16) for n, i in work]
        for (n, i), s in zip(work, logits):
            bias = jnp.where(sc_ref[n, pair_ds(i), :] >= thr[n], 0.0, -jnp.inf).astype(BF16)
            sb = s + jnp.concatenate([bias] * DSA_HEADS, axis=1)
            m_old = m_ref[n]
            m_new = jnp.maximum(m_old, jnp.max(sb, axis=0, keepdims=True).astype(F32))
            alpha = jnp.exp(m_old - m_new)
            pb = jnp.exp(sb - m_new.astype(BF16))
            m_ref[n] = m_new
            vt = pair_rows(avt_ref, n, i, 1)
            for g in range(DSA_KV_HEADS):
                cols = slice(g * GW, (g + 1) * GW)
                vg = jnp.concatenate([vt[g * DSA_HD:(g + 1) * DSA_HD, :], ones_rows], axis=0)
                acc_ref[n, g] = alpha[:, cols] * acc_ref[n, g] + _dot(vg, pb[:, cols])

    two_pair_trips(attn_pairs)

    for n in batch:
        heads = []
        for h in range(DSA_HEADS):
            a = acc_ref[n, h // DSA_GROUP][:, (h % DSA_GROUP) * LANES:(h % DSA_GROUP + 1) * LANES]
            heads.append(a[:DSA_HD, :] / a[DSA_HD:DSA_HD + 1, :])
        ot = jnp.concatenate(heads, axis=0)
        z = az_ref[n, 0]
        o_ref[n, 0] = (ot.T * (z * jax.nn.sigmoid(z))).astype(BF16)


def _dsa(ik2, ak, avt, aq, iq, iwt, az, B, nblk, topk):
    kern = functools.partial(_dsa_kernel, topk=topk, nblk=nblk)
    nb = DSA_NB
    whole = lambda b_, j: (b_, 0, 0, 0)
    qblk = lambda b_, j: (b_, j, 0, 0)
    by_block = lambda a: a.reshape(B, nblk, -1, a.shape[-1])
    out = pl.pallas_call(
        kern,
        grid=(B // nb, nblk),
        in_specs=[
            pl.BlockSpec((nb, nblk, LANES, LANES), whole),
            pl.BlockSpec((nb, nblk, LANES, LANES), whole),
            pl.BlockSpec((nb, nblk, LANES, LANES), whole),
            pl.BlockSpec((nb, 1, LANES, DSA_W), qblk),
            pl.BlockSpec((nb, 1, LANES, IDX_HEADS * IDX_HD), qblk),
            pl.BlockSpec((nb, 1, IDX_HEADS, LANES), qblk),
            pl.BlockSpec((nb, 1, LANES, DSA_W), qblk),
        ],
        out_specs=pl.BlockSpec((nb, 1, LANES, DSA_W), qblk),
        out_shape=jax.ShapeDtypeStruct((B, nblk, LANES, DSA_W), BF16),
        scratch_shapes=[
            pltpu.VMEM((nb, (nblk + 1) // 2 * 2 * LANES, LANES), F32),
            pltpu.VMEM((nb, DSA_HEADS * LANES, LANES), BF16),
            pltpu.VMEM((nb, IDX_HEADS * LANES, LANES), BF16),
            pltpu.VMEM((nb, 1, IDX_HEADS * LANES), F32),
            pltpu.VMEM((nb, 1, DSA_HEADS * LANES), F32),
            pltpu.VMEM((nb, DSA_KV_HEADS, DSA_HD + ONES_ROWS, DSA_GROUP * LANES), F32),
            pltpu.VMEM((nb, (nblk + 1) // 2, 32, SUBLANES, LANES), I32),
            pltpu.VMEM((nb, 1, LANES), F32),
            pltpu.VMEM((nb, 1, LANES), I32),
        ],
        compiler_params=pltpu.CompilerParams(vmem_limit_bytes=VMEM_LIMIT),
        name="dsa",
    )(by_block(ik2), by_block(ak), by_block(avt), by_block(aq), by_block(iq), by_block(iwt), by_block(az))
    return out.reshape(B * nblk, LANES, DSA_W)


def _out_kernel(mg_ref, md_ref, h_ref, w_ref, g_ref, b_ref, o_ref, *, alpha):
    y = _dot(mg_ref[0], w_ref[:GLA_W, :]) + _dot(md_ref[0], w_ref[GLA_W:, :])
    o_ref[0] = _layer_norm_rows(alpha * h_ref[0] + y, g_ref[...], b_ref[...])


def _out(mg, md, h, w, g, b, B, LP, alpha, drop_filler):
    rows = OUT_ROWS
    if drop_filler:
        S = LP - LANES
        grid, nout = (B, S // rows), S
        src_blk = lambda width: (pl.Element(1), pl.Element(rows), pl.Element(width))
        src = lambda b_, j: (b_, pl.multiple_of(LANES + j * rows, LANES), 0)
    else:
        grid, nout = (B * LP // rows,), LP
        src_blk = lambda width: (1, rows, width)
        mg, md, h = (a.reshape(1, B * LP, a.shape[-1]) for a in (mg, md, h))
        src = lambda i: (0, i, 0)
    dst = (lambda b_, j: (b_, j, 0)) if drop_filler else (lambda i: (0, i, 0))
    const = (lambda b_, j: (0, 0)) if drop_filler else (lambda i: (0, 0))
    out_shape = (B, nout, D_MODEL) if drop_filler else (1, B * LP, D_MODEL)
    kern = functools.partial(_out_kernel, alpha=alpha)
    return pl.pallas_call(
        kern,
        grid=grid,
        in_specs=[
            pl.BlockSpec(src_blk(GLA_W), src),
            pl.BlockSpec(src_blk(DSA_W), src),
            pl.BlockSpec(src_blk(D_MODEL), src),
            pl.BlockSpec((MIX_W, D_MODEL), const),
            pl.BlockSpec((1, D_MODEL), const),
            pl.BlockSpec((1, D_MODEL), const),
        ],
        out_specs=pl.BlockSpec((1, rows, D_MODEL), dst),
        out_shape=jax.ShapeDtypeStruct(out_shape, F32),
        name="out_proj_ln",
    )(mg, md, h, w, g, b)


def _pack_w_in(w):
    splits = (256, 256, 512, GLA_RANK, 512, 512, 128, 128, 512, IDX_HD, IDX_HEADS, 512)
    offs = np.cumsum((0,) + splits)
    gq, gk, gv, glr, gz, aq, ak, av, iq, ik, iw, az = [w[:, offs[i]:offs[i + 1]] for i in range(12)]

    def perm_heads(a):
        a = a.reshape(D_MODEL, DSA_KV_HEADS, DSA_GROUP, DSA_HD)
        return a.transpose(0, 2, 1, 3).reshape(D_MODEL, DSA_W)

    pad = jnp.zeros((D_MODEL, LANES - IDX_HD - GLA_RANK - IDX_HEADS), w.dtype)
    pieces = [gq * (GLA_DK ** -0.5), gk, gv, gz, perm_heads(aq) * (DSA_HD ** -0.5), ak, av,
              iq * (IDX_HD ** -0.5), az, ik, glr, iw, pad]
    return jnp.concatenate([p.astype(BF16) for p in pieces], axis=1)


def _rope_lane_tables(reps, LP):
    inv = ROPE_THETA ** (-jnp.arange(0, ROPE_DIM, 2, dtype=F32) / ROPE_DIM)
    pos = (jnp.arange(LP, dtype=F32) - PAD)[:, None]
    ang = pos * inv[None, :]
    cos, sin = jnp.cos(ang), jnp.sin(ang)
    ones = jnp.ones((LP, DSA_HD - ROPE_DIM), F32)
    zeros = jnp.zeros((LP, DSA_HD - ROPE_DIM), F32)
    zh = jnp.zeros((LP, ROPE_HALF), F32)
    c = jnp.concatenate([cos, cos, ones], axis=1)
    sa = jnp.concatenate([zh, sin, zeros], axis=1)
    sb = jnp.concatenate([-sin, zh, zeros], axis=1)
    tile = lambda t: jnp.tile(jnp.concatenate([t, t], axis=1), (reps, 1))
    return tile(c), tile(sa), tile(sb)


def kernel(x, meta_tokens, ln_in_g, ln_in_b, w_in, gla_wg2, gla_bg, gla_norm_g, idx_k_g, idx_k_b,
           w_out, ln_g, ln_b):
    B, S, D = x.shape
    depth = w_in.shape[0]
    nblk = S // LANES + 1
    LP = nblk * LANES
    R = B * LP
    topk = min(TOPK_MAX, S // 4)
    alpha = (2.0 * depth) ** 0.25
    tm = PROJ_ROWS

    meta_pad = jnp.concatenate([jnp.zeros((PAD, D), x.dtype), meta_tokens.astype(x.dtype)], axis=0)
    h = _embed(x, meta_pad, ln_in_g.reshape(1, D), ln_in_b.reshape(1, D))
    cos_t, sa_t, sb_t = _rope_lane_tables(tm // math.gcd(tm, LP), LP)

    for i in range(depth):
        w = _pack_w_in(w_in[i])
        wg2 = jnp.zeros((LANES, 256), F32).at[MISC_GLR:MISC_GLR + GLA_RANK].set(gla_wg2[i])
        wg2h = wg2.astype(BF16)
        wg2l = (wg2 - wg2h.astype(F32)).astype(BF16)
        ikg = jnp.zeros((1, LANES), F32).at[0, :IDX_HD].set(idx_k_g[i])
        ikb = jnp.zeros((1, LANES), F32).at[0, :IDX_HD].set(idx_k_b[i])
        (gq, gk, gv, glog, gz, aq, ak, avt, iq, ik2, iwt, az) = _proj(
            h.reshape(R, D), w, wg2h, wg2l, gla_bg[i].reshape(1, 256), ikg, ikb,
            cos_t, sa_t, sb_t, tm)
        mix_gla = _gla(gq, gk, glog, gv, gz, gla_norm_g[i].reshape(1, GLA_DV), B, LP)
        mix_dsa = _dsa(ik2, ak, avt, aq, iq, iwt, az, B, nblk, topk)
        h = _out(mix_gla, mix_dsa.reshape(B, LP, DSA_W), h.reshape(B, LP, D),
                 w_out[i].astype(BF16), ln_g[i].reshape(1, D), ln_b[i].reshape(1, D),
                 B, LP, alpha, drop_filler=(i == depth - 1))
    return h.reshape(B, S, D)
```

```python
import functools
import math

import numpy as np
import jax
import jax.numpy as jnp
from jax import lax
from jax.experimental import pallas as pl
from jax.experimental.pallas import tpu as pltpu

F32 = jnp.float32
BF16 = jnp.bfloat16
I32 = jnp.int32

D_MODEL = 1024
N_META = 16
ROPE_THETA = 500000.0
LN_EPS = 1e-5
GLA_HEADS = 4
GLA_DK = 64
GLA_DV = 128
GLA_RANK = 16
GLA_TAU = 16.0
GLA_CHUNK = 64
GLA_W = GLA_HEADS * GLA_DV
DSA_HEADS = 8
DSA_KV_HEADS = 2
DSA_GROUP = DSA_HEADS // DSA_KV_HEADS
DSA_HD = 64
DSA_W = DSA_HEADS * DSA_HD
IDX_HEADS = 8
IDX_HD = 64
TOPK_MAX = 256
ROPE_DIM = DSA_HD // 4
ROPE_HALF = ROPE_DIM // 2
MIX_W = GLA_W + DSA_W

LANES = 128
SUBLANES = 8
ONES_ROWS = 16
PAD = LANES - N_META
INT_MIN = -(2 ** 31)
M_INIT = -(2.0 ** 100)
VMEM_LIMIT = 48 * 1024 * 1024
PROJ_ROWS = 512
OUT_ROWS = 512
OUT_SUB_ROWS = 256

O_GQ, O_GK, O_GV, O_GZ = 0, 256, 512, 1024
O_AQ, O_AK, O_AV, O_IQ, O_AZ, O_MISC = 1536, 2048, 2176, 2304, 2816, 3328
W_PACKED = 3456
MISC_GLR = IDX_HD
MISC_IW = IDX_HD + GLA_RANK


def _dot(a, b):
    return jnp.dot(a, b, preferred_element_type=F32)


def _dot_nt(a, b):
    return lax.dot_general(a, b, (((1,), (1,)), ((), ())), preferred_element_type=F32)


def _dot_tn(a, b):
    return lax.dot_general(a, b, (((0,), (0,)), ((), ())), preferred_element_type=F32)


def _layer_norm_rows(u, g, b):
    mu = jnp.mean(u, axis=-1, keepdims=True)
    d = u - mu
    var = jnp.mean(d * d, axis=-1, keepdims=True)
    return d * lax.rsqrt(var + LN_EPS) * g + b


def _embed_kernel(x_ref, meta_ref, g_ref, b_ref, o_ref):
    g, b = g_ref[...], b_ref[...]
    o_ref[0] = _layer_norm_rows(meta_ref[...], g, b)
    for c in range(x_ref.shape[1]):
        o_ref[c + 1] = _layer_norm_rows(x_ref[0, c], g, b)


def _embed(x, meta_pad, g, b):
    B, S, D = x.shape
    nblk = S // LANES + 1
    x4 = x.reshape(B, S // LANES, LANES, D)
    return pl.pallas_call(
        _embed_kernel,
        grid=(B,),
        in_specs=[
            pl.BlockSpec((1, S // LANES, LANES, D), lambda b_: (b_, 0, 0, 0)),
            pl.BlockSpec((LANES, D), lambda b_: (0, 0)),
            pl.BlockSpec((1, D), lambda b_: (0, 0)),
            pl.BlockSpec((1, D), lambda b_: (0, 0)),
        ],
        out_specs=pl.BlockSpec((nblk, LANES, D), lambda b_: (b_, 0, 0)),
        out_shape=jax.ShapeDtypeStruct((B * nblk, LANES, D), F32),
        compiler_params=pltpu.CompilerParams(vmem_limit_bytes=VMEM_LIMIT),
        name="embed_ln",
    )(x4, meta_pad, g, b)


def _rope_slab(x, c, sa, sb):
    return x * c + pltpu.roll(x, ROPE_HALF, 1) * sa + pltpu.roll(x, LANES - ROPE_HALF, 1) * sb


def _proj_kernel(h_ref, w_ref, wg2h_ref, wg2l_ref, bg_ref, ikg_ref, ikb_ref,
                 cos_ref, sa_ref, sb_ref,
                 gq_ref, gk_ref, gv_ref, glog_ref, gz_ref, aq_ref, ak_ref, avt_ref,
                 iq_ref, ik_ref, iwt_ref, az_ref, *, tm):
    hb = h_ref[...].astype(BF16)

    def seg(o, w):
        return _dot(hb, w_ref[:, o:o + w])

    c, sa, sb = cos_ref[...], sa_ref[...], sb_ref[...]
    misc = seg(O_MISC, LANES)
    kv = seg(O_AK, 2 * LANES)
    ak, av = kv[:, :LANES], kv[:, LANES:]
    aq = seg(O_AQ, 512)

    misc_t_scale = IDX_HEADS ** -0.5
    for r in range(tm // LANES):
        rows = slice(r * LANES, (r + 1) * LANES)
        avt_ref[r] = av[rows, :].T.astype(BF16)
        iwt_ref[r] = misc[rows, :].T[MISC_IW:MISC_IW + IDX_HEADS, :] * misc_t_scale
    ak_ref[...] = _rope_slab(ak, c, sa, sb).astype(BF16)
    iq = seg(O_IQ, 512)

    m_hi = misc.astype(BF16)
    m_lo = (misc - m_hi.astype(F32)).astype(BF16)
    xg = (_dot(m_hi, wg2h_ref[...]) + _dot(m_lo, wg2h_ref[...]) + _dot(m_hi, wg2l_ref[...])
          + bg_ref[...])
    gq_ref[...] = seg(O_GQ, 256)
    glog_ref[...] = (jnp.minimum(xg, 0.0) - jnp.log1p(jnp.exp(-jnp.abs(xg)))) * (1.0 / GLA_TAU)

    lane = lax.broadcasted_iota(I32, (1, LANES), 1)
    is_key = lane < IDX_HD
    mu = jnp.sum(jnp.where(is_key, misc, 0.0), axis=-1, keepdims=True) * (1.0 / IDX_HD)
    d = jnp.where(is_key, misc - mu, 0.0)
    var = jnp.sum(d * d, axis=-1, keepdims=True) * (1.0 / IDX_HD)
    ikn = d * lax.rsqrt(var + LN_EPS) * ikg_ref[...] + ikb_ref[...]
    ikr = _rope_slab(ikn, c, sa, sb)
    gk_ref[...] = seg(O_GK, 256)
    ik_ref[...] = jnp.where(is_key, ikr, pltpu.roll(ikr, IDX_HD, 1)).astype(BF16)

    gv = seg(O_GV, 512)
    for m in range(4):
        sl = slice(m * LANES, (m + 1) * LANES)
        aq_ref[:, sl] = _rope_slab(aq[:, sl], c, sa, sb).astype(BF16)
    gv_ref[...] = gv.astype(BF16)
    gz = seg(O_GZ, 512)
    for m in range(4):
        sl = slice(m * LANES, (m + 1) * LANES)
        iq_ref[:, sl] = _rope_slab(iq[:, sl], c, sa, sb).astype(BF16)
    gz_ref[...] = gz
    az_ref[...] = seg(O_AZ, 512)


def _proj(h2, w, wg2h, wg2l, bg, ikg, ikb, cos_t, sa_t, sb_t, tm):
    R = h2.shape[0]
    nt = R // tm
    nsub = tm // LANES
    row = lambda i: (i, 0)
    const = lambda i: (0, 0)
    table_tiles = cos_t.shape[0] // tm
    table_row = lambda i: (i % table_tiles, 0)
    kern = functools.partial(_proj_kernel, tm=tm)
    out_shape = (
        jax.ShapeDtypeStruct((R, 256), F32),
        jax.ShapeDtypeStruct((R, 256), F32),
        jax.ShapeDtypeStruct((R, 512), BF16),
        jax.ShapeDtypeStruct((R, 256), F32),
        jax.ShapeDtypeStruct((R, 512), F32),
        jax.ShapeDtypeStruct((R, 512), BF16),
        jax.ShapeDtypeStruct((R, LANES), BF16),
        jax.ShapeDtypeStruct((R // LANES, LANES, LANES), BF16),
        jax.ShapeDtypeStruct((R, 512), BF16),
        jax.ShapeDtypeStruct((R, LANES), BF16),
        jax.ShapeDtypeStruct((R // LANES, IDX_HEADS, LANES), F32),
        jax.ShapeDtypeStruct((R, 512), F32),
    )
    out_specs = (
        pl.BlockSpec((tm, 256), row),
        pl.BlockSpec((tm, 256), row),
        pl.BlockSpec((tm, 512), row),
        pl.BlockSpec((tm, 256), row),
        pl.BlockSpec((tm, 512), row),
        pl.BlockSpec((tm, 512), row),
        pl.BlockSpec((tm, LANES), row),
        pl.BlockSpec((nsub, LANES, LANES), lambda i: (i, 0, 0)),
        pl.BlockSpec((tm, 512), row),
        pl.BlockSpec((tm, LANES), row),
        pl.BlockSpec((nsub, IDX_HEADS, LANES), lambda i: (i, 0, 0)),
        pl.BlockSpec((tm, 512), row),
    )
    in_specs = [
        pl.BlockSpec((tm, D_MODEL), row),
        pl.BlockSpec((D_MODEL, W_PACKED), const),
        pl.BlockSpec((LANES, 256), const),
        pl.BlockSpec((LANES, 256), const),
        pl.BlockSpec((1, 256), const),
        pl.BlockSpec((1, LANES), const),
        pl.BlockSpec((1, LANES), const),
        pl.BlockSpec((tm, LANES), table_row),
        pl.BlockSpec((tm, LANES), table_row),
        pl.BlockSpec((tm, LANES), table_row),
    ]
    return pl.pallas_call(
        kern, grid=(nt,), in_specs=in_specs, out_specs=out_specs, out_shape=out_shape,
        compiler_params=pltpu.CompilerParams(vmem_limit_bytes=VMEM_LIMIT),
        name="in_proj",
    )(h2, w, wg2h, wg2l, bg, ikg, ikb, cos_t, sa_t, sb_t)


GLA_NB = 8
GLA_PIECES = 17
GLA_SAFE_EXP = 80.0


def _gla_kernel(q_ref, k_ref, g_ref, v_ref, z_ref, ng_ref, o_ref, st_ref, a_ref, tmp_ref,
                *, n_chunks):
    C = GLA_CHUNK
    piece = pl.program_id(1)

    @pl.when(piece == 0)
    def _():
        st_ref[...] = jnp.zeros_like(st_ref)

    lane = lax.broadcasted_iota(I32, (1, LANES), 1)
    head_lanes = (lane < GLA_DK, lane >= GLA_DK)
    rr = lax.broadcasted_iota(I32, (C, C), 0)
    cc = lax.broadcasted_iota(I32, (C, C), 1)
    causal = rr >= cc
    tril = causal.astype(BF16)
    row = lax.broadcasted_iota(I32, (C, 1), 0)
    ng = ng_ref[...]
    row0 = piece * (n_chunks * C)

    def body(ci, carry):
        r0 = pl.multiple_of(ci * C, C)
        rows = pl.ds(r0, C)
        valid = (row0 + r0 + row) >= PAD
        pairs = [(bi, pr) for bi in range(GLA_NB) for pr in range(GLA_HEADS // 2)]

        def cum_decay(bi, pr):
            g = g_ref[bi, rows, pr * LANES:(pr + 1) * LANES]
            g_hi = g.astype(BF16)
            g_lo = (g - g_hi.astype(F32)).astype(BF16)
            return _dot(tril, g_hi) + _dot(tril, g_lo)

        def load_qk(bi, pr):
            pl_ = slice(pr * LANES, (pr + 1) * LANES)
            return q_ref[bi, rows, pl_], jnp.where(valid, k_ref[bi, rows, pl_], 0.0)

        bs = [cum_decay(bi, pr) for bi, pr in pairs]
        ops = []
        span = jnp.zeros((C, LANES), F32)
        for (bi, pr), b in zip(pairs, bs):
            q, k = load_qk(bi, pr)
            b_mid = b[C // 2 - 1:C // 2, :]
            b_last = b[C - 1:C, :]
            d_mid = b - b_mid
            span = jnp.maximum(span, jnp.abs(d_mid))
            q_in = (q * jnp.exp(d_mid)).astype(BF16)
            k_in = (k * jnp.exp(-d_mid)).astype(BF16)
            q_st = (q * jnp.exp(b)).astype(BF16)
            k_st = (k * jnp.exp(b_last - b)).astype(BF16)
            ops.append((q_in, k_in, q_st, k_st, jnp.exp(b_last)))
        heads = [(bi, pr, h) for bi, pr in pairs for h in range(2)]
        zero = jnp.zeros((C, LANES), BF16)
        part = []
        for n, (bi, pr, h) in enumerate(heads):
            q_in, k_in, q_st, k_st, decay = ops[n // 2]
            hl = head_lanes[h]
            head = 2 * pr + h
            cols = slice(head * GLA_DV, (head + 1) * GLA_DV)
            v = jnp.where(valid, v_ref[bi, rows, cols], jnp.zeros((), BF16))
            st = st_ref[bi, head]
            a_ref[n] = jnp.where(causal, _dot_nt(jnp.where(hl, q_in, zero), k_in), 0.0)
            o_st = _dot_nt(jnp.where(hl, q_st, zero), st.astype(BF16))
            st_ref[bi, head] = st * decay + _dot_tn(v, jnp.where(hl, k_st, zero))
            part.append((o_st, v, cols))

        @pl.when(jnp.max(span) > GLA_SAFE_EXP)
        def _():
            key_lane = lax.broadcasted_iota(I32, (1, C), 1)
            for p, (bi, pr) in enumerate(pairs):
                q, k = load_qk(bi, pr)
                b = cum_decay(bi, pr)
                tmp_ref[0] = b
                tmp_ref[1] = k

                def key_column(s, acc, q=q, b=b):
                    b_s = tmp_ref[0, pl.ds(s, 1), :]
                    k_s = tmp_ref[1, pl.ds(s, 1), :]
                    w = q * k_s * jnp.exp(jnp.where(row >= s, b - b_s, -jnp.inf))
                    hit = (key_lane == s).astype(F32)
                    return tuple(
                        acc[h] + jnp.sum(jnp.where(head_lanes[h], w, 0.0), axis=-1, keepdims=True) * hit
                        for h in range(2))

                a0, a1 = lax.fori_loop(0, C, key_column, (jnp.zeros((C, C), F32),) * 2)
                a_ref[2 * p] = a0
                a_ref[2 * p + 1] = a1

        for n, ((bi, pr, h), (o_st, v, cols)) in enumerate(zip(heads, part)):
            o = _dot(a_ref[n].astype(BF16), v) + o_st
            on = o * lax.rsqrt(jnp.mean(o * o, axis=-1, keepdims=True) + LN_EPS) * ng
            z = z_ref[bi, rows, cols]
            o_ref[bi, rows, cols] = (on * (z * jax.nn.sigmoid(z))).astype(BF16)
        return carry

    lax.fori_loop(0, n_chunks, body, 0)


def _gla(gq, gk, glog, gv, gz, ng, B, LP):
    rows = LP // GLA_PIECES
    kern = functools.partial(_gla_kernel, n_chunks=rows // GLA_CHUNK)
    blk = lambda b_, p: (b_, p, 0)
    return pl.pallas_call(
        kern,
        grid=(B // GLA_NB, GLA_PIECES),
        in_specs=[
            pl.BlockSpec((GLA_NB, rows, 256), blk),
            pl.BlockSpec((GLA_NB, rows, 256), blk),
            pl.BlockSpec((GLA_NB, rows, 256), blk),
            pl.BlockSpec((GLA_NB, rows, GLA_W), blk),
            pl.BlockSpec((GLA_NB, rows, GLA_W), blk),
            pl.BlockSpec((1, GLA_DV), lambda b_, p: (0, 0)),
        ],
        out_specs=pl.BlockSpec((GLA_NB, rows, GLA_W), blk),
        out_shape=jax.ShapeDtypeStruct((B, LP, GLA_W), BF16),
        scratch_shapes=[
            pltpu.VMEM((GLA_NB, GLA_HEADS, GLA_DV, LANES), F32),
            pltpu.VMEM((GLA_NB * GLA_HEADS, GLA_CHUNK, GLA_CHUNK), F32),
            pltpu.VMEM((2, GLA_CHUNK, LANES), F32),
        ],
        compiler_params=pltpu.CompilerParams(vmem_limit_bytes=VMEM_LIMIT),
        name="gla",
    )(gq.reshape(B, LP, 256), gk.reshape(B, LP, 256), glog.reshape(B, LP, 256),
      gv.reshape(B, LP, GLA_W), gz.reshape(B, LP, GLA_W), ng)


DSA_NB = 2


def _dsa_kernel(ik_ref, ak_ref, avt_ref, aq_ref, iq_ref, iwt_ref, az_ref, o_ref,
                sc_ref, qz_ref, iqz_ref, w_ref, m_ref, acc_ref, plane_ref, thr_ref, cnt_ref,
                *, topk, nblk):
    j = pl.program_id(1)
    npair = (j + 2) // 2
    KP = 2 * LANES
    batch = range(DSA_NB)

    def pair_rows(ref, n, i, axis):
        kb1 = jnp.minimum(2 * i + 1, nblk - 1)
        return jnp.concatenate([ref[n, 2 * i], ref[n, kb1]], axis=axis)

    def pair_ds(i):
        return pl.ds(pl.multiple_of(i * KP, KP), KP)

    lane = lax.broadcasted_iota(I32, (1, LANES), 1)
    low = lane < DSA_HD
    zero_b = jnp.zeros((LANES, LANES), BF16)
    for n in batch:
        for m in range(4):
            sl = slice(m * LANES, (m + 1) * LANES)
            qs = aq_ref[n, 0, :, sl]
            qz_ref[n, m * LANES:(m + 1) * LANES, :] = jnp.where(low, qs, zero_b)
            qz_ref[n, (m + 4) * LANES:(m + 5) * LANES, :] = jnp.where(low, zero_b, qs)
            iqs = iq_ref[n, 0, :, sl]
            iqz_ref[n, (2 * m) * LANES:(2 * m + 1) * LANES, :] = jnp.where(low, iqs, zero_b)
            iqz_ref[n, (2 * m + 1) * LANES:(2 * m + 2) * LANES, :] = jnp.where(low, zero_b, iqs)
        for h in range(IDX_HEADS):
            w_ref[n, :, h * LANES:(h + 1) * LANES] = iwt_ref[n, 0, h:h + 1, :]
    t_pos = j * LANES + lane
    s_loc = lax.broadcasted_iota(I32, (KP, 1), 0)

    @pl.when((pl.program_id(0) == 0) & (j == 0))
    def _():
        plane_ref[...] = jnp.zeros_like(plane_ref)

    def shift_const(x, n):
        return jnp.full(x.shape, n, I32)

    def store_planes(n, i, scores):
        bits = lax.bitcast_convert_type(scores, I32)
        u = bits ^ (lax.shift_right_arithmetic(bits, shift_const(bits, 31)) | jnp.int32(INT_MIN))
        a = [u[SUBLANES * v:SUBLANES * (v + 1), :] for v in range(32)]
        step, mask = 16, 0x0000FFFF
        while step:
            for k in range(32):
                if not k & step:
                    t = (a[k] ^ lax.shift_right_logical(a[k + step], shift_const(a[k], step))) & jnp.int32(mask)
                    a[k] = a[k] ^ t
                    a[k + step] = a[k + step] ^ lax.shift_left(t, shift_const(t, step))
            step >>= 1
            mask = (mask ^ (mask << step)) & 0xFFFFFFFF
        for r in range(32):
            plane_ref[n, i, r] = a[r]

    def score_pairs(pairs):
        work = [(n, i) for i in pairs for n in batch]
        dots = [_dot_nt(pair_rows(ik_ref, n, i, 0), iqz_ref[n]) for n, i in work]
        for (n, i), d in zip(work, dots):
            d = jnp.maximum(d, 0.0) * w_ref[n]
            acc = d[:, :LANES]
            for h in range(1, IDX_HEADS):
                acc = acc + d[:, h * LANES:(h + 1) * LANES]
            s_pos = i * KP + s_loc
            valid = (s_pos <= t_pos) & ((s_pos >= PAD) | (t_pos < PAD))
            scores = jnp.where(valid, acc, -jnp.inf)
            sc_ref[n, pair_ds(i), :] = scores
            store_planes(n, i, scores)

    def two_pair_trips(body):
        def trip(u, carry):
            body((2 * u, 2 * u + 1))
            return carry

        lax.fori_loop(0, npair // 2, trip, 0)

        @pl.when(npair % 2 == 1)
        def _():
            body((npair - 1,))

    two_pair_trips(score_pairs)

    def count(preds):
        flat = [(n, p) for n in batch for p in preds[n]]

        def cnt_block(i, cnts):
            xs = [sc_ref[n, pair_ds(i), :] for n in batch]
            out = []
            for (n, pred), cnt in zip(flat, cnts):
                hit = pred(xs[n]).astype(I32)
                parts = [hit[r:r + SUBLANES, :] for r in range(0, KP, SUBLANES)]
                while len(parts) > 1:
                    parts = [parts[k] + parts[k + 1] for k in range(0, len(parts), 2)]
                out.append(cnt + parts[0])
            return tuple(out)

        zero = jnp.zeros((SUBLANES, LANES), I32)
        cnts = lax.fori_loop(0, npair, cnt_block, (zero,) * len(flat))
        sums = [jnp.sum(c, axis=0, keepdims=True) for c in cnts]
        per = len(flat) // DSA_NB
        return [tuple(sums[n * per:(n + 1) * per]) for n in batch]

    select_all = jnp.float32(jnp.finfo(F32).min)

    ones_v = jnp.full((SUBLANES, LANES), -1, I32)
    zeros_v = jnp.zeros((SUBLANES, LANES), I32)

    def radix_select(n_pairs):
        alive0 = tuple(tuple(jnp.where(i < npair, ones_v, zeros_v) for i in range(n_pairs)) for n in batch)

        def radix_step(t, carry):
            alive_all, k_all, u_all = carry
            out_alive, out_k, out_u = [], [], []
            counts = []
            for n in batch:
                alive = alive_all[n]
                hi = [plane_ref[n, i, 2 * t] for i in range(n_pairs)]
                lo = [plane_ref[n, i, 2 * t + 1] for i in range(n_pairs)]
                c11 = c1x = c01 = zeros_v
                for i in range(n_pairs):
                    a1 = alive[i] & hi[i]
                    a0 = alive[i] ^ a1
                    c11 = c11 + lax.population_count(a1 & lo[i])
                    c1x = c1x + lax.population_count(a1)
                    c01 = c01 + lax.population_count(a0 & lo[i])
                counts.append((hi, lo, c11, c1x, c01))
            for n in batch:
                hi, lo, c11, c1x, c01 = counts[n]
                alive, k_left = alive_all[n], k_all[n]
                c11 = jnp.sum(c11, axis=0, keepdims=True)
                c1x = jnp.sum(c1x, axis=0, keepdims=True)
                c01 = jnp.sum(c01, axis=0, keepdims=True)
                in3 = k_left <= c11
                in32 = k_left <= c1x
                in321 = k_left <= c1x + c01
                bit_hi = in32
                bit_lo = in3 | (~in32 & in321)
                k_left = jnp.where(in3, k_left,
                                   jnp.where(in32, k_left - c11,
                                             jnp.where(in321, k_left - c1x, k_left - c1x - c01)))
                f_hi = jnp.where(bit_hi, jnp.int32(0), jnp.int32(-1))
                f_lo = jnp.where(bit_lo, jnp.int32(0), jnp.int32(-1))
                out_alive.append(tuple(alive[i] & (hi[i] ^ f_hi) & (lo[i] ^ f_lo) for i in range(n_pairs)))
                digit = jnp.where(bit_hi, jnp.int32(2), jnp.int32(0)) | jnp.where(bit_lo, jnp.int32(1), jnp.int32(0))
                out_k.append(k_left)
                out_u.append(u_all[n] | lax.shift_left(digit, jnp.full(digit.shape, 30 - 2 * t, I32)))
            return tuple(out_alive), tuple(out_k), tuple(out_u)

        _, _, u_thr = lax.fori_loop(
            0, 16, radix_step,
            (alive0, (jnp.full((1, LANES), topk, I32),) * DSA_NB, (jnp.zeros((1, LANES), I32),) * DSA_NB))
        for n in batch:
            thr_bits = jnp.where(u_thr[n] < 0, u_thr[n] ^ jnp.int32(INT_MIN), ~u_thr[n])
            tf = lax.bitcast_convert_type(thr_bits, F32)
            thr_ref[n] = jnp.where(tf == -jnp.inf, select_all, tf)

    n_pairs_max = plane_ref.shape[1]
    n_pairs_few = (n_pairs_max + 1) // 2

    @pl.when(npair <= n_pairs_few)
    def _():
        radix_select(n_pairs_few)

    @pl.when(npair > n_pairs_few)
    def _():
        radix_select(n_pairs_max)

    thr_fast = [thr_ref[n] for n in batch]

    def ge_gt(n):
        return (lambda x: x >= thr_fast[n], lambda x: x > thr_fast[n])

    n_bad = jnp.int32(0)
    for n, (n_ge, n_gt) in zip(batch, count([ge_gt(n) for n in batch])):
        is_all = thr_fast[n] == select_all
        good = is_all | ((n_gt < topk) & (n_ge >= topk))
        cnt_ref[n] = jnp.where(is_all, 0, n_ge)
        n_bad = n_bad + jnp.sum(jnp.where(good, 0, 1))

    @pl.when(n_bad > 0)
    def _():
        def key_to_float(key):
            bits = jnp.where(key < 0, key ^ jnp.int32(0x7FFFFFFF), key)
            return lax.bitcast_convert_type(bits, F32)

        def count_ge(cand_keys):
            cands = [key_to_float(c) for c in cand_keys]
            return [c[0] for c in count([((lambda x, n=n: x >= cands[n]),) for n in batch])]

        c0 = count_ge([jnp.zeros((1, LANES), I32)] * DSA_NB)
        base = tuple(jnp.where(c >= topk, jnp.int32(0), jnp.int32(INT_MIN)) for c in c0)
        base_cnt = tuple(jnp.where(c >= topk, c, 0) for c in c0)

        def bit_step(i, carry):
            base, base_cnt = carry
            cand = [b | jnp.left_shift(jnp.int32(1), 30 - i) for b in base]
            cs = count_ge(cand)
            ok = [c >= topk for c in cs]
            return (tuple(jnp.where(ok[n], cand[n], base[n]) for n in batch),
                    tuple(jnp.where(ok[n], cs[n], base_cnt[n]) for n in batch))

        base, base_cnt = lax.fori_loop(0, 31, bit_step, (base, base_cnt))
        for n in batch:
            thr_ref[n] = jnp.where(base[n] == INT_MIN, select_all, key_to_float(base[n]))
            cnt_ref[n] = base_cnt[n]

    thr = [thr_ref[n] for n in batch]
    base_cnt = [cnt_ref[n] for n in batch]

    most = base_cnt[0]
    for n in batch[1:]:
        most = jnp.maximum(most, base_cnt[n])

    @pl.when(jnp.max(most) > topk)
    def _():
        n_gt = count([((lambda x, n=n: x > thr[n]),) for n in batch])
        rr = lax.broadcasted_iota(I32, (KP, KP), 0)
        cc = lax.broadcasted_iota(I32, (KP, KP), 1)
        tri = (rr >= cc).astype(BF16)
        for n in batch:
            need = (topk - n_gt[n][0]).astype(F32)

            def strike(i, seen, n=n, need=need):
                x = sc_ref[n, pair_ds(i), :]
                eq = x == thr[n]
                rank = _dot(tri, jnp.where(eq, 1.0, 0.0).astype(BF16)) + seen
                sc_ref[n, pair_ds(i), :] = jnp.where(eq & (rank > need), -jnp.inf, x)
                return rank[KP - 1:KP, :]

            lax.fori_loop(0, npair, strike, jnp.zeros((1, LANES), F32))

    m_ref[...] = jnp.full_like(m_ref, M_INIT)
    acc_ref[...] = jnp.zeros_like(acc_ref)
    GW = DSA_GROUP * LANES
    ones_rows = jnp.ones((ONES_ROWS, KP), BF16)

    def attn_pairs(pairs):
        work = [(n, i) for i in pairs for n in batch]
        logits = [_dot_nt(pair_rows(ak_ref, n, i, 0), qz_ref[n]).astype(BF16) for n, i in work]
        for (n, i), s in zip(work, logits):
            bias = jnp.where(sc_ref[n, pair_ds(i), :] >= thr[n], 0.0, -jnp.inf).astype(BF16)
            sb = s + jnp.concatenate([bias] * DSA_HEADS, axis=1)
            m_old = m_ref[n]
            m_new = jnp.maximum(m_old, jnp.max(sb, axis=0, keepdims=True).astype(F32))
            alpha = jnp.exp(m_old - m_new)
            pb = jnp.exp(sb - m_new.astype(BF16))
            m_ref[n] = m_new
            vt = pair_rows(avt_ref, n, i, 1)
            for g in range(DSA_KV_HEADS):
                cols = slice(g * GW, (g + 1) * GW)
                vg = jnp.concatenate([vt[g * DSA_HD:(g + 1) * DSA_HD, :], ones_rows], axis=0)
                acc_ref[n, g] = alpha[:, cols] * acc_ref[n, g] + _dot(vg, pb[:, cols])

    two_pair_trips(attn_pairs)

    for n in batch:
        heads = []
        for h in range(DSA_HEADS):
            a = acc_ref[n, h // DSA_GROUP][:, (h % DSA_GROUP) * LANES:(h % DSA_GROUP + 1) * LANES]
            heads.append(a[:DSA_HD, :] / a[DSA_HD:DSA_HD + 1, :])
        ot = jnp.concatenate(heads, axis=0)
        z = az_ref[n, 0]
        o_ref[n, 0] = (ot.T * (z * jax.nn.sigmoid(z))).astype(BF16)


def _dsa(ik2, ak, avt, aq, iq, iwt, az, B, nblk, topk):
    kern = functools.partial(_dsa_kernel, topk=topk, nblk=nblk)
    nb = DSA_NB
    whole = lambda b_, j: (b_, 0, 0, 0)
    qblk = lambda b_, j: (b_, j, 0, 0)
    by_block = lambda a: a.reshape(B, nblk, -1, a.shape[-1])
    out = pl.pallas_call(
        kern,
        grid=(B // nb, nblk),
        in_specs=[
            pl.BlockSpec((nb, nblk, LANES, LANES), whole),
            pl.BlockSpec((nb, nblk, LANES, LANES), whole),
            pl.BlockSpec((nb, nblk, LANES, LANES), whole),
            pl.BlockSpec((nb, 1, LANES, DSA_W), qblk),
            pl.BlockSpec((nb, 1, LANES, IDX_HEADS * IDX_HD), qblk),
            pl.BlockSpec((nb, 1, IDX_HEADS, LANES), qblk),
            pl.BlockSpec((nb, 1, LANES, DSA_W), qblk),
        ],
        out_specs=pl.BlockSpec((nb, 1, LANES, DSA_W), qblk),
        out_shape=jax.ShapeDtypeStruct((B, nblk, LANES, DSA_W), BF16),
        scratch_shapes=[
            pltpu.VMEM((nb, (nblk + 1) // 2 * 2 * LANES, LANES), F32),
            pltpu.VMEM((nb, DSA_HEADS * LANES, LANES), BF16),
            pltpu.VMEM((nb, IDX_HEADS * LANES, LANES), BF16),
            pltpu.VMEM((nb, 1, IDX_HEADS * LANES), F32),
            pltpu.VMEM((nb, 1, DSA_HEADS * LANES), F32),
            pltpu.VMEM((nb, DSA_KV_HEADS, DSA_HD + ONES_ROWS, DSA_GROUP * LANES), F32),
            pltpu.VMEM((nb, (nblk + 1) // 2, 32, SUBLANES, LANES), I32),
            pltpu.VMEM((nb, 1, LANES), F32),
            pltpu.VMEM((nb, 1, LANES), I32),
        ],
        compiler_params=pltpu.CompilerParams(vmem_limit_bytes=VMEM_LIMIT),
        name="dsa",
    )(by_block(ik2), by_block(ak), by_block(avt), by_block(aq), by_block(iq), by_block(iwt), by_block(az))
    return out.reshape(B * nblk, LANES, DSA_W)


def _out_kernel(mg_ref, md_ref, h_ref, w_ref, g_ref, b_ref, o_ref, *, alpha):
    n_sub = OUT_ROWS // OUT_SUB_ROWS
    subs = [slice(r * OUT_SUB_ROWS, (r + 1) * OUT_SUB_ROWS) for r in range(n_sub)]
    ys = [_dot(mg_ref[0, s, :], w_ref[:GLA_W, :]) + _dot(md_ref[0, s, :], w_ref[GLA_W:, :]) for s in subs]
    for s, y in zip(subs, ys):
        o_ref[0, s, :] = _layer_norm_rows(alpha * h_ref[0, s, :] + y, g_ref[...], b_ref[...])


def _out(mg, md, h, w, g, b, B, LP, alpha, drop_filler):
    rows = OUT_ROWS
    if drop_filler:
        S = LP - LANES
        grid, nout = (B, S // rows), S
        src_blk = lambda width: (pl.Element(1), pl.Element(rows), pl.Element(width))
        src = lambda b_, j: (b_, pl.multiple_of(LANES + j * rows, LANES), 0)
    else:
        grid, nout = (B * LP // rows,), LP
        src_blk = lambda width: (1, rows, width)
        mg, md, h = (a.reshape(1, B * LP, a.shape[-1]) for a in (mg, md, h))
        src = lambda i: (0, i, 0)
    dst = (lambda b_, j: (b_, j, 0)) if drop_filler else (lambda i: (0, i, 0))
    const = (lambda b_, j: (0, 0)) if drop_filler else (lambda i: (0, 0))
    out_shape = (B, nout, D_MODEL) if drop_filler else (1, B * LP, D_MODEL)
    kern = functools.partial(_out_kernel, alpha=alpha)
    return pl.pallas_call(
        kern,
        grid=grid,
        in_specs=[
            pl.BlockSpec(src_blk(GLA_W), src),
            pl.BlockSpec(src_blk(DSA_W), src),
            pl.BlockSpec(src_blk(D_MODEL), src),
            pl.BlockSpec((MIX_W, D_MODEL), const),
            pl.BlockSpec((1, D_MODEL), const),
            pl.BlockSpec((1, D_MODEL), const),
        ],
        out_specs=pl.BlockSpec((1, rows, D_MODEL), dst),
        out_shape=jax.ShapeDtypeStruct(out_shape, F32),
        name="out_proj_ln",
    )(mg, md, h, w, g, b)


def _pack_w_in(w):
    splits = (256, 256, 512, GLA_RANK, 512, 512, 128, 128, 512, IDX_HD, IDX_HEADS, 512)
    offs = np.cumsum((0,) + splits)
    gq, gk, gv, glr, gz, aq, ak, av, iq, ik, iw, az = [w[:, offs[i]:offs[i + 1]] for i in range(12)]

    def perm_heads(a):
        a = a.reshape(D_MODEL, DSA_KV_HEADS, DSA_GROUP, DSA_HD)
        return a.transpose(0, 2, 1, 3).reshape(D_MODEL, DSA_W)

    pad = jnp.zeros((D_MODEL, LANES - IDX_HD - GLA_RANK - IDX_HEADS), w.dtype)
    pieces = [gq * (GLA_DK ** -0.5), gk, gv, gz, perm_heads(aq) * (DSA_HD ** -0.5), ak, av,
              iq * (IDX_HD ** -0.5), az, ik, glr, iw, pad]
    return jnp.concatenate([p.astype(BF16) for p in pieces], axis=1)


def _rope_lane_tables(reps, LP):
    inv = ROPE_THETA ** (-jnp.arange(0, ROPE_DIM, 2, dtype=F32) / ROPE_DIM)
    pos = (jnp.arange(LP, dtype=F32) - PAD)[:, None]
    ang = pos * inv[None, :]
    cos, sin = jnp.cos(ang), jnp.sin(ang)
    ones = jnp.ones((LP, DSA_HD - ROPE_DIM), F32)
    zeros = jnp.zeros((LP, DSA_HD - ROPE_DIM), F32)
    zh = jnp.zeros((LP, ROPE_HALF), F32)
    c = jnp.concatenate([cos, cos, ones], axis=1)
    sa = jnp.concatenate([zh, sin, zeros], axis=1)
    sb = jnp.concatenate([-sin, zh, zeros], axis=1)
    tile = lambda t: jnp.tile(jnp.concatenate([t, t], axis=1), (reps, 1))
    return tile(c), tile(sa), tile(sb)


def kernel(x, meta_tokens, ln_in_g, ln_in_b, w_in, gla_wg2, gla_bg, gla_norm_g, idx_k_g, idx_k_b,
           w_out, ln_g, ln_b):
    B, S, D = x.shape
    depth = w_in.shape[0]
    nblk = S // LANES + 1
    LP = nblk * LANES
    R = B * LP
    topk = min(TOPK_MAX, S // 4)
    alpha = (2.0 * depth) ** 0.25
    tm = PROJ_ROWS

    meta_pad = jnp.concatenate([jnp.zeros((PAD, D), x.dtype), meta_tokens.astype(x.dtype)], axis=0)
    h = _embed(x, meta_pad, ln_in_g.reshape(1, D), ln_in_b.reshape(1, D))
    cos_t, sa_t, sb_t = _rope_lane_tables(tm // math.gcd(tm, LP), LP)

    for i in range(depth):
        w = _pack_w_in(w_in[i])
        wg2 = jnp.zeros((LANES, 256), F32).at[MISC_GLR:MISC_GLR + GLA_RANK].set(gla_wg2[i])
        wg2h = wg2.astype(BF16)
        wg2l = (wg2 - wg2h.astype(F32)).astype(BF16)
        ikg = jnp.zeros((1, LANES), F32).at[0, :IDX_HD].set(idx_k_g[i])
        ikb = jnp.zeros((1, LANES), F32).at[0, :IDX_HD].set(idx_k_b[i])
        (gq, gk, gv, glog, gz, aq, ak, avt, iq, ik2, iwt, az) = _proj(
            h.reshape(R, D), w, wg2h, wg2l, gla_bg[i].reshape(1, 256), ikg, ikb,
            cos_t, sa_t, sb_t, tm)
        mix_gla = _gla(gq, gk, glog, gv, gz, gla_norm_g[i].reshape(1, GLA_DV), B, LP)
        mix_dsa = _dsa(ik2, ak, avt, aq, iq, iwt, az, B, nblk, topk)
        h = _out(mix_gla, mix_dsa.reshape(B, LP, DSA_W), h.reshape(B, LP, D),
                 w_out[i].astype(BF16), ln_g[i].reshape(1, D), ln_b[i].reshape(1, D),
                 B, LP, alpha, drop_filler=(i == depth - 1))
    return h.reshape(B, S, D)
```

```python
import functools
import math

import numpy as np
import jax
import jax.numpy as jnp
from jax import lax
from jax.experimental import pallas as pl
from jax.experimental.pallas import tpu as pltpu

F32 = jnp.float32
BF16 = jnp.bfloat16
I32 = jnp.int32

D_MODEL = 1024
N_META = 16
ROPE_THETA = 500000.0
LN_EPS = 1e-5
GLA_HEADS = 4
GLA_DK = 64
GLA_DV = 128
GLA_RANK = 16
GLA_TAU = 16.0
GLA_CHUNK = 64
GLA_W = GLA_HEADS * GLA_DV
DSA_HEADS = 8
DSA_KV_HEADS = 2
DSA_GROUP = DSA_HEADS // DSA_KV_HEADS
DSA_HD = 64
DSA_W = DSA_HEADS * DSA_HD
IDX_HEADS = 8
IDX_HD = 64
TOPK_MAX = 256
ROPE_DIM = DSA_HD // 4
ROPE_HALF = ROPE_DIM // 2
MIX_W = GLA_W + DSA_W

LANES = 128
SUBLANES = 8
ONES_ROWS = 16
PAD = LANES - N_META
INT_MIN = -(2 ** 31)
M_INIT = -(2.0 ** 100)
VMEM_LIMIT = 48 * 1024 * 1024
PROJ_ROWS = 512
OUT_ROWS = 512
OUT_SUB_ROWS = 256

O_GQ, O_GK, O_GV, O_GZ = 0, 256, 512, 1024
O_AQ, O_AK, O_AV, O_IQ, O_AZ, O_MISC = 1536, 2048, 2176, 2304, 2816, 3328
W_PACKED = 3456
MISC_GLR = IDX_HD
MISC_IW = IDX_HD + GLA_RANK


def _dot(a, b):
    return jnp.dot(a, b, preferred_element_type=F32)


def _dot_nt(a, b):
    return lax.dot_general(a, b, (((1,), (1,)), ((), ())), preferred_element_type=F32)


def _dot_tn(a, b):
    return lax.dot_general(a, b, (((0,), (0,)), ((), ())), preferred_element_type=F32)


def _layer_norm_rows(u, g, b):
    mu = jnp.mean(u, axis=-1, keepdims=True)
    d = u - mu
    var = jnp.mean(d * d, axis=-1, keepdims=True)
    return d * lax.rsqrt(var + LN_EPS) * g + b


def _embed_kernel(x_ref, meta_ref, g_ref, b_ref, o_ref):
    g, b = g_ref[...], b_ref[...]
    o_ref[0] = _layer_norm_rows(meta_ref[...], g, b)
    for c in range(x_ref.shape[1]):
        o_ref[c + 1] = _layer_norm_rows(x_ref[0, c], g, b)


def _embed(x, meta_pad, g, b):
    B, S, D = x.shape
    nblk = S // LANES + 1
    x4 = x.reshape(B, S // LANES, LANES, D)
    return pl.pallas_call(
        _embed_kernel,
        grid=(B,),
        in_specs=[
            pl.BlockSpec((1, S // LANES, LANES, D), lambda b_: (b_, 0, 0, 0)),
            pl.BlockSpec((LANES, D), lambda b_: (0, 0)),
            pl.BlockSpec((1, D), lambda b_: (0, 0)),
            pl.BlockSpec((1, D), lambda b_: (0, 0)),
        ],
        out_specs=pl.BlockSpec((nblk, LANES, D), lambda b_: (b_, 0, 0)),
        out_shape=jax.ShapeDtypeStruct((B * nblk, LANES, D), F32),
        compiler_params=pltpu.CompilerParams(vmem_limit_bytes=VMEM_LIMIT),
        name="embed_ln",
    )(x4, meta_pad, g, b)


def _rope_slab(x, c, sa, sb):
    return x * c + pltpu.roll(x, ROPE_HALF, 1) * sa + pltpu.roll(x, LANES - ROPE_HALF, 1) * sb


def _proj_kernel(h_ref, w_ref, wg2h_ref, wg2l_ref, bg_ref, ikg_ref, ikb_ref,
                 cos_ref, sa_ref, sb_ref,
                 gq_ref, gk_ref, gv_ref, glog_ref, gz_ref, aq_ref, ak_ref, avt_ref,
                 iq_ref, ik_ref, iwt_ref, az_ref, *, tm):
    hb = h_ref[...].astype(BF16)

    def seg(o, w):
        return _dot(hb, w_ref[:, o:o + w])

    c, sa, sb = cos_ref[...], sa_ref[...], sb_ref[...]
    misc = seg(O_MISC, LANES)
    kv = seg(O_AK, 2 * LANES)
    ak, av = kv[:, :LANES], kv[:, LANES:]
    aq = seg(O_AQ, 512)

    misc_t_scale = IDX_HEADS ** -0.5
    for r in range(tm // LANES):
        rows = slice(r * LANES, (r + 1) * LANES)
        avt_ref[r] = av[rows, :].T.astype(BF16)
        iwt_ref[r] = misc[rows, :].T[MISC_IW:MISC_IW + IDX_HEADS, :] * misc_t_scale
    ak_ref[...] = _rope_slab(ak, c, sa, sb).astype(BF16)
    iq = seg(O_IQ, 512)

    m_hi = misc.astype(BF16)
    m_lo = (misc - m_hi.astype(F32)).astype(BF16)
    xg = (_dot(m_hi, wg2h_ref[...]) + _dot(m_lo, wg2h_ref[...]) + _dot(m_hi, wg2l_ref[...])
          + bg_ref[...])
    gq_ref[...] = seg(O_GQ, 256)
    glog_ref[...] = (jnp.minimum(xg, 0.0) - jnp.log1p(jnp.exp(-jnp.abs(xg)))) * (1.0 / GLA_TAU)

    lane = lax.broadcasted_iota(I32, (1, LANES), 1)
    is_key = lane < IDX_HD
    mu = jnp.sum(jnp.where(is_key, misc, 0.0), axis=-1, keepdims=True) * (1.0 / IDX_HD)
    d = jnp.where(is_key, misc - mu, 0.0)
    var = jnp.sum(d * d, axis=-1, keepdims=True) * (1.0 / IDX_HD)
    ikn = d * lax.rsqrt(var + LN_EPS) * ikg_ref[...] + ikb_ref[...]
    ikr = _rope_slab(ikn, c, sa, sb)
    gk_ref[...] = seg(O_GK, 256)
    ik_ref[...] = jnp.where(is_key, ikr, pltpu.roll(ikr, IDX_HD, 1)).astype(BF16)

    gv = seg(O_GV, 512)
    for m in range(4):
        sl = slice(m * LANES, (m + 1) * LANES)
        aq_ref[:, sl] = _rope_slab(aq[:, sl], c, sa, sb).astype(BF16)
    gv_ref[...] = gv.astype(BF16)
    gz = seg(O_GZ, 512)
    for m in range(4):
        sl = slice(m * LANES, (m + 1) * LANES)
        iq_ref[:, sl] = _rope_slab(iq[:, sl], c, sa, sb).astype(BF16)
    gz_ref[...] = gz
    az_ref[...] = seg(O_AZ, 512)


def _proj(h2, w, wg2h, wg2l, bg, ikg, ikb, cos_t, sa_t, sb_t, tm):
    R = h2.shape[0]
    nt = R // tm
    nsub = tm // LANES
    row = lambda i: (i, 0)
    const = lambda i: (0, 0)
    table_tiles = cos_t.shape[0] // tm
    table_row = lambda i: (i % table_tiles, 0)
    kern = functools.partial(_proj_kernel, tm=tm)
    out_shape = (
        jax.ShapeDtypeStruct((R, 256), F32),
        jax.ShapeDtypeStruct((R, 256), F32),
        jax.ShapeDtypeStruct((R, 512), BF16),
        jax.ShapeDtypeStruct((R, 256), F32),
        jax.ShapeDtypeStruct((R, 512), F32),
        jax.ShapeDtypeStruct((R, 512), BF16),
        jax.ShapeDtypeStruct((R, LANES), BF16),
        jax.ShapeDtypeStruct((R // LANES, LANES, LANES), BF16),
        jax.ShapeDtypeStruct((R, 512), BF16),
        jax.ShapeDtypeStruct((R, LANES), BF16),
        jax.ShapeDtypeStruct((R // LANES, IDX_HEADS, LANES), F32),
        jax.ShapeDtypeStruct((R, 512), F32),
    )
    out_specs = (
        pl.BlockSpec((tm, 256), row),
        pl.BlockSpec((tm, 256), row),
        pl.BlockSpec((tm, 512), row),
        pl.BlockSpec((tm, 256), row),
        pl.BlockSpec((tm, 512), row),
        pl.BlockSpec((tm, 512), row),
        pl.BlockSpec((tm, LANES), row),
        pl.BlockSpec((nsub, LANES, LANES), lambda i: (i, 0, 0)),
        pl.BlockSpec((tm, 512), row),
        pl.BlockSpec((tm, LANES), row),
        pl.BlockSpec((nsub, IDX_HEADS, LANES), lambda i: (i, 0, 0)),
        pl.BlockSpec((tm, 512), row),
    )
    in_specs = [
        pl.BlockSpec((tm, D_MODEL), row),
        pl.BlockSpec((D_MODEL, W_PACKED), const),
        pl.BlockSpec((LANES, 256), const),
        pl.BlockSpec((LANES, 256), const),
        pl.BlockSpec((1, 256), const),
        pl.BlockSpec((1, LANES), const),
        pl.BlockSpec((1, LANES), const),
        pl.BlockSpec((tm, LANES), table_row),
        pl.BlockSpec((tm, LANES), table_row),
        pl.BlockSpec((tm, LANES), table_row),
    ]
    return pl.pallas_call(
        kern, grid=(nt,), in_specs=in_specs, out_specs=out_specs, out_shape=out_shape,
        compiler_params=pltpu.CompilerParams(vmem_limit_bytes=VMEM_LIMIT),
        name="in_proj",
    )(h2, w, wg2h, wg2l, bg, ikg, ikb, cos_t, sa_t, sb_t)


GLA_NB = 8
GLA_PIECES = 17
GLA_SAFE_EXP = 80.0


def _gla_kernel(q_ref, k_ref, g_ref, v_ref, z_ref, ng_ref, o_ref, st_ref, a_ref, tmp_ref,
                *, n_chunks):
    C = GLA_CHUNK
    piece = pl.program_id(1)

    @pl.when(piece == 0)
    def _():
        st_ref[...] = jnp.zeros_like(st_ref)

    lane = lax.broadcasted_iota(I32, (1, LANES), 1)
    head_lanes = (lane < GLA_DK, lane >= GLA_DK)
    rr = lax.broadcasted_iota(I32, (C, C), 0)
    cc = lax.broadcasted_iota(I32, (C, C), 1)
    causal = rr >= cc
    tril = causal.astype(BF16)
    row = lax.broadcasted_iota(I32, (C, 1), 0)
    ng = ng_ref[...]
    row0 = piece * (n_chunks * C)

    def body(ci, carry):
        r0 = pl.multiple_of(ci * C, C)
        rows = pl.ds(r0, C)
        valid = (row0 + r0 + row) >= PAD
        pairs = [(bi, pr) for bi in range(GLA_NB) for pr in range(GLA_HEADS // 2)]

        def cum_decay(bi, pr):
            g = g_ref[bi, rows, pr * LANES:(pr + 1) * LANES]
            g_hi = g.astype(BF16)
            g_lo = (g - g_hi.astype(F32)).astype(BF16)
            return _dot(tril, g_hi) + _dot(tril, g_lo)

        def load_qk(bi, pr):
            pl_ = slice(pr * LANES, (pr + 1) * LANES)
            return q_ref[bi, rows, pl_], jnp.where(valid, k_ref[bi, rows, pl_], 0.0)

        bs = [cum_decay(bi, pr) for bi, pr in pairs]
        ops = []
        span = jnp.zeros((C, LANES), F32)
        for (bi, pr), b in zip(pairs, bs):
            q, k = load_qk(bi, pr)
            b_mid = b[C // 2 - 1:C // 2, :]
            b_last = b[C - 1:C, :]
            d_mid = b - b_mid
            span = jnp.maximum(span, jnp.abs(d_mid))
            q_in = (q * jnp.exp(d_mid)).astype(BF16)
            k_in = (k * jnp.exp(-d_mid)).astype(BF16)
            q_st = (q * jnp.exp(b)).astype(BF16)
            k_st = (k * jnp.exp(b_last - b)).astype(BF16)
            ops.append((q_in, k_in, q_st, k_st, jnp.exp(b_last)))
        heads = [(bi, pr, h) for bi, pr in pairs for h in range(2)]
        zero = jnp.zeros((C, LANES), BF16)
        part = []
        for n, (bi, pr, h) in enumerate(heads):
            q_in, k_in, q_st, k_st, decay = ops[n // 2]
            hl = head_lanes[h]
            head = 2 * pr + h
            cols = slice(head * GLA_DV, (head + 1) * GLA_DV)
            v = jnp.where(valid, v_ref[bi, rows, cols], jnp.zeros((), BF16))
            st = st_ref[bi, head]
            a_ref[n] = jnp.where(causal, _dot_nt(jnp.where(hl, q_in, zero), k_in), 0.0)
            o_st = _dot_nt(jnp.where(hl, q_st, zero), st.astype(BF16))
            st_ref[bi, head] = st * decay + _dot_tn(v, jnp.where(hl, k_st, zero))
            part.append((o_st, v, cols))

        @pl.when(jnp.max(span) > GLA_SAFE_EXP)
        def _():
            key_lane = lax.broadcasted_iota(I32, (1, C), 1)
            for p, (bi, pr) in enumerate(pairs):
                q, k = load_qk(bi, pr)
                b = cum_decay(bi, pr)
                tmp_ref[0] = b
                tmp_ref[1] = k

                def key_column(s, acc, q=q, b=b):
                    b_s = tmp_ref[0, pl.ds(s, 1), :]
                    k_s = tmp_ref[1, pl.ds(s, 1), :]
                    w = q * k_s * jnp.exp(jnp.where(row >= s, b - b_s, -jnp.inf))
                    hit = (key_lane == s).astype(F32)
                    return tuple(
                        acc[h] + jnp.sum(jnp.where(head_lanes[h], w, 0.0), axis=-1, keepdims=True) * hit
                        for h in range(2))

                a0, a1 = lax.fori_loop(0, C, key_column, (jnp.zeros((C, C), F32),) * 2)
                a_ref[2 * p] = a0
                a_ref[2 * p + 1] = a1

        for n, ((bi, pr, h), (o_st, v, cols)) in enumerate(zip(heads, part)):
            o = _dot(a_ref[n].astype(BF16), v) + o_st
            on = o * lax.rsqrt(jnp.mean(o * o, axis=-1, keepdims=True) + LN_EPS) * ng
            z = z_ref[bi, rows, cols]
            o_ref[bi, rows, cols] = (on * (z * jax.nn.sigmoid(z))).astype(BF16)
        return carry

    lax.fori_loop(0, n_chunks, body, 0)


def _gla(gq, gk, glog, gv, gz, ng, B, LP):
    rows = LP // GLA_PIECES
    kern = functools.partial(_gla_kernel, n_chunks=rows // GLA_CHUNK)
    blk = lambda b_, p: (b_, p, 0)
    return pl.pallas_call(
        kern,
        grid=(B // GLA_NB, GLA_PIECES),
        in_specs=[
            pl.BlockSpec((GLA_NB, rows, 256), blk),
            pl.BlockSpec((GLA_NB, rows, 256), blk),
            pl.BlockSpec((GLA_NB, rows, 256), blk),
            pl.BlockSpec((GLA_NB, rows, GLA_W), blk),
            pl.BlockSpec((GLA_NB, rows, GLA_W), blk),
            pl.BlockSpec((1, GLA_DV), lambda b_, p: (0, 0)),
        ],
        out_specs=pl.BlockSpec((GLA_NB, rows, GLA_W), blk),
        out_shape=jax.ShapeDtypeStruct((B, LP, GLA_W), BF16),
        scratch_shapes=[
            pltpu.VMEM((GLA_NB, GLA_HEADS, GLA_DV, LANES), F32),
            pltpu.VMEM((GLA_NB * GLA_HEADS, GLA_CHUNK, GLA_CHUNK), F32),
            pltpu.VMEM((2, GLA_CHUNK, LANES), F32),
        ],
        compiler_params=pltpu.CompilerParams(vmem_limit_bytes=VMEM_LIMIT),
        name="gla",
    )(gq.reshape(B, LP, 256), gk.reshape(B, LP, 256), glog.reshape(B, LP, 256),
      gv.reshape(B, LP, GLA_W), gz.reshape(B, LP, GLA_W), ng)


DSA_NB = 2


def _dsa_kernel(ik_ref, ak_ref, avt_ref, aq_ref, iq_ref, iwt_ref, az_ref, o_ref,
                sc_ref, qz_ref, iqz_ref, w_ref, m_ref, acc_ref, plane_ref, thr_ref, cnt_ref,
                *, topk, nblk):
    j = pl.program_id(1)
    npair = (j + 2) // 2
    KP = 2 * LANES
    batch = range(DSA_NB)

    def pair_rows(ref, n, i, axis):
        kb1 = jnp.minimum(2 * i + 1, nblk - 1)
        return jnp.concatenate([ref[n, 2 * i], ref[n, kb1]], axis=axis)

    def pair_ds(i):
        return pl.ds(pl.multiple_of(i * KP, KP), KP)

    lane = lax.broadcasted_iota(I32, (1, LANES), 1)
    low = lane < DSA_HD
    zero_b = jnp.zeros((LANES, LANES), BF16)
    for n in batch:
        for m in range(4):
            sl = slice(m * LANES, (m + 1) * LANES)
            qs = aq_ref[n, 0, :, sl]
            qz_ref[n, m * LANES:(m + 1) * LANES, :] = jnp.where(low, qs, zero_b)
            qz_ref[n, (m + 4) * LANES:(m + 5) * LANES, :] = jnp.where(low, zero_b, qs)
            iqs = iq_ref[n, 0, :, sl]
            iqz_ref[n, (2 * m) * LANES:(2 * m + 1) * LANES, :] = jnp.where(low, iqs, zero_b)
            iqz_ref[n, (2 * m + 1) * LANES:(2 * m + 2) * LANES, :] = jnp.where(low, zero_b, iqs)
        for h in range(IDX_HEADS):
            w_ref[n, :, h * LANES:(h + 1) * LANES] = iwt_ref[n, 0, h:h + 1, :]
    t_pos = j * LANES + lane
    s_loc = lax.broadcasted_iota(I32, (KP, 1), 0)

    @pl.when((pl.program_id(0) == 0) & (j == 0))
    def _():
        plane_ref[...] = jnp.zeros_like(plane_ref)

    def shift_const(x, n):
        return jnp.full(x.shape, n, I32)

    def store_planes(n, i, scores):
        bits = lax.bitcast_convert_type(scores, I32)
        u = bits ^ (lax.shift_right_arithmetic(bits, shift_const(bits, 31)) | jnp.int32(INT_MIN))
        a = [u[SUBLANES * v:SUBLANES * (v + 1), :] for v in range(32)]
        step, mask = 16, 0x0000FFFF
        while step:
            for k in range(32):
                if not k & step:
                    t = (a[k] ^ lax.shift_right_logical(a[k + step], shift_const(a[k], step))) & jnp.int32(mask)
                    a[k] = a[k] ^ t
                    a[k + step] = a[k + step] ^ lax.shift_left(t, shift_const(t, step))
            step >>= 1
            mask = (mask ^ (mask << step)) & 0xFFFFFFFF
        for r in range(32):
            plane_ref[n, i, r] = a[r]

    def score_pairs(pairs):
        work = [(n, i) for i in pairs for n in batch]
        dots = [_dot_nt(pair_rows(ik_ref, n, i, 0), iqz_ref[n]) for n, i in work]
        for (n, i), d in zip(work, dots):
            d = jnp.maximum(d, 0.0) * w_ref[n]
            acc = d[:, :LANES]
            for h in range(1, IDX_HEADS):
                acc = acc + d[:, h * LANES:(h + 1) * LANES]
            s_pos = i * KP + s_loc
            valid = (s_pos <= t_pos) & ((s_pos >= PAD) | (t_pos < PAD))
            scores = jnp.where(valid, acc, -jnp.inf)
            sc_ref[n, pair_ds(i), :] = scores
            store_planes(n, i, scores)

    def two_pair_trips(body):
        odd = npair % 2 == 1

        @pl.when(odd & (npair == 1))
        def _():
            body((0,))

        @pl.when(odd & (npair > 1))
        def _():
            body((0, 1, 2))

        first = jnp.where(odd, jnp.where(npair > 1, 3, 1), 0)

        def trip(u, carry):
            body((first + 2 * u, first + 2 * u + 1))
            return carry

        lax.fori_loop(0, (npair - first) // 2, trip, 0)

    two_pair_trips(score_pairs)

    def count(preds):
        flat = [(n, p) for n in batch for p in preds[n]]

        def cnt_block(i, cnts):
            xs = [sc_ref[n, pair_ds(i), :] for n in batch]
            out = []
            for (n, pred), cnt in zip(flat, cnts):
                hit = pred(xs[n]).astype(I32)
                parts = [hit[r:r + SUBLANES, :] for r in range(0, KP, SUBLANES)]
                while len(parts) > 1:
                    parts = [parts[k] + parts[k + 1] for k in range(0, len(parts), 2)]
                out.append(cnt + parts[0])
            return tuple(out)

        zero = jnp.zeros((SUBLANES, LANES), I32)
        cnts = lax.fori_loop(0, npair, cnt_block, (zero,) * len(flat))
        sums = [jnp.sum(c, axis=0, keepdims=True) for c in cnts]
        per = len(flat) // DSA_NB
        return [tuple(sums[n * per:(n + 1) * per]) for n in batch]

    select_all = jnp.float32(jnp.finfo(F32).min)

    ones_v = jnp.full((SUBLANES, LANES), -1, I32)
    zeros_v = jnp.zeros((SUBLANES, LANES), I32)

    def radix_select(n_pairs):
        alive0 = tuple(tuple(jnp.where(i < npair, ones_v, zeros_v) for i in range(n_pairs)) for n in batch)

        def radix_step(t, carry):
            alive_all, k_all, u_all = carry
            out_alive, out_k, out_u = [], [], []
            counts = []
            for n in batch:
                alive = alive_all[n]
                hi = [plane_ref[n, i, 2 * t] for i in range(n_pairs)]
                lo = [plane_ref[n, i, 2 * t + 1] for i in range(n_pairs)]
                c11 = c1x = c01 = zeros_v
                for i in range(n_pairs):
                    a1 = alive[i] & hi[i]
                    a0 = alive[i] ^ a1
                    c11 = c11 + lax.population_count(a1 & lo[i])
                    c1x = c1x + lax.population_count(a1)
                    c01 = c01 + lax.population_count(a0 & lo[i])
                counts.append((hi, lo, c11, c1x, c01))
            for n in batch:
                hi, lo, c11, c1x, c01 = counts[n]
                alive, k_left = alive_all[n], k_all[n]
                c11 = jnp.sum(c11, axis=0, keepdims=True)
                c1x = jnp.sum(c1x, axis=0, keepdims=True)
                c01 = jnp.sum(c01, axis=0, keepdims=True)
                in3 = k_left <= c11
                in32 = k_left <= c1x
                in321 = k_left <= c1x + c01
                bit_hi = in32
                bit_lo = in3 | (~in32 & in321)
                k_left = jnp.where(in3, k_left,
                                   jnp.where(in32, k_left - c11,
                                             jnp.where(in321, k_left - c1x, k_left - c1x - c01)))
                f_hi = jnp.where(bit_hi, jnp.int32(0), jnp.int32(-1))
                f_lo = jnp.where(bit_lo, jnp.int32(0), jnp.int32(-1))
                out_alive.append(tuple(alive[i] & (hi[i] ^ f_hi) & (lo[i] ^ f_lo) for i in range(n_pairs)))
                digit = jnp.where(bit_hi, jnp.int32(2), jnp.int32(0)) | jnp.where(bit_lo, jnp.int32(1), jnp.int32(0))
                out_k.append(k_left)
                out_u.append(u_all[n] | lax.shift_left(digit, jnp.full(digit.shape, 30 - 2 * t, I32)))
            return tuple(out_alive), tuple(out_k), tuple(out_u)

        _, _, u_thr = lax.fori_loop(
            0, 16, radix_step,
            (alive0, (jnp.full((1, LANES), topk, I32),) * DSA_NB, (jnp.zeros((1, LANES), I32),) * DSA_NB))
        for n in batch:
            thr_bits = jnp.where(u_thr[n] < 0, u_thr[n] ^ jnp.int32(INT_MIN), ~u_thr[n])
            tf = lax.bitcast_convert_type(thr_bits, F32)
            thr_ref[n] = jnp.where(tf == -jnp.inf, select_all, tf)

    n_pairs_max = plane_ref.shape[1]
    n_pairs_few = (n_pairs_max + 1) // 2

    @pl.when(npair <= n_pairs_few)
    def _():
        radix_select(n_pairs_few)

    @pl.when(npair > n_pairs_few)
    def _():
        radix_select(n_pairs_max)

    thr_fast = [thr_ref[n] for n in batch]

    def ge_gt(n):
        return (lambda x: x >= thr_fast[n], lambda x: x > thr_fast[n])

    n_bad = jnp.int32(0)
    for n, (n_ge, n_gt) in zip(batch, count([ge_gt(n) for n in batch])):
        is_all = thr_fast[n] == select_all
        good = is_all | ((n_gt < topk) & (n_ge >= topk))
        cnt_ref[n] = jnp.where(is_all, 0, n_ge)
        n_bad = n_bad + jnp.sum(jnp.where(good, 0, 1))

    @pl.when(n_bad > 0)
    def _():
        def key_to_float(key):
            bits = jnp.where(key < 0, key ^ jnp.int32(0x7FFFFFFF), key)
            return lax.bitcast_convert_type(bits, F32)

        def count_ge(cand_keys):
            cands = [key_to_float(c) for c in cand_keys]
            return [c[0] for c in count([((lambda x, n=n: x >= cands[n]),) for n in batch])]

        c0 = count_ge([jnp.zeros((1, LANES), I32)] * DSA_NB)
        base = tuple(jnp.where(c >= topk, jnp.int32(0), jnp.int32(INT_MIN)) for c in c0)
        base_cnt = tuple(jnp.where(c >= topk, c, 0) for c in c0)

        def bit_step(i, carry):
            base, base_cnt = carry
            cand = [b | jnp.left_shift(jnp.int32(1), 30 - i) for b in base]
            cs = count_ge(cand)
            ok = [c >= topk for c in cs]
            return (tuple(jnp.where(ok[n], cand[n], base[n]) for n in batch),
                    tuple(jnp.where(ok[n], cs[n], base_cnt[n]) for n in batch))

        base, base_cnt = lax.fori_loop(0, 31, bit_step, (base, base_cnt))
        for n in batch:
            thr_ref[n] = jnp.where(base[n] == INT_MIN, select_all, key_to_float(base[n]))
            cnt_ref[n] = base_cnt[n]

    thr = [thr_ref[n] for n in batch]
    base_cnt = [cnt_ref[n] for n in batch]

    most = base_cnt[0]
    for n in batch[1:]:
        most = jnp.maximum(most, base_cnt[n])

    @pl.when(jnp.max(most) > topk)
    def _():
        n_gt = count([((lambda x, n=n: x > thr[n]),) for n in batch])
        rr = lax.broadcasted_iota(I32, (KP, KP), 0)
        cc = lax.broadcasted_iota(I32, (KP, KP), 1)
        tri = (rr >= cc).astype(BF16)
        for n in batch:
            need = (topk - n_gt[n][0]).astype(F32)

            def strike(i, seen, n=n, need=need):
                x = sc_ref[n, pair_ds(i), :]
                eq = x == thr[n]
                rank = _dot(tri, jnp.where(eq, 1.0, 0.0).astype(BF16)) + seen
                sc_ref[n, pair_ds(i), :] = jnp.where(eq & (rank > need), -jnp.inf, x)
                return rank[KP - 1:KP, :]

            lax.fori_loop(0, npair, strike, jnp.zeros((1, LANES), F32))

    m_ref[...] = jnp.full_like(m_ref, M_INIT)
    acc_ref[...] = jnp.zeros_like(acc_ref)
    GW = DSA_GROUP * LANES
    ones_rows = jnp.ones((ONES_ROWS, KP), BF16)

    def attn_pairs(pairs):
        work = [(n, i) for i in pairs for n in batch]
        logits = [_dot_nt(pair_rows(ak_ref, n, i, 0), qz_ref[n]).astype(BF16) for n, i in work]
        for (n, i), s in zip(work, logits):
            bias = jnp.where(sc_ref[n, pair_ds(i), :] >= thr[n], 0.0, -jnp.inf).astype(BF16)
            sb = s + jnp.concatenate([bias] * DSA_HEADS, axis=1)
            m_old = m_ref[n]
            m_new = jnp.maximum(m_old, jnp.max(sb, axis=0, keepdims=True).astype(F32))
            alpha = jnp.exp(m_old - m_new)
            pb = jnp.exp(sb - m_new.astype(BF16))
            m_ref[n] = m_new
            vt = pair_rows(avt_ref, n, i, 1)
            for g in range(DSA_KV_HEADS):
                cols = slice(g * GW, (g + 1) * GW)
                vg = jnp.concatenate([vt[g * DSA_HD:(g + 1) * DSA_HD, :], ones_rows], axis=0)
                acc_ref[n, g] = alpha[:, cols] * acc_ref[n, g] + _dot(vg, pb[:, cols])

    two_pair_trips(attn_pairs)

    for n in batch:
        heads = []
        for h in range(DSA_HEADS):
            a = acc_ref[n, h // DSA_GROUP][:, (h % DSA_GROUP) * LANES:(h % DSA_GROUP + 1) * LANES]
            heads.append(a[:DSA_HD, :] / a[DSA_HD:DSA_HD + 1, :])
        ot = jnp.concatenate(heads, axis=0)
        z = az_ref[n, 0]
        o_ref[n, 0] = (ot.T * (z * jax.nn.sigmoid(z))).astype(BF16)


def _dsa(ik2, ak, avt, aq, iq, iwt, az, B, nblk, topk):
    kern = functools.partial(_dsa_kernel, topk=topk, nblk=nblk)
    nb = DSA_NB
    whole = lambda b_, j: (b_, 0, 0, 0)
    qblk = lambda b_, j: (b_, j, 0, 0)
    by_block = lambda a: a.reshape(B, nblk, -1, a.shape[-1])
    out = pl.pallas_call(
        kern,
        grid=(B // nb, nblk),
        in_specs=[
            pl.BlockSpec((nb, nblk, LANES, LANES), whole),
            pl.BlockSpec((nb, nblk, LANES, LANES), whole),
            pl.BlockSpec((nb, nblk, LANES, LANES), whole),
            pl.BlockSpec((nb, 1, LANES, DSA_W), qblk),
            pl.BlockSpec((nb, 1, LANES, IDX_HEADS * IDX_HD), qblk),
            pl.BlockSpec((nb, 1, IDX_HEADS, LANES), qblk),
            pl.BlockSpec((nb, 1, LANES, DSA_W), qblk),
        ],
        out_specs=pl.BlockSpec((nb, 1, LANES, DSA_W), qblk),
        out_shape=jax.ShapeDtypeStruct((B, nblk, LANES, DSA_W), BF16),
        scratch_shapes=[
            pltpu.VMEM((nb, (nblk + 1) // 2 * 2 * LANES, LANES), F32),
            pltpu.VMEM((nb, DSA_HEADS * LANES, LANES), BF16),
            pltpu.VMEM((nb, IDX_HEADS * LANES, LANES), BF16),
            pltpu.VMEM((nb, 1, IDX_HEADS * LANES), F32),
            pltpu.VMEM((nb, 1, DSA_HEADS * LANES), F32),
            pltpu.VMEM((nb, DSA_KV_HEADS, DSA_HD + ONES_ROWS, DSA_GROUP * LANES), F32),
            pltpu.VMEM((nb, (nblk + 1) // 2, 32, SUBLANES, LANES), I32),
            pltpu.VMEM((nb, 1, LANES), F32),
            pltpu.VMEM((nb, 1, LANES), I32),
        ],
        compiler_params=pltpu.CompilerParams(vmem_limit_bytes=VMEM_LIMIT),
        name="dsa",
    )(by_block(ik2), by_block(ak), by_block(avt), by_block(aq), by_block(iq), by_block(iwt), by_block(az))
    return out.reshape(B * nblk, LANES, DSA_W)


def _out_kernel(mg_ref, md_ref, h_ref, w_ref, g_ref, b_ref, o_ref, *, alpha):
    n_sub = OUT_ROWS // OUT_SUB_ROWS
    subs = [slice(r * OUT_SUB_ROWS, (r + 1) * OUT_SUB_ROWS) for r in range(n_sub)]
    ys = [_dot(mg_ref[0, s, :], w_ref[:GLA_W, :]) + _dot(md_ref[0, s, :], w_ref[GLA_W:, :]) for s in subs]
    for s, y in zip(subs, ys):
        o_ref[0, s, :] = _layer_norm_rows(alpha * h_ref[0, s, :] + y, g_ref[...], b_ref[...])


def _out(mg, md, h, w, g, b, B, LP, alpha, drop_filler):
    rows = OUT_ROWS
    if drop_filler:
        S = LP - LANES
        grid, nout = (B, S // rows), S
        src_blk = lambda width: (pl.Element(1), pl.Element(rows), pl.Element(width))
        src = lambda b_, j: (b_, pl.multiple_of(LANES + j * rows, LANES), 0)
    else:
        grid, nout = (B * LP // rows,), LP
        src_blk = lambda width: (1, rows, width)
        mg, md, h = (a.reshape(1, B * LP, a.shape[-1]) for a in (mg, md, h))
        src = lambda i: (0, i, 0)
    dst = (lambda b_, j: (b_, j, 0)) if drop_filler else (lambda i: (0, i, 0))
    const = (lambda b_, j: (0, 0)) if drop_filler else (lambda i: (0, 0))
    out_shape = (B, nout, D_MODEL) if drop_filler else (1, B * LP, D_MODEL)
    kern = functools.partial(_out_kernel, alpha=alpha)
    return pl.pallas_call(
        kern,
        grid=grid,
        in_specs=[
            pl.BlockSpec(src_blk(GLA_W), src),
            pl.BlockSpec(src_blk(DSA_W), src),
            pl.BlockSpec(src_blk(D_MODEL), src),
            pl.BlockSpec((MIX_W, D_MODEL), const),
            pl.BlockSpec((1, D_MODEL), const),
            pl.BlockSpec((1, D_MODEL), const),
        ],
        out_specs=pl.BlockSpec((1, rows, D_MODEL), dst),
        out_shape=jax.ShapeDtypeStruct(out_shape, F32),
        name="out_proj_ln",
    )(mg, md, h, w, g, b)


def _pack_w_in(w):
    splits = (256, 256, 512, GLA_RANK, 512, 512, 128, 128, 512, IDX_HD, IDX_HEADS, 512)
    offs = np.cumsum((0,) + splits)
    gq, gk, gv, glr, gz, aq, ak, av, iq, ik, iw, az = [w[:, offs[i]:offs[i + 1]] for i in range(12)]

    def perm_heads(a):
        a = a.reshape(D_MODEL, DSA_KV_HEADS, DSA_GROUP, DSA_HD)
        return a.transpose(0, 2, 1, 3).reshape(D_MODEL, DSA_W)

    pad = jnp.zeros((D_MODEL, LANES - IDX_HD - GLA_RANK - IDX_HEADS), w.dtype)
    pieces = [gq * (GLA_DK ** -0.5), gk, gv, gz, perm_heads(aq) * (DSA_HD ** -0.5), ak, av,
              iq * (IDX_HD ** -0.5), az, ik, glr, iw, pad]
    return jnp.concatenate([p.astype(BF16) for p in pieces], axis=1)


def _rope_lane_tables(reps, LP):
    inv = ROPE_THETA ** (-jnp.arange(0, ROPE_DIM, 2, dtype=F32) / ROPE_DIM)
    pos = (jnp.arange(LP, dtype=F32) - PAD)[:, None]
    ang = pos * inv[None, :]
    cos, sin = jnp.cos(ang), jnp.sin(ang)
    ones = jnp.ones((LP, DSA_HD - ROPE_DIM), F32)
    zeros = jnp.zeros((LP, DSA_HD - ROPE_DIM), F32)
    zh = jnp.zeros((LP, ROPE_HALF), F32)
    c = jnp.concatenate([cos, cos, ones], axis=1)
    sa = jnp.concatenate([zh, sin, zeros], axis=1)
    sb = jnp.concatenate([-sin, zh, zeros], axis=1)
    tile = lambda t: jnp.tile(jnp.concatenate([t, t], axis=1), (reps, 1))
    return tile(c), tile(sa), tile(sb)


def kernel(x, meta_tokens, ln_in_g, ln_in_b, w_in, gla_wg2, gla_bg, gla_norm_g, idx_k_g, idx_k_b,
           w_out, ln_g, ln_b):
    B, S, D = x.shape
    depth = w_in.shape[0]
    nblk = S // LANES + 1
    LP = nblk * LANES
    R = B * LP
    topk = min(TOPK_MAX, S // 4)
    alpha = (2.0 * depth) ** 0.25
    tm = PROJ_ROWS

    meta_pad = jnp.concatenate([jnp.zeros((PAD, D), x.dtype), meta_tokens.astype(x.dtype)], axis=0)
    h = _embed(x, meta_pad, ln_in_g.reshape(1, D), ln_in_b.reshape(1, D))
    cos_t, sa_t, sb_t = _rope_lane_tables(tm // math.gcd(tm, LP), LP)

    for i in range(depth):
        w = _pack_w_in(w_in[i])
        wg2 = jnp.zeros((LANES, 256), F32).at[MISC_GLR:MISC_GLR + GLA_RANK].set(gla_wg2[i])
        wg2h = wg2.astype(BF16)
        wg2l = (wg2 - wg2h.astype(F32)).astype(BF16)
        ikg = jnp.zeros((1, LANES), F32).at[0, :IDX_HD].set(idx_k_g[i])
        ikb = jnp.zeros((1, LANES), F32).at[0, :IDX_HD].set(idx_k_b[i])
        (gq, gk, gv, glog, gz, aq, ak, avt, iq, ik2, iwt, az) = _proj(
            h.reshape(R, D), w, wg2h, wg2l, gla_bg[i].reshape(1, 256), ikg, ikb,
            cos_t, sa_t, sb_t, tm)
        mix_gla = _gla(gq, gk, glog, gv, gz, gla_norm_g[i].reshape(1, GLA_DV), B, LP)
        mix_dsa = _dsa(ik2, ak, avt, aq, iq, iwt, az, B, nblk, topk)
        h = _out(mix_gla, mix_dsa.reshape(B, LP, DSA_W), h.reshape(B, LP, D),
                 w_out[i].astype(BF16), ln_g[i].reshape(1, D), ln_b[i].reshape(1, D),
                 B, LP, alpha, drop_filler=(i == depth - 1))
    return h.reshape(B, S, D)
```

```python
import functools
import math

import numpy as np
import jax
import jax.numpy as jnp
from jax import lax
from jax.experimental import pallas as pl
from jax.experimental.pallas import tpu as pltpu

F32 = jnp.float32
BF16 = jnp.bfloat16
I32 = jnp.int32

D_MODEL = 1024
N_META = 16
ROPE_THETA = 500000.0
LN_EPS = 1e-5
GLA_HEADS = 4
GLA_DK = 64
GLA_DV = 128
GLA_RANK = 16
GLA_TAU = 16.0
GLA_CHUNK = 64
GLA_W = GLA_HEADS * GLA_DV
DSA_HEADS = 8
DSA_KV_HEADS = 2
DSA_GROUP = DSA_HEADS // DSA_KV_HEADS
DSA_HD = 64
DSA_W = DSA_HEADS * DSA_HD
IDX_HEADS = 8
IDX_HD = 64
TOPK_MAX = 256
ROPE_DIM = DSA_HD // 4
ROPE_HALF = ROPE_DIM // 2
MIX_W = GLA_W + DSA_W

LANES = 128
SUBLANES = 8
ONES_ROWS = 16
PAD = LANES - N_META
INT_MIN = -(2 ** 31)
M_INIT = -(2.0 ** 100)
VMEM_LIMIT = 48 * 1024 * 1024
PROJ_ROWS = 512
OUT_ROWS = 512
OUT_SUB_ROWS = 256

O_GQ, O_GK, O_GV, O_GZ = 0, 256, 512, 1024
O_AQ, O_AK, O_AV, O_IQ, O_AZ, O_MISC = 1536, 2048, 2176, 2304, 2816, 3328
W_PACKED = 3456
MISC_GLR = IDX_HD
MISC_IW = IDX_HD + GLA_RANK


def _dot(a, b):
    return jnp.dot(a, b, preferred_element_type=F32)


def _dot_nt(a, b):
    return lax.dot_general(a, b, (((1,), (1,)), ((), ())), preferred_element_type=F32)


def _dot_tn(a, b):
    return lax.dot_general(a, b, (((0,), (0,)), ((), ())), preferred_element_type=F32)


def _layer_norm_rows(u, g, b):
    mu = jnp.mean(u, axis=-1, keepdims=True)
    d = u - mu
    var = jnp.mean(d * d, axis=-1, keepdims=True)
    return d * lax.rsqrt(var + LN_EPS) * g + b


def _embed_kernel(x_ref, meta_ref, g_ref, b_ref, o_ref):
    g, b = g_ref[...], b_ref[...]
    o_ref[0] = _layer_norm_rows(meta_ref[...], g, b)
    for c in range(x_ref.shape[1]):
        o_ref[c + 1] = _layer_norm_rows(x_ref[0, c], g, b)


def _embed(x, meta_pad, g, b):
    B, S, D = x.shape
    nblk = S // LANES + 1
    x4 = x.reshape(B, S // LANES, LANES, D)
    return pl.pallas_call(
        _embed_kernel,
        grid=(B,),
        in_specs=[
            pl.BlockSpec((1, S // LANES, LANES, D), lambda b_: (b_, 0, 0, 0)),
            pl.BlockSpec((LANES, D), lambda b_: (0, 0)),
            pl.BlockSpec((1, D), lambda b_: (0, 0)),
            pl.BlockSpec((1, D), lambda b_: (0, 0)),
        ],
        out_specs=pl.BlockSpec((nblk, LANES, D), lambda b_: (b_, 0, 0)),
        out_shape=jax.ShapeDtypeStruct((B * nblk, LANES, D), F32),
        compiler_params=pltpu.CompilerParams(vmem_limit_bytes=VMEM_LIMIT),
        name="embed_ln",
    )(x4, meta_pad, g, b)


def _rope_slab(x, c, sa, sb):
    return x * c + pltpu.roll(x, ROPE_HALF, 1) * sa + pltpu.roll(x, LANES - ROPE_HALF, 1) * sb


def _proj_kernel(h_ref, w_ref, wg2h_ref, wg2l_ref, bg_ref, ikg_ref, ikb_ref,
                 cos_ref, sa_ref, sb_ref,
                 gq_ref, gk_ref, gv_ref, glog_ref, gz_ref, aq_ref, ak_ref, avt_ref,
                 iq_ref, ik_ref, iwt_ref, az_ref, *, tm):
    hb = h_ref[...].astype(BF16)

    def seg(o, w):
        return _dot(hb, w_ref[:, o:o + w])

    c, sa, sb = cos_ref[...], sa_ref[...], sb_ref[...]
    misc = seg(O_MISC, LANES)
    kv = seg(O_AK, 2 * LANES)
    ak, av = kv[:, :LANES], kv[:, LANES:]
    aq = seg(O_AQ, 512)

    misc_t_scale = IDX_HEADS ** -0.5
    for r in range(tm // LANES):
        rows = slice(r * LANES, (r + 1) * LANES)
        avt_ref[r] = av[rows, :].T.astype(BF16)
        iwt_ref[r] = misc[rows, :].T[MISC_IW:MISC_IW + IDX_HEADS, :] * misc_t_scale
    ak_ref[...] = _rope_slab(ak, c, sa, sb).astype(BF16)
    iq = seg(O_IQ, 512)

    m_hi = misc.astype(BF16)
    m_lo = (misc - m_hi.astype(F32)).astype(BF16)
    xg = (_dot(m_hi, wg2h_ref[...]) + _dot(m_lo, wg2h_ref[...]) + _dot(m_hi, wg2l_ref[...])
          + bg_ref[...])
    gq_ref[...] = seg(O_GQ, 256)
    glog_ref[...] = (jnp.minimum(xg, 0.0) - jnp.log1p(jnp.exp(-jnp.abs(xg)))) * (1.0 / GLA_TAU)

    lane = lax.broadcasted_iota(I32, (1, LANES), 1)
    is_key = lane < IDX_HD
    mu = jnp.sum(jnp.where(is_key, misc, 0.0), axis=-1, keepdims=True) * (1.0 / IDX_HD)
    d = jnp.where(is_key, misc - mu, 0.0)
    var = jnp.sum(d * d, axis=-1, keepdims=True) * (1.0 / IDX_HD)
    ikn = d * lax.rsqrt(var + LN_EPS) * ikg_ref[...] + ikb_ref[...]
    ikr = _rope_slab(ikn, c, sa, sb)
    gk_ref[...] = seg(O_GK, 256)
    ik_ref[...] = jnp.where(is_key, ikr, pltpu.roll(ikr, IDX_HD, 1)).astype(BF16)

    gv = seg(O_GV, 512)
    for m in range(4):
        sl = slice(m * LANES, (m + 1) * LANES)
        aq_ref[:, sl] = _rope_slab(aq[:, sl], c, sa, sb).astype(BF16)
    gv_ref[...] = gv.astype(BF16)
    gz = seg(O_GZ, 512)
    for m in range(4):
        sl = slice(m * LANES, (m + 1) * LANES)
        iq_ref[:, sl] = _rope_slab(iq[:, sl], c, sa, sb).astype(BF16)
    gz_ref[...] = gz
    az_ref[...] = seg(O_AZ, 512)


def _proj(h2, w, wg2h, wg2l, bg, ikg, ikb, cos_t, sa_t, sb_t, tm):
    R = h2.shape[0]
    nt = R // tm
    nsub = tm // LANES
    row = lambda i: (i, 0)
    const = lambda i: (0, 0)
    table_tiles = cos_t.shape[0] // tm
    table_row = lambda i: (i % table_tiles, 0)
    kern = functools.partial(_proj_kernel, tm=tm)
    out_shape = (
        jax.ShapeDtypeStruct((R, 256), F32),
        jax.ShapeDtypeStruct((R, 256), F32),
        jax.ShapeDtypeStruct((R, 512), BF16),
        jax.ShapeDtypeStruct((R, 256), F32),
        jax.ShapeDtypeStruct((R, 512), F32),
        jax.ShapeDtypeStruct((R, 512), BF16),
        jax.ShapeDtypeStruct((R, LANES), BF16),
        jax.ShapeDtypeStruct((R // LANES, LANES, LANES), BF16),
        jax.ShapeDtypeStruct((R, 512), BF16),
        jax.ShapeDtypeStruct((R, LANES), BF16),
        jax.ShapeDtypeStruct((R // LANES, IDX_HEADS, LANES), F32),
        jax.ShapeDtypeStruct((R, 512), F32),
    )
    out_specs = (
        pl.BlockSpec((tm, 256), row),
        pl.BlockSpec((tm, 256), row),
        pl.BlockSpec((tm, 512), row),
        pl.BlockSpec((tm, 256), row),
        pl.BlockSpec((tm, 512), row),
        pl.BlockSpec((tm, 512), row),
        pl.BlockSpec((tm, LANES), row),
        pl.BlockSpec((nsub, LANES, LANES), lambda i: (i, 0, 0)),
        pl.BlockSpec((tm, 512), row),
        pl.BlockSpec((tm, LANES), row),
        pl.BlockSpec((nsub, IDX_HEADS, LANES), lambda i: (i, 0, 0)),
        pl.BlockSpec((tm, 512), row),
    )
    in_specs = [
        pl.BlockSpec((tm, D_MODEL), row),
        pl.BlockSpec((D_MODEL, W_PACKED), const),
        pl.BlockSpec((LANES, 256), const),
        pl.BlockSpec((LANES, 256), const),
        pl.BlockSpec((1, 256), const),
        pl.BlockSpec((1, LANES), const),
        pl.BlockSpec((1, LANES), const),
        pl.BlockSpec((tm, LANES), table_row),
        pl.BlockSpec((tm, LANES), table_row),
        pl.BlockSpec((tm, LANES), table_row),
    ]
    return pl.pallas_call(
        kern, grid=(nt,), in_specs=in_specs, out_specs=out_specs, out_shape=out_shape,
        compiler_params=pltpu.CompilerParams(vmem_limit_bytes=VMEM_LIMIT),
        name="in_proj",
    )(h2, w, wg2h, wg2l, bg, ikg, ikb, cos_t, sa_t, sb_t)


GLA_NB = 8
GLA_PIECES = 17
GLA_SAFE_EXP = 80.0


def _gla_kernel(q_ref, k_ref, g_ref, v_ref, z_ref, ng_ref, o_ref, st_ref, a_ref, tmp_ref,
                *, n_chunks):
    C = GLA_CHUNK
    piece = pl.program_id(1)

    @pl.when(piece == 0)
    def _():
        st_ref[...] = jnp.zeros_like(st_ref)

    lane = lax.broadcasted_iota(I32, (1, LANES), 1)
    head_lanes = (lane < GLA_DK, lane >= GLA_DK)
    rr = lax.broadcasted_iota(I32, (C, C), 0)
    cc = lax.broadcasted_iota(I32, (C, C), 1)
    causal = rr >= cc
    tril = causal.astype(BF16)
    row = lax.broadcasted_iota(I32, (C, 1), 0)
    ng = ng_ref[...]
    row0 = piece * (n_chunks * C)

    def body(ci, carry):
        r0 = pl.multiple_of(ci * C, C)
        rows = pl.ds(r0, C)
        valid = (row0 + r0 + row) >= PAD
        pairs = [(bi, pr) for bi in range(GLA_NB) for pr in range(GLA_HEADS // 2)]

        def cum_decay(bi, pr):
            g = g_ref[bi, rows, pr * LANES:(pr + 1) * LANES]
            g_hi = g.astype(BF16)
            g_lo = (g - g_hi.astype(F32)).astype(BF16)
            return _dot(tril, g_hi) + _dot(tril, g_lo)

        def load_qk(bi, pr):
            pl_ = slice(pr * LANES, (pr + 1) * LANES)
            return q_ref[bi, rows, pl_], jnp.where(valid, k_ref[bi, rows, pl_], 0.0)

        bs = [cum_decay(bi, pr) for bi, pr in pairs]
        ops = []
        span = jnp.zeros((C, LANES), F32)
        for (bi, pr), b in zip(pairs, bs):
            q, k = load_qk(bi, pr)
            b_mid = b[C // 2 - 1:C // 2, :]
            b_last = b[C - 1:C, :]
            d_mid = b - b_mid
            span = jnp.maximum(span, jnp.abs(d_mid))
            q_in = (q * jnp.exp(d_mid)).astype(BF16)
            k_in = (k * jnp.exp(-d_mid)).astype(BF16)
            q_st = (q * jnp.exp(b)).astype(BF16)
            k_st = (k * jnp.exp(b_last - b)).astype(BF16)
            ops.append((q_in, k_in, q_st, k_st, jnp.exp(b_last)))
        heads = [(bi, pr, h) for bi, pr in pairs for h in range(2)]
        zero = jnp.zeros((C, LANES), BF16)
        part = []
        for n, (bi, pr, h) in enumerate(heads):
            q_in, k_in, q_st, k_st, decay = ops[n // 2]
            hl = head_lanes[h]
            head = 2 * pr + h
            cols = slice(head * GLA_DV, (head + 1) * GLA_DV)
            v = jnp.where(valid, v_ref[bi, rows, cols], jnp.zeros((), BF16))
            st = st_ref[bi, head]
            a_ref[n] = jnp.where(causal, _dot_nt(jnp.where(hl, q_in, zero), k_in), 0.0)
            o_st = _dot_nt(jnp.where(hl, q_st, zero), st.astype(BF16))
            st_ref[bi, head] = st * decay + _dot_tn(v, jnp.where(hl, k_st, zero))
            part.append((o_st, v, cols))

        @pl.when(jnp.max(span) > GLA_SAFE_EXP)
        def _():
            key_lane = lax.broadcasted_iota(I32, (1, C), 1)
            for p, (bi, pr) in enumerate(pairs):
                q, k = load_qk(bi, pr)
                b = cum_decay(bi, pr)
                tmp_ref[0] = b
                tmp_ref[1] = k

                def key_column(s, acc, q=q, b=b):
                    b_s = tmp_ref[0, pl.ds(s, 1), :]
                    k_s = tmp_ref[1, pl.ds(s, 1), :]
                    w = q * k_s * jnp.exp(jnp.where(row >= s, b - b_s, -jnp.inf))
                    hit = (key_lane == s).astype(F32)
                    return tuple(
                        acc[h] + jnp.sum(jnp.where(head_lanes[h], w, 0.0), axis=-1, keepdims=True) * hit
                        for h in range(2))

                a0, a1 = lax.fori_loop(0, C, key_column, (jnp.zeros((C, C), F32),) * 2)
                a_ref[2 * p] = a0
                a_ref[2 * p + 1] = a1

        for n, ((bi, pr, h), (o_st, v, cols)) in enumerate(zip(heads, part)):
            o = _dot(a_ref[n].astype(BF16), v) + o_st
            on = o * lax.rsqrt(jnp.mean(o * o, axis=-1, keepdims=True) + LN_EPS) * ng
            z = z_ref[bi, rows, cols]
            o_ref[bi, rows, cols] = (on * (z * jax.nn.sigmoid(z))).astype(BF16)
        return carry

    lax.fori_loop(0, n_chunks, body, 0)


def _gla(gq, gk, glog, gv, gz, ng, B, LP):
    rows = LP // GLA_PIECES
    kern = functools.partial(_gla_kernel, n_chunks=rows // GLA_CHUNK)
    blk = lambda b_, p: (b_, p, 0)
    return pl.pallas_call(
        kern,
        grid=(B // GLA_NB, GLA_PIECES),
        in_specs=[
            pl.BlockSpec((GLA_NB, rows, 256), blk),
            pl.BlockSpec((GLA_NB, rows, 256), blk),
            pl.BlockSpec((GLA_NB, rows, 256), blk),
            pl.BlockSpec((GLA_NB, rows, GLA_W), blk),
            pl.BlockSpec((GLA_NB, rows, GLA_W), blk),
            pl.BlockSpec((1, GLA_DV), lambda b_, p: (0, 0)),
        ],
        out_specs=pl.BlockSpec((GLA_NB, rows, GLA_W), blk),
        out_shape=jax.ShapeDtypeStruct((B, LP, GLA_W), BF16),
        scratch_shapes=[
            pltpu.VMEM((GLA_NB, GLA_HEADS, GLA_DV, LANES), F32),
            pltpu.VMEM((GLA_NB * GLA_HEADS, GLA_CHUNK, GLA_CHUNK), F32),
            pltpu.VMEM((2, GLA_CHUNK, LANES), F32),
        ],
        compiler_params=pltpu.CompilerParams(vmem_limit_bytes=VMEM_LIMIT),
        name="gla",
    )(gq.reshape(B, LP, 256), gk.reshape(B, LP, 256), glog.reshape(B, LP, 256),
      gv.reshape(B, LP, GLA_W), gz.reshape(B, LP, GLA_W), ng)


DSA_NB = 2


def _dsa_kernel(ik_ref, ak_ref, avt_ref, aq_ref, iq_ref, iwt_ref, az_ref, o_ref,
                sc_ref, qz_ref, iqz_ref, w_ref, m_ref, acc_ref, plane_ref, thr_ref, cnt_ref,
                *, topk, nblk):
    j = pl.program_id(1)
    npair = (j + 2) // 2
    KP = 2 * LANES
    batch = range(DSA_NB)

    def pair_rows(ref, n, i, axis):
        kb1 = jnp.minimum(2 * i + 1, nblk - 1)
        return jnp.concatenate([ref[n, 2 * i], ref[n, kb1]], axis=axis)

    def pair_ds(i):
        return pl.ds(pl.multiple_of(i * KP, KP), KP)

    lane = lax.broadcasted_iota(I32, (1, LANES), 1)
    low = lane < DSA_HD
    zero_b = jnp.zeros((LANES, LANES), BF16)
    for n in batch:
        for m in range(4):
            sl = slice(m * LANES, (m + 1) * LANES)
            qs = aq_ref[n, 0, :, sl]
            qz_ref[n, m * LANES:(m + 1) * LANES, :] = jnp.where(low, qs, zero_b)
            qz_ref[n, (m + 4) * LANES:(m + 5) * LANES, :] = jnp.where(low, zero_b, qs)
            iqs = iq_ref[n, 0, :, sl]
            iqz_ref[n, (2 * m) * LANES:(2 * m + 1) * LANES, :] = jnp.where(low, iqs, zero_b)
            iqz_ref[n, (2 * m + 1) * LANES:(2 * m + 2) * LANES, :] = jnp.where(low, zero_b, iqs)
        for h in range(IDX_HEADS):
            w_ref[n, :, h * LANES:(h + 1) * LANES] = iwt_ref[n, 0, h:h + 1, :]
    t_pos = j * LANES + lane
    s_loc = lax.broadcasted_iota(I32, (KP, 1), 0)

    @pl.when((pl.program_id(0) == 0) & (j == 0))
    def _():
        plane_ref[...] = jnp.zeros_like(plane_ref)

    def shift_const(x, n):
        return jnp.full(x.shape, n, I32)

    def store_planes(n, i, scores):
        bits = lax.bitcast_convert_type(scores, I32)
        u = bits ^ (lax.shift_right_arithmetic(bits, shift_const(bits, 31)) | jnp.int32(INT_MIN))
        a = [u[SUBLANES * v:SUBLANES * (v + 1), :] for v in range(32)]
        step, mask = 16, 0x0000FFFF
        while step:
            for k in range(32):
                if not k & step:
                    t = (a[k] ^ lax.shift_right_logical(a[k + step], shift_const(a[k], step))) & jnp.int32(mask)
                    a[k] = a[k] ^ t
                    a[k + step] = a[k + step] ^ lax.shift_left(t, shift_const(t, step))
            step >>= 1
            mask = (mask ^ (mask << step)) & 0xFFFFFFFF
        for r in range(32):
            plane_ref[n, i, r] = a[r]

    def score_pairs(pairs):
        work = [(n, i) for i in pairs for n in batch]
        dots = [_dot_nt(pair_rows(ik_ref, n, i, 0), iqz_ref[n]) for n, i in work]
        for (n, i), d in zip(work, dots):
            d = jnp.maximum(d, 0.0) * w_ref[n]
            acc = d[:, :LANES]
            for h in range(1, IDX_HEADS):
                acc = acc + d[:, h * LANES:(h + 1) * LANES]
            s_pos = i * KP + s_loc
            valid = (s_pos <= t_pos) & ((s_pos >= PAD) | (t_pos < PAD))
            scores = jnp.where(valid, acc, -jnp.inf)
            sc_ref[n, pair_ds(i), :] = scores
            store_planes(n, i, scores)

    def two_pair_trips(body):
        odd = npair % 2 == 1

        @pl.when(odd & (npair == 1))
        def _():
            body((0,))

        @pl.when(odd & (npair > 1))
        def _():
            body((0, 1, 2))

        first = jnp.where(odd, jnp.where(npair > 1, 3, 1), 0)

        def trip(u, carry):
            body((first + 2 * u, first + 2 * u + 1))
            return carry

        lax.fori_loop(0, (npair - first) // 2, trip, 0)

    two_pair_trips(score_pairs)

    def count(preds):
        flat = [(n, p) for n in batch for p in preds[n]]

        def cnt_block(i, cnts):
            xs = [sc_ref[n, pair_ds(i), :] for n in batch]
            out = []
            for (n, pred), cnt in zip(flat, cnts):
                hit = pred(xs[n]).astype(I32)
                parts = [hit[r:r + SUBLANES, :] for r in range(0, KP, SUBLANES)]
                while len(parts) > 1:
                    parts = [parts[k] + parts[k + 1] for k in range(0, len(parts), 2)]
                out.append(cnt + parts[0])
            return tuple(out)

        zero = jnp.zeros((SUBLANES, LANES), I32)
        cnts = lax.fori_loop(0, npair, cnt_block, (zero,) * len(flat))
        sums = [jnp.sum(c, axis=0, keepdims=True) for c in cnts]
        per = len(flat) // DSA_NB
        return [tuple(sums[n * per:(n + 1) * per]) for n in batch]

    select_all = jnp.float32(jnp.finfo(F32).min)

    ones_v = jnp.full((SUBLANES, LANES), -1, I32)
    zeros_v = jnp.zeros((SUBLANES, LANES), I32)

    def radix_select(n_pairs):
        alive0 = tuple(tuple(jnp.where(i < npair, ones_v, zeros_v) for i in range(n_pairs)) for n in batch)

        def radix_step(t, carry):
            alive_all, k_all, u_all = carry
            out_alive, out_k, out_u = [], [], []
            counts = []
            for n in batch:
                alive = alive_all[n]
                hi = [plane_ref[n, i, 2 * t] for i in range(n_pairs)]
                lo = [plane_ref[n, i, 2 * t + 1] for i in range(n_pairs)]
                c11 = c1x = c01 = zeros_v
                for i in range(n_pairs):
                    a1 = alive[i] & hi[i]
                    a0 = alive[i] ^ a1
                    c11 = c11 + lax.population_count(a1 & lo[i])
                    c1x = c1x + lax.population_count(a1)
                    c01 = c01 + lax.population_count(a0 & lo[i])
                counts.append((hi, lo, c11, c1x, c01))
            for n in batch:
                hi, lo, c11, c1x, c01 = counts[n]
                alive, k_left = alive_all[n], k_all[n]
                c11 = jnp.sum(c11, axis=0, keepdims=True)
                c1x = jnp.sum(c1x, axis=0, keepdims=True)
                c01 = jnp.sum(c01, axis=0, keepdims=True)
                in3 = k_left <= c11
                in32 = k_left <= c1x
                in321 = k_left <= c1x + c01
                bit_hi = in32
                bit_lo = in3 | (~in32 & in321)
                k_left = jnp.where(in3, k_left,
                                   jnp.where(in32, k_left - c11,
                                             jnp.where(in321, k_left - c1x, k_left - c1x - c01)))
                f_hi = jnp.where(bit_hi, jnp.int32(0), jnp.int32(-1))
                f_lo = jnp.where(bit_lo, jnp.int32(0), jnp.int32(-1))
                out_alive.append(tuple(alive[i] & (hi[i] ^ f_hi) & (lo[i] ^ f_lo) for i in range(n_pairs)))
                digit = jnp.where(bit_hi, jnp.int32(2), jnp.int32(0)) | jnp.where(bit_lo, jnp.int32(1), jnp.int32(0))
                out_k.append(k_left)
                out_u.append(u_all[n] | lax.shift_left(digit, jnp.full(digit.shape, 30 - 2 * t, I32)))
            return tuple(out_alive), tuple(out_k), tuple(out_u)

        _, _, u_thr = lax.fori_loop(
            0, 16, radix_step,
            (alive0, (jnp.full((1, LANES), topk, I32),) * DSA_NB, (jnp.zeros((1, LANES), I32),) * DSA_NB))
        for n in batch:
            thr_bits = jnp.where(u_thr[n] < 0, u_thr[n] ^ jnp.int32(INT_MIN), ~u_thr[n])
            tf = lax.bitcast_convert_type(thr_bits, F32)
            thr_ref[n] = jnp.where(tf == -jnp.inf, select_all, tf)

    n_pairs_max = plane_ref.shape[1]
    n_pairs_few = (n_pairs_max + 1) // 2

    @pl.when(npair <= n_pairs_few)
    def _():
        radix_select(n_pairs_few)

    @pl.when(npair > n_pairs_few)
    def _():
        radix_select(n_pairs_max)

    thr_fast = [thr_ref[n] for n in batch]

    n_open = jnp.int32(0)
    for n, (n_ge,) in zip(batch, count([((lambda x, n=n: x >= thr_fast[n]),) for n in batch])):
        is_all = thr_fast[n] == select_all
        cnt_ref[n] = jnp.where(is_all, 0, n_ge)
        n_open = n_open + jnp.sum(jnp.where(is_all | (n_ge == topk), 0, 1))

    @pl.when(n_open > 0)
    def _():
        n_bad = jnp.int32(0)
        for n, (n_gt,) in zip(batch, count([((lambda x, n=n: x > thr_fast[n]),) for n in batch])):
            good = (thr_fast[n] == select_all) | ((n_gt < topk) & (cnt_ref[n] >= topk))
            n_bad = n_bad + jnp.sum(jnp.where(good, 0, 1))

        @pl.when(n_bad > 0)
        def _():
            def key_to_float(key):
                bits = jnp.where(key < 0, key ^ jnp.int32(0x7FFFFFFF), key)
                return lax.bitcast_convert_type(bits, F32)

            def count_ge(cand_keys):
                cands = [key_to_float(c) for c in cand_keys]
                return [c[0] for c in count([((lambda x, n=n: x >= cands[n]),) for n in batch])]

            c0 = count_ge([jnp.zeros((1, LANES), I32)] * DSA_NB)
            base = tuple(jnp.where(c >= topk, jnp.int32(0), jnp.int32(INT_MIN)) for c in c0)
            base_cnt = tuple(jnp.where(c >= topk, c, 0) for c in c0)

            def bit_step(i, carry):
                base, base_cnt = carry
                cand = [b | jnp.left_shift(jnp.int32(1), 30 - i) for b in base]
                cs = count_ge(cand)
                ok = [c >= topk for c in cs]
                return (tuple(jnp.where(ok[n], cand[n], base[n]) for n in batch),
                        tuple(jnp.where(ok[n], cs[n], base_cnt[n]) for n in batch))

            base, base_cnt = lax.fori_loop(0, 31, bit_step, (base, base_cnt))
            for n in batch:
                thr_ref[n] = jnp.where(base[n] == INT_MIN, select_all, key_to_float(base[n]))
                cnt_ref[n] = base_cnt[n]

    thr = [thr_ref[n] for n in batch]
    base_cnt = [cnt_ref[n] for n in batch]

    most = base_cnt[0]
    for n in batch[1:]:
        most = jnp.maximum(most, base_cnt[n])

    @pl.when(jnp.max(most) > topk)
    def _():
        n_gt = count([((lambda x, n=n: x > thr[n]),) for n in batch])
        rr = lax.broadcasted_iota(I32, (KP, KP), 0)
        cc = lax.broadcasted_iota(I32, (KP, KP), 1)
        tri = (rr >= cc).astype(BF16)
        for n in batch:
            need = (topk - n_gt[n][0]).astype(F32)

            def strike(i, seen, n=n, need=need):
                x = sc_ref[n, pair_ds(i), :]
                eq = x == thr[n]
                rank = _dot(tri, jnp.where(eq, 1.0, 0.0).astype(BF16)) + seen
                sc_ref[n, pair_ds(i), :] = jnp.where(eq & (rank > need), -jnp.inf, x)
                return rank[KP - 1:KP, :]

            lax.fori_loop(0, npair, strike, jnp.zeros((1, LANES), F32))

    m_ref[...] = jnp.full_like(m_ref, M_INIT)
    acc_ref[...] = jnp.zeros_like(acc_ref)
    GW = DSA_GROUP * LANES
    ones_rows = jnp.ones((ONES_ROWS, KP), BF16)

    def attn_pairs(pairs):
        work = [(n, i) for i in pairs for n in batch]
        logits = [_dot_nt(pair_rows(ak_ref, n, i, 0), qz_ref[n]).astype(BF16) for n, i in work]
        for (n, i), s in zip(work, logits):
            bias = jnp.where(sc_ref[n, pair_ds(i), :] >= thr[n], 0.0, -jnp.inf).astype(BF16)
            sb = s + jnp.concatenate([bias] * DSA_HEADS, axis=1)
            m_old = m_ref[n]
            m_new = jnp.maximum(m_old, jnp.max(sb, axis=0, keepdims=True).astype(F32))
            alpha = jnp.exp(m_old - m_new)
            pb = jnp.exp(sb - m_new.astype(BF16))
            m_ref[n] = m_new
            vt = pair_rows(avt_ref, n, i, 1)
            for g in range(DSA_KV_HEADS):
                cols = slice(g * GW, (g + 1) * GW)
                vg = jnp.concatenate([vt[g * DSA_HD:(g + 1) * DSA_HD, :], ones_rows], axis=0)
                acc_ref[n, g] = alpha[:, cols] * acc_ref[n, g] + _dot(vg, pb[:, cols])

    two_pair_trips(attn_pairs)

    for n in batch:
        heads = []
        for h in range(DSA_HEADS):
            a = acc_ref[n, h // DSA_GROUP][:, (h % DSA_GROUP) * LANES:(h % DSA_GROUP + 1) * LANES]
            heads.append(a[:DSA_HD, :] / a[DSA_HD:DSA_HD + 1, :])
        ot = jnp.concatenate(heads, axis=0)
        z = az_ref[n, 0]
        o_ref[n, 0] = (ot.T * (z * jax.nn.sigmoid(z))).astype(BF16)


def _dsa(ik2, ak, avt, aq, iq, iwt, az, B, nblk, topk):
    kern = functools.partial(_dsa_kernel, topk=topk, nblk=nblk)
    nb = DSA_NB
    whole = lambda b_, j: (b_, 0, 0, 0)
    qblk = lambda b_, j: (b_, j, 0, 0)
    by_block = lambda a: a.reshape(B, nblk, -1, a.shape[-1])
    out = pl.pallas_call(
        kern,
        grid=(B // nb, nblk),
        in_specs=[
            pl.BlockSpec((nb, nblk, LANES, LANES), whole),
            pl.BlockSpec((nb, nblk, LANES, LANES), whole),
            pl.BlockSpec((nb, nblk, LANES, LANES), whole),
            pl.BlockSpec((nb, 1, LANES, DSA_W), qblk),
            pl.BlockSpec((nb, 1, LANES, IDX_HEADS * IDX_HD), qblk),
            pl.BlockSpec((nb, 1, IDX_HEADS, LANES), qblk),
            pl.BlockSpec((nb, 1, LANES, DSA_W), qblk),
        ],
        out_specs=pl.BlockSpec((nb, 1, LANES, DSA_W), qblk),
        out_shape=jax.ShapeDtypeStruct((B, nblk, LANES, DSA_W), BF16),
        scratch_shapes=[
            pltpu.VMEM((nb, (nblk + 1) // 2 * 2 * LANES, LANES), F32),
            pltpu.VMEM((nb, DSA_HEADS * LANES, LANES), BF16),
            pltpu.VMEM((nb, IDX_HEADS * LANES, LANES), BF16),
            pltpu.VMEM((nb, 1, IDX_HEADS * LANES), F32),
            pltpu.VMEM((nb, 1, DSA_HEADS * LANES), F32),
            pltpu.VMEM((nb, DSA_KV_HEADS, DSA_HD + ONES_ROWS, DSA_GROUP * LANES), F32),
            pltpu.VMEM((nb, (nblk + 1) // 2, 32, SUBLANES, LANES), I32),
            pltpu.VMEM((nb, 1, LANES), F32),
            pltpu.VMEM((nb, 1, LANES), I32),
        ],
        compiler_params=pltpu.CompilerParams(vmem_limit_bytes=VMEM_LIMIT),
        name="dsa",
    )(by_block(ik2), by_block(ak), by_block(avt), by_block(aq), by_block(iq), by_block(iwt), by_block(az))
    return out.reshape(B * nblk, LANES, DSA_W)


def _out_kernel(mg_ref, md_ref, h_ref, w_ref, g_ref, b_ref, o_ref, *, alpha):
    n_sub = OUT_ROWS // OUT_SUB_ROWS
    subs = [slice(r * OUT_SUB_ROWS, (r + 1) * OUT_SUB_ROWS) for r in range(n_sub)]
    ys = [_dot(mg_ref[0, s, :], w_ref[:GLA_W, :]) + _dot(md_ref[0, s, :], w_ref[GLA_W:, :]) for s in subs]
    for s, y in zip(subs, ys):
        o_ref[0, s, :] = _layer_norm_rows(alpha * h_ref[0, s, :] + y, g_ref[...], b_ref[...])


def _out(mg, md, h, w, g, b, B, LP, alpha, drop_filler):
    rows = OUT_ROWS
    if drop_filler:
        S = LP - LANES
        grid, nout = (B, S // rows), S
        src_blk = lambda width: (pl.Element(1), pl.Element(rows), pl.Element(width))
        src = lambda b_, j: (b_, pl.multiple_of(LANES + j * rows, LANES), 0)
    else:
        grid, nout = (B * LP // rows,), LP
        src_blk = lambda width: (1, rows, width)
        mg, md, h = (a.reshape(1, B * LP, a.shape[-1]) for a in (mg, md, h))
        src = lambda i: (0, i, 0)
    dst = (lambda b_, j: (b_, j, 0)) if drop_filler else (lambda i: (0, i, 0))
    const = (lambda b_, j: (0, 0)) if drop_filler else (lambda i: (0, 0))
    out_shape = (B, nout, D_MODEL) if drop_filler else (1, B * LP, D_MODEL)
    kern = functools.partial(_out_kernel, alpha=alpha)
    return pl.pallas_call(
        kern,
        grid=grid,
        in_specs=[
            pl.BlockSpec(src_blk(GLA_W), src),
            pl.BlockSpec(src_blk(DSA_W), src),
            pl.BlockSpec(src_blk(D_MODEL), src),
            pl.BlockSpec((MIX_W, D_MODEL), const),
            pl.BlockSpec((1, D_MODEL), const),
            pl.BlockSpec((1, D_MODEL), const),
        ],
        out_specs=pl.BlockSpec((1, rows, D_MODEL), dst),
        out_shape=jax.ShapeDtypeStruct(out_shape, F32),
        name="out_proj_ln",
    )(mg, md, h, w, g, b)


def _pack_w_in(w):
    splits = (256, 256, 512, GLA_RANK, 512, 512, 128, 128, 512, IDX_HD, IDX_HEADS, 512)
    offs = np.cumsum((0,) + splits)
    gq, gk, gv, glr, gz, aq, ak, av, iq, ik, iw, az = [w[:, offs[i]:offs[i + 1]] for i in range(12)]

    def perm_heads(a):
        a = a.reshape(D_MODEL, DSA_KV_HEADS, DSA_GROUP, DSA_HD)
        return a.transpose(0, 2, 1, 3).reshape(D_MODEL, DSA_W)

    pad = jnp.zeros((D_MODEL, LANES - IDX_HD - GLA_RANK - IDX_HEADS), w.dtype)
    pieces = [gq * (GLA_DK ** -0.5), gk, gv, gz, perm_heads(aq) * (DSA_HD ** -0.5), ak, av,
              iq * (IDX_HD ** -0.5), az, ik, glr, iw, pad]
    return jnp.concatenate([p.astype(BF16) for p in pieces], axis=1)


def _rope_lane_tables(reps, LP):
    inv = ROPE_THETA ** (-jnp.arange(0, ROPE_DIM, 2, dtype=F32) / ROPE_DIM)
    pos = (jnp.arange(LP, dtype=F32) - PAD)[:, None]
    ang = pos * inv[None, :]
    cos, sin = jnp.cos(ang), jnp.sin(ang)
    ones = jnp.ones((LP, DSA_HD - ROPE_DIM), F32)
    zeros = jnp.zeros((LP, DSA_HD - ROPE_DIM), F32)
    zh = jnp.zeros((LP, ROPE_HALF), F32)
    c = jnp.concatenate([cos, cos, ones], axis=1)
    sa = jnp.concatenate([zh, sin, zeros], axis=1)
    sb = jnp.concatenate([-sin, zh, zeros], axis=1)
    tile = lambda t: jnp.tile(jnp.concatenate([t, t], axis=1), (reps, 1))
    return tile(c), tile(sa), tile(sb)


def kernel(x, meta_tokens, ln_in_g, ln_in_b, w_in, gla_wg2, gla_bg, gla_norm_g, idx_k_g, idx_k_b,
           w_out, ln_g, ln_b):
    B, S, D = x.shape
    depth = w_in.shape[0]
    nblk = S // LANES + 1
    LP = nblk * LANES
    R = B * LP
    topk = min(TOPK_MAX, S // 4)
    alpha = (2.0 * depth) ** 0.25
    tm = PROJ_ROWS

    meta_pad = jnp.concatenate([jnp.zeros((PAD, D), x.dtype), meta_tokens.astype(x.dtype)], axis=0)
    h = _embed(x, meta_pad, ln_in_g.reshape(1, D), ln_in_b.reshape(1, D))
    cos_t, sa_t, sb_t = _rope_lane_tables(tm // math.gcd(tm, LP), LP)

    for i in range(depth):
        w = _pack_w_in(w_in[i])
        wg2 = jnp.zeros((LANES, 256), F32).at[MISC_GLR:MISC_GLR + GLA_RANK].set(gla_wg2[i])
        wg2h = wg2.astype(BF16)
        wg2l = (wg2 - wg2h.astype(F32)).astype(BF16)
        ikg = jnp.zeros((1, LANES), F32).at[0, :IDX_HD].set(idx_k_g[i])
        ikb = jnp.zeros((1, LANES), F32).at[0, :IDX_HD].set(idx_k_b[i])
        (gq, gk, gv, glog, gz, aq, ak, avt, iq, ik2, iwt, az) = _proj(
            h.reshape(R, D), w, wg2h, wg2l, gla_bg[i].reshape(1, 256), ikg, ikb,
            cos_t, sa_t, sb_t, tm)
        mix_gla = _gla(gq, gk, glog, gv, gz, gla_norm_g[i].reshape(1, GLA_DV), B, LP)
        mix_dsa = _dsa(ik2, ak, avt, aq, iq, iwt, az, B, nblk, topk)
        h = _out(mix_gla, mix_dsa.reshape(B, LP, DSA_W), h.reshape(B, LP, D),
                 w_out[i].astype(BF16), ln_g[i].reshape(1, D), ln_b[i].reshape(1, D),
                 B, LP, alpha, drop_filler=(i == depth - 1))
    return h.reshape(B, S, D)
```

```python
import functools
import math

import numpy as np
import jax
import jax.numpy as jnp
from jax import lax
from jax.experimental import pallas as pl
from jax.experimental.pallas import tpu as pltpu

F32 = jnp.float32
BF16 = jnp.bfloat16
I32 = jnp.int32

D_MODEL = 1024
N_META = 16
ROPE_THETA = 500000.0
LN_EPS = 1e-5
GLA_HEADS = 4
GLA_DK = 64
GLA_DV = 128
GLA_RANK = 16
GLA_TAU = 16.0
GLA_CHUNK = 64
GLA_W = GLA_HEADS * GLA_DV
DSA_HEADS = 8
DSA_KV_HEADS = 2
DSA_GROUP = DSA_HEADS // DSA_KV_HEADS
DSA_HD = 64
DSA_W = DSA_HEADS * DSA_HD
IDX_HEADS = 8
IDX_HD = 64
TOPK_MAX = 256
ROPE_DIM = DSA_HD // 4
ROPE_HALF = ROPE_DIM // 2
MIX_W = GLA_W + DSA_W

LANES = 128
SUBLANES = 8
ONES_ROWS = 16
PAD = LANES - N_META
INT_MIN = -(2 ** 31)
M_INIT = -(2.0 ** 100)
VMEM_LIMIT = 48 * 1024 * 1024
PROJ_ROWS = 512
OUT_ROWS = 512
OUT_SUB_ROWS = 256

O_GQ, O_GK, O_GV, O_GZ = 0, 256, 512, 1024
O_AQ, O_AK, O_AV, O_IQ, O_AZ, O_MISC = 1536, 2048, 2176, 2304, 2816, 3328
W_PACKED = 3456
MISC_GLR = IDX_HD
MISC_IW = IDX_HD + GLA_RANK


def _dot(a, b):
    return jnp.dot(a, b, preferred_element_type=F32)


def _dot_nt(a, b):
    return lax.dot_general(a, b, (((1,), (1,)), ((), ())), preferred_element_type=F32)


def _dot_tn(a, b):
    return lax.dot_general(a, b, (((0,), (0,)), ((), ())), preferred_element_type=F32)


def _layer_norm_rows(u, g, b):
    mu = jnp.mean(u, axis=-1, keepdims=True)
    d = u - mu
    var = jnp.mean(d * d, axis=-1, keepdims=True)
    return d * lax.rsqrt(var + LN_EPS) * g + b


def _embed_kernel(x_ref, meta_ref, g_ref, b_ref, o_ref):
    g, b = g_ref[...], b_ref[...]
    o_ref[0] = _layer_norm_rows(meta_ref[...], g, b)
    for c in range(x_ref.shape[1]):
        o_ref[c + 1] = _layer_norm_rows(x_ref[0, c], g, b)


def _embed(x, meta_pad, g, b):
    B, S, D = x.shape
    nblk = S // LANES + 1
    x4 = x.reshape(B, S // LANES, LANES, D)
    return pl.pallas_call(
        _embed_kernel,
        grid=(B,),
        in_specs=[
            pl.BlockSpec((1, S // LANES, LANES, D), lambda b_: (b_, 0, 0, 0)),
            pl.BlockSpec((LANES, D), lambda b_: (0, 0)),
            pl.BlockSpec((1, D), lambda b_: (0, 0)),
            pl.BlockSpec((1, D), lambda b_: (0, 0)),
        ],
        out_specs=pl.BlockSpec((nblk, LANES, D), lambda b_: (b_, 0, 0)),
        out_shape=jax.ShapeDtypeStruct((B * nblk, LANES, D), F32),
        compiler_params=pltpu.CompilerParams(vmem_limit_bytes=VMEM_LIMIT),
        name="embed_ln",
    )(x4, meta_pad, g, b)


def _rope_slab(x, c, sa, sb):
    return x * c + pltpu.roll(x, ROPE_HALF, 1) * sa + pltpu.roll(x, LANES - ROPE_HALF, 1) * sb


def _proj_kernel(h_ref, w_ref, wg2h_ref, wg2l_ref, bg_ref, ikg_ref, ikb_ref,
                 cos_ref, sa_ref, sb_ref,
                 gq_ref, gk_ref, gv_ref, glog_ref, gz_ref, aq_ref, ak_ref, avt_ref,
                 iq_ref, ik_ref, iwt_ref, az_ref, *, tm):
    hb = h_ref[...].astype(BF16)

    def seg(o, w):
        return _dot(hb, w_ref[:, o:o + w])

    c, sa, sb = cos_ref[...], sa_ref[...], sb_ref[...]
    misc = seg(O_MISC, LANES)
    kv = seg(O_AK, 2 * LANES)
    ak, av = kv[:, :LANES], kv[:, LANES:]
    aq = seg(O_AQ, 512)

    misc_t_scale = IDX_HEADS ** -0.5
    for r in range(tm // LANES):
        rows = slice(r * LANES, (r + 1) * LANES)
        avt_ref[r] = av[rows, :].T.astype(BF16)
        iwt_ref[r] = misc[rows, :].T[MISC_IW:MISC_IW + IDX_HEADS, :] * misc_t_scale
    ak_ref[...] = _rope_slab(ak, c, sa, sb).astype(BF16)
    iq = seg(O_IQ, 512)

    m_hi = misc.astype(BF16)
    m_lo = (misc - m_hi.astype(F32)).astype(BF16)
    xg = (_dot(m_hi, wg2h_ref[...]) + _dot(m_lo, wg2h_ref[...]) + _dot(m_hi, wg2l_ref[...])
          + bg_ref[...])
    gq_ref[...] = seg(O_GQ, 256)
    glog_ref[...] = (jnp.minimum(xg, 0.0) - jnp.log1p(jnp.exp(-jnp.abs(xg)))) * (1.0 / GLA_TAU)

    lane = lax.broadcasted_iota(I32, (1, LANES), 1)
    is_key = lane < IDX_HD
    mu = jnp.sum(jnp.where(is_key, misc, 0.0), axis=-1, keepdims=True) * (1.0 / IDX_HD)
    d = jnp.where(is_key, misc - mu, 0.0)
    var = jnp.sum(d * d, axis=-1, keepdims=True) * (1.0 / IDX_HD)
    ikn = d * lax.rsqrt(var + LN_EPS) * ikg_ref[...] + ikb_ref[...]
    ikr = _rope_slab(ikn, c, sa, sb)
    gk_ref[...] = seg(O_GK, 256)
    ik_ref[...] = jnp.where(is_key, ikr, pltpu.roll(ikr, IDX_HD, 1)).astype(BF16)

    gv = seg(O_GV, 512)
    for m in range(4):
        sl = slice(m * LANES, (m + 1) * LANES)
        aq_ref[:, sl] = _rope_slab(aq[:, sl], c, sa, sb).astype(BF16)
    gv_ref[...] = gv.astype(BF16)
    gz = seg(O_GZ, 512)
    for m in range(4):
        sl = slice(m * LANES, (m + 1) * LANES)
        iq_ref[:, sl] = _rope_slab(iq[:, sl], c, sa, sb).astype(BF16)
    gz_ref[...] = gz
    az_ref[...] = seg(O_AZ, 512)


def _proj(h2, w, wg2h, wg2l, bg, ikg, ikb, cos_t, sa_t, sb_t, tm):
    R = h2.shape[0]
    nt = R // tm
    nsub = tm // LANES
    row = lambda i: (i, 0)
    const = lambda i: (0, 0)
    table_tiles = cos_t.shape[0] // tm
    table_row = lambda i: (i % table_tiles, 0)
    kern = functools.partial(_proj_kernel, tm=tm)
    out_shape = (
        jax.ShapeDtypeStruct((R, 256), F32),
        jax.ShapeDtypeStruct((R, 256), F32),
        jax.ShapeDtypeStruct((R, 512), BF16),
        jax.ShapeDtypeStruct((R, 256), F32),
        jax.ShapeDtypeStruct((R, 512), F32),
        jax.ShapeDtypeStruct((R, 512), BF16),
        jax.ShapeDtypeStruct((R, LANES), BF16),
        jax.ShapeDtypeStruct((R // LANES, LANES, LANES), BF16),
        jax.ShapeDtypeStruct((R, 512), BF16),
        jax.ShapeDtypeStruct((R, LANES), BF16),
        jax.ShapeDtypeStruct((R // LANES, IDX_HEADS, LANES), F32),
        jax.ShapeDtypeStruct((R, 512), F32),
    )
    out_specs = (
        pl.BlockSpec((tm, 256), row),
        pl.BlockSpec((tm, 256), row),
        pl.BlockSpec((tm, 512), row),
        pl.BlockSpec((tm, 256), row),
        pl.BlockSpec((tm, 512), row),
        pl.BlockSpec((tm, 512), row),
        pl.BlockSpec((tm, LANES), row),
        pl.BlockSpec((nsub, LANES, LANES), lambda i: (i, 0, 0)),
        pl.BlockSpec((tm, 512), row),
        pl.BlockSpec((tm, LANES), row),
        pl.BlockSpec((nsub, IDX_HEADS, LANES), lambda i: (i, 0, 0)),
        pl.BlockSpec((tm, 512), row),
    )
    in_specs = [
        pl.BlockSpec((tm, D_MODEL), row),
        pl.BlockSpec((D_MODEL, W_PACKED), const),
        pl.BlockSpec((LANES, 256), const),
        pl.BlockSpec((LANES, 256), const),
        pl.BlockSpec((1, 256), const),
        pl.BlockSpec((1, LANES), const),
        pl.BlockSpec((1, LANES), const),
        pl.BlockSpec((tm, LANES), table_row),
        pl.BlockSpec((tm, LANES), table_row),
        pl.BlockSpec((tm, LANES), table_row),
    ]
    return pl.pallas_call(
        kern, grid=(nt,), in_specs=in_specs, out_specs=out_specs, out_shape=out_shape,
        compiler_params=pltpu.CompilerParams(vmem_limit_bytes=VMEM_LIMIT),
        name="in_proj",
    )(h2, w, wg2h, wg2l, bg, ikg, ikb, cos_t, sa_t, sb_t)


GLA_NB = 8
GLA_PIECES = 17
GLA_SAFE_EXP = 80.0


def _gla_kernel(q_ref, k_ref, g_ref, v_ref, z_ref, ng_ref, o_ref, st_ref, a_ref, tmp_ref,
                *, n_chunks):
    C = GLA_CHUNK
    piece = pl.program_id(1)

    @pl.when(piece == 0)
    def _():
        st_ref[...] = jnp.zeros_like(st_ref)

    lane = lax.broadcasted_iota(I32, (1, LANES), 1)
    head_lanes = (lane < GLA_DK, lane >= GLA_DK)
    rr = lax.broadcasted_iota(I32, (C, C), 0)
    cc = lax.broadcasted_iota(I32, (C, C), 1)
    causal = rr >= cc
    tril = causal.astype(BF16)
    row = lax.broadcasted_iota(I32, (C, 1), 0)
    ng = ng_ref[...]
    row0 = piece * (n_chunks * C)

    def body(ci, carry):
        r0 = pl.multiple_of(ci * C, C)
        rows = pl.ds(r0, C)
        valid = (row0 + r0 + row) >= PAD
        pairs = [(bi, pr) for bi in range(GLA_NB) for pr in range(GLA_HEADS // 2)]

        def cum_decay(bi, pr):
            g = g_ref[bi, rows, pr * LANES:(pr + 1) * LANES]
            g_hi = g.astype(BF16)
            g_lo = (g - g_hi.astype(F32)).astype(BF16)
            return _dot(tril, g_hi) + _dot(tril, g_lo)

        def load_qk(bi, pr):
            pl_ = slice(pr * LANES, (pr + 1) * LANES)
            return q_ref[bi, rows, pl_], jnp.where(valid, k_ref[bi, rows, pl_], 0.0)

        bs = [cum_decay(bi, pr) for bi, pr in pairs]
        ops = []
        span = jnp.zeros((C, LANES), F32)
        for (bi, pr), b in zip(pairs, bs):
            q, k = load_qk(bi, pr)
            b_mid = b[C // 2 - 1:C // 2, :]
            b_last = b[C - 1:C, :]
            d_mid = b - b_mid
            span = jnp.maximum(span, jnp.abs(d_mid))
            q_in = (q * jnp.exp(d_mid)).astype(BF16)
            k_in = (k * jnp.exp(-d_mid)).astype(BF16)
            q_st = (q * jnp.exp(b)).astype(BF16)
            k_st = (k * jnp.exp(b_last - b)).astype(BF16)
            ops.append((q_in, k_in, q_st, k_st, jnp.exp(b_last)))
        heads = [(bi, pr, h) for bi, pr in pairs for h in range(2)]
        zero = jnp.zeros((C, LANES), BF16)
        part = []
        for n, (bi, pr, h) in enumerate(heads):
            q_in, k_in, q_st, k_st, decay = ops[n // 2]
            hl = head_lanes[h]
            head = 2 * pr + h
            cols = slice(head * GLA_DV, (head + 1) * GLA_DV)
            v = jnp.where(valid, v_ref[bi, rows, cols], jnp.zeros((), BF16))
            st = st_ref[bi, head]
            a_ref[n] = jnp.where(causal, _dot_nt(jnp.where(hl, q_in, zero), k_in), 0.0)
            o_st = _dot_nt(jnp.where(hl, q_st, zero), st.astype(BF16))
            st_ref[bi, head] = st * decay + _dot_tn(v, jnp.where(hl, k_st, zero))
            part.append((o_st, v, cols))

        @pl.when(jnp.max(span) > GLA_SAFE_EXP)
        def _():
            key_lane = lax.broadcasted_iota(I32, (1, C), 1)
            for p, (bi, pr) in enumerate(pairs):
                q, k = load_qk(bi, pr)
                b = cum_decay(bi, pr)
                tmp_ref[0] = b
                tmp_ref[1] = k

                def key_column(s, acc, q=q, b=b):
                    b_s = tmp_ref[0, pl.ds(s, 1), :]
                    k_s = tmp_ref[1, pl.ds(s, 1), :]
                    w = q * k_s * jnp.exp(jnp.where(row >= s, b - b_s, -jnp.inf))
                    hit = (key_lane == s).astype(F32)
                    return tuple(
                        acc[h] + jnp.sum(jnp.where(head_lanes[h], w, 0.0), axis=-1, keepdims=True) * hit
                        for h in range(2))

                a0, a1 = lax.fori_loop(0, C, key_column, (jnp.zeros((C, C), F32),) * 2)
                a_ref[2 * p] = a0
                a_ref[2 * p + 1] = a1

        for n, ((bi, pr, h), (o_st, v, cols)) in enumerate(zip(heads, part)):
            o = _dot(a_ref[n].astype(BF16), v) + o_st
            on = o * lax.rsqrt(jnp.mean(o * o, axis=-1, keepdims=True) + LN_EPS) * ng
            z = z_ref[bi, rows, cols]
            o_ref[bi, rows, cols] = (on * (z * jax.nn.sigmoid(z))).astype(BF16)
        return carry

    lax.fori_loop(0, n_chunks, body, 0)


def _gla(gq, gk, glog, gv, gz, ng, B, LP):
    rows = LP // GLA_PIECES
    kern = functools.partial(_gla_kernel, n_chunks=rows // GLA_CHUNK)
    blk = lambda b_, p: (b_, p, 0)
    return pl.pallas_call(
        kern,
        grid=(B // GLA_NB, GLA_PIECES),
        in_specs=[
            pl.BlockSpec((GLA_NB, rows, 256), blk),
            pl.BlockSpec((GLA_NB, rows, 256), blk),
            pl.BlockSpec((GLA_NB, rows, 256), blk),
            pl.BlockSpec((GLA_NB, rows, GLA_W), blk),
            pl.BlockSpec((GLA_NB, rows, GLA_W), blk),
            pl.BlockSpec((1, GLA_DV), lambda b_, p: (0, 0)),
        ],
        out_specs=pl.BlockSpec((GLA_NB, rows, GLA_W), blk),
        out_shape=jax.ShapeDtypeStruct((B, LP, GLA_W), BF16),
        scratch_shapes=[
            pltpu.VMEM((GLA_NB, GLA_HEADS, GLA_DV, LANES), F32),
            pltpu.VMEM((GLA_NB * GLA_HEADS, GLA_CHUNK, GLA_CHUNK), F32),
            pltpu.VMEM((2, GLA_CHUNK, LANES), F32),
        ],
        compiler_params=pltpu.CompilerParams(vmem_limit_bytes=VMEM_LIMIT),
        name="gla",
    )(gq.reshape(B, LP, 256), gk.reshape(B, LP, 256), glog.reshape(B, LP, 256),
      gv.reshape(B, LP, GLA_W), gz.reshape(B, LP, GLA_W), ng)


DSA_NB = 4


def _dsa_kernel(ik_ref, ak_ref, avt_ref, aq_ref, iq_ref, iwt_ref, az_ref, o_ref,
                sc_ref, qz_ref, iqz_ref, w_ref, m_ref, acc_ref, plane_ref, thr_ref, cnt_ref,
                *, topk, nblk):
    j = pl.program_id(1)
    npair = (j + 2) // 2
    KP = 2 * LANES
    batch = range(DSA_NB)

    def pair_rows(ref, n, i, axis):
        kb1 = jnp.minimum(2 * i + 1, nblk - 1)
        return jnp.concatenate([ref[n, 2 * i], ref[n, kb1]], axis=axis)

    def pair_ds(i):
        return pl.ds(pl.multiple_of(i * KP, KP), KP)

    lane = lax.broadcasted_iota(I32, (1, LANES), 1)
    low = lane < DSA_HD
    zero_b = jnp.zeros((LANES, LANES), BF16)
    for n in batch:
        for m in range(4):
            sl = slice(m * LANES, (m + 1) * LANES)
            qs = aq_ref[n, 0, :, sl]
            qz_ref[n, m * LANES:(m + 1) * LANES, :] = jnp.where(low, qs, zero_b)
            qz_ref[n, (m + 4) * LANES:(m + 5) * LANES, :] = jnp.where(low, zero_b, qs)
            iqs = iq_ref[n, 0, :, sl]
            iqz_ref[n, (2 * m) * LANES:(2 * m + 1) * LANES, :] = jnp.where(low, iqs, zero_b)
            iqz_ref[n, (2 * m + 1) * LANES:(2 * m + 2) * LANES, :] = jnp.where(low, zero_b, iqs)
        for h in range(IDX_HEADS):
            w_ref[n, :, h * LANES:(h + 1) * LANES] = iwt_ref[n, 0, h:h + 1, :]
    t_pos = j * LANES + lane
    s_loc = lax.broadcasted_iota(I32, (KP, 1), 0)

    @pl.when((pl.program_id(0) == 0) & (j == 0))
    def _():
        plane_ref[...] = jnp.zeros_like(plane_ref)

    def shift_const(x, n):
        return jnp.full(x.shape, n, I32)

    def store_planes(n, i, scores):
        bits = lax.bitcast_convert_type(scores, I32)
        u = bits ^ (lax.shift_right_arithmetic(bits, shift_const(bits, 31)) | jnp.int32(INT_MIN))
        a = [u[SUBLANES * v:SUBLANES * (v + 1), :] for v in range(32)]
        step, mask = 16, 0x0000FFFF
        while step:
            for k in range(32):
                if not k & step:
                    t = (a[k] ^ lax.shift_right_logical(a[k + step], shift_const(a[k], step))) & jnp.int32(mask)
                    a[k] = a[k] ^ t
                    a[k + step] = a[k + step] ^ lax.shift_left(t, shift_const(t, step))
            step >>= 1
            mask = (mask ^ (mask << step)) & 0xFFFFFFFF
        for r in range(32):
            plane_ref[n, i, r] = a[r]

    def score_pairs(pairs):
        work = [(n, i) for i in pairs for n in batch]
        dots = [_dot_nt(pair_rows(ik_ref, n, i, 0), iqz_ref[n]) for n, i in work]
        for (n, i), d in zip(work, dots):
            d = jnp.maximum(d, 0.0) * w_ref[n]
            acc = d[:, :LANES]
            for h in range(1, IDX_HEADS):
                acc = acc + d[:, h * LANES:(h + 1) * LANES]
            s_pos = i * KP + s_loc
            valid = (s_pos <= t_pos) & ((s_pos >= PAD) | (t_pos < PAD))
            scores = jnp.where(valid, acc, -jnp.inf)
            sc_ref[n, pair_ds(i), :] = scores
            store_planes(n, i, scores)

    def two_pair_trips(body):
        odd = npair % 2 == 1

        @pl.when(odd & (npair == 1))
        def _():
            body((0,))

        @pl.when(odd & (npair > 1))
        def _():
            body((0, 1, 2))

        first = jnp.where(odd, jnp.where(npair > 1, 3, 1), 0)

        def trip(u, carry):
            body((first + 2 * u, first + 2 * u + 1))
            return carry

        lax.fori_loop(0, (npair - first) // 2, trip, 0)

    two_pair_trips(score_pairs)

    def count(preds):
        flat = [(n, p) for n in batch for p in preds[n]]

        def cnt_block(i, cnts):
            xs = [sc_ref[n, pair_ds(i), :] for n in batch]
            out = []
            for (n, pred), cnt in zip(flat, cnts):
                hit = pred(xs[n]).astype(I32)
                parts = [hit[r:r + SUBLANES, :] for r in range(0, KP, SUBLANES)]
                while len(parts) > 1:
                    parts = [parts[k] + parts[k + 1] for k in range(0, len(parts), 2)]
                out.append(cnt + parts[0])
            return tuple(out)

        zero = jnp.zeros((SUBLANES, LANES), I32)
        cnts = lax.fori_loop(0, npair, cnt_block, (zero,) * len(flat))
        sums = [jnp.sum(c, axis=0, keepdims=True) for c in cnts]
        per = len(flat) // DSA_NB
        return [tuple(sums[n * per:(n + 1) * per]) for n in batch]

    select_all = jnp.float32(jnp.finfo(F32).min)

    ones_v = jnp.full((SUBLANES, LANES), -1, I32)
    zeros_v = jnp.zeros((SUBLANES, LANES), I32)

    def radix_select(n_pairs):
        alive0 = tuple(tuple(jnp.where(i < npair, ones_v, zeros_v) for i in range(n_pairs)) for n in batch)

        def radix_step(t, carry):
            alive_all, k_all, u_all = carry
            out_alive, out_k, out_u = [], [], []
            counts = []
            for n in batch:
                alive = alive_all[n]
                hi = [plane_ref[n, i, 2 * t] for i in range(n_pairs)]
                lo = [plane_ref[n, i, 2 * t + 1] for i in range(n_pairs)]
                c11 = c1x = c01 = zeros_v
                for i in range(n_pairs):
                    a1 = alive[i] & hi[i]
                    a0 = alive[i] ^ a1
                    c11 = c11 + lax.population_count(a1 & lo[i])
                    c1x = c1x + lax.population_count(a1)
                    c01 = c01 + lax.population_count(a0 & lo[i])
                counts.append((hi, lo, c11, c1x, c01))
            for n in batch:
                hi, lo, c11, c1x, c01 = counts[n]
                alive, k_left = alive_all[n], k_all[n]
                c11 = jnp.sum(c11, axis=0, keepdims=True)
                c1x = jnp.sum(c1x, axis=0, keepdims=True)
                c01 = jnp.sum(c01, axis=0, keepdims=True)
                in3 = k_left <= c11
                in32 = k_left <= c1x
                in321 = k_left <= c1x + c01
                bit_hi = in32
                bit_lo = in3 | (~in32 & in321)
                k_left = jnp.where(in3, k_left,
                                   jnp.where(in32, k_left - c11,
                                             jnp.where(in321, k_left - c1x, k_left - c1x - c01)))
                f_hi = jnp.where(bit_hi, jnp.int32(0), jnp.int32(-1))
                f_lo = jnp.where(bit_lo, jnp.int32(0), jnp.int32(-1))
                out_alive.append(tuple(alive[i] & (hi[i] ^ f_hi) & (lo[i] ^ f_lo) for i in range(n_pairs)))
                digit = jnp.where(bit_hi, jnp.int32(2), jnp.int32(0)) | jnp.where(bit_lo, jnp.int32(1), jnp.int32(0))
                out_k.append(k_left)
                out_u.append(u_all[n] | lax.shift_left(digit, jnp.full(digit.shape, 30 - 2 * t, I32)))
            return tuple(out_alive), tuple(out_k), tuple(out_u)

        _, _, u_thr = lax.fori_loop(
            0, 16, radix_step,
            (alive0, (jnp.full((1, LANES), topk, I32),) * DSA_NB, (jnp.zeros((1, LANES), I32),) * DSA_NB))
        for n in batch:
            thr_bits = jnp.where(u_thr[n] < 0, u_thr[n] ^ jnp.int32(INT_MIN), ~u_thr[n])
            tf = lax.bitcast_convert_type(thr_bits, F32)
            thr_ref[n] = jnp.where(tf == -jnp.inf, select_all, tf)

    n_pairs_max = plane_ref.shape[1]
    n_pairs_few = (n_pairs_max + 1) // 2

    @pl.when(npair <= n_pairs_few)
    def _():
        radix_select(n_pairs_few)

    @pl.when(npair > n_pairs_few)
    def _():
        radix_select(n_pairs_max)

    thr_fast = [thr_ref[n] for n in batch]

    def ge_gt(n):
        return (lambda x: x >= thr_fast[n], lambda x: x > thr_fast[n])

    n_bad = jnp.int32(0)
    for n, (n_ge, n_gt) in zip(batch, count([ge_gt(n) for n in batch])):
        is_all = thr_fast[n] == select_all
        good = is_all | ((n_gt < topk) & (n_ge >= topk))
        cnt_ref[n] = jnp.where(is_all, 0, n_ge)
        n_bad = n_bad + jnp.sum(jnp.where(good, 0, 1))

    @pl.when(n_bad > 0)
    def _():
        def key_to_float(key):
            bits = jnp.where(key < 0, key ^ jnp.int32(0x7FFFFFFF), key)
            return lax.bitcast_convert_type(bits, F32)

        def count_ge(cand_keys):
            cands = [key_to_float(c) for c in cand_keys]
            return [c[0] for c in count([((lambda x, n=n: x >= cands[n]),) for n in batch])]

        c0 = count_ge([jnp.zeros((1, LANES), I32)] * DSA_NB)
        base = tuple(jnp.where(c >= topk, jnp.int32(0), jnp.int32(INT_MIN)) for c in c0)
        base_cnt = tuple(jnp.where(c >= topk, c, 0) for c in c0)

        def bit_step(i, carry):
            base, base_cnt = carry
            cand = [b | jnp.left_shift(jnp.int32(1), 30 - i) for b in base]
            cs = count_ge(cand)
            ok = [c >= topk for c in cs]
            return (tuple(jnp.where(ok[n], cand[n], base[n]) for n in batch),
                    tuple(jnp.where(ok[n], cs[n], base_cnt[n]) for n in batch))

        base, base_cnt = lax.fori_loop(0, 31, bit_step, (base, base_cnt))
        for n in batch:
            thr_ref[n] = jnp.where(base[n] == INT_MIN, select_all, key_to_float(base[n]))
            cnt_ref[n] = base_cnt[n]

    thr = [thr_ref[n] for n in batch]
    base_cnt = [cnt_ref[n] for n in batch]

    most = base_cnt[0]
    for n in batch[1:]:
        most = jnp.maximum(most, base_cnt[n])

    @pl.when(jnp.max(most) > topk)
    def _():
        n_gt = count([((lambda x, n=n: x > thr[n]),) for n in batch])
        rr = lax.broadcasted_iota(I32, (KP, KP), 0)
        cc = lax.broadcasted_iota(I32, (KP, KP), 1)
        tri = (rr >= cc).astype(BF16)
        for n in batch:
            need = (topk - n_gt[n][0]).astype(F32)

            def strike(i, seen, n=n, need=need):
                x = sc_ref[n, pair_ds(i), :]
                eq = x == thr[n]
                rank = _dot(tri, jnp.where(eq, 1.0, 0.0).astype(BF16)) + seen
                sc_ref[n, pair_ds(i), :] = jnp.where(eq & (rank > need), -jnp.inf, x)
                return rank[KP - 1:KP, :]

            lax.fori_loop(0, npair, strike, jnp.zeros((1, LANES), F32))

    m_ref[...] = jnp.full_like(m_ref, M_INIT)
    acc_ref[...] = jnp.zeros_like(acc_ref)
    GW = DSA_GROUP * LANES
    ones_rows = jnp.ones((ONES_ROWS, KP), BF16)

    def attn_pairs(pairs):
        work = [(n, i) for i in pairs for n in batch]
        logits = [_dot_nt(pair_rows(ak_ref, n, i, 0), qz_ref[n]).astype(BF16) for n, i in work]
        for (n, i), s in zip(work, logits):
            bias = jnp.where(sc_ref[n, pair_ds(i), :] >= thr[n], 0.0, -jnp.inf).astype(BF16)
            sb = s + jnp.concatenate([bias] * DSA_HEADS, axis=1)
            m_old = m_ref[n]
            m_new = jnp.maximum(m_old, jnp.max(sb, axis=0, keepdims=True).astype(F32))
            alpha = jnp.exp(m_old - m_new)
            pb = jnp.exp(sb - m_new.astype(BF16))
            m_ref[n] = m_new
            vt = pair_rows(avt_ref, n, i, 1)
            for g in range(DSA_KV_HEADS):
                cols = slice(g * GW, (g + 1) * GW)
                vg = jnp.concatenate([vt[g * DSA_HD:(g + 1) * DSA_HD, :], ones_rows], axis=0)
                acc_ref[n, g] = alpha[:, cols] * acc_ref[n, g] + _dot(vg, pb[:, cols])

    two_pair_trips(attn_pairs)

    for n in batch:
        heads = []
        for h in range(DSA_HEADS):
            a = acc_ref[n, h // DSA_GROUP][:, (h % DSA_GROUP) * LANES:(h % DSA_GROUP + 1) * LANES]
            heads.append(a[:DSA_HD, :] / a[DSA_HD:DSA_HD + 1, :])
        ot = jnp.concatenate(heads, axis=0)
        z = az_ref[n, 0]
        o_ref[n, 0] = (ot.T * (z * jax.nn.sigmoid(z))).astype(BF16)


def _dsa(ik2, ak, avt, aq, iq, iwt, az, B, nblk, topk):
    kern = functools.partial(_dsa_kernel, topk=topk, nblk=nblk)
    nb = DSA_NB
    whole = lambda b_, j: (b_, 0, 0, 0)
    qblk = lambda b_, j: (b_, j, 0, 0)
    by_block = lambda a: a.reshape(B, nblk, -1, a.shape[-1])
    out = pl.pallas_call(
        kern,
        grid=(B // nb, nblk),
        in_specs=[
            pl.BlockSpec((nb, nblk, LANES, LANES), whole),
            pl.BlockSpec((nb, nblk, LANES, LANES), whole),
            pl.BlockSpec((nb, nblk, LANES, LANES), whole),
            pl.BlockSpec((nb, 1, LANES, DSA_W), qblk),
            pl.BlockSpec((nb, 1, LANES, IDX_HEADS * IDX_HD), qblk),
            pl.BlockSpec((nb, 1, IDX_HEADS, LANES), qblk),
            pl.BlockSpec((nb, 1, LANES, DSA_W), qblk),
        ],
        out_specs=pl.BlockSpec((nb, 1, LANES, DSA_W), qblk),
        out_shape=jax.ShapeDtypeStruct((B, nblk, LANES, DSA_W), BF16),
        scratch_shapes=[
            pltpu.VMEM((nb, (nblk + 1) // 2 * 2 * LANES, LANES), F32),
            pltpu.VMEM((nb, DSA_HEADS * LANES, LANES), BF16),
            pltpu.VMEM((nb, IDX_HEADS * LANES, LANES), BF16),
            pltpu.VMEM((nb, 1, IDX_HEADS * LANES), F32),
            pltpu.VMEM((nb, 1, DSA_HEADS * LANES), F32),
            pltpu.VMEM((nb, DSA_KV_HEADS, DSA_HD + ONES_ROWS, DSA_GROUP * LANES), F32),
            pltpu.VMEM((nb, (nblk + 1) // 2, 32, SUBLANES, LANES), I32),
            pltpu.VMEM((nb, 1, LANES), F32),
            pltpu.VMEM((nb, 1, LANES), I32),
        ],
        compiler_params=pltpu.CompilerParams(vmem_limit_bytes=VMEM_LIMIT),
        name="dsa",
    )(by_block(ik2), by_block(ak), by_block(avt), by_block(aq), by_block(iq), by_block(iwt), by_block(az))
    return out.reshape(B * nblk, LANES, DSA_W)


def _out_kernel(mg_ref, md_ref, h_ref, w_ref, g_ref, b_ref, o_ref, *, alpha):
    n_sub = OUT_ROWS // OUT_SUB_ROWS
    subs = [slice(r * OUT_SUB_ROWS, (r + 1) * OUT_SUB_ROWS) for r in range(n_sub)]
    ys = [_dot(mg_ref[0, s, :], w_ref[:GLA_W, :]) + _dot(md_ref[0, s, :], w_ref[GLA_W:, :]) for s in subs]
    for s, y in zip(subs, ys):
        o_ref[0, s, :] = _layer_norm_rows(alpha * h_ref[0, s, :] + y, g_ref[...], b_ref[...])


def _out(mg, md, h, w, g, b, B, LP, alpha, drop_filler):
    rows = OUT_ROWS
    if drop_filler:
        S = LP - LANES
        grid, nout = (B, S // rows), S
        src_blk = lambda width: (pl.Element(1), pl.Element(rows), pl.Element(width))
        src = lambda b_, j: (b_, pl.multiple_of(LANES + j * rows, LANES), 0)
    else:
        grid, nout = (B * LP // rows,), LP
        src_blk = lambda width: (1, rows, width)
        mg, md, h = (a.reshape(1, B * LP, a.shape[-1]) for a in (mg, md, h))
        src = lambda i: (0, i, 0)
    dst = (lambda b_, j: (b_, j, 0)) if drop_filler else (lambda i: (0, i, 0))
    const = (lambda b_, j: (0, 0)) if drop_filler else (lambda i: (0, 0))
    out_shape = (B, nout, D_MODEL) if drop_filler else (1, B * LP, D_MODEL)
    kern = functools.partial(_out_kernel, alpha=alpha)
    return pl.pallas_call(
        kern,
        grid=grid,
        in_specs=[
            pl.BlockSpec(src_blk(GLA_W), src),
            pl.BlockSpec(src_blk(DSA_W), src),
            pl.BlockSpec(src_blk(D_MODEL), src),
            pl.BlockSpec((MIX_W, D_MODEL), const),
            pl.BlockSpec((1, D_MODEL), const),
            pl.BlockSpec((1, D_MODEL), const),
        ],
        out_specs=pl.BlockSpec((1, rows, D_MODEL), dst),
        out_shape=jax.ShapeDtypeStruct(out_shape, F32),
        name="out_proj_ln",
    )(mg, md, h, w, g, b)


def _pack_w_in(w):
    splits = (256, 256, 512, GLA_RANK, 512, 512, 128, 128, 512, IDX_HD, IDX_HEADS, 512)
    offs = np.cumsum((0,) + splits)
    gq, gk, gv, glr, gz, aq, ak, av, iq, ik, iw, az = [w[:, offs[i]:offs[i + 1]] for i in range(12)]

    def perm_heads(a):
        a = a.reshape(D_MODEL, DSA_KV_HEADS, DSA_GROUP, DSA_HD)
        return a.transpose(0, 2, 1, 3).reshape(D_MODEL, DSA_W)

    pad = jnp.zeros((D_MODEL, LANES - IDX_HD - GLA_RANK - IDX_HEADS), w.dtype)
    pieces = [gq * (GLA_DK ** -0.5), gk, gv, gz, perm_heads(aq) * (DSA_HD ** -0.5), ak, av,
              iq * (IDX_HD ** -0.5), az, ik, glr, iw, pad]
    return jnp.concatenate([p.astype(BF16) for p in pieces], axis=1)


def _rope_lane_tables(reps, LP):
    inv = ROPE_THETA ** (-jnp.arange(0, ROPE_DIM, 2, dtype=F32) / ROPE_DIM)
    pos = (jnp.arange(LP, dtype=F32) - PAD)[:, None]
    ang = pos * inv[None, :]
    cos, sin = jnp.cos(ang), jnp.sin(ang)
    ones = jnp.ones((LP, DSA_HD - ROPE_DIM), F32)
    zeros = jnp.zeros((LP, DSA_HD - ROPE_DIM), F32)
    zh = jnp.zeros((LP, ROPE_HALF), F32)
    c = jnp.concatenate([cos, cos, ones], axis=1)
    sa = jnp.concatenate([zh, sin, zeros], axis=1)
    sb = jnp.concatenate([-sin, zh, zeros], axis=1)
    tile = lambda t: jnp.tile(jnp.concatenate([t, t], axis=1), (reps, 1))
    return tile(c), tile(sa), tile(sb)


def kernel(x, meta_tokens, ln_in_g, ln_in_b, w_in, gla_wg2, gla_bg, gla_norm_g, idx_k_g, idx_k_b,
           w_out, ln_g, ln_b):
    B, S, D = x.shape
    depth = w_in.shape[0]
    nblk = S // LANES + 1
    LP = nblk * LANES
    R = B * LP
    topk = min(TOPK_MAX, S // 4)
    alpha = (2.0 * depth) ** 0.25
    tm = PROJ_ROWS

    meta_pad = jnp.concatenate([jnp.zeros((PAD, D), x.dtype), meta_tokens.astype(x.dtype)], axis=0)
    h = _embed(x, meta_pad, ln_in_g.reshape(1, D), ln_in_b.reshape(1, D))
    cos_t, sa_t, sb_t = _rope_lane_tables(tm // math.gcd(tm, LP), LP)

    for i in range(depth):
        w = _pack_w_in(w_in[i])
        wg2 = jnp.zeros((LANES, 256), F32).at[MISC_GLR:MISC_GLR + GLA_RANK].set(gla_wg2[i])
        wg2h = wg2.astype(BF16)
        wg2l = (wg2 - wg2h.astype(F32)).astype(BF16)
        ikg = jnp.zeros((1, LANES), F32).at[0, :IDX_HD].set(idx_k_g[i])
        ikb = jnp.zeros((1, LANES), F32).at[0, :IDX_HD].set(idx_k_b[i])
        (gq, gk, gv, glog, gz, aq, ak, avt, iq, ik2, iwt, az) = _proj(
            h.reshape(R, D), w, wg2h, wg2l, gla_bg[i].reshape(1, 256), ikg, ikb,
            cos_t, sa_t, sb_t, tm)
        mix_gla = _gla(gq, gk, glog, gv, gz, gla_norm_g[i].reshape(1, GLA_DV), B, LP)
        mix_dsa = _dsa(ik2, ak, avt, aq, iq, iwt, az, B, nblk, topk)
        h = _out(mix_gla, mix_dsa.reshape(B, LP, DSA_W), h.reshape(B, LP, D),
                 w_out[i].astype(BF16), ln_g[i].reshape(1, D), ln_b[i].reshape(1, D),
                 B, LP, alpha, drop_filler=(i == depth - 1))
    return h.reshape(B, S, D)
```

```python
import functools
import math

import numpy as np
import jax
import jax.numpy as jnp
from jax import lax
from jax.experimental import pallas as pl
from jax.experimental.pallas import tpu as pltpu

F32 = jnp.float32
BF16 = jnp.bfloat16
I32 = jnp.int32

D_MODEL = 1024
N_META = 16
ROPE_THETA = 500000.0
LN_EPS = 1e-5
GLA_HEADS = 4
GLA_DK = 64
GLA_DV = 128
GLA_RANK = 16
GLA_TAU = 16.0
GLA_CHUNK = 64
GLA_W = GLA_HEADS * GLA_DV
DSA_HEADS = 8
DSA_KV_HEADS = 2
DSA_GROUP = DSA_HEADS // DSA_KV_HEADS
DSA_HD = 64
DSA_W = DSA_HEADS * DSA_HD
IDX_HEADS = 8
IDX_HD = 64
TOPK_MAX = 256
ROPE_DIM = DSA_HD // 4
ROPE_HALF = ROPE_DIM // 2
MIX_W = GLA_W + DSA_W

LANES = 128
SUBLANES = 8
ONES_ROWS = 16
PAD = LANES - N_META
INT_MIN = -(2 ** 31)
M_INIT = -(2.0 ** 100)
VMEM_LIMIT = 48 * 1024 * 1024
PROJ_ROWS = 512
OUT_ROWS = 1024
OUT_SUB_ROWS = 256

O_GQ, O_GK, O_GV, O_GZ = 0, 256, 512, 1024
O_AQ, O_AK, O_AV, O_IQ, O_AZ, O_MISC = 1536, 2048, 2176, 2304, 2816, 3328
W_PACKED = 3456
MISC_GLR = IDX_HD
MISC_IW = IDX_HD + GLA_RANK


def _dot(a, b):
    return jnp.dot(a, b, preferred_element_type=F32)


def _dot_nt(a, b):
    return lax.dot_general(a, b, (((1,), (1,)), ((), ())), preferred_element_type=F32)


def _dot_tn(a, b):
    return lax.dot_general(a, b, (((0,), (0,)), ((), ())), preferred_element_type=F32)


def _layer_norm_rows(u, g, b):
    mu = jnp.mean(u, axis=-1, keepdims=True)
    d = u - mu
    var = jnp.mean(d * d, axis=-1, keepdims=True)
    return d * lax.rsqrt(var + LN_EPS) * g + b


def _embed_kernel(x_ref, meta_ref, g_ref, b_ref, o_ref):
    g, b = g_ref[...], b_ref[...]
    o_ref[0] = _layer_norm_rows(meta_ref[...], g, b)
    for c in range(x_ref.shape[1]):
        o_ref[c + 1] = _layer_norm_rows(x_ref[0, c], g, b)


def _embed(x, meta_pad, g, b):
    B, S, D = x.shape
    nblk = S // LANES + 1
    x4 = x.reshape(B, S // LANES, LANES, D)
    return pl.pallas_call(
        _embed_kernel,
        grid=(B,),
        in_specs=[
            pl.BlockSpec((1, S // LANES, LANES, D), lambda b_: (b_, 0, 0, 0)),
            pl.BlockSpec((LANES, D), lambda b_: (0, 0)),
            pl.BlockSpec((1, D), lambda b_: (0, 0)),
            pl.BlockSpec((1, D), lambda b_: (0, 0)),
        ],
        out_specs=pl.BlockSpec((nblk, LANES, D), lambda b_: (b_, 0, 0)),
        out_shape=jax.ShapeDtypeStruct((B * nblk, LANES, D), F32),
        compiler_params=pltpu.CompilerParams(vmem_limit_bytes=VMEM_LIMIT),
        name="embed_ln",
    )(x4, meta_pad, g, b)


def _rope_slab(x, c, sa, sb):
    return x * c + pltpu.roll(x, ROPE_HALF, 1) * sa + pltpu.roll(x, LANES - ROPE_HALF, 1) * sb


def _proj_kernel(h_ref, w_ref, wg2h_ref, wg2l_ref, bg_ref, ikg_ref, ikb_ref,
                 cos_ref, sa_ref, sb_ref,
                 gq_ref, gk_ref, gv_ref, glog_ref, gz_ref, aq_ref, ak_ref, avt_ref,
                 iq_ref, ik_ref, iwt_ref, az_ref, *, tm):
    hb = h_ref[...].astype(BF16)

    def seg(o, w):
        return _dot(hb, w_ref[:, o:o + w])

    c, sa, sb = cos_ref[...], sa_ref[...], sb_ref[...]
    misc = seg(O_MISC, LANES)
    kv = seg(O_AK, 2 * LANES)
    ak, av = kv[:, :LANES], kv[:, LANES:]
    aq = seg(O_AQ, 512)

    misc_t_scale = IDX_HEADS ** -0.5
    for r in range(tm // LANES):
        rows = slice(r * LANES, (r + 1) * LANES)
        avt_ref[r] = av[rows, :].T.astype(BF16)
        iwt_ref[r] = misc[rows, :].T[MISC_IW:MISC_IW + IDX_HEADS, :] * misc_t_scale
    ak_ref[...] = _rope_slab(ak, c, sa, sb).astype(BF16)
    iq = seg(O_IQ, 512)

    m_hi = misc.astype(BF16)
    m_lo = (misc - m_hi.astype(F32)).astype(BF16)
    xg = (_dot(m_hi, wg2h_ref[...]) + _dot(m_lo, wg2h_ref[...]) + _dot(m_hi, wg2l_ref[...])
          + bg_ref[...])
    gq_ref[...] = seg(O_GQ, 256)
    glog_ref[...] = (jnp.minimum(xg, 0.0) - jnp.log1p(jnp.exp(-jnp.abs(xg)))) * (1.0 / GLA_TAU)

    lane = lax.broadcasted_iota(I32, (1, LANES), 1)
    is_key = lane < IDX_HD
    mu = jnp.sum(jnp.where(is_key, misc, 0.0), axis=-1, keepdims=True) * (1.0 / IDX_HD)
    d = jnp.where(is_key, misc - mu, 0.0)
    var = jnp.sum(d * d, axis=-1, keepdims=True) * (1.0 / IDX_HD)
    ikn = d * lax.rsqrt(var + LN_EPS) * ikg_ref[...] + ikb_ref[...]
    ikr = _rope_slab(ikn, c, sa, sb)
    gk_ref[...] = seg(O_GK, 256)
    ik_ref[...] = jnp.where(is_key, ikr, pltpu.roll(ikr, IDX_HD, 1)).astype(BF16)

    gv = seg(O_GV, 512)
    for m in range(4):
        sl = slice(m * LANES, (m + 1) * LANES)
        aq_ref[:, sl] = _rope_slab(aq[:, sl], c, sa, sb).astype(BF16)
    gv_ref[...] = gv.astype(BF16)
    gz = seg(O_GZ, 512)
    for m in range(4):
        sl = slice(m * LANES, (m + 1) * LANES)
        iq_ref[:, sl] = _rope_slab(iq[:, sl], c, sa, sb).astype(BF16)
    gz_ref[...] = gz
    az_ref[...] = seg(O_AZ, 512)


def _proj(h2, w, wg2h, wg2l, bg, ikg, ikb, cos_t, sa_t, sb_t, tm):
    R = h2.shape[0]
    nt = R // tm
    nsub = tm // LANES
    row = lambda i: (i, 0)
    const = lambda i: (0, 0)
    table_tiles = cos_t.shape[0] // tm
    table_row = lambda i: (i % table_tiles, 0)
    kern = functools.partial(_proj_kernel, tm=tm)
    out_shape = (
        jax.ShapeDtypeStruct((R, 256), F32),
        jax.ShapeDtypeStruct((R, 256), F32),
        jax.ShapeDtypeStruct((R, 512), BF16),
        jax.ShapeDtypeStruct((R, 256), F32),
        jax.ShapeDtypeStruct((R, 512), F32),
        jax.ShapeDtypeStruct((R, 512), BF16),
        jax.ShapeDtypeStruct((R, LANES), BF16),
        jax.ShapeDtypeStruct((R // LANES, LANES, LANES), BF16),
        jax.ShapeDtypeStruct((R, 512), BF16),
        jax.ShapeDtypeStruct((R, LANES), BF16),
        jax.ShapeDtypeStruct((R // LANES, IDX_HEADS, LANES), F32),
        jax.ShapeDtypeStruct((R, 512), F32),
    )
    out_specs = (
        pl.BlockSpec((tm, 256), row),
        pl.BlockSpec((tm, 256), row),
        pl.BlockSpec((tm, 512), row),
        pl.BlockSpec((tm, 256), row),
        pl.BlockSpec((tm, 512), row),
        pl.BlockSpec((tm, 512), row),
        pl.BlockSpec((tm, LANES), row),
        pl.BlockSpec((nsub, LANES, LANES), lambda i: (i, 0, 0)),
        pl.BlockSpec((tm, 512), row),
        pl.BlockSpec((tm, LANES), row),
        pl.BlockSpec((nsub, IDX_HEADS, LANES), lambda i: (i, 0, 0)),
        pl.BlockSpec((tm, 512), row),
    )
    in_specs = [
        pl.BlockSpec((tm, D_MODEL), row),
        pl.BlockSpec((D_MODEL, W_PACKED), const),
        pl.BlockSpec((LANES, 256), const),
        pl.BlockSpec((LANES, 256), const),
        pl.BlockSpec((1, 256), const),
        pl.BlockSpec((1, LANES), const),
        pl.BlockSpec((1, LANES), const),
        pl.BlockSpec((tm, LANES), table_row),
        pl.BlockSpec((tm, LANES), table_row),
        pl.BlockSpec((tm, LANES), table_row),
    ]
    return pl.pallas_call(
        kern, grid=(nt,), in_specs=in_specs, out_specs=out_specs, out_shape=out_shape,
        compiler_params=pltpu.CompilerParams(vmem_limit_bytes=VMEM_LIMIT),
        name="in_proj",
    )(h2, w, wg2h, wg2l, bg, ikg, ikb, cos_t, sa_t, sb_t)


GLA_NB = 8
GLA_PIECES = 17
GLA_SAFE_EXP = 80.0


def _gla_kernel(q_ref, k_ref, g_ref, v_ref, z_ref, ng_ref, o_ref, st_ref, a_ref, tmp_ref,
                *, n_chunks):
    C = GLA_CHUNK
    piece = pl.program_id(1)

    @pl.when(piece == 0)
    def _():
        st_ref[...] = jnp.zeros_like(st_ref)

    lane = lax.broadcasted_iota(I32, (1, LANES), 1)
    head_lanes = (lane < GLA_DK, lane >= GLA_DK)
    rr = lax.broadcasted_iota(I32, (C, C), 0)
    cc = lax.broadcasted_iota(I32, (C, C), 1)
    causal = rr >= cc
    tril = causal.astype(BF16)
    row = lax.broadcasted_iota(I32, (C, 1), 0)
    ng = ng_ref[...]
    row0 = piece * (n_chunks * C)

    def body(ci, carry):
        r0 = pl.multiple_of(ci * C, C)
        rows = pl.ds(r0, C)
        valid = (row0 + r0 + row) >= PAD
        pairs = [(bi, pr) for bi in range(GLA_NB) for pr in range(GLA_HEADS // 2)]

        def cum_decay(bi, pr):
            g = g_ref[bi, rows, pr * LANES:(pr + 1) * LANES]
            g_hi = g.astype(BF16)
            g_lo = (g - g_hi.astype(F32)).astype(BF16)
            return _dot(tril, g_hi) + _dot(tril, g_lo)

        def load_qk(bi, pr):
            pl_ = slice(pr * LANES, (pr + 1) * LANES)
            return q_ref[bi, rows, pl_], jnp.where(valid, k_ref[bi, rows, pl_], 0.0)

        bs = [cum_decay(bi, pr) for bi, pr in pairs]
        ops = []
        span = jnp.zeros((C, LANES), F32)
        for (bi, pr), b in zip(pairs, bs):
            q, k = load_qk(bi, pr)
            b_mid = b[C // 2 - 1:C // 2, :]
            b_last = b[C - 1:C, :]
            d_mid = b - b_mid
            span = jnp.maximum(span, jnp.abs(d_mid))
            q_in = (q * jnp.exp(d_mid)).astype(BF16)
            k_in = (k * jnp.exp(-d_mid)).astype(BF16)
            q_st = (q * jnp.exp(b)).astype(BF16)
            k_st = (k * jnp.exp(b_last - b)).astype(BF16)
            ops.append((q_in, k_in, q_st, k_st, jnp.exp(b_last)))
        heads = [(bi, pr, h) for bi, pr in pairs for h in range(2)]
        zero = jnp.zeros((C, LANES), BF16)
        part = []
        for n, (bi, pr, h) in enumerate(heads):
            q_in, k_in, q_st, k_st, decay = ops[n // 2]
            hl = head_lanes[h]
            head = 2 * pr + h
            cols = slice(head * GLA_DV, (head + 1) * GLA_DV)
            v = jnp.where(valid, v_ref[bi, rows, cols], jnp.zeros((), BF16))
            st = st_ref[bi, head]
            a_ref[n] = jnp.where(causal, _dot_nt(jnp.where(hl, q_in, zero), k_in), 0.0)
            o_st = _dot_nt(jnp.where(hl, q_st, zero), st.astype(BF16))
            st_ref[bi, head] = st * decay + _dot_tn(v, jnp.where(hl, k_st, zero))
            part.append((o_st, v, cols))

        @pl.when(jnp.max(span) > GLA_SAFE_EXP)
        def _():
            key_lane = lax.broadcasted_iota(I32, (1, C), 1)
            for p, (bi, pr) in enumerate(pairs):
                q, k = load_qk(bi, pr)
                b = cum_decay(bi, pr)
                tmp_ref[0] = b
                tmp_ref[1] = k

                def key_column(s, acc, q=q, b=b):
                    b_s = tmp_ref[0, pl.ds(s, 1), :]
                    k_s = tmp_ref[1, pl.ds(s, 1), :]
                    w = q * k_s * jnp.exp(jnp.where(row >= s, b - b_s, -jnp.inf))
                    hit = (key_lane == s).astype(F32)
                    return tuple(
                        acc[h] + jnp.sum(jnp.where(head_lanes[h], w, 0.0), axis=-1, keepdims=True) * hit
                        for h in range(2))

                a0, a1 = lax.fori_loop(0, C, key_column, (jnp.zeros((C, C), F32),) * 2)
                a_ref[2 * p] = a0
                a_ref[2 * p + 1] = a1

        for n, ((bi, pr, h), (o_st, v, cols)) in enumerate(zip(heads, part)):
            o = _dot(a_ref[n].astype(BF16), v) + o_st
            on = o * lax.rsqrt(jnp.mean(o * o, axis=-1, keepdims=True) + LN_EPS) * ng
            z = z_ref[bi, rows, cols]
            o_ref[bi, rows, cols] = (on * (z * jax.nn.sigmoid(z))).astype(BF16)
        return carry

    lax.fori_loop(0, n_chunks, body, 0)


def _gla(gq, gk, glog, gv, gz, ng, B, LP):
    rows = LP // GLA_PIECES
    kern = functools.partial(_gla_kernel, n_chunks=rows // GLA_CHUNK)
    blk = lambda b_, p: (b_, p, 0)
    return pl.pallas_call(
        kern,
        grid=(B // GLA_NB, GLA_PIECES),
        in_specs=[
            pl.BlockSpec((GLA_NB, rows, 256), blk),
            pl.BlockSpec((GLA_NB, rows, 256), blk),
            pl.BlockSpec((GLA_NB, rows, 256), blk),
            pl.BlockSpec((GLA_NB, rows, GLA_W), blk),
            pl.BlockSpec((GLA_NB, rows, GLA_W), blk),
            pl.BlockSpec((1, GLA_DV), lambda b_, p: (0, 0)),
        ],
        out_specs=pl.BlockSpec((GLA_NB, rows, GLA_W), blk),
        out_shape=jax.ShapeDtypeStruct((B, LP, GLA_W), BF16),
        scratch_shapes=[
            pltpu.VMEM((GLA_NB, GLA_HEADS, GLA_DV, LANES), F32),
            pltpu.VMEM((GLA_NB * GLA_HEADS, GLA_CHUNK, GLA_CHUNK), F32),
            pltpu.VMEM((2, GLA_CHUNK, LANES), F32),
        ],
        compiler_params=pltpu.CompilerParams(vmem_limit_bytes=VMEM_LIMIT),
        name="gla",
    )(gq.reshape(B, LP, 256), gk.reshape(B, LP, 256), glog.reshape(B, LP, 256),
      gv.reshape(B, LP, GLA_W), gz.reshape(B, LP, GLA_W), ng)


DSA_NB = 4


def _dsa_kernel(ik_ref, ak_ref, avt_ref, aq_ref, iq_ref, iwt_ref, az_ref, o_ref,
                sc_ref, qz_ref, iqz_ref, w_ref, m_ref, acc_ref, plane_ref, thr_ref, cnt_ref,
                *, topk, nblk):
    j = pl.program_id(1)
    npair = (j + 2) // 2
    KP = 2 * LANES
    batch = range(DSA_NB)

    def pair_rows(ref, n, i, axis):
        kb1 = jnp.minimum(2 * i + 1, nblk - 1)
        return jnp.concatenate([ref[n, 2 * i], ref[n, kb1]], axis=axis)

    def pair_ds(i):
        return pl.ds(pl.multiple_of(i * KP, KP), KP)

    lane = lax.broadcasted_iota(I32, (1, LANES), 1)
    low = lane < DSA_HD
    zero_b = jnp.zeros((LANES, LANES), BF16)
    for n in batch:
        for m in range(4):
            sl = slice(m * LANES, (m + 1) * LANES)
            qs = aq_ref[n, 0, :, sl]
            qz_ref[n, m * LANES:(m + 1) * LANES, :] = jnp.where(low, qs, zero_b)
            qz_ref[n, (m + 4) * LANES:(m + 5) * LANES, :] = jnp.where(low, zero_b, qs)
            iqs = iq_ref[n, 0, :, sl]
            iqz_ref[n, (2 * m) * LANES:(2 * m + 1) * LANES, :] = jnp.where(low, iqs, zero_b)
            iqz_ref[n, (2 * m + 1) * LANES:(2 * m + 2) * LANES, :] = jnp.where(low, zero_b, iqs)
        for h in range(IDX_HEADS):
            w_ref[n, :, h * LANES:(h + 1) * LANES] = iwt_ref[n, 0, h:h + 1, :]
    t_pos = j * LANES + lane
    s_loc = lax.broadcasted_iota(I32, (KP, 1), 0)

    @pl.when((pl.program_id(0) == 0) & (j == 0))
    def _():
        plane_ref[...] = jnp.zeros_like(plane_ref)

    def shift_const(x, n):
        return jnp.full(x.shape, n, I32)

    def store_planes(n, i, scores):
        bits = lax.bitcast_convert_type(scores, I32)
        u = bits ^ (lax.shift_right_arithmetic(bits, shift_const(bits, 31)) | jnp.int32(INT_MIN))
        a = [u[SUBLANES * v:SUBLANES * (v + 1), :] for v in range(32)]
        step, mask = 16, 0x0000FFFF
        while step:
            for k in range(32):
                if not k & step:
                    t = (a[k] ^ lax.shift_right_logical(a[k + step], shift_const(a[k], step))) & jnp.int32(mask)
                    a[k] = a[k] ^ t
                    a[k + step] = a[k + step] ^ lax.shift_left(t, shift_const(t, step))
            step >>= 1
            mask = (mask ^ (mask << step)) & 0xFFFFFFFF
        for r in range(32):
            plane_ref[n, i, r] = a[r]

    def score_pairs(pairs):
        work = [(n, i) for i in pairs for n in batch]
        dots = [_dot_nt(pair_rows(ik_ref, n, i, 0), iqz_ref[n]) for n, i in work]
        for (n, i), d in zip(work, dots):
            d = jnp.maximum(d, 0.0) * w_ref[n]
            acc = d[:, :LANES]
            for h in range(1, IDX_HEADS):
                acc = acc + d[:, h * LANES:(h + 1) * LANES]
            s_pos = i * KP + s_loc
            valid = (s_pos <= t_pos) & ((s_pos >= PAD) | (t_pos < PAD))
            scores = jnp.where(valid, acc, -jnp.inf)
            sc_ref[n, pair_ds(i), :] = scores
            store_planes(n, i, scores)

    def two_pair_trips(body):
        odd = npair % 2 == 1

        @pl.when(odd & (npair == 1))
        def _():
            body((0,))

        @pl.when(odd & (npair > 1))
        def _():
            body((0, 1, 2))

        first = jnp.where(odd, jnp.where(npair > 1, 3, 1), 0)

        def trip(u, carry):
            body((first + 2 * u, first + 2 * u + 1))
            return carry

        lax.fori_loop(0, (npair - first) // 2, trip, 0)

    two_pair_trips(score_pairs)

    def count(preds):
        flat = [(n, p) for n in batch for p in preds[n]]

        def cnt_block(i, cnts):
            xs = [sc_ref[n, pair_ds(i), :] for n in batch]
            out = []
            for (n, pred), cnt in zip(flat, cnts):
                hit = pred(xs[n]).astype(I32)
                parts = [hit[r:r + SUBLANES, :] for r in range(0, KP, SUBLANES)]
                while len(parts) > 1:
                    parts = [parts[k] + parts[k + 1] for k in range(0, len(parts), 2)]
                out.append(cnt + parts[0])
            return tuple(out)

        zero = jnp.zeros((SUBLANES, LANES), I32)
        cnts = lax.fori_loop(0, npair, cnt_block, (zero,) * len(flat))
        sums = [jnp.sum(c, axis=0, keepdims=True) for c in cnts]
        per = len(flat) // DSA_NB
        return [tuple(sums[n * per:(n + 1) * per]) for n in batch]

    select_all = jnp.float32(jnp.finfo(F32).min)

    ones_v = jnp.full((SUBLANES, LANES), -1, I32)
    zeros_v = jnp.zeros((SUBLANES, LANES), I32)

    def radix_select(n_pairs):
        alive0 = tuple(tuple(jnp.where(i < npair, ones_v, zeros_v) for i in range(n_pairs)) for n in batch)

        def radix_step(t, carry):
            alive_all, k_all, u_all = carry
            out_alive, out_k, out_u = [], [], []
            counts = []
            for n in batch:
                alive = alive_all[n]
                hi = [plane_ref[n, i, 2 * t] for i in range(n_pairs)]
                lo = [plane_ref[n, i, 2 * t + 1] for i in range(n_pairs)]
                c11 = c1x = c01 = zeros_v
                for i in range(n_pairs):
                    a1 = alive[i] & hi[i]
                    a0 = alive[i] ^ a1
                    c11 = c11 + lax.population_count(a1 & lo[i])
                    c1x = c1x + lax.population_count(a1)
                    c01 = c01 + lax.population_count(a0 & lo[i])
                counts.append((hi, lo, c11, c1x, c01))
            for n in batch:
                hi, lo, c11, c1x, c01 = counts[n]
                alive, k_left = alive_all[n], k_all[n]
                c11 = jnp.sum(c11, axis=0, keepdims=True)
                c1x = jnp.sum(c1x, axis=0, keepdims=True)
                c01 = jnp.sum(c01, axis=0, keepdims=True)
                in3 = k_left <= c11
                in32 = k_left <= c1x
                in321 = k_left <= c1x + c01
                bit_hi = in32
                bit_lo = in3 | (~in32 & in321)
                k_left = jnp.where(in3, k_left,
                                   jnp.where(in32, k_left - c11,
                                             jnp.where(in321, k_left - c1x, k_left - c1x - c01)))
                f_hi = jnp.where(bit_hi, jnp.int32(0), jnp.int32(-1))
                f_lo = jnp.where(bit_lo, jnp.int32(0), jnp.int32(-1))
                out_alive.append(tuple(alive[i] & (hi[i] ^ f_hi) & (lo[i] ^ f_lo) for i in range(n_pairs)))
                digit = jnp.where(bit_hi, jnp.int32(2), jnp.int32(0)) | jnp.where(bit_lo, jnp.int32(1), jnp.int32(0))
                out_k.append(k_left)
                out_u.append(u_all[n] | lax.shift_left(digit, jnp.full(digit.shape, 30 - 2 * t, I32)))
            return tuple(out_alive), tuple(out_k), tuple(out_u)

        _, _, u_thr = lax.fori_loop(
            0, 16, radix_step,
            (alive0, (jnp.full((1, LANES), topk, I32),) * DSA_NB, (jnp.zeros((1, LANES), I32),) * DSA_NB))
        for n in batch:
            thr_bits = jnp.where(u_thr[n] < 0, u_thr[n] ^ jnp.int32(INT_MIN), ~u_thr[n])
            tf = lax.bitcast_convert_type(thr_bits, F32)
            thr_ref[n] = jnp.where(tf == -jnp.inf, select_all, tf)

    n_pairs_max = plane_ref.shape[1]
    n_pairs_few = (n_pairs_max + 1) // 2

    @pl.when(npair <= n_pairs_few)
    def _():
        radix_select(n_pairs_few)

    @pl.when(npair > n_pairs_few)
    def _():
        radix_select(n_pairs_max)

    thr_fast = [thr_ref[n] for n in batch]

    def ge_gt(n):
        return (lambda x: x >= thr_fast[n], lambda x: x > thr_fast[n])

    n_bad = jnp.int32(0)
    for n, (n_ge, n_gt) in zip(batch, count([ge_gt(n) for n in batch])):
        is_all = thr_fast[n] == select_all
        good = is_all | ((n_gt < topk) & (n_ge >= topk))
        cnt_ref[n] = jnp.where(is_all, 0, n_ge)
        n_bad = n_bad + jnp.sum(jnp.where(good, 0, 1))

    @pl.when(n_bad > 0)
    def _():
        def key_to_float(key):
            bits = jnp.where(key < 0, key ^ jnp.int32(0x7FFFFFFF), key)
            return lax.bitcast_convert_type(bits, F32)

        def count_ge(cand_keys):
            cands = [key_to_float(c) for c in cand_keys]
            return [c[0] for c in count([((lambda x, n=n: x >= cands[n]),) for n in batch])]

        c0 = count_ge([jnp.zeros((1, LANES), I32)] * DSA_NB)
        base = tuple(jnp.where(c >= topk, jnp.int32(0), jnp.int32(INT_MIN)) for c in c0)
        base_cnt = tuple(jnp.where(c >= topk, c, 0) for c in c0)

        def bit_step(i, carry):
            base, base_cnt = carry
            cand = [b | jnp.left_shift(jnp.int32(1), 30 - i) for b in base]
            cs = count_ge(cand)
            ok = [c >= topk for c in cs]
            return (tuple(jnp.where(ok[n], cand[n], base[n]) for n in batch),
                    tuple(jnp.where(ok[n], cs[n], base_cnt[n]) for n in batch))

        base, base_cnt = lax.fori_loop(0, 31, bit_step, (base, base_cnt))
        for n in batch:
            thr_ref[n] = jnp.where(base[n] == INT_MIN, select_all, key_to_float(base[n]))
            cnt_ref[n] = base_cnt[n]

    thr = [thr_ref[n] for n in batch]
    base_cnt = [cnt_ref[n] for n in batch]

    most = base_cnt[0]
    for n in batch[1:]:
        most = jnp.maximum(most, base_cnt[n])

    @pl.when(jnp.max(most) > topk)
    def _():
        n_gt = count([((lambda x, n=n: x > thr[n]),) for n in batch])
        rr = lax.broadcasted_iota(I32, (KP, KP), 0)
        cc = lax.broadcasted_iota(I32, (KP, KP), 1)
        tri = (rr >= cc).astype(BF16)
        for n in batch:
            need = (topk - n_gt[n][0]).astype(F32)

            def strike(i, seen, n=n, need=need):
                x = sc_ref[n, pair_ds(i), :]
                eq = x == thr[n]
                rank = _dot(tri, jnp.where(eq, 1.0, 0.0).astype(BF16)) + seen
                sc_ref[n, pair_ds(i), :] = jnp.where(eq & (rank > need), -jnp.inf, x)
                return rank[KP - 1:KP, :]

            lax.fori_loop(0, npair, strike, jnp.zeros((1, LANES), F32))

    m_ref[...] = jnp.full_like(m_ref, M_INIT)
    acc_ref[...] = jnp.zeros_like(acc_ref)
    GW = DSA_GROUP * LANES
    ones_rows = jnp.ones((ONES_ROWS, KP), BF16)

    def attn_pairs(pairs):
        work = [(n, i) for i in pairs for n in batch]
        logits = [_dot_nt(pair_rows(ak_ref, n, i, 0), qz_ref[n]).astype(BF16) for n, i in work]
        for (n, i), s in zip(work, logits):
            bias = jnp.where(sc_ref[n, pair_ds(i), :] >= thr[n], 0.0, -jnp.inf).astype(BF16)
            sb = s + jnp.concatenate([bias] * DSA_HEADS, axis=1)
            m_old = m_ref[n]
            m_new = jnp.maximum(m_old, jnp.max(sb, axis=0, keepdims=True).astype(F32))
            alpha = jnp.exp(m_old - m_new)
            pb = jnp.exp(sb - m_new.astype(BF16))
            m_ref[n] = m_new
            vt = pair_rows(avt_ref, n, i, 1)
            for g in range(DSA_KV_HEADS):
                cols = slice(g * GW, (g + 1) * GW)
                vg = jnp.concatenate([vt[g * DSA_HD:(g + 1) * DSA_HD, :], ones_rows], axis=0)
                acc_ref[n, g] = alpha[:, cols] * acc_ref[n, g] + _dot(vg, pb[:, cols])

    two_pair_trips(attn_pairs)

    for n in batch:
        heads = []
        for h in range(DSA_HEADS):
            a = acc_ref[n, h // DSA_GROUP][:, (h % DSA_GROUP) * LANES:(h % DSA_GROUP + 1) * LANES]
            heads.append(a[:DSA_HD, :] / a[DSA_HD:DSA_HD + 1, :])
        ot = jnp.concatenate(heads, axis=0)
        z = az_ref[n, 0]
        o_ref[n, 0] = (ot.T * (z * jax.nn.sigmoid(z))).astype(BF16)


def _dsa(ik2, ak, avt, aq, iq, iwt, az, B, nblk, topk):
    kern = functools.partial(_dsa_kernel, topk=topk, nblk=nblk)
    nb = DSA_NB
    whole = lambda b_, j: (b_, 0, 0, 0)
    qblk = lambda b_, j: (b_, j, 0, 0)
    by_block = lambda a: a.reshape(B, nblk, -1, a.shape[-1])
    out = pl.pallas_call(
        kern,
        grid=(B // nb, nblk),
        in_specs=[
            pl.BlockSpec((nb, nblk, LANES, LANES), whole),
            pl.BlockSpec((nb, nblk, LANES, LANES), whole),
            pl.BlockSpec((nb, nblk, LANES, LANES), whole),
            pl.BlockSpec((nb, 1, LANES, DSA_W), qblk),
            pl.BlockSpec((nb, 1, LANES, IDX_HEADS * IDX_HD), qblk),
            pl.BlockSpec((nb, 1, IDX_HEADS, LANES), qblk),
            pl.BlockSpec((nb, 1, LANES, DSA_W), qblk),
        ],
        out_specs=pl.BlockSpec((nb, 1, LANES, DSA_W), qblk),
        out_shape=jax.ShapeDtypeStruct((B, nblk, LANES, DSA_W), BF16),
        scratch_shapes=[
            pltpu.VMEM((nb, (nblk + 1) // 2 * 2 * LANES, LANES), F32),
            pltpu.VMEM((nb, DSA_HEADS * LANES, LANES), BF16),
            pltpu.VMEM((nb, IDX_HEADS * LANES, LANES), BF16),
            pltpu.VMEM((nb, 1, IDX_HEADS * LANES), F32),
            pltpu.VMEM((nb, 1, DSA_HEADS * LANES), F32),
            pltpu.VMEM((nb, DSA_KV_HEADS, DSA_HD + ONES_ROWS, DSA_GROUP * LANES), F32),
            pltpu.VMEM((nb, (nblk + 1) // 2, 32, SUBLANES, LANES), I32),
            pltpu.VMEM((nb, 1, LANES), F32),
            pltpu.VMEM((nb, 1, LANES), I32),
        ],
        compiler_params=pltpu.CompilerParams(vmem_limit_bytes=VMEM_LIMIT),
        name="dsa",
    )(by_block(ik2), by_block(ak), by_block(avt), by_block(aq), by_block(iq), by_block(iwt), by_block(az))
    return out.reshape(B * nblk, LANES, DSA_W)


def _out_kernel(mg_ref, md_ref, h_ref, w_ref, g_ref, b_ref, o_ref, *, alpha):
    n_sub = OUT_ROWS // OUT_SUB_ROWS
    subs = [slice(r * OUT_SUB_ROWS, (r + 1) * OUT_SUB_ROWS) for r in range(n_sub)]
    ys = [_dot(mg_ref[0, s, :], w_ref[:GLA_W, :]) + _dot(md_ref[0, s, :], w_ref[GLA_W:, :]) for s in subs]
    for s, y in zip(subs, ys):
        o_ref[0, s, :] = _layer_norm_rows(alpha * h_ref[0, s, :] + y, g_ref[...], b_ref[...])


def _out(mg, md, h, w, g, b, B, LP, alpha, drop_filler):
    rows = OUT_ROWS
    if drop_filler:
        S = LP - LANES
        grid, nout = (B, S // rows), S
        src_blk = lambda width: (pl.Element(1), pl.Element(rows), pl.Element(width))
        src = lambda b_, j: (b_, pl.multiple_of(LANES + j * rows, LANES), 0)
    else:
        grid, nout = (B * LP // rows,), LP
        src_blk = lambda width: (1, rows, width)
        mg, md, h = (a.reshape(1, B * LP, a.shape[-1]) for a in (mg, md, h))
        src = lambda i: (0, i, 0)
    dst = (lambda b_, j: (b_, j, 0)) if drop_filler else (lambda i: (0, i, 0))
    const = (lambda b_, j: (0, 0)) if drop_filler else (lambda i: (0, 0))
    out_shape = (B, nout, D_MODEL) if drop_filler else (1, B * LP, D_MODEL)
    kern = functools.partial(_out_kernel, alpha=alpha)
    return pl.pallas_call(
        kern,
        grid=grid,
        in_specs=[
            pl.BlockSpec(src_blk(GLA_W), src),
            pl.BlockSpec(src_blk(DSA_W), src),
            pl.BlockSpec(src_blk(D_MODEL), src),
            pl.BlockSpec((MIX_W, D_MODEL), const),
            pl.BlockSpec((1, D_MODEL), const),
            pl.BlockSpec((1, D_MODEL), const),
        ],
        out_specs=pl.BlockSpec((1, rows, D_MODEL), dst),
        out_shape=jax.ShapeDtypeStruct(out_shape, F32),
        name="out_proj_ln",
    )(mg, md, h, w, g, b)


def _pack_w_in(w):
    splits = (256, 256, 512, GLA_RANK, 512, 512, 128, 128, 512, IDX_HD, IDX_HEADS, 512)
    offs = np.cumsum((0,) + splits)
    gq, gk, gv, glr, gz, aq, ak, av, iq, ik, iw, az = [w[:, offs[i]:offs[i + 1]] for i in range(12)]

    def perm_heads(a):
        a = a.reshape(D_MODEL, DSA_KV_HEADS, DSA_GROUP, DSA_HD)
        return a.transpose(0, 2, 1, 3).reshape(D_MODEL, DSA_W)

    pad = jnp.zeros((D_MODEL, LANES - IDX_HD - GLA_RANK - IDX_HEADS), w.dtype)
    pieces = [gq * (GLA_DK ** -0.5), gk, gv, gz, perm_heads(aq) * (DSA_HD ** -0.5), ak, av,
              iq * (IDX_HD ** -0.5), az, ik, glr, iw, pad]
    return jnp.concatenate([p.astype(BF16) for p in pieces], axis=1)


def _rope_lane_tables(reps, LP):
    inv = ROPE_THETA ** (-jnp.arange(0, ROPE_DIM, 2, dtype=F32) / ROPE_DIM)
    pos = (jnp.arange(LP, dtype=F32) - PAD)[:, None]
    ang = pos * inv[None, :]
    cos, sin = jnp.cos(ang), jnp.sin(ang)
    ones = jnp.ones((LP, DSA_HD - ROPE_DIM), F32)
    zeros = jnp.zeros((LP, DSA_HD - ROPE_DIM), F32)
    zh = jnp.zeros((LP, ROPE_HALF), F32)
    c = jnp.concatenate([cos, cos, ones], axis=1)
    sa = jnp.concatenate([zh, sin, zeros], axis=1)
    sb = jnp.concatenate([-sin, zh, zeros], axis=1)
    tile = lambda t: jnp.tile(jnp.concatenate([t, t], axis=1), (reps, 1))
    return tile(c), tile(sa), tile(sb)


def kernel(x, meta_tokens, ln_in_g, ln_in_b, w_in, gla_wg2, gla_bg, gla_norm_g, idx_k_g, idx_k_b,
           w_out, ln_g, ln_b):
    B, S, D = x.shape
    depth = w_in.shape[0]
    nblk = S // LANES + 1
    LP = nblk * LANES
    R = B * LP
    topk = min(TOPK_MAX, S // 4)
    alpha = (2.0 * depth) ** 0.25
    tm = PROJ_ROWS

    meta_pad = jnp.concatenate([jnp.zeros((PAD, D), x.dtype), meta_tokens.astype(x.dtype)], axis=0)
    h = _embed(x, meta_pad, ln_in_g.reshape(1, D), ln_in_b.reshape(1, D))
    cos_t, sa_t, sb_t = _rope_lane_tables(tm // math.gcd(tm, LP), LP)

    for i in range(depth):
        w = _pack_w_in(w_in[i])
        wg2 = jnp.zeros((LANES, 256), F32).at[MISC_GLR:MISC_GLR + GLA_RANK].set(gla_wg2[i])
        wg2h = wg2.astype(BF16)
        wg2l = (wg2 - wg2h.astype(F32)).astype(BF16)
        ikg = jnp.zeros((1, LANES), F32).at[0, :IDX_HD].set(idx_k_g[i])
        ikb = jnp.zeros((1, LANES), F32).at[0, :IDX_HD].set(idx_k_b[i])
        (gq, gk, gv, glog, gz, aq, ak, avt, iq, ik2, iwt, az) = _proj(
            h.reshape(R, D), w, wg2h, wg2l, gla_bg[i].reshape(1, 256), ikg, ikb,
            cos_t, sa_t, sb_t, tm)
        mix_gla = _gla(gq, gk, glog, gv, gz, gla_norm_g[i].reshape(1, GLA_DV), B, LP)
        mix_dsa = _dsa(ik2, ak, avt, aq, iq, iwt, az, B, nblk, topk)
        h = _out(mix_gla, mix_dsa.reshape(B, LP, DSA_W), h.reshape(B, LP, D),
                 w_out[i].astype(BF16), ln_g[i].reshape(1, D), ln_b[i].reshape(1, D),
                 B, LP, alpha, drop_filler=(i == depth - 1))
    return h.reshape(B, S, D)
```
